```python
import jax, jax.numpy as jnp
from jax import lax
import numpy as np

D_MODEL = 1024
BATCH = 16
SEQ = 2048
DEPTH = 4

GRID_W = 64
CTX_LEN = 256
N_MIXERS = 4
EPS = 1e-6
ROPE_THETA = 10000.0
BLOCK = 128
N_FOURIER_GROUPS = 4
FOURIER_GROUP = D_MODEL // N_FOURIER_GROUPS
CONV_WIDTH = 3
MLA_HEADS = 16
MLA_Q_RANK = 256
MLA_KV_RANK = 128
MLA_NOPE = 128
MLA_ROPE = 64
MLA_V = 128
MLA_SCALE = (MLA_NOPE + MLA_ROPE) ** -0.5
GQA_Q_HEADS = 16
GQA_KV_HEADS = 4
GQA_GROUP = GQA_Q_HEADS // GQA_KV_HEADS
GQA_HEAD_DIM = 64
GQA_SCALE = GQA_HEAD_DIM ** -0.5
WINDOW = 128
N_EXPERTS = 16
N_EXPERT_GROUPS = 4
EXPERTS_PER_GROUP = N_EXPERTS // N_EXPERT_GROUPS
TOP_K = 2
EXPERT_FF = 256
N_FOURIER_LAYERS = (DEPTH + 3) // N_MIXERS
N_CONV_LAYERS = (DEPTH + 2) // N_MIXERS
N_MLA_LAYERS = (DEPTH + 1) // N_MIXERS
N_GQA_LAYERS = DEPTH // N_MIXERS

kernel_name = "hybrid_interleaved_diffusion_backbone"


def _rmsnorm(x, g):
    xf = x.astype(jnp.float32)
    y = xf * lax.rsqrt(jnp.mean(xf * xf, axis=-1, keepdims=True) + EPS)
    return (y * g.astype(jnp.float32)).astype(x.dtype)


def _axial_rope_tables(n_tokens, dim):
    rows = n_tokens // GRID_W
    row = jnp.repeat(jnp.arange(rows, dtype=jnp.float32), GRID_W)
    col = jnp.tile(jnp.arange(GRID_W, dtype=jnp.float32), rows)
    q = dim // 4
    inv = ROPE_THETA ** (-jnp.arange(q, dtype=jnp.float32) / q)
    ang = jnp.stack([row[:, None] * inv, col[:, None] * inv], axis=1)
    return jnp.cos(ang), jnp.sin(ang)


def _apply_axial_rope(x, cos, sin):
    q = x.shape[-1] // 4
    xr = x.reshape(*x.shape[:-1], 2, 2, q)
    x1, x2 = xr[..., 0, :], xr[..., 1, :]
    extra = (1,) * (x.ndim - 3)
    c = cos.reshape(cos.shape[0], *extra, 2, q).astype(x.dtype)
    s = sin.reshape(sin.shape[0], *extra, 2, q).astype(x.dtype)
    out = jnp.stack([x1 * c - x2 * s, x2 * c + x1 * s], axis=-2)
    return out.reshape(x.shape)


def _fourier_mix(h, w_out):
    b, l, d = h.shape
    hg = h.astype(jnp.float32).reshape(b, l, N_FOURIER_GROUPS, FOURIER_GROUP)
    f = jnp.fft.fft2(hg, axes=(1, 3), norm="ortho").real
    return f.reshape(b, l, d).astype(h.dtype) @ w_out


def _short_conv_mix(h, w_in, conv_w, w_out):
    b_gate, c_gate, xv = jnp.split(h @ w_in, 3, axis=-1)
    u = c_gate * xv
    up = jnp.pad(u, ((0, 0), (1, 1), (0, 0)))
    z = conv_w[0] * up[:, :-2] + conv_w[1] * up[:, 1:-1] + conv_w[2] * up[:, 2:]
    return (b_gate * z) @ w_out


def _mla_q(h, w_in, q_norm, w_uq, q_gain, cos, sin):
    b, l, _ = h.shape
    cq = _rmsnorm(h @ w_in[:, :MLA_Q_RANK], q_norm)
    q = (cq @ w_uq).reshape(b, l, MLA_HEADS, MLA_NOPE + MLA_ROPE)
    q_nope = _rmsnorm(q[..., :MLA_NOPE], q_gain[:MLA_NOPE])
    q_rope = _rmsnorm(q[..., MLA_NOPE:], q_gain[MLA_NOPE:])
    if cos is not None:
        q_rope = _apply_axial_rope(q_rope, cos, sin)
    return q_nope, q_rope


def _mla_kv(h, w_in, kv_norm, w_ukv, k_gain, cos, sin):
    b, l, _ = h.shape
    ckr = h @ w_in[:, MLA_Q_RANK:]
    ckv = _rmsnorm(ckr[..., :MLA_KV_RANK], kv_norm)
    kv = (ckv @ w_ukv).reshape(b, l, MLA_HEADS, MLA_NOPE + MLA_V)
    k_nope = _rmsnorm(kv[..., :MLA_NOPE], k_gain[:MLA_NOPE])
    v = kv[..., MLA_NOPE:]
    k_rope = _rmsnorm(ckr[..., MLA_KV_RANK:], k_gain[MLA_NOPE:])
    if cos is not None:
        k_rope = _apply_axial_rope(k_rope, cos, sin)
    return k_nope, k_rope, v


def _mla_softmax(qn, qr, kn, kr, v):
    s = (jnp.einsum('bqhd,bkhd->bhqk', qn, kn, preferred_element_type=jnp.float32)
         + jnp.einsum('bqhd,bkd->bhqk', qr, kr, preferred_element_type=jnp.float32)) * MLA_SCALE
    p = jax.nn.softmax(s, axis=-1).astype(v.dtype)
    return jnp.einsum('bhqk,bkhd->bqhd', p, v)


def _mla_latent_attention(q_nope, q_rope, k_nope, k_rope, v):
    b, l = q_nope.shape[:2]
    nb = l // BLOCK

    def blocks(t):
        return jnp.moveaxis(t.reshape(b, nb, BLOCK, *t.shape[2:]), 1, 0)

    out = lax.map(lambda a: _mla_softmax(a[0], a[1], k_nope, k_rope, v), (blocks(q_nope), blocks(q_rope)))
    return jnp.moveaxis(out, 0, 1).reshape(b, l, MLA_HEADS * MLA_V)


def _gqa_q(h, w_in, q_gain, cos, sin):
    b, l, _ = h.shape
    q = _rmsnorm((h @ w_in[:, :GQA_Q_HEADS * GQA_HEAD_DIM]).reshape(b, l, GQA_Q_HEADS, GQA_HEAD_DIM), q_gain)
    if cos is not None:
        q = _apply_axial_rope(q, cos, sin)
    return q


def _gqa_kv(h, w_in, k_gain, cos, sin):
    b, l, _ = h.shape
    kv = (h @ w_in[:, GQA_Q_HEADS * GQA_HEAD_DIM:]).reshape(b, l, 2, GQA_KV_HEADS, GQA_HEAD_DIM)
    k = _rmsnorm(kv[:, :, 0], k_gain)
    v = kv[:, :, 1]
    if cos is not None:
        k = _apply_axial_rope(k, cos, sin)
    return k, v


def _window_attention(q, k, v, kc, vc, sink):
    b, l = q.shape[:2]
    lc = kc.shape[1]
    nb = l // BLOCK
    qb = q.reshape(b, nb, BLOCK, GQA_KV_HEADS, GQA_GROUP, GQA_HEAD_DIM)

    def band(t):
        tp = jnp.pad(t, ((0, 0), (BLOCK, BLOCK), (0, 0), (0, 0))).reshape(b, nb + 2, BLOCK, GQA_KV_HEADS, GQA_HEAD_DIM)
        return jnp.concatenate([tp[:, :-2], tp[:, 1:-1], tp[:, 2:]], axis=2)

    kw, vw = band(k), band(v)
    qpos = jnp.arange(nb)[:, None] * BLOCK + jnp.arange(BLOCK)[None, :]
    kpos = jnp.arange(nb)[:, None] * BLOCK - BLOCK + jnp.arange(3 * BLOCK)[None, :]
    valid = ((jnp.abs(qpos[:, :, None] - kpos[:, None, :]) <= WINDOW)
             & (kpos[:, None, :] >= 0) & (kpos[:, None, :] < l))
    s_band = jnp.einsum('bnqhgd,bnkhd->bnhgqk', qb, kw, preferred_element_type=jnp.float32) * GQA_SCALE
    s_band = jnp.where(valid[None, :, None, None], s_band, -1e30)
    s_ctx = jnp.einsum('bnqhgd,bchd->bnhgqc', qb, kc, preferred_element_type=jnp.float32) * GQA_SCALE
    s_sink = jnp.broadcast_to(sink.astype(jnp.float32).reshape(GQA_KV_HEADS, GQA_GROUP)[None, None, :, :, None, None],
                              s_ctx.shape[:-1] + (1,))
    p = jax.nn.softmax(jnp.concatenate([s_ctx, s_band, s_sink], axis=-1), axis=-1).astype(v.dtype)
    out = (jnp.einsum('bnhgqc,bchd->bnqhgd', p[..., :lc], vc)
           + jnp.einsum('bnhgqk,bnkhd->bnqhgd', p[..., lc:lc + 3 * BLOCK], vw))
    return out.reshape(b, l, GQA_Q_HEADS * GQA_HEAD_DIM)


def _ctx_sink_attention(qc, kc, vc, sink):
    b, lc = qc.shape[:2]
    qg = qc.reshape(b, lc, GQA_KV_HEADS, GQA_GROUP, GQA_HEAD_DIM)
    s = jnp.einsum('bqhgd,bkhd->bhgqk', qg, kc, preferred_element_type=jnp.float32) * GQA_SCALE
    s_sink = jnp.broadcast_to(sink.astype(jnp.float32).reshape(GQA_KV_HEADS, GQA_GROUP)[None, :, :, None, None],
                              s.shape[:-1] + (1,))
    p = jax.nn.softmax(jnp.concatenate([s, s_sink], axis=-1), axis=-1)[..., :-1].astype(vc.dtype)
    return jnp.einsum('bhgqk,bkhd->bqhgd', p, vc).reshape(b, lc, GQA_Q_HEADS * GQA_HEAD_DIM)


def _moe(h, router_w, router_bias, w_gate, w_up, w_down):
    n = h.shape[0]
    scores = jax.nn.sigmoid((h @ router_w).astype(jnp.float32))
    biased = (scores + router_bias.astype(jnp.float32)).reshape(n, N_EXPERT_GROUPS, EXPERTS_PER_GROUP)
    group_score = lax.top_k(biased, TOP_K)[0].sum(-1)
    best = jnp.argmax(group_score, axis=-1)
    in_group = jax.nn.one_hot(best, N_EXPERT_GROUPS, dtype=jnp.float32)[:, :, None] > 0
    masked = jnp.where(in_group, biased, -jnp.inf).reshape(n, N_EXPERTS)
    _, idx = lax.top_k(masked, TOP_K)
    w = jnp.take_along_axis(scores, idx, axis=-1)
    w = w / jnp.sum(w, axis=-1, keepdims=True)
    comb = jnp.sum(jax.nn.one_hot(idx, N_EXPERTS, dtype=jnp.float32) * w[..., None], axis=1).astype(h.dtype)
    y = jnp.zeros_like(h)
    for e in range(N_EXPERTS):
        he = jax.nn.silu(h @ w_gate[e]) * (h @ w_up[e])
        y = y + comb[:, e:e + 1] * (he @ w_down[e])
    return y


def setup_inputs(seed: int = 0) -> dict:
    key = jax.random.key(seed)
    ks = jax.random.split(key, 32)
    f32 = jnp.float32
    D = D_MODEL

    def w(k, shape, fan_in, gain=1.0):
        return jax.random.normal(k, shape, f32) * (gain * fan_in ** -0.5)

    def g(k, shape):
        return 1.0 + 0.02 * jax.random.normal(k, shape, f32)

    gqa_in = (GQA_Q_HEADS + 2 * GQA_KV_HEADS) * GQA_HEAD_DIM
    return {
        "x": jax.random.normal(ks[0], (BATCH, SEQ, D), f32),
        "c": jax.random.normal(ks[1], (BATCH, D), f32),
        "ctx": jax.random.normal(ks[2], (BATCH, CTX_LEN, D), f32),
        "c_ctx": jax.random.normal(ks[3], (D,), f32),
        "w_ada": w(ks[4], (DEPTH, D, 6 * D), D, 0.5),
        "b_ada": 0.02 * jax.random.normal(ks[5], (DEPTH, 6 * D), f32),
        "norm_mix": g(ks[6], (DEPTH, D)),
        "norm_ffn": g(ks[7], (DEPTH, D)),
        "fourier_w_out": w(ks[8], (N_FOURIER_LAYERS, D, D), D),
        "conv_w_in": w(ks[9], (N_CONV_LAYERS, D, 3 * D), D),
        "conv_w": w(ks[10], (N_CONV_LAYERS, CONV_WIDTH, D), CONV_WIDTH),
        "conv_w_out": w(ks[11], (N_CONV_LAYERS, D, D), D),
        "mla_w_in": w(ks[12], (N_MLA_LAYERS, D, MLA_Q_RANK + MLA_KV_RANK + MLA_ROPE), D),
        "mla_q_norm": g(ks[13], (N_MLA_LAYERS, MLA_Q_RANK)),
        "mla_kv_norm": g(ks[14], (N_MLA_LAYERS, MLA_KV_RANK)),
        "mla_w_uq": w(ks[15], (N_MLA_LAYERS, MLA_Q_RANK, MLA_HEADS * (MLA_NOPE + MLA_ROPE)), MLA_Q_RANK),
        "mla_w_ukv": w(ks[16], (N_MLA_LAYERS, MLA_KV_RANK, MLA_HEADS * (MLA_NOPE + MLA_V)), MLA_KV_RANK),
        "mla_q_gain": g(ks[17], (N_MLA_LAYERS, MLA_NOPE + MLA_ROPE)),
        "mla_k_gain": g(ks[18], (N_MLA_LAYERS, MLA_NOPE + MLA_ROPE)),
        "mla_w_out": w(ks[19], (N_MLA_LAYERS, MLA_HEADS * MLA_V, D), MLA_HEADS * MLA_V),
        "gqa_w_in": w(ks[20], (N_GQA_LAYERS, D, gqa_in), D),
        "gqa_q_gain": g(ks[21], (N_GQA_LAYERS, GQA_HEAD_DIM)),
        "gqa_k_gain": g(ks[22], (N_GQA_LAYERS, GQA_HEAD_DIM)),
        "gqa_sink": 0.5 * jax.random.normal(ks[23], (N_GQA_LAYERS, GQA_Q_HEADS), f32),
        "gqa_w_out": w(ks[24], (N_GQA_LAYERS, GQA_Q_HEADS * GQA_HEAD_DIM, D), GQA_Q_HEADS * GQA_HEAD_DIM),
        "router_w": w(ks[25], (D, N_EXPERTS), D),
        "router_bias": 0.01 * jax.random.normal(ks[26], (N_EXPERTS,), f32),
        "moe_w_gate": w(ks[27], (DEPTH, N_EXPERTS, D, EXPERT_FF), D),
        "moe_w_up": w(ks[28], (DEPTH, N_EXPERTS, D, EXPERT_FF), D),
        "moe_w_down": w(ks[29], (DEPTH, N_EXPERTS, EXPERT_FF, D), EXPERT_FF),
    }


def reference(x, c, ctx, c_ctx, w_ada, b_ada, norm_mix, norm_ffn, fourier_w_out, conv_w_in, conv_w, conv_w_out,
              mla_w_in, mla_q_norm, mla_kv_norm, mla_w_uq, mla_w_ukv, mla_q_gain, mla_k_gain, mla_w_out,
              gqa_w_in, gqa_q_gain, gqa_k_gain, gqa_sink, gqa_w_out, router_w, router_bias,
              moe_w_gate, moe_w_up, moe_w_down):
    b, l, d = x.shape
    lc = ctx.shape[1]
    cos_m, sin_m = _axial_rope_tables(l, MLA_ROPE)
    cos_g, sin_g = _axial_rope_tables(l, GQA_HEAD_DIM)
    silu_c = jax.nn.silu(c)
    silu_cc = jax.nn.silu(c_ctx)
    for i in range(DEPTH):
        kind, j = i % N_MIXERS, i // N_MIXERS
        ctx_next = i < DEPTH - 1
        need_hc = ctx_next or kind >= 2
        sh1, sc1, g1, sh2, sc2, g2 = jnp.split((silu_c @ w_ada[i] + b_ada[i])[:, None, :], 6, axis=-1)
        hx = _rmsnorm(x, norm_mix[i]) * (1 + sc1) + sh1
        if need_hc:
            csh1, csc1, cg1, csh2, csc2, cg2 = jnp.split(silu_cc @ w_ada[i] + b_ada[i], 6, axis=-1)
            hc = _rmsnorm(ctx, norm_mix[i]) * (1 + csc1) + csh1
        if kind == 0:
            yx = _fourier_mix(hx, fourier_w_out[j])
            if ctx_next:
                yc = _fourier_mix(hc, fourier_w_out[j])
        elif kind == 1:
            yx = _short_conv_mix(hx, conv_w_in[j], conv_w[j], conv_w_out[j])
            if ctx_next:
                yc = _short_conv_mix(hc, conv_w_in[j], conv_w[j], conv_w_out[j])
        elif kind == 2:
            kn_c, kr_c, v_c = _mla_kv(hc, mla_w_in[j], mla_kv_norm[j], mla_w_ukv[j], mla_k_gain[j], None, None)
            kn_x, kr_x, v_x = _mla_kv(hx, mla_w_in[j], mla_kv_norm[j], mla_w_ukv[j], mla_k_gain[j], cos_m, sin_m)
            qn_x, qr_x = _mla_q(hx, mla_w_in[j], mla_q_norm[j], mla_w_uq[j], mla_q_gain[j], cos_m, sin_m)
            kn = jnp.concatenate([kn_c, kn_x], axis=1)
            kr = jnp.concatenate([kr_c, kr_x], axis=1)
            vv = jnp.concatenate([v_c, v_x], axis=1)
            yx = _mla_latent_attention(qn_x, qr_x, kn, kr, vv) @ mla_w_out[j]
            if ctx_next:
                qn_c, qr_c = _mla_q(hc, mla_w_in[j], mla_q_norm[j], mla_w_uq[j], mla_q_gain[j], None, None)
                yc = _mla_softmax(qn_c, qr_c, kn_c, kr_c, v_c).reshape(b, lc, MLA_HEADS * MLA_V) @ mla_w_out[j]
        else:
            kc, vc = _gqa_kv(hc, gqa_w_in[j], gqa_k_gain[j], None, None)
            kx, vx = _gqa_kv(hx, gqa_w_in[j], gqa_k_gain[j], cos_g, sin_g)
            qx = _gqa_q(hx, gqa_w_in[j], gqa_q_gain[j], cos_g, sin_g)
            yx = _window_attention(qx, kx, vx, kc, vc, gqa_sink[j]) @ gqa_w_out[j]
            if ctx_next:
                qc = _gqa_q(hc, gqa_w_in[j], gqa_q_gain[j], None, None)
                yc = _ctx_sink_attention(qc, kc, vc, gqa_sink[j]) @ gqa_w_out[j]
        x = x + g1 * yx
        hx = _rmsnorm(x, norm_ffn[i]) * (1 + sc2) + sh2
        if ctx_next:
            ctx = ctx + cg1 * yc
            hc = _rmsnorm(ctx, norm_ffn[i]) * (1 + csc2) + csh2
            tokens = jnp.concatenate([hc.reshape(-1, d), hx.reshape(-1, d)], axis=0)
            y = _moe(tokens, router_w, router_bias, moe_w_gate[i], moe_w_up[i], moe_w_down[i])
            ctx = ctx + cg2 * y[:b * lc].reshape(b, lc, d)
            x = x + g2 * y[b * lc:].reshape(b, l, d)
        else:
            y = _moe(hx.reshape(-1, d), router_w, router_bias, moe_w_gate[i], moe_w_up[i], moe_w_down[i])
            x = x + g2 * y.reshape(b, l, d)
    return x
```

```python
import functools

import numpy as np
import jax
import jax.numpy as jnp
from jax import lax
from jax.experimental import pallas as pl
from jax.experimental.pallas import tpu as pltpu

F32, BF16 = jnp.float32, jnp.bfloat16
SDS = jax.ShapeDtypeStruct

EPS = 1e-6
GRID_W = 64
ROPE_THETA = 10000.0
N_FOURIER_GROUPS = 4
MLA_HEADS, MLA_Q_RANK, MLA_KV_RANK = 16, 256, 128
MLA_NOPE, MLA_ROPE, MLA_V = 128, 64, 128
MLA_SCALE = (MLA_NOPE + MLA_ROPE) ** -0.5
GQA_Q_HEADS, GQA_KV_HEADS, GQA_HEAD_DIM = 16, 4, 64
GQA_GROUP = GQA_Q_HEADS // GQA_KV_HEADS
GQA_SCALE = GQA_HEAD_DIM ** -0.5
WINDOW = 128
BLOCK = 128
N_EXPERTS, N_EXPERT_GROUPS, EXPERT_FF = 16, 4, 256
EXPERTS_PER_GROUP = N_EXPERTS // N_EXPERT_GROUPS
ROPE_DIM = 64

V7X_VMEM_BYTES = 64 * 1024 * 1024
LANES = 128
SH1, SC1, G1, SH2, SC2, G2 = range(6)


def _params(sem, vmem_mb):
    return pltpu.CompilerParams(dimension_semantics=sem, vmem_limit_bytes=vmem_mb * 1024 * 1024)


def _sigmoid(v):
    return 1.0 / (1.0 + jnp.exp(-v))


def _modnorm(x, gain, sc, sh):
    ms = jnp.mean(x * x, axis=-1, keepdims=True)
    return x * lax.rsqrt(ms + EPS) * (gain * (1.0 + sc)) + sh


def _nt(a, b):
    return lax.dot_general(a, b, (((1,), (1,)), ((), ())), preferred_element_type=F32)


def _dot(a, b):
    return jnp.dot(a, b, preferred_element_type=F32)


class Stream:
    def __init__(self, n, seg, tile, batch, is_ctx):
        self.n, self.seg, self.tile, self.batch, self.is_ctx = n, seg, tile, batch, is_ctx
        self.tiles = n // tile
        self.tiles_per_seg = seg // tile

    def mod_row(self, t):
        return self.batch if self.is_ctx else t // self.tiles_per_seg

    def seg_row(self, b):
        return self.batch if self.is_ctx else b


def _mod_spec(st, chunk, d):
    return pl.BlockSpec((None, 1, d), lambda t: (st.mod_row(t), 0, chunk))


def _full(shape):
    nd = len(shape)
    return pl.BlockSpec(shape, lambda *_: (0,) * nd)


def _resident(shape):
    nd = len(shape)
    return pl.BlockSpec(shape, lambda *_: (0,) * nd, pipeline_mode=pl.Buffered(1))


def _ada_call(cvec, w_ada, b_ada):
    depth, d, d6 = w_ada.shape
    r8 = cvec.shape[0]
    tn = d6 // 4

    def kern(c_ref, w_ref, b_ref, o_ref):
        c = c_ref[...]
        s = (c * _sigmoid(c)).astype(BF16)
        o_ref[...] = _dot(s, w_ref[...].astype(BF16)) + b_ref[...]

    return pl.pallas_call(
        kern, grid=(depth, d6 // tn),
        in_specs=[pl.BlockSpec((r8, d), lambda i, j: (0, 0)),
                  pl.BlockSpec((None, d, tn), lambda i, j: (i, 0, j)),
                  pl.BlockSpec((None, 1, tn), lambda i, j: (i, 0, j))],
        out_specs=pl.BlockSpec((None, r8, tn), lambda i, j: (i, 0, j)),
        out_shape=SDS((depth, r8, d6), F32), name="ada",
        compiler_params=_params(("arbitrary", "arbitrary"), 40))(cvec, w_ada, b_ada.reshape(depth, 1, d6))


def _proj_call(st, x, mods, gain, w, name):
    n, d = x.shape
    nout = w.shape[1]
    t = st.tile

    def kern(x_ref, g_ref, sc_ref, sh_ref, w_ref, o_ref):
        h = _modnorm(x_ref[...], g_ref[...], sc_ref[...], sh_ref[...]).astype(BF16)
        o_ref[...] = _dot(h, w_ref[...]).astype(BF16)

    return pl.pallas_call(
        kern, grid=(st.tiles,),
        in_specs=[pl.BlockSpec((t, d), lambda i: (i, 0)), _full((1, d)),
                  _mod_spec(st, SC1, d), _mod_spec(st, SH1, d), _resident((d, nout))],
        out_specs=pl.BlockSpec((t, nout), lambda i: (i, 0)),
        out_shape=SDS((n, nout), BF16), name=name,
        compiler_params=_params(("parallel",), 40))(x, gain, mods, mods, w)


def _outproj_call(st, a, x, mods, w, name):
    n, d = x.shape
    k = a.shape[1]
    t = st.tile

    def kern(a_ref, x_ref, g1_ref, w_ref, o_ref):
        o_ref[...] = x_ref[...] + g1_ref[...] * _dot(a_ref[...], w_ref[...])

    return pl.pallas_call(
        kern, grid=(st.tiles,),
        in_specs=[pl.BlockSpec((t, k), lambda i: (i, 0)), pl.BlockSpec((t, d), lambda i: (i, 0)),
                  _mod_spec(st, G1, d), _resident((k, d))],
        out_specs=pl.BlockSpec((t, d), lambda i: (i, 0)),
        out_shape=SDS((n, d), F32), name=name,
        compiler_params=_params(("parallel",), 40))(a, x, mods, w)


def _dft_tables(length, radix):
    lr = length // radix
    m = np.arange(lr)[None, :, None]
    j = np.arange(radix)[:, None, None]
    nn = np.arange(lr)[None, None, :]
    ang = 2.0 * np.pi * (((radix * m + j) * nn) % length) / length
    e = np.concatenate([np.cos(ang), np.sin(ang)], axis=-1) / np.sqrt(length)
    return jnp.asarray(e, dtype=F32).astype(BF16)


def _group_dft_tables(group):
    k = np.arange(group)
    ang = 2.0 * np.pi * ((k[:, None] * k[None, :]) % group) / group
    return (jnp.asarray(np.cos(ang) / np.sqrt(group), dtype=F32),
            jnp.asarray(np.sin(ang) / np.sqrt(group), dtype=F32))


def _fourier_weight_call(w_out):
    d = w_out.shape[0]
    grp = d // N_FOURIER_GROUPS
    cg, sg = _group_dft_tables(grp)

    def kern(cg_ref, sg_ref, w_ref, o_ref):
        w = w_ref[...]
        o_ref[:, :d] = jnp.dot(cg_ref[...], w, preferred_element_type=F32,
                               precision=lax.Precision.HIGHEST).astype(BF16)
        o_ref[:, d:] = jnp.dot(sg_ref[...], w, preferred_element_type=F32,
                               precision=lax.Precision.HIGHEST).astype(BF16)

    return pl.pallas_call(
        kern, grid=(N_FOURIER_GROUPS,),
        in_specs=[_full((grp, grp)), _full((grp, grp)), pl.BlockSpec((grp, d), lambda g: (g, 0))],
        out_specs=pl.BlockSpec((grp, 2 * d), lambda g: (g, 0)),
        out_shape=SDS((d, 2 * d), BF16), name="fourier_w",
        compiler_params=_params(("arbitrary",), 32))(cg, sg, w_out)


def _radix_terms(radix, j):
    real, imag = [], []
    for q in range(radix):
        k = (j * q * (4 // radix)) % 4 if radix > 1 else 0
        if k == 0:
            real.append((1, 0, q)); imag.append((-1, 1, q))
        elif k == 1:
            real.append((-1, 1, q)); imag.append((-1, 0, q))
        elif k == 2:
            real.append((-1, 0, q)); imag.append((1, 1, q))
        else:
            real.append((1, 1, q)); imag.append((1, 0, q))
    return real, imag


def _seq_dft_call(st, p, x, mods, radix):
    n, d = x.shape
    seg = st.seg
    lr = seg // radix
    e = _dft_tables(seg, radix)
    nb = n // seg
    cblk = 256

    def kern(p_ref, x_ref, e_ref, g1_ref, o_ref, v_ref):
        jdyn = pl.program_id(1)

        def fill(j):
            real, imag = _radix_terms(radix, j)
            for c in range(d // cblk):
                def piece(part, q):
                    return p_ref[q * lr:(q + 1) * lr, part * d + c * cblk:part * d + (c + 1) * cblk].astype(F32)

                def comb(terms):
                    acc = None
                    for sgn, part, q in terms:
                        v = piece(part, q)
                        if acc is None:
                            acc = v if sgn > 0 else -v
                        else:
                            acc = acc + v if sgn > 0 else acc - v
                    return acc

                v_ref[0:lr, c * cblk:(c + 1) * cblk] = comb(real).astype(BF16)
                v_ref[lr:2 * lr, c * cblk:(c + 1) * cblk] = comb(imag).astype(BF16)

        if radix == 1:
            fill(0)
        else:
            for j in range(radix):
                pl.when(jdyn == j)(functools.partial(fill, j))
        z = _dot(e_ref[...], v_ref[...])
        o_ref[...] = x_ref[...] + g1_ref[...] * z

    xv = x.reshape(n // radix, radix * d)
    out = pl.pallas_call(
        kern, grid=(nb, radix),
        in_specs=[pl.BlockSpec((seg, 2 * d), lambda b, j: (b, 0)),
                  pl.BlockSpec((lr, d), lambda b, j: (b, j)),
                  pl.BlockSpec((None, lr, 2 * lr), lambda b, j: (j, 0, 0)),
                  pl.BlockSpec((None, 1, d), lambda b, j: (st.seg_row(b), 0, G1))],
        out_specs=pl.BlockSpec((lr, d), lambda b, j: (b, j)),
        out_shape=SDS(xv.shape, F32),
        scratch_shapes=[pltpu.VMEM((2 * lr, d), BF16)], name=f"seq_dft_r{radix}",
        compiler_params=_params(("parallel", "arbitrary"), 48))(p, xv, e, mods)
    return out.reshape(n, d)


def _fourier_layer(streams, xs, mods, gain, w_out):
    wcs = _fourier_weight_call(w_out)
    outs = []
    for st, x in zip(streams, xs):
        if x is None:
            outs.append(None)
            continue
        p = _proj_call(st, x, mods, gain, wcs, "fourier_proj")
        outs.append(_seq_dft_call(st, p, x, mods, 4 if st.seg >= 1024 else 1))
    return outs


def _conv_in_call(st, x, mods, gain, w_in):
    n, d = x.shape
    t = st.tile

    def kern(x_ref, g_ref, sc_ref, sh_ref, w_ref, bg_ref, u_ref):
        h = _modnorm(x_ref[...], g_ref[...], sc_ref[...], sh_ref[...]).astype(BF16)
        bg_ref[...] = _dot(h, w_ref[:, 0:d]).astype(BF16)
        u_ref[...] = (_dot(h, w_ref[:, d:2 * d]) * _dot(h, w_ref[:, 2 * d:3 * d])).astype(BF16)

    return pl.pallas_call(
        kern, grid=(st.tiles,),
        in_specs=[pl.BlockSpec((t, d), lambda i: (i, 0)), _full((1, d)),
                  _mod_spec(st, SC1, d), _mod_spec(st, SH1, d), _resident((d, 3 * d))],
        out_specs=[pl.BlockSpec((t, d), lambda i: (i, 0))] * 2,
        out_shape=[SDS((n, d), BF16)] * 2, name="conv_in",
        compiler_params=_params(("parallel",), 48))(x, gain, mods, mods, w_in)


def _conv_out_call(st, bg, u, x, mods, conv_w, w_out):
    n, d = x.shape
    t = st.tile
    halo = 16
    hb = t // halo
    nhalo = n // halo
    seg = st.seg

    def kern(u_ref, up_ref, un_ref, bg_ref, x_ref, cw_ref, g1_ref, w_ref, o_ref):
        i = pl.program_id(0)
        u = u_ref[...].astype(F32)
        row = lax.broadcasted_iota(jnp.int32, (t, 1), 0)
        pos = (i * t + row) & (seg - 1)
        prev_row = up_ref[...].astype(F32)[halo - 1:halo, :]
        next_row = un_ref[...].astype(F32)[0:1, :]
        um = jnp.where(row == 0, prev_row, pltpu.roll(u, 1, axis=0))
        um = jnp.where(pos == 0, 0.0, um)
        up = jnp.where(row == t - 1, next_row, pltpu.roll(u, t - 1, axis=0))
        up = jnp.where(pos == seg - 1, 0.0, up)
        z = cw_ref[0:1, :] * um + cw_ref[1:2, :] * u + cw_ref[2:3, :] * up
        a = (bg_ref[...].astype(F32) * z).astype(BF16)
        o_ref[...] = x_ref[...] + g1_ref[...] * _dot(a, w_ref[...])

    tile_spec = pl.BlockSpec((t, d), lambda i: (i, 0))
    return pl.pallas_call(
        kern, grid=(st.tiles,),
        in_specs=[tile_spec,
                  pl.BlockSpec((halo, d), lambda i: (jnp.maximum(i * hb - 1, 0), 0)),
                  pl.BlockSpec((halo, d), lambda i: (jnp.minimum((i + 1) * hb, nhalo - 1), 0)),
                  tile_spec, tile_spec, _full((3, d)), _mod_spec(st, G1, d), _resident((d, d))],
        out_specs=tile_spec, out_shape=SDS((n, d), F32), name="conv_out",
        compiler_params=_params(("parallel",), 40))(u, u, u, bg, x, conv_w, mods, w_out)


def _conv_layer(streams, xs, mods, gain, w_in, conv_w, w_out):
    w_in_b, w_out_b = w_in.astype(BF16), w_out.astype(BF16)
    outs = []
    for st, x in zip(streams, xs):
        if x is None:
            outs.append(None)
            continue
        bg, u = _conv_in_call(st, x, mods, gain, w_in_b)
        outs.append(_conv_out_call(st, bg, u, x, mods, conv_w, w_out_b))
    return outs


def _rope_tables(length):
    q = ROPE_DIM // 4
    pos = np.arange(length)
    rc = np.stack([pos // GRID_W, pos % GRID_W], axis=1).astype(np.float32)
    inv = (ROPE_THETA ** (-np.arange(q, dtype=np.float32) / q)).astype(np.float32)
    lane = np.arange(ROPE_DIM)
    ang = rc[:, lane >> 5] * inv[lane & (q - 1)][None, :]
    sign = np.where((lane & q) == 0, -1.0, 1.0)[None, :]
    cos = np.cos(ang.astype(np.float32)).astype(np.float32)
    sin = (np.sin(ang.astype(np.float32)) * sign).astype(np.float32)
    reps = LANES // ROPE_DIM
    return jnp.asarray(np.tile(cos, (1, reps))), jnp.asarray(np.tile(sin, (1, reps)))


def _rope_partner_perm(n_heads):
    lane = np.arange(n_heads * ROPE_DIM)
    return lane ^ (ROPE_DIM // 4)


def _seg_rinv(x, seg):
    t, w = x.shape
    cols = []
    for c in range(w // LANES):
        blk = x[:, c * LANES:(c + 1) * LANES]
        sq = blk * blk
        if seg == LANES:
            cols.append(jnp.broadcast_to(
                lax.rsqrt(jnp.mean(sq, axis=-1, keepdims=True) + EPS), (t, LANES)))
        else:
            lane = lax.broadcasted_iota(jnp.int32, (t, LANES), 1)
            low = lane < seg
            lo = jnp.sum(jnp.where(low, sq, 0.0), axis=-1, keepdims=True) * (1.0 / seg)
            hi = jnp.sum(jnp.where(low, 0.0, sq), axis=-1, keepdims=True) * (1.0 / seg)
            cols.append(jnp.where(low, lax.rsqrt(lo + EPS), lax.rsqrt(hi + EPS)))
    return cols[0] if len(cols) == 1 else jnp.concatenate(cols, axis=-1)


def _tile_lanes(v, w):
    reps = w // LANES
    return v if reps == 1 else jnp.concatenate([v] * reps, axis=-1)


def _mla_proj_call(st, x, mods, gain, wts, rope, need_q):
    n, d = x.shape
    t = st.tile
    h = MLA_HEADS
    wn, wr = h * MLA_NOPE, h * MLA_ROPE
    (w_in, q_norm, kv_norm, w_uq, w_ukv, qg_n, qg_r, qg_rp, kg_n, kg_r, kg_rp) = wts
    use_rope = rope is not None

    def kern(*refs):
        refs = list(refs)
        x_ref, g_ref, sc_ref, sh_ref, win_ref, qn_ref_, kvn_ref_, wuq_ref, wukv_ref = refs[:9]
        qgn_ref, qgr_ref, qgrp_ref, kgn_ref, kgr_ref, kgrp_ref = refs[9:15]
        rest = refs[15:]
        if use_rope:
            cos_ref, sin_ref = rest[:2]
            rest = rest[2:]
        if need_q:
            oqn_ref, oqr_ref = rest[:2]
            rest = rest[2:]
        okn_ref, okr_ref, ov_ref = rest
        hm = _modnorm(x_ref[...], g_ref[...], sc_ref[...], sh_ref[...]).astype(BF16)
        ck = _dot(hm, win_ref[...])
        kr2 = ck[:, MLA_Q_RANK + MLA_KV_RANK:]
        kr_rinv = _seg_rinv(kr2, MLA_ROPE)[:, 0:MLA_ROPE]
        kr_raw = kr2[:, 0:MLA_ROPE] * kgr_ref[...]
        if use_rope:
            kr_par = kr2[:, MLA_ROPE:] * kgrp_ref[...]
            kr = kr_rinv * (kr_raw * cos_ref[:, 0:MLA_ROPE] + kr_par * sin_ref[:, 0:MLA_ROPE])
        else:
            kr = kr_rinv * kr_raw
        okr_ref[...] = kr.astype(BF16)
        ckv = ck[:, MLA_Q_RANK:MLA_Q_RANK + MLA_KV_RANK]
        ckv = (ckv * _seg_rinv(ckv, LANES) * kvn_ref_[...]).astype(BF16)
        kv = _dot(ckv, wukv_ref[...])
        kn = kv[:, 0:wn]
        okn_ref[...] = (kn * _seg_rinv(kn, MLA_NOPE) * kgn_ref[...]).astype(BF16)
        ov_ref[...] = kv[:, wn:].astype(BF16)
        if need_q:
            cq = ck[:, 0:MLA_Q_RANK]
            rq = lax.rsqrt(jnp.mean(cq * cq, axis=-1, keepdims=True) + EPS)
            cq = (cq * rq * qn_ref_[...]).astype(BF16)
            q = _dot(cq, wuq_ref[...])
            qn = q[:, 0:wn]
            oqn_ref[...] = (qn * _seg_rinv(qn, MLA_NOPE) * (qgn_ref[...] * MLA_SCALE)).astype(BF16)
            qr_raw = q[:, wn:wn + wr]
            rinv = _seg_rinv(qr_raw, MLA_ROPE) * MLA_SCALE
            if use_rope:
                qr_par = q[:, wn + wr:]
                cos = _tile_lanes(cos_ref[...], wr)
                sin = _tile_lanes(sin_ref[...], wr)
                qr = rinv * (qr_raw * qgr_ref[...] * cos + qr_par * qgrp_ref[...] * sin)
            else:
                qr = rinv * (qr_raw * qgr_ref[...])
            oqr_ref[...] = qr.astype(BF16)

    tile = lambda w: pl.BlockSpec((t, w), lambda i: (i, 0))
    in_specs = [tile(d), _full((1, d)), _mod_spec(st, SC1, d), _mod_spec(st, SH1, d),
                _resident(w_in.shape), _full(q_norm.shape), _full(kv_norm.shape),
                _resident(w_uq.shape), _resident(w_ukv.shape),
                _full(qg_n.shape), _full(qg_r.shape), _full(qg_rp.shape),
                _full(kg_n.shape), _full(kg_r.shape), _full(kg_rp.shape)]
    args = [x, gain, mods, mods, w_in, q_norm, kv_norm, w_uq, w_ukv, qg_n, qg_r, qg_rp, kg_n, kg_r, kg_rp]
    if use_rope:
        tps = st.tiles_per_seg
        in_specs += [pl.BlockSpec((t, LANES), lambda i: (i % tps, 0))] * 2
        args += list(rope)
    out_specs, out_shape = [], []
    if need_q:
        out_specs += [tile(wn), tile(wr)]
        out_shape += [SDS((n, wn), BF16), SDS((n, wr), BF16)]
    out_specs += [tile(wn), tile(MLA_ROPE), tile(h * MLA_V)]
    out_shape += [SDS((n, wn), BF16), SDS((n, MLA_ROPE), BF16), SDS((n, h * MLA_V), BF16)]
    outs = pl.pallas_call(
        kern, grid=(st.tiles,), in_specs=in_specs, out_specs=out_specs, out_shape=out_shape,
        name="mla_proj", compiler_params=_params(("parallel",), 48))(*args)
    if need_q:
        return tuple(outs)
    return (None, None) + tuple(outs)


def _mla_attn_call(batch, lq, tq, q, kv_sets):
    qn, qr = q
    hp = 2
    n_hp = MLA_HEADS // hp
    nq = lq // tq
    nsets = len(kv_sets)

    def kern(*refs):
        qn_ref, qr_ref = refs[:2]
        o_ref = refs[-1]
        sets = [refs[2 + 3 * s:5 + 3 * s] for s in range(nsets)]
        for hh in range(hp):
            qn_h = qn_ref[:, hh * MLA_NOPE:(hh + 1) * MLA_NOPE]
            qr_h = qr_ref[:, hh * MLA_ROPE:(hh + 1) * MLA_ROPE]
            scores = [_nt(qn_h, kn_ref[:, hh * MLA_NOPE:(hh + 1) * MLA_NOPE]) + _nt(qr_h, kr_ref[...])
                      for kn_ref, kr_ref, _ in sets]
            m = functools.reduce(jnp.maximum, [jnp.max(s, axis=-1, keepdims=True) for s in scores])
            ps = [jnp.exp(s - m) for s in scores]
            den = functools.reduce(lambda a, b: a + b, [jnp.sum(p, axis=-1, keepdims=True) for p in ps])
            acc = functools.reduce(lambda a, b: a + b, [
                _dot(p.astype(BF16), v_ref[:, hh * MLA_V:(hh + 1) * MLA_V])
                for p, (_, _, v_ref) in zip(ps, sets)])
            o_ref[:, hh * MLA_V:(hh + 1) * MLA_V] = (acc / den).astype(BF16)

    in_specs = [pl.BlockSpec((tq, hp * MLA_NOPE), lambda b, h, i: (b * nq + i, h)),
                pl.BlockSpec((tq, hp * MLA_ROPE), lambda b, h, i: (b * nq + i, h))]
    args = [qn, qr]
    for kn, kr, v, length in kv_sets:
        in_specs += [pl.BlockSpec((length, hp * MLA_NOPE), lambda b, h, i: (b, h)),
                     pl.BlockSpec((length, MLA_ROPE), lambda b, h, i: (b, 0)),
                     pl.BlockSpec((length, hp * MLA_V), lambda b, h, i: (b, h))]
        args += [kn, kr, v]
    return pl.pallas_call(
        kern, grid=(batch, n_hp, nq), in_specs=in_specs,
        out_specs=pl.BlockSpec((tq, hp * MLA_V), lambda b, h, i: (b * nq + i, h)),
        out_shape=SDS((batch * lq, MLA_HEADS * MLA_V), BF16), name="mla_attn",
        compiler_params=_params(("parallel", "parallel", "arbitrary"), 56))(*args)


def _mla_layer(streams, xs, mods, gain, p, rope, ctx_next):
    st_c, st_x = streams
    xc, xx = xs
    h = MLA_HEADS
    w_in, q_norm, kv_norm, w_uq, w_ukv, q_gain, k_gain, w_out = p
    par = _rope_partner_perm(1)
    kr_cols = w_in[:, MLA_Q_RANK + MLA_KV_RANK:]
    w_in_p = jnp.concatenate([w_in, kr_cols[:, par]], axis=1).astype(BF16)
    wq = w_uq.reshape(MLA_Q_RANK, h, MLA_NOPE + MLA_ROPE)
    wq_n = wq[:, :, :MLA_NOPE].reshape(MLA_Q_RANK, h * MLA_NOPE)
    wq_r = wq[:, :, MLA_NOPE:]
    w_uq_p = jnp.concatenate([wq_n, wq_r.reshape(MLA_Q_RANK, h * MLA_ROPE),
                              wq_r[:, :, par].reshape(MLA_Q_RANK, h * MLA_ROPE)], axis=1).astype(BF16)
    wkv = w_ukv.reshape(MLA_KV_RANK, h, MLA_NOPE + MLA_V)
    w_ukv_p = jnp.concatenate([wkv[:, :, :MLA_NOPE].reshape(MLA_KV_RANK, h * MLA_NOPE),
                               wkv[:, :, MLA_NOPE:].reshape(MLA_KV_RANK, h * MLA_V)], axis=1).astype(BF16)
    qg_r = q_gain[MLA_NOPE:]
    kg_r = k_gain[MLA_NOPE:]
    wts = (w_in_p, q_norm.reshape(1, -1), kv_norm.reshape(1, -1), w_uq_p, w_ukv_p,
           jnp.tile(q_gain[:MLA_NOPE], h).reshape(1, -1), jnp.tile(qg_r, h).reshape(1, -1),
           jnp.tile(qg_r[par], h).reshape(1, -1), jnp.tile(k_gain[:MLA_NOPE], h).reshape(1, -1),
           kg_r.reshape(1, -1), kg_r[par].reshape(1, -1))
    w_out_b = w_out.astype(BF16)
    qn_c, qr_c, kn_c, kr_c, v_c = _mla_proj_call(st_c, xc, mods, gain, wts, None, ctx_next)
    qn_x, qr_x, kn_x, kr_x, v_x = _mla_proj_call(st_x, xx, mods, gain, wts, rope, True)
    b = st_x.batch
    a_x = _mla_attn_call(b, st_x.seg, 512, (qn_x, qr_x),
                         [(kn_c, kr_c, v_c, st_c.seg), (kn_x, kr_x, v_x, st_x.seg)])
    out_x = _outproj_call(st_x, a_x, xx, mods, w_out_b, "mla_out")
    out_c = None
    if ctx_next:
        a_c = _mla_attn_call(b, st_c.seg, st_c.seg, (qn_c, qr_c), [(kn_c, kr_c, v_c, st_c.seg)])
        out_c = _outproj_call(st_c, a_c, xc, mods, w_out_b, "mla_out")
    return [out_c, out_x]


def _gqa_proj_call(st, x, mods, gain, wts, rope, need_q):
    n, d = x.shape
    t = st.tile
    wq, wk = GQA_Q_HEADS * GQA_HEAD_DIM, GQA_KV_HEADS * GQA_HEAD_DIM
    w_in, qg, qgp, kg, kgp = wts
    use_rope = rope is not None

    def kern(*refs):
        refs = list(refs)
        x_ref, g_ref, sc_ref, sh_ref, w_ref, qg_ref, qgp_ref, kg_ref, kgp_ref = refs[:9]
        rest = refs[9:]
        if use_rope:
            cos_ref, sin_ref = rest[:2]
            rest = rest[2:]
        if need_q:
            oq_ref = rest[0]
            rest = rest[1:]
        ok_ref, ov_ref = rest
        hm = _modnorm(x_ref[...], g_ref[...], sc_ref[...], sh_ref[...]).astype(BF16)
        kvp = _dot(hm, w_ref[:, 0:3 * wk])
        k_raw = kvp[:, 0:wk]
        k_rinv = _seg_rinv(k_raw, GQA_HEAD_DIM)
        ov_ref[...] = kvp[:, wk:2 * wk].astype(BF16)
        if use_rope:
            cos_k, sin_k = _tile_lanes(cos_ref[...], wk), _tile_lanes(sin_ref[...], wk)
            k = k_rinv * (k_raw * kg_ref[...] * cos_k + kvp[:, 2 * wk:] * kgp_ref[...] * sin_k)
        else:
            k = k_rinv * (k_raw * kg_ref[...])
        ok_ref[...] = k.astype(BF16)
        if need_q:
            qp = _dot(hm, w_ref[:, 3 * wk:])
            q_raw = qp[:, 0:wq]
            rinv = _seg_rinv(q_raw, GQA_HEAD_DIM) * GQA_SCALE
            if use_rope:
                cos_q, sin_q = _tile_lanes(cos_ref[...], wq), _tile_lanes(sin_ref[...], wq)
                q = rinv * (q_raw * qg_ref[...] * cos_q + qp[:, wq:] * qgp_ref[...] * sin_q)
            else:
                q = rinv * (q_raw * qg_ref[...])
            oq_ref[...] = q.astype(BF16)

    tile = lambda w: pl.BlockSpec((t, w), lambda i: (i, 0))
    in_specs = [tile(d), _full((1, d)), _mod_spec(st, SC1, d), _mod_spec(st, SH1, d),
                _resident(w_in.shape), _full(qg.shape), _full(qgp.shape), _full(kg.shape), _full(kgp.shape)]
    args = [x, gain, mods, mods, w_in, qg, qgp, kg, kgp]
    if use_rope:
        tps = st.tiles_per_seg
        in_specs += [pl.BlockSpec((t, LANES), lambda i: (i % tps, 0))] * 2
        args += list(rope)
    out_specs, out_shape = [], []
    if need_q:
        out_specs.append(tile(wq))
        out_shape.append(SDS((n, wq), BF16))
    out_specs += [tile(wk), tile(wk)]
    out_shape += [SDS((n, wk), BF16)] * 2
    outs = pl.pallas_call(
        kern, grid=(st.tiles,), in_specs=in_specs, out_specs=out_specs, out_shape=out_shape,
        name="gqa_proj", compiler_params=_params(("parallel",), 48))(*args)
    return tuple(outs) if need_q else (None,) + tuple(outs)


def _gqa_window_call(batch, length, lc, q, k, v, kc, vc, sink):
    nb = length // BLOCK
    hd = GQA_HEAD_DIM
    wq, wk = GQA_Q_HEADS * hd, GQA_KV_HEADS * hd
    rows = GQA_GROUP * BLOCK

    def kern(sink_ref, q_ref, kp_ref, k0_ref, kn_ref, vp_ref, v0_ref, vn_ref, kc_ref, vc_ref, o_ref):
        nblk = pl.program_id(1)
        r = lax.broadcasted_iota(jnp.int32, (rows, 3 * BLOCK), 0) & (BLOCK - 1)
        c = lax.broadcasted_iota(jnp.int32, (rows, 3 * BLOCK), 1)
        valid = (c >= r) & (c <= r + 2 * WINDOW)
        valid = valid & ((c >= BLOCK) | (nblk > 0)) & ((c < 2 * BLOCK) | (nblk < nb - 1))
        hrow = lax.broadcasted_iota(jnp.int32, (rows, 1), 0) // BLOCK
        for g in range(GQA_KV_HEADS):
            sl = slice(g * hd, (g + 1) * hd)
            qg = jnp.concatenate([q_ref[:, (g * GQA_GROUP + j) * hd:(g * GQA_GROUP + j + 1) * hd]
                                  for j in range(GQA_GROUP)], axis=0)
            kband = jnp.concatenate([kp_ref[:, sl], k0_ref[:, sl], kn_ref[:, sl]], axis=0)
            vband = jnp.concatenate([vp_ref[:, sl], v0_ref[:, sl], vn_ref[:, sl]], axis=0)
            s_c = _nt(qg, kc_ref[:, sl])
            s_b = jnp.where(valid, _nt(qg, kband), -1e30)
            snk = jnp.zeros((rows, 1), F32)
            for j in range(GQA_GROUP):
                snk = jnp.where(hrow == j, sink_ref[g * GQA_GROUP + j], snk)
            m = jnp.maximum(jnp.maximum(jnp.max(s_c, axis=-1, keepdims=True),
                                        jnp.max(s_b, axis=-1, keepdims=True)), snk)
            p_c = jnp.exp(s_c - m)
            p_b = jnp.exp(s_b - m)
            den = (jnp.sum(p_c, axis=-1, keepdims=True) + jnp.sum(p_b, axis=-1, keepdims=True)
                   + jnp.exp(snk - m))
            o = (_dot(p_c.astype(BF16), vc_ref[:, sl]) + _dot(p_b.astype(BF16), vband)) / den
            o_ref[:, g * GQA_GROUP * hd:(g + 1) * GQA_GROUP * hd] = jnp.concatenate(
                [o[j * BLOCK:(j + 1) * BLOCK, :] for j in range(GQA_GROUP)], axis=-1).astype(BF16)

    blk = lambda f: pl.BlockSpec((BLOCK, wk), f)
    prev_ = lambda b, i: (b * nb + jnp.maximum(i - 1, 0), 0)
    cur_ = lambda b, i: (b * nb + i, 0)
    next_ = lambda b, i: (b * nb + jnp.minimum(i + 1, nb - 1), 0)
    ctx_spec = pl.BlockSpec((lc, wk), lambda b, i: (b, 0))
    return pl.pallas_call(
        kern, grid=(batch, nb),
        in_specs=[pl.BlockSpec(memory_space=pltpu.SMEM),
                  pl.BlockSpec((BLOCK, wq), cur_), blk(prev_), blk(cur_), blk(next_),
                  blk(prev_), blk(cur_), blk(next_), ctx_spec, ctx_spec],
        out_specs=pl.BlockSpec((BLOCK, wq), cur_),
        out_shape=SDS((batch * length, wq), BF16), name="gqa_window",
        compiler_params=_params(("parallel", "arbitrary"), 40))(sink, q, k, k, k, v, v, v, kc, vc)


def _gqa_layer(streams, xs, mods, gain, p, rope, ctx_next):
    assert not ctx_next, "the windowed-GQA mixer is only implemented as the last layer"
    st_c, st_x = streams
    xc, xx = xs
    w_in, q_gain, k_gain, sink, w_out = p
    wq, wk = GQA_Q_HEADS * GQA_HEAD_DIM, GQA_KV_HEADS * GQA_HEAD_DIM
    w_q, w_k, w_v = w_in[:, :wq], w_in[:, wq:wq + wk], w_in[:, wq + wk:]
    w_in_p = jnp.concatenate([w_k, w_v, w_k[:, _rope_partner_perm(GQA_KV_HEADS)],
                              w_q, w_q[:, _rope_partner_perm(GQA_Q_HEADS)]], axis=1).astype(BF16)
    par = _rope_partner_perm(1)
    wts = (w_in_p, jnp.tile(q_gain, GQA_Q_HEADS).reshape(1, -1), jnp.tile(q_gain[par], GQA_Q_HEADS).reshape(1, -1),
           jnp.tile(k_gain, GQA_KV_HEADS).reshape(1, -1), jnp.tile(k_gain[par], GQA_KV_HEADS).reshape(1, -1))
    _, kc, vc = _gqa_proj_call(st_c, xc, mods, gain, wts, None, False)
    q, k, v = _gqa_proj_call(st_x, xx, mods, gain, wts, rope, True)
    a = _gqa_window_call(st_x.batch, st_x.seg, st_c.seg, q, k, v, kc, vc, sink)
    return [None, _outproj_call(st_x, a, xx, mods, w_out.astype(BF16), "gqa_out")]


def _route(logits_t, bias_col):
    scores = _sigmoid(logits_t)
    biased = scores + bias_col
    rows = [biased[e:e + 1, :] for e in range(N_EXPERTS)]
    srow = [scores[e:e + 1, :] for e in range(N_EXPERTS)]
    epg = EXPERTS_PER_GROUP
    gscore = []
    for g in range(N_EXPERT_GROUPS):
        v = rows[g * epg:(g + 1) * epg]
        pair = [v[a] + v[b] for a in range(epg) for b in range(a + 1, epg)]
        gscore.append(functools.reduce(jnp.maximum, pair))
    sel = []
    for g in range(N_EXPERT_GROUPS):
        best = None
        for g2 in range(N_EXPERT_GROUPS):
            if g2 == g:
                continue
            cnd = gscore[g] > gscore[g2] if g2 < g else gscore[g] >= gscore[g2]
            best = cnd if best is None else best & cnd
        for e in range(g * epg, (g + 1) * epg):
            rank = None
            for e2 in range(g * epg, (g + 1) * epg):
                if e2 == e:
                    continue
                ahead = rows[e2] >= rows[e] if e2 < e else rows[e2] > rows[e]
                one = jnp.where(ahead, 1.0, 0.0)
                rank = one if rank is None else rank + one
            sel.append(jnp.where(best & (rank < 2.0), srow[e], 0.0))
    den = functools.reduce(lambda a, b: a + b, sel)
    return jnp.concatenate(sel, axis=0) / den


def _moe_call(st, x, mods, gain, router_wt, router_bias, w_gate, w_up, w_down):
    n, d = x.shape
    t = st.tile
    ne, _, ff = w_gate.shape

    def kern(x_ref, g_ref, sc_ref, sh_ref, g2_ref, rw_ref, rb_ref, wg_ref, wu_ref, wd_ref, o_ref):
        xin = x_ref[...]
        hb = _modnorm(xin, g_ref[...], sc_ref[...], sh_ref[...]).astype(BF16)
        comb_t = _route(_nt(rw_ref[...], hb), rb_ref[...])
        comb = comb_t.T
        acc = jnp.zeros((t, d), F32)
        for e in range(ne):
            gt = _dot(hb, wg_ref[e])
            up = _dot(hb, wu_ref[e])
            act = gt * _sigmoid(gt) * up * comb[:, e:e + 1]
            acc = acc + _dot(act.astype(BF16), wd_ref[e])
        o_ref[...] = xin + g2_ref[...] * acc

    tile_spec = pl.BlockSpec((t, d), lambda i: (i, 0))
    return pl.pallas_call(
        kern, grid=(st.tiles,),
        in_specs=[tile_spec, _full((1, d)), _mod_spec(st, SC2, d), _mod_spec(st, SH2, d),
                  _mod_spec(st, G2, d), _full((ne, d)), _full((ne, 1)),
                  _resident((ne, d, ff)), _resident((ne, d, ff)), _resident((ne, ff, d))],
        out_specs=tile_spec, out_shape=SDS((n, d), F32), name="moe",
        compiler_params=_params(("parallel",), 56))(
            x, gain, mods, mods, mods, router_wt, router_bias, w_gate, w_up, w_down)


def kernel(x, c, ctx, c_ctx, w_ada, b_ada, norm_mix, norm_ffn, fourier_w_out, conv_w_in, conv_w, conv_w_out,
           mla_w_in, mla_q_norm, mla_kv_norm, mla_w_uq, mla_w_ukv, mla_q_gain, mla_k_gain, mla_w_out,
           gqa_w_in, gqa_q_gain, gqa_k_gain, gqa_sink, gqa_w_out, router_w, router_bias,
           moe_w_gate, moe_w_up, moe_w_down):
    b, l, d = x.shape
    lc = ctx.shape[1]
    depth = w_ada.shape[0]
    st_c = Stream(b * lc, lc, min(lc, 256), b, True)
    st_x = Stream(b * l, l, 512, b, False)
    streams = [st_c, st_x]
    r8 = -(-(b + 1) // 8) * 8
    cvec = jnp.concatenate([c, c_ctx[None, :], jnp.zeros((r8 - b - 1, d), F32)], axis=0)
    mods_all = _ada_call(cvec, w_ada, b_ada).reshape(depth, r8, 1, 6 * d)
    rope = _rope_tables(l)
    router_wt = router_w.T.astype(BF16)
    router_b = router_bias.reshape(-1, 1)
    xs = [ctx.reshape(b * lc, d), x.reshape(b * l, d)]
    for i in range(depth):
        kind, j = i % 4, i // 4
        ctx_next = i < depth - 1
        mods = mods_all[i]
        gain = norm_mix[i].reshape(1, d)
        if not (ctx_next or kind >= 2):
            xs[0] = None
        if kind == 0:
            ys = _fourier_layer(streams, xs, mods, gain, fourier_w_out[j])
        elif kind == 1:
            ys = _conv_layer(streams, xs, mods, gain, conv_w_in[j], conv_w[j], conv_w_out[j])
        elif kind == 2:
            ys = _mla_layer(streams, xs, mods, gain,
                            (mla_w_in[j], mla_q_norm[j], mla_kv_norm[j], mla_w_uq[j], mla_w_ukv[j],
                             mla_q_gain[j], mla_k_gain[j], mla_w_out[j]), rope, ctx_next)
        else:
            ys = _gqa_layer(streams, xs, mods, gain,
                            (gqa_w_in[j], gqa_q_gain[j], gqa_k_gain[j], gqa_sink[j], gqa_w_out[j]),
                            rope, ctx_next)
        if not ctx_next:
            ys[0] = None
        gain2 = norm_ffn[i].reshape(1, d)
        wg, wu, wd = moe_w_gate[i].astype(BF16), moe_w_up[i].astype(BF16), moe_w_down[i].astype(BF16)
        xs = [None if y is None else _moe_call(st, y, mods, gain2, router_wt, router_b, wg, wu, wd)
              for st, y in zip(streams, ys)]
    return xs[1].reshape(b, l, d)
```

```python
import functools

import numpy as np
import jax
import jax.numpy as jnp
from jax import lax
from jax.experimental import pallas as pl
from jax.experimental.pallas import tpu as pltpu

F32, BF16 = jnp.float32, jnp.bfloat16
SDS = jax.ShapeDtypeStruct

EPS = 1e-6
GRID_W = 64
ROPE_THETA = 10000.0
N_FOURIER_GROUPS = 4
MLA_HEADS, MLA_Q_RANK, MLA_KV_RANK = 16, 256, 128
MLA_NOPE, MLA_ROPE, MLA_V = 128, 64, 128
MLA_SCALE = (MLA_NOPE + MLA_ROPE) ** -0.5
GQA_Q_HEADS, GQA_KV_HEADS, GQA_HEAD_DIM = 16, 4, 64
GQA_GROUP = GQA_Q_HEADS // GQA_KV_HEADS
GQA_SCALE = GQA_HEAD_DIM ** -0.5
WINDOW = 128
BLOCK = 128
N_EXPERTS, N_EXPERT_GROUPS, EXPERT_FF = 16, 4, 256
EXPERTS_PER_GROUP = N_EXPERTS // N_EXPERT_GROUPS
ROPE_DIM = 64
LOG2E = 1.4426950408889634
MAX_FIXED_SHIFT = 50.0
SHIFT_MARGIN = 1.02

V7X_VMEM_BYTES = 64 * 1024 * 1024
LANES = 128
SH1, SC1, G1, SH2, SC2, G2 = range(6)


def _params(sem, vmem_mb):
    return pltpu.CompilerParams(dimension_semantics=sem, vmem_limit_bytes=vmem_mb * 1024 * 1024)


def _sigmoid(v):
    return 1.0 / (1.0 + jnp.exp(-v))


def _modnorm(x, gain, sc, sh):
    ms = jnp.mean(x * x, axis=-1, keepdims=True)
    return x * lax.rsqrt(ms + EPS) * (gain * (1.0 + sc)) + sh


def _nt(a, b):
    return lax.dot_general(a, b, (((1,), (1,)), ((), ())), preferred_element_type=F32)


def _dot(a, b):
    return jnp.dot(a, b, preferred_element_type=F32)


class Stream:
    def __init__(self, n, seg, tile, batch, is_ctx):
        self.n, self.seg, self.tile, self.batch, self.is_ctx = n, seg, tile, batch, is_ctx
        self.tiles = n // tile
        self.tiles_per_seg = seg // tile

    def mod_row(self, t):
        return self.batch if self.is_ctx else t // self.tiles_per_seg

    def seg_row(self, b):
        return self.batch if self.is_ctx else b


def _mod_spec(st, chunk, d):
    return pl.BlockSpec((None, 1, d), lambda t: (st.mod_row(t), 0, chunk))


def _full(shape):
    nd = len(shape)
    return pl.BlockSpec(shape, lambda *_: (0,) * nd)


def _resident(shape):
    nd = len(shape)
    return pl.BlockSpec(shape, lambda *_: (0,) * nd, pipeline_mode=pl.Buffered(1))


def _ada_call(cvec, w_ada, b_ada):
    depth, d, d6 = w_ada.shape
    r8 = cvec.shape[0]
    tn = d6 // 4

    def kern(c_ref, w_ref, b_ref, o_ref):
        c = c_ref[...]
        s = (c * _sigmoid(c)).astype(BF16)
        o_ref[...] = _dot(s, w_ref[...].astype(BF16)) + b_ref[...]

    return pl.pallas_call(
        kern, grid=(depth, d6 // tn),
        in_specs=[pl.BlockSpec((r8, d), lambda i, j: (0, 0)),
                  pl.BlockSpec((None, d, tn), lambda i, j: (i, 0, j)),
                  pl.BlockSpec((None, 1, tn), lambda i, j: (i, 0, j))],
        out_specs=pl.BlockSpec((None, r8, tn), lambda i, j: (i, 0, j)),
        out_shape=SDS((depth, r8, d6), F32), name="ada",
        compiler_params=_params(("arbitrary", "arbitrary"), 40))(cvec, w_ada, b_ada.reshape(depth, 1, d6))


def _proj_call(st, x, mods, gain, w, name):
    n, d = x.shape
    nout = w.shape[1]
    t = st.tile

    def kern(x_ref, g_ref, sc_ref, sh_ref, w_ref, o_ref):
        h = _modnorm(x_ref[...], g_ref[...], sc_ref[...], sh_ref[...]).astype(BF16)
        o_ref[...] = _dot(h, w_ref[...]).astype(BF16)

    return pl.pallas_call(
        kern, grid=(st.tiles,),
        in_specs=[pl.BlockSpec((t, d), lambda i: (i, 0)), _full((1, d)),
                  _mod_spec(st, SC1, d), _mod_spec(st, SH1, d), _resident((d, nout))],
        out_specs=pl.BlockSpec((t, nout), lambda i: (i, 0)),
        out_shape=SDS((n, nout), BF16), name=name,
        compiler_params=_params(("parallel",), 40))(x, gain, mods, mods, w)


def _outproj_call(st, a, x, mods, w, name):
    n, d = x.shape
    k = a.shape[1]
    t = st.tile

    def kern(a_ref, x_ref, g1_ref, w_ref, o_ref):
        o_ref[...] = x_ref[...] + g1_ref[...] * _dot(a_ref[...], w_ref[...])

    return pl.pallas_call(
        kern, grid=(st.tiles,),
        in_specs=[pl.BlockSpec((t, k), lambda i: (i, 0)), pl.BlockSpec((t, d), lambda i: (i, 0)),
                  _mod_spec(st, G1, d), _resident((k, d))],
        out_specs=pl.BlockSpec((t, d), lambda i: (i, 0)),
        out_shape=SDS((n, d), F32), name=name,
        compiler_params=_params(("parallel",), 40))(a, x, mods, w)


def _dft_tables(length, radix):
    lr = length // radix
    m = np.arange(lr)[None, :, None]
    j = np.arange(radix)[:, None, None]
    nn = np.arange(lr)[None, None, :]
    ang = 2.0 * np.pi * (((radix * m + j) * nn) % length) / length
    e = np.concatenate([np.cos(ang), np.sin(ang)], axis=-1) / np.sqrt(length)
    return jnp.asarray(e, dtype=F32).astype(BF16)


def _group_dft_tables(group):
    k = np.arange(group)
    ang = 2.0 * np.pi * ((k[:, None] * k[None, :]) % group) / group
    return (jnp.asarray(np.cos(ang) / np.sqrt(group), dtype=F32),
            jnp.asarray(np.sin(ang) / np.sqrt(group), dtype=F32))


def _fourier_weight_call(w_out):
    d = w_out.shape[0]
    grp = d // N_FOURIER_GROUPS
    cg, sg = _group_dft_tables(grp)

    def kern(cg_ref, sg_ref, w_ref, o_ref):
        w = w_ref[...]
        o_ref[:, :d] = jnp.dot(cg_ref[...], w, preferred_element_type=F32,
                               precision=lax.Precision.HIGHEST).astype(BF16)
        o_ref[:, d:] = jnp.dot(sg_ref[...], w, preferred_element_type=F32,
                               precision=lax.Precision.HIGHEST).astype(BF16)

    return pl.pallas_call(
        kern, grid=(N_FOURIER_GROUPS,),
        in_specs=[_full((grp, grp)), _full((grp, grp)), pl.BlockSpec((grp, d), lambda g: (g, 0))],
        out_specs=pl.BlockSpec((grp, 2 * d), lambda g: (g, 0)),
        out_shape=SDS((d, 2 * d), BF16), name="fourier_w",
        compiler_params=_params(("arbitrary",), 32))(cg, sg, w_out)


def _radix_terms(radix, j):
    real, imag = [], []
    for q in range(radix):
        k = (j * q * (4 // radix)) % 4 if radix > 1 else 0
        if k == 0:
            real.append((1, 0, q)); imag.append((-1, 1, q))
        elif k == 1:
            real.append((-1, 1, q)); imag.append((-1, 0, q))
        elif k == 2:
            real.append((-1, 0, q)); imag.append((1, 1, q))
        else:
            real.append((1, 1, q)); imag.append((1, 0, q))
    return real, imag


def _seq_dft_call(st, p, x, mods, radix):
    n, d = x.shape
    seg = st.seg
    lr = seg // radix
    e = _dft_tables(seg, radix)
    nb = n // seg
    cblk = 2 * LANES
    ncb = d // cblk

    def kern(pc_ref, ps_ref, x_ref, e_ref, g1_ref, o_ref, v_ref, *z_refs):
        parts = (pc_ref, ps_ref)
        for j in range(radix):
            real, imag = _radix_terms(radix, j)

            def comb(terms):
                acc = None
                for sgn, part, q in terms:
                    v = parts[part][q * lr:(q + 1) * lr, :].astype(F32)
                    if acc is None:
                        acc = v if sgn > 0 else -v
                    else:
                        acc = acc + v if sgn > 0 else acc - v
                return acc

            v_ref[0:lr, :] = comb(real).astype(BF16)
            v_ref[lr:2 * lr, :] = comb(imag).astype(BF16)
            z = _dot(e_ref[j], v_ref[...])
            for k, z_ref in enumerate(z_refs):
                z_ref[pl.ds(j, lr, stride=radix), :] = z[:, k * LANES:(k + 1) * LANES]
        for k, z_ref in enumerate(z_refs):
            sl = slice(k * LANES, (k + 1) * LANES)
            o_ref[:, sl] = x_ref[:, sl] + g1_ref[:, sl] * z_ref[...]

    return pl.pallas_call(
        kern, grid=(nb, ncb),
        in_specs=[pl.BlockSpec((seg, cblk), lambda b, c: (b, c)),
                  pl.BlockSpec((seg, cblk), lambda b, c: (b, ncb + c)),
                  pl.BlockSpec((seg, cblk), lambda b, c: (b, c)),
                  _resident((radix, lr, 2 * lr)),
                  pl.BlockSpec((None, 1, cblk), lambda b, c: (st.seg_row(b), 0, G1 * ncb + c))],
        out_specs=pl.BlockSpec((seg, cblk), lambda b, c: (b, c)),
        out_shape=SDS((n, d), F32),
        scratch_shapes=[pltpu.VMEM((2 * lr, cblk), BF16)] + [pltpu.VMEM((seg, LANES), F32)] * (cblk // LANES),
        name=f"seq_dft_r{radix}",
        compiler_params=_params(("parallel", "parallel"), 40))(p, p, x, e, mods)


def _fourier_layer(streams, xs, mods, gain, w_out):
    wcs = _fourier_weight_call(w_out)
    outs = []
    for st, x in zip(streams, xs):
        if x is None:
            outs.append(None)
            continue
        p = _proj_call(st, x, mods, gain, wcs, "fourier_proj")
        outs.append(_seq_dft_call(st, p, x, mods, 4 if st.seg >= 1024 else 1))
    return outs


def _conv_in_call(st, x, mods, gain, w_in):
    n, d = x.shape
    t = st.tile

    def kern(x_ref, g_ref, sc_ref, sh_ref, w_ref, bg_ref, u_ref):
        h = _modnorm(x_ref[...], g_ref[...], sc_ref[...], sh_ref[...]).astype(BF16)
        bg_ref[...] = _dot(h, w_ref[:, 0:d]).astype(BF16)
        u_ref[...] = (_dot(h, w_ref[:, d:2 * d]) * _dot(h, w_ref[:, 2 * d:3 * d])).astype(BF16)

    return pl.pallas_call(
        kern, grid=(st.tiles,),
        in_specs=[pl.BlockSpec((t, d), lambda i: (i, 0)), _full((1, d)),
                  _mod_spec(st, SC1, d), _mod_spec(st, SH1, d), _resident((d, 3 * d))],
        out_specs=[pl.BlockSpec((t, d), lambda i: (i, 0))] * 2,
        out_shape=[SDS((n, d), BF16)] * 2, name="conv_in",
        compiler_params=_params(("parallel",), 48))(x, gain, mods, mods, w_in)


def _conv_out_call(st, bg, u, x, mods, conv_w, w_out):
    n, d = x.shape
    t = st.tile
    halo = 16
    hb = t // halo
    nhalo = n // halo
    seg = st.seg

    def kern(u_ref, up_ref, un_ref, bg_ref, x_ref, cw_ref, g1_ref, w_ref, o_ref):
        i = pl.program_id(0)
        u = u_ref[...].astype(F32)
        row = lax.broadcasted_iota(jnp.int32, (t, 1), 0)
        pos = (i * t + row) & (seg - 1)
        prev_row = up_ref[...].astype(F32)[halo - 1:halo, :]
        next_row = un_ref[...].astype(F32)[0:1, :]
        um = jnp.where(row == 0, prev_row, pltpu.roll(u, 1, axis=0))
        um = jnp.where(pos == 0, 0.0, um)
        up = jnp.where(row == t - 1, next_row, pltpu.roll(u, t - 1, axis=0))
        up = jnp.where(pos == seg - 1, 0.0, up)
        z = cw_ref[0:1, :] * um + cw_ref[1:2, :] * u + cw_ref[2:3, :] * up
        a = (bg_ref[...].astype(F32) * z).astype(BF16)
        o_ref[...] = x_ref[...] + g1_ref[...] * _dot(a, w_ref[...])

    tile_spec = pl.BlockSpec((t, d), lambda i: (i, 0))
    return pl.pallas_call(
        kern, grid=(st.tiles,),
        in_specs=[tile_spec,
                  pl.BlockSpec((halo, d), lambda i: (jnp.maximum(i * hb - 1, 0), 0)),
                  pl.BlockSpec((halo, d), lambda i: (jnp.minimum((i + 1) * hb, nhalo - 1), 0)),
                  tile_spec, tile_spec, _full((3, d)), _mod_spec(st, G1, d), _resident((d, d))],
        out_specs=tile_spec, out_shape=SDS((n, d), F32), name="conv_out",
        compiler_params=_params(("parallel",), 40))(u, u, u, bg, x, conv_w, mods, w_out)


def _conv_layer(streams, xs, mods, gain, w_in, conv_w, w_out):
    w_in_b, w_out_b = w_in.astype(BF16), w_out.astype(BF16)
    outs = []
    for st, x in zip(streams, xs):
        if x is None:
            outs.append(None)
            continue
        bg, u = _conv_in_call(st, x, mods, gain, w_in_b)
        outs.append(_conv_out_call(st, bg, u, x, mods, conv_w, w_out_b))
    return outs


def _rope_tables(length):
    q = ROPE_DIM // 4
    pos = np.arange(length)
    rc = np.stack([pos // GRID_W, pos % GRID_W], axis=1).astype(np.float32)
    inv = (ROPE_THETA ** (-np.arange(q, dtype=np.float32) / q)).astype(np.float32)
    lane = np.arange(ROPE_DIM)
    ang = rc[:, lane >> 5] * inv[lane & (q - 1)][None, :]
    sign = np.where((lane & q) == 0, -1.0, 1.0)[None, :]
    cos = np.cos(ang.astype(np.float32)).astype(np.float32)
    sin = (np.sin(ang.astype(np.float32)) * sign).astype(np.float32)
    reps = LANES // ROPE_DIM
    return jnp.asarray(np.tile(cos, (1, reps))), jnp.asarray(np.tile(sin, (1, reps)))


def _rope_partner_perm(n_heads):
    lane = np.arange(n_heads * ROPE_DIM)
    return lane ^ (ROPE_DIM // 4)


def _seg_rinv(x, seg):
    t, w = x.shape
    cols = []
    for c in range(w // LANES):
        blk = x[:, c * LANES:(c + 1) * LANES]
        sq = blk * blk
        if seg == LANES:
            cols.append(jnp.broadcast_to(
                lax.rsqrt(jnp.mean(sq, axis=-1, keepdims=True) + EPS), (t, LANES)))
        else:
            lane = lax.broadcasted_iota(jnp.int32, (t, LANES), 1)
            low = lane < seg
            lo = jnp.sum(jnp.where(low, sq, 0.0), axis=-1, keepdims=True) * (1.0 / seg)
            hi = jnp.sum(jnp.where(low, 0.0, sq), axis=-1, keepdims=True) * (1.0 / seg)
            cols.append(jnp.where(low, lax.rsqrt(lo + EPS), lax.rsqrt(hi + EPS)))
    return cols[0] if len(cols) == 1 else jnp.concatenate(cols, axis=-1)


def _tile_lanes(v, w):
    reps = w // LANES
    return v if reps == 1 else jnp.concatenate([v] * reps, axis=-1)


def _mla_proj_call(st, x, mods, gain, wts, rope, need_q):
    n, d = x.shape
    t = st.tile
    h = MLA_HEADS
    wn, wr = h * MLA_NOPE, h * MLA_ROPE
    (w_in, q_norm, kv_norm, w_uq, w_ukv, qg_n, qg_r, qg_rp, kg_n, kg_r, kg_rp) = wts
    use_rope = rope is not None

    def kern(*refs):
        refs = list(refs)
        x_ref, g_ref, sc_ref, sh_ref, win_ref, qn_ref_, kvn_ref_, wuq_ref, wukv_ref = refs[:9]
        qgn_ref, qgr_ref, qgrp_ref, kgn_ref, kgr_ref, kgrp_ref = refs[9:15]
        rest = refs[15:]
        if use_rope:
            cos_ref, sin_ref = rest[:2]
            rest = rest[2:]
        if need_q:
            oqn_ref, oqr_ref = rest[:2]
            rest = rest[2:]
        okn_ref, okr_ref, ov_ref = rest
        hm = _modnorm(x_ref[...], g_ref[...], sc_ref[...], sh_ref[...]).astype(BF16)
        ck = _dot(hm, win_ref[...])
        kr2 = ck[:, MLA_Q_RANK + MLA_KV_RANK:]
        kr_rinv = _seg_rinv(kr2, MLA_ROPE)[:, 0:MLA_ROPE]
        kr_raw = kr2[:, 0:MLA_ROPE] * kgr_ref[...]
        if use_rope:
            kr_par = kr2[:, MLA_ROPE:] * kgrp_ref[...]
            kr = kr_rinv * (kr_raw * cos_ref[:, 0:MLA_ROPE] + kr_par * sin_ref[:, 0:MLA_ROPE])
        else:
            kr = kr_rinv * kr_raw
        okr_ref[...] = kr.astype(BF16)
        ckv = ck[:, MLA_Q_RANK:MLA_Q_RANK + MLA_KV_RANK]
        ckv = (ckv * _seg_rinv(ckv, LANES) * kvn_ref_[...]).astype(BF16)
        kv = _dot(ckv, wukv_ref[...])
        kn = kv[:, 0:wn]
        okn_ref[...] = (kn * _seg_rinv(kn, MLA_NOPE) * kgn_ref[...]).astype(BF16)
        ov_ref[...] = kv[:, wn:].astype(BF16)
        if need_q:
            cq = ck[:, 0:MLA_Q_RANK]
            rq = lax.rsqrt(jnp.mean(cq * cq, axis=-1, keepdims=True) + EPS)
            cq = (cq * rq * qn_ref_[...]).astype(BF16)
            q = _dot(cq, wuq_ref[...])
            qn = q[:, 0:wn]
            oqn_ref[...] = (qn * _seg_rinv(qn, MLA_NOPE) * (qgn_ref[...] * (MLA_SCALE * LOG2E))).astype(BF16)
            qr_raw = q[:, wn:wn + wr]
            rinv = _seg_rinv(qr_raw, MLA_ROPE) * (MLA_SCALE * LOG2E)
            if use_rope:
                qr_par = q[:, wn + wr:]
                cos = _tile_lanes(cos_ref[...], wr)
                sin = _tile_lanes(sin_ref[...], wr)
                qr = rinv * (qr_raw * qgr_ref[...] * cos + qr_par * qgrp_ref[...] * sin)
            else:
                qr = rinv * (qr_raw * qgr_ref[...])
            oqr_ref[...] = qr.astype(BF16)

    tile = lambda w: pl.BlockSpec((t, w), lambda i: (i, 0))
    in_specs = [tile(d), _full((1, d)), _mod_spec(st, SC1, d), _mod_spec(st, SH1, d),
                _resident(w_in.shape), _full(q_norm.shape), _full(kv_norm.shape),
                _resident(w_uq.shape), _resident(w_ukv.shape),
                _full(qg_n.shape), _full(qg_r.shape), _full(qg_rp.shape),
                _full(kg_n.shape), _full(kg_r.shape), _full(kg_rp.shape)]
    args = [x, gain, mods, mods, w_in, q_norm, kv_norm, w_uq, w_ukv, qg_n, qg_r, qg_rp, kg_n, kg_r, kg_rp]
    if use_rope:
        tps = st.tiles_per_seg
        in_specs += [pl.BlockSpec((t, LANES), lambda i: (i % tps, 0))] * 2
        args += list(rope)
    out_specs, out_shape = [], []
    if need_q:
        out_specs += [tile(wn), tile(wr)]
        out_shape += [SDS((n, wn), BF16), SDS((n, wr), BF16)]
    out_specs += [tile(wn), tile(MLA_ROPE), tile(h * MLA_V)]
    out_shape += [SDS((n, wn), BF16), SDS((n, MLA_ROPE), BF16), SDS((n, h * MLA_V), BF16)]
    outs = pl.pallas_call(
        kern, grid=(st.tiles,), in_specs=in_specs, out_specs=out_specs, out_shape=out_shape,
        name="mla_proj", compiler_params=_params(("parallel",), 48))(*args)
    if need_q:
        return tuple(outs)
    return (None, None) + tuple(outs)


def _mla_attn_fast_call(batch, lq, tq, q, kv_sets, shift):
    qn, qr = q
    hp = 2
    n_hp = MLA_HEADS // hp
    nq = lq // tq
    nsets = len(kv_sets)
    lengths = [s[3] for s in kv_sets]
    lk = sum(lengths)
    kw = 2 * LANES
    pad = kw - MLA_NOPE - MLA_ROPE

    def kern(*refs):
        sh_ref, qn_ref, qr_ref = refs[:3]
        sets = [refs[3 + 3 * s:6 + 3 * s] for s in range(nsets)]
        o_ref, kq_ref, vq_ref = refs[-3:]

        @pl.when(pl.program_id(2) == 0)
        def _():
            lane_k = lax.broadcasted_iota(jnp.int32, (lk, pad), 1)
            lane_v = lax.broadcasted_iota(jnp.int32, (lk, kw - MLA_V), 1)
            for hh in range(hp):
                off = 0
                for (kn_ref, kr_ref, v_ref), length in zip(sets, lengths):
                    kq_ref[hh, off:off + length, 0:MLA_NOPE] = kn_ref[:, hh * MLA_NOPE:(hh + 1) * MLA_NOPE]
                    kq_ref[hh, off:off + length, MLA_NOPE:MLA_NOPE + MLA_ROPE] = kr_ref[...]
                    vq_ref[hh, off:off + length, 0:MLA_V] = v_ref[:, hh * MLA_V:(hh + 1) * MLA_V]
                    off += length
                kq_ref[hh, :, MLA_NOPE + MLA_ROPE:] = jnp.where(lane_k == 0, -sh_ref[:, 0:pad], 0.0).astype(BF16)
                vq_ref[hh, :, MLA_V:] = jnp.where(lane_v == 0, 1.0, 0.0).astype(BF16)

        lane_q = lax.broadcasted_iota(jnp.int32, (tq, pad), 1)
        one = jnp.where(lane_q == 0, 1.0, 0.0).astype(BF16)
        for hh in range(hp):
            qc = jnp.concatenate([qn_ref[:, hh * MLA_NOPE:(hh + 1) * MLA_NOPE],
                                  qr_ref[:, hh * MLA_ROPE:(hh + 1) * MLA_ROPE], one], axis=1)
            p = jnp.exp2(_nt(qc, kq_ref[hh])).astype(BF16)
            acc = _dot(p, vq_ref[hh])
            o_ref[:, hh * MLA_V:(hh + 1) * MLA_V] = (acc[:, 0:MLA_V] / acc[:, MLA_V:MLA_V + 1]).astype(BF16)

    in_specs = [_full((1, LANES)),
                pl.BlockSpec((tq, hp * MLA_NOPE), lambda b, h, i: (b * nq + i, h)),
                pl.BlockSpec((tq, hp * MLA_ROPE), lambda b, h, i: (b * nq + i, h))]
    args = [shift, qn, qr]
    for kn, kr, v, length in kv_sets:
        in_specs += [pl.BlockSpec((length, hp * MLA_NOPE), lambda b, h, i: (b, h)),
                     pl.BlockSpec((length, MLA_ROPE), lambda b, h, i: (b, 0)),
                     pl.BlockSpec((length, hp * MLA_V), lambda b, h, i: (b, h))]
        args += [kn, kr, v]
    return pl.pallas_call(
        kern, grid=(batch, n_hp, nq), in_specs=in_specs,
        out_specs=pl.BlockSpec((tq, hp * MLA_V), lambda b, h, i: (b * nq + i, h)),
        out_shape=SDS((batch * lq, MLA_HEADS * MLA_V), BF16),
        scratch_shapes=[pltpu.VMEM((hp, lk, kw), BF16), pltpu.VMEM((hp, lk, kw), BF16)],
        name="mla_attn_fast",
        compiler_params=_params(("parallel", "parallel", "arbitrary"), 56))(*args)


def _mla_attn_call(batch, lq, tq, q, kv_sets):
    qn, qr = q
    hp = 2
    n_hp = MLA_HEADS // hp
    nq = lq // tq
    nsets = len(kv_sets)

    def kern(*refs):
        qn_ref, qr_ref = refs[:2]
        o_ref = refs[-1]
        sets = [refs[2 + 3 * s:5 + 3 * s] for s in range(nsets)]
        for hh in range(hp):
            qn_h = qn_ref[:, hh * MLA_NOPE:(hh + 1) * MLA_NOPE]
            qr_h = qr_ref[:, hh * MLA_ROPE:(hh + 1) * MLA_ROPE]
            scores = [_nt(qn_h, kn_ref[:, hh * MLA_NOPE:(hh + 1) * MLA_NOPE]) + _nt(qr_h, kr_ref[...])
                      for kn_ref, kr_ref, _ in sets]
            m = functools.reduce(jnp.maximum, [jnp.max(s, axis=-1, keepdims=True) for s in scores])
            ps = [jnp.exp2(s - m) for s in scores]
            den = functools.reduce(lambda a, b: a + b, [jnp.sum(p, axis=-1, keepdims=True) for p in ps])
            acc = functools.reduce(lambda a, b: a + b, [
                _dot(p.astype(BF16), v_ref[:, hh * MLA_V:(hh + 1) * MLA_V])
                for p, (_, _, v_ref) in zip(ps, sets)])
            o_ref[:, hh * MLA_V:(hh + 1) * MLA_V] = (acc / den).astype(BF16)

    in_specs = [pl.BlockSpec((tq, hp * MLA_NOPE), lambda b, h, i: (b * nq + i, h)),
                pl.BlockSpec((tq, hp * MLA_ROPE), lambda b, h, i: (b * nq + i, h))]
    args = [qn, qr]
    for kn, kr, v, length in kv_sets:
        in_specs += [pl.BlockSpec((length, hp * MLA_NOPE), lambda b, h, i: (b, h)),
                     pl.BlockSpec((length, MLA_ROPE), lambda b, h, i: (b, 0)),
                     pl.BlockSpec((length, hp * MLA_V), lambda b, h, i: (b, h))]
        args += [kn, kr, v]
    return pl.pallas_call(
        kern, grid=(batch, n_hp, nq), in_specs=in_specs,
        out_specs=pl.BlockSpec((tq, hp * MLA_V), lambda b, h, i: (b * nq + i, h)),
        out_shape=SDS((batch * lq, MLA_HEADS * MLA_V), BF16), name="mla_attn",
        compiler_params=_params(("parallel", "parallel", "arbitrary"), 56))(*args)


def _mla_layer(streams, xs, mods, gain, p, rope, ctx_next):
    st_c, st_x = streams
    xc, xx = xs
    h = MLA_HEADS
    w_in, q_norm, kv_norm, w_uq, w_ukv, q_gain, k_gain, w_out = p
    par = _rope_partner_perm(1)
    kr_cols = w_in[:, MLA_Q_RANK + MLA_KV_RANK:]
    w_in_p = jnp.concatenate([w_in, kr_cols[:, par]], axis=1).astype(BF16)
    wq = w_uq.reshape(MLA_Q_RANK, h, MLA_NOPE + MLA_ROPE)
    wq_n = wq[:, :, :MLA_NOPE].reshape(MLA_Q_RANK, h * MLA_NOPE)
    wq_r = wq[:, :, MLA_NOPE:]
    w_uq_p = jnp.concatenate([wq_n, wq_r.reshape(MLA_Q_RANK, h * MLA_ROPE),
                              wq_r[:, :, par].reshape(MLA_Q_RANK, h * MLA_ROPE)], axis=1).astype(BF16)
    wkv = w_ukv.reshape(MLA_KV_RANK, h, MLA_NOPE + MLA_V)
    w_ukv_p = jnp.concatenate([wkv[:, :, :MLA_NOPE].reshape(MLA_KV_RANK, h * MLA_NOPE),
                               wkv[:, :, MLA_NOPE:].reshape(MLA_KV_RANK, h * MLA_V)], axis=1).astype(BF16)
    qg_r = q_gain[MLA_NOPE:]
    kg_r = k_gain[MLA_NOPE:]
    wts = (w_in_p, q_norm.reshape(1, -1), kv_norm.reshape(1, -1), w_uq_p, w_ukv_p,
           jnp.tile(q_gain[:MLA_NOPE], h).reshape(1, -1), jnp.tile(qg_r, h).reshape(1, -1),
           jnp.tile(qg_r[par], h).reshape(1, -1), jnp.tile(k_gain[:MLA_NOPE], h).reshape(1, -1),
           kg_r.reshape(1, -1), kg_r[par].reshape(1, -1))
    w_out_b = w_out.astype(BF16)
    qn_c, qr_c, kn_c, kr_c, v_c = _mla_proj_call(st_c, xc, mods, gain, wts, None, ctx_next)
    qn_x, qr_x, kn_x, kr_x, v_x = _mla_proj_call(st_x, xx, mods, gain, wts, rope, True)
    b = st_x.batch
    amax = lambda g: jnp.max(jnp.abs(g))
    bound = LOG2E * MLA_SCALE * (MLA_NOPE * amax(q_gain[:MLA_NOPE]) * amax(k_gain[:MLA_NOPE])
                                 + MLA_ROPE * amax(qg_r) * amax(kg_r))
    shift = bound * SHIFT_MARGIN + 1.0
    shift_row = jnp.full((1, LANES), shift, F32)

    def attend(lq, tq, q, kv_sets):
        return lax.cond(shift <= MAX_FIXED_SHIFT,
                        lambda: _mla_attn_fast_call(b, lq, tq, q, kv_sets, shift_row),
                        lambda: _mla_attn_call(b, lq, tq, q, kv_sets))

    a_x = attend(st_x.seg, 512, (qn_x, qr_x), [(kn_c, kr_c, v_c, st_c.seg), (kn_x, kr_x, v_x, st_x.seg)])
    out_x = _outproj_call(st_x, a_x, xx, mods, w_out_b, "mla_out")
    out_c = None
    if ctx_next:
        a_c = attend(st_c.seg, st_c.seg, (qn_c, qr_c), [(kn_c, kr_c, v_c, st_c.seg)])
        out_c = _outproj_call(st_c, a_c, xc, mods, w_out_b, "mla_out")
    return [out_c, out_x]


def _gqa_proj_call(st, x, mods, gain, wts, rope, need_q):
    n, d = x.shape
    t = st.tile
    wq, wk = GQA_Q_HEADS * GQA_HEAD_DIM, GQA_KV_HEADS * GQA_HEAD_DIM
    w_in, qg, qgp, kg, kgp = wts
    use_rope = rope is not None

    def kern(*refs):
        refs = list(refs)
        x_ref, g_ref, sc_ref, sh_ref, w_ref, qg_ref, qgp_ref, kg_ref, kgp_ref = refs[:9]
        rest = refs[9:]
        if use_rope:
            cos_ref, sin_ref = rest[:2]
            rest = rest[2:]
        if need_q:
            oq_ref = rest[0]
            rest = rest[1:]
        ok_ref, ov_ref = rest
        hm = _modnorm(x_ref[...], g_ref[...], sc_ref[...], sh_ref[...]).astype(BF16)
        kvp = _dot(hm, w_ref[:, 0:3 * wk])
        k_raw = kvp[:, 0:wk]
        k_rinv = _seg_rinv(k_raw, GQA_HEAD_DIM)
        ov_ref[...] = kvp[:, wk:2 * wk].astype(BF16)
        if use_rope:
            cos_k, sin_k = _tile_lanes(cos_ref[...], wk), _tile_lanes(sin_ref[...], wk)
            k = k_rinv * (k_raw * kg_ref[...] * cos_k + kvp[:, 2 * wk:] * kgp_ref[...] * sin_k)
        else:
            k = k_rinv * (k_raw * kg_ref[...])
        ok_ref[...] = k.astype(BF16)
        if need_q:
            qp = _dot(hm, w_ref[:, 3 * wk:])
            q_raw = qp[:, 0:wq]
            rinv = _seg_rinv(q_raw, GQA_HEAD_DIM) * (GQA_SCALE * LOG2E)
            if use_rope:
                cos_q, sin_q = _tile_lanes(cos_ref[...], wq), _tile_lanes(sin_ref[...], wq)
                q = rinv * (q_raw * qg_ref[...] * cos_q + qp[:, wq:] * qgp_ref[...] * sin_q)
            else:
                q = rinv * (q_raw * qg_ref[...])
            oq_ref[...] = q.astype(BF16)

    tile = lambda w: pl.BlockSpec((t, w), lambda i: (i, 0))
    in_specs = [tile(d), _full((1, d)), _mod_spec(st, SC1, d), _mod_spec(st, SH1, d),
                _resident(w_in.shape), _full(qg.shape), _full(qgp.shape), _full(kg.shape), _full(kgp.shape)]
    args = [x, gain, mods, mods, w_in, qg, qgp, kg, kgp]
    if use_rope:
        tps = st.tiles_per_seg
        in_specs += [pl.BlockSpec((t, LANES), lambda i: (i % tps, 0))] * 2
        args += list(rope)
    out_specs, out_shape = [], []
    if need_q:
        out_specs.append(tile(wq))
        out_shape.append(SDS((n, wq), BF16))
    out_specs += [tile(wk), tile(wk)]
    out_shape += [SDS((n, wk), BF16)] * 2
    outs = pl.pallas_call(
        kern, grid=(st.tiles,), in_specs=in_specs, out_specs=out_specs, out_shape=out_shape,
        name="gqa_proj", compiler_params=_params(("parallel",), 48))(*args)
    return tuple(outs) if need_q else (None,) + tuple(outs)


def _gqa_window_call(batch, length, lc, q, k, v, kc, vc, sink):
    nb = length // BLOCK
    hd = GQA_HEAD_DIM
    wq, wk = GQA_Q_HEADS * hd, GQA_KV_HEADS * hd
    rows = GQA_GROUP * BLOCK

    def kern(sink_ref, q_ref, kp_ref, k0_ref, kn_ref, vp_ref, v0_ref, vn_ref, kc_ref, vc_ref, o_ref):
        nblk = pl.program_id(1)
        r = lax.broadcasted_iota(jnp.int32, (rows, 3 * BLOCK), 0) & (BLOCK - 1)
        c = lax.broadcasted_iota(jnp.int32, (rows, 3 * BLOCK), 1)
        valid = (c >= r + BLOCK - WINDOW) & (c <= r + BLOCK + WINDOW)
        valid = valid & ((c >= BLOCK) | (nblk > 0)) & ((c < 2 * BLOCK) | (nblk < nb - 1))
        hrow = lax.broadcasted_iota(jnp.int32, (rows, 1), 0) // BLOCK
        for g in range(GQA_KV_HEADS):
            sl = slice(g * hd, (g + 1) * hd)
            qg = jnp.concatenate([q_ref[:, (g * GQA_GROUP + j) * hd:(g * GQA_GROUP + j + 1) * hd]
                                  for j in range(GQA_GROUP)], axis=0)
            kband = jnp.concatenate([kp_ref[:, sl], k0_ref[:, sl], kn_ref[:, sl]], axis=0)
            vband = jnp.concatenate([vp_ref[:, sl], v0_ref[:, sl], vn_ref[:, sl]], axis=0)
            s_c = _nt(qg, kc_ref[:, sl])
            s_b = jnp.where(valid, _nt(qg, kband), -1e30)
            snk = jnp.zeros((rows, 1), F32)
            for j in range(GQA_GROUP):
                snk = jnp.where(hrow == j, sink_ref[g * GQA_GROUP + j], snk)
            m = jnp.maximum(jnp.maximum(jnp.max(s_c, axis=-1, keepdims=True),
                                        jnp.max(s_b, axis=-1, keepdims=True)), snk)
            p_c = jnp.exp2(s_c - m)
            p_b = jnp.exp2(s_b - m)
            den = (jnp.sum(p_c, axis=-1, keepdims=True) + jnp.sum(p_b, axis=-1, keepdims=True)
                   + jnp.exp2(snk - m))
            o = (_dot(p_c.astype(BF16), vc_ref[:, sl]) + _dot(p_b.astype(BF16), vband)) / den
            o_ref[:, g * GQA_GROUP * hd:(g + 1) * GQA_GROUP * hd] = jnp.concatenate(
                [o[j * BLOCK:(j + 1) * BLOCK, :] for j in range(GQA_GROUP)], axis=-1).astype(BF16)

    blk = lambda f: pl.BlockSpec((BLOCK, wk), f)
    prev_ = lambda b, i: (b * nb + jnp.maximum(i - 1, 0), 0)
    cur_ = lambda b, i: (b * nb + i, 0)
    next_ = lambda b, i: (b * nb + jnp.minimum(i + 1, nb - 1), 0)
    ctx_spec = pl.BlockSpec((lc, wk), lambda b, i: (b, 0))
    return pl.pallas_call(
        kern, grid=(batch, nb),
        in_specs=[pl.BlockSpec(memory_space=pltpu.SMEM),
                  pl.BlockSpec((BLOCK, wq), cur_), blk(prev_), blk(cur_), blk(next_),
                  blk(prev_), blk(cur_), blk(next_), ctx_spec, ctx_spec],
        out_specs=pl.BlockSpec((BLOCK, wq), cur_),
        out_shape=SDS((batch * length, wq), BF16), name="gqa_window",
        compiler_params=_params(("parallel", "arbitrary"), 40))(sink, q, k, k, k, v, v, v, kc, vc)


def _gqa_window_fast_call(batch, length, lc, q, k, v, kc, vc, sink2, shift):
    nb = length // BLOCK
    hd = GQA_HEAD_DIM
    wq, wk = GQA_Q_HEADS * hd, GQA_KV_HEADS * hd
    rows = GQA_GROUP * BLOCK

    def kern(sink_ref, shift_ref, q_ref, kp_ref, k0_ref, kn_ref, vp_ref, v0_ref, vn_ref, kc_ref, vc_ref, o_ref):
        nblk = pl.program_id(1)
        r = lax.broadcasted_iota(jnp.int32, (rows, 3 * BLOCK), 0) & (BLOCK - 1)
        c = lax.broadcasted_iota(jnp.int32, (rows, 3 * BLOCK), 1)
        valid = (c >= r + BLOCK - WINDOW) & (c <= r + BLOCK + WINDOW)
        valid = valid & ((c >= BLOCK) | (nblk > 0)) & ((c < 2 * BLOCK) | (nblk < nb - 1))
        hrow = lax.broadcasted_iota(jnp.int32, (rows, 1), 0) // BLOCK
        lane0_q = lax.broadcasted_iota(jnp.int32, (BLOCK, hd), 1) == 0

        def with_one(x):
            lane0 = lax.broadcasted_iota(jnp.int32, (x.shape[0], hd), 1) == 0
            return jnp.concatenate([x, jnp.where(lane0, 1.0, 0.0).astype(BF16)], axis=1)

        for g in range(GQA_KV_HEADS):
            sl = slice(g * hd, (g + 1) * hd)
            heads = [g * GQA_GROUP + j for j in range(GQA_GROUP)]
            qg = jnp.concatenate(
                [jnp.concatenate([q_ref[:, h * hd:(h + 1) * hd],
                                  jnp.where(lane0_q, -shift_ref[h], 0.0).astype(BF16)], axis=1)
                 for h in heads], axis=0)
            kband = with_one(jnp.concatenate([kp_ref[:, sl], k0_ref[:, sl], kn_ref[:, sl]], axis=0))
            vband = with_one(jnp.concatenate([vp_ref[:, sl], v0_ref[:, sl], vn_ref[:, sl]], axis=0))
            p_c = jnp.exp2(_nt(qg, with_one(kc_ref[:, sl]))).astype(BF16)
            p_b = jnp.where(valid, jnp.exp2(_nt(qg, kband)), 0.0).astype(BF16)
            acc = _dot(p_c, with_one(vc_ref[:, sl])) + _dot(p_b, vband)
            snk = jnp.zeros((rows, 1), F32)
            for j, h in enumerate(heads):
                snk = jnp.where(hrow == j, sink_ref[h] - shift_ref[h], snk)
            o = acc[:, 0:hd] / (acc[:, hd:hd + 1] + jnp.exp2(snk))
            o_ref[:, g * GQA_GROUP * hd:(g + 1) * GQA_GROUP * hd] = jnp.concatenate(
                [o[j * BLOCK:(j + 1) * BLOCK, :] for j in range(GQA_GROUP)], axis=-1).astype(BF16)

    blk = lambda f: pl.BlockSpec((BLOCK, wk), f)
    prev_ = lambda b, i: (b * nb + jnp.maximum(i - 1, 0), 0)
    cur_ = lambda b, i: (b * nb + i, 0)
    next_ = lambda b, i: (b * nb + jnp.minimum(i + 1, nb - 1), 0)
    ctx_spec = pl.BlockSpec((lc, wk), lambda b, i: (b, 0))
    smem = pl.BlockSpec(memory_space=pltpu.SMEM)
    return pl.pallas_call(
        kern, grid=(batch, nb),
        in_specs=[smem, smem, pl.BlockSpec((BLOCK, wq), cur_), blk(prev_), blk(cur_), blk(next_),
                  blk(prev_), blk(cur_), blk(next_), ctx_spec, ctx_spec],
        out_specs=pl.BlockSpec((BLOCK, wq), cur_),
        out_shape=SDS((batch * length, wq), BF16), name="gqa_window_fast",
        compiler_params=_params(("parallel", "arbitrary"), 40))(sink2, shift, q, k, k, k, v, v, v, kc, vc)


def _gqa_layer(streams, xs, mods, gain, p, rope, ctx_next):
    assert not ctx_next, "the windowed-GQA mixer is only implemented as the last layer"
    st_c, st_x = streams
    xc, xx = xs
    w_in, q_gain, k_gain, sink, w_out = p
    wq, wk = GQA_Q_HEADS * GQA_HEAD_DIM, GQA_KV_HEADS * GQA_HEAD_DIM
    w_q, w_k, w_v = w_in[:, :wq], w_in[:, wq:wq + wk], w_in[:, wq + wk:]
    w_in_p = jnp.concatenate([w_k, w_v, w_k[:, _rope_partner_perm(GQA_KV_HEADS)],
                              w_q, w_q[:, _rope_partner_perm(GQA_Q_HEADS)]], axis=1).astype(BF16)
    par = _rope_partner_perm(1)
    wts = (w_in_p, jnp.tile(q_gain, GQA_Q_HEADS).reshape(1, -1), jnp.tile(q_gain[par], GQA_Q_HEADS).reshape(1, -1),
           jnp.tile(k_gain, GQA_KV_HEADS).reshape(1, -1), jnp.tile(k_gain[par], GQA_KV_HEADS).reshape(1, -1))
    _, kc, vc = _gqa_proj_call(st_c, xc, mods, gain, wts, None, False)
    q, k, v = _gqa_proj_call(st_x, xx, mods, gain, wts, rope, True)
    sink2 = sink * LOG2E
    bound = LOG2E * GQA_SCALE * GQA_HEAD_DIM * jnp.max(jnp.abs(q_gain)) * jnp.max(jnp.abs(k_gain))
    shift = jnp.maximum(bound * SHIFT_MARGIN + 1.0, sink2).astype(BF16).astype(F32)
    args = (st_x.batch, st_x.seg, st_c.seg, q, k, v, kc, vc, sink2)
    a = lax.cond(jnp.max(shift) <= MAX_FIXED_SHIFT,
                 lambda: _gqa_window_fast_call(*args, shift),
                 lambda: _gqa_window_call(*args))
    return [None, _outproj_call(st_x, a, xx, mods, w_out.astype(BF16), "gqa_out")]


def _cast_experts_call(w, layer):
    _, ne, a, b = w.shape
    eb = 4

    def kern(w_ref, o_ref):
        o_ref[...] = w_ref[...].astype(BF16)

    return pl.pallas_call(
        kern, grid=(ne // eb,),
        in_specs=[pl.BlockSpec((None, eb, a, b), lambda i: (layer, i, 0, 0))],
        out_specs=pl.BlockSpec((eb, a, b), lambda i: (i, 0, 0)),
        out_shape=SDS((ne, a, b), BF16), name="cast_experts",
        compiler_params=_params(("parallel",), 32))(w)


def _route(logits_t, bias_col):
    scores = _sigmoid(logits_t)
    biased = scores + bias_col
    rows = [biased[e:e + 1, :] for e in range(N_EXPERTS)]
    srow = [scores[e:e + 1, :] for e in range(N_EXPERTS)]
    epg = EXPERTS_PER_GROUP
    gscore = []
    for g in range(N_EXPERT_GROUPS):
        v = rows[g * epg:(g + 1) * epg]
        pair = [v[a] + v[b] for a in range(epg) for b in range(a + 1, epg)]
        gscore.append(functools.reduce(jnp.maximum, pair))
    sel = []
    for g in range(N_EXPERT_GROUPS):
        best = None
        for g2 in range(N_EXPERT_GROUPS):
            if g2 == g:
                continue
            cnd = gscore[g] > gscore[g2] if g2 < g else gscore[g] >= gscore[g2]
            best = cnd if best is None else best & cnd
        for e in range(g * epg, (g + 1) * epg):
            rank = None
            for e2 in range(g * epg, (g + 1) * epg):
                if e2 == e:
                    continue
                ahead = rows[e2] >= rows[e] if e2 < e else rows[e2] > rows[e]
                one = jnp.where(ahead, 1.0, 0.0)
                rank = one if rank is None else rank + one
            sel.append(jnp.where(best & (rank < 2.0), srow[e], 0.0))
    den = functools.reduce(lambda a, b: a + b, sel)
    return jnp.concatenate(sel, axis=0) / den


def _moe_call(st, x, mods, gain, router_wt, router_bias, w_gate, w_up, w_down):
    n, d = x.shape
    t = st.tile
    ne, _, ff = w_gate.shape

    def kern(x_ref, g_ref, sc_ref, sh_ref, g2_ref, rw_ref, rb_ref, wg_ref, wu_ref, wd_ref, o_ref):
        xin = x_ref[...]
        hb = _modnorm(xin, g_ref[...], sc_ref[...], sh_ref[...]).astype(BF16)
        comb_t = _route(_nt(rw_ref[...], hb), rb_ref[...])
        comb = comb_t.T
        acc = jnp.zeros((t, d), F32)
        for e in range(ne):
            gt = _dot(hb, wg_ref[e])
            up = _dot(hb, wu_ref[e])
            act = gt * _sigmoid(gt) * up * comb[:, e:e + 1]
            acc = acc + _dot(act.astype(BF16), wd_ref[e])
        o_ref[...] = xin + g2_ref[...] * acc

    tile_spec = pl.BlockSpec((t, d), lambda i: (i, 0))
    return pl.pallas_call(
        kern, grid=(st.tiles,),
        in_specs=[tile_spec, _full((1, d)), _mod_spec(st, SC2, d), _mod_spec(st, SH2, d),
                  _mod_spec(st, G2, d), _full((ne, d)), _full((ne, 1)),
                  _resident((ne, d, ff)), _resident((ne, d, ff)), _resident((ne, ff, d))],
        out_specs=tile_spec, out_shape=SDS((n, d), F32), name="moe",
        compiler_params=_params(("parallel",), 56))(
            x, gain, mods, mods, mods, router_wt, router_bias, w_gate, w_up, w_down)


def kernel(x, c, ctx, c_ctx, w_ada, b_ada, norm_mix, norm_ffn, fourier_w_out, conv_w_in, conv_w, conv_w_out,
           mla_w_in, mla_q_norm, mla_kv_norm, mla_w_uq, mla_w_ukv, mla_q_gain, mla_k_gain, mla_w_out,
           gqa_w_in, gqa_q_gain, gqa_k_gain, gqa_sink, gqa_w_out, router_w, router_bias,
           moe_w_gate, moe_w_up, moe_w_down):
    b, l, d = x.shape
    lc = ctx.shape[1]
    depth = w_ada.shape[0]
    st_c = Stream(b * lc, lc, min(lc, 256), b, True)
    st_x = Stream(b * l, l, 512, b, False)
    streams = [st_c, st_x]
    r8 = -(-(b + 1) // 8) * 8
    cvec = jnp.concatenate([c, c_ctx[None, :], jnp.zeros((r8 - b - 1, d), F32)], axis=0)
    mods_all = _ada_call(cvec, w_ada, b_ada).reshape(depth, r8, 1, 6 * d)
    rope = _rope_tables(l)
    router_wt = router_w.T.astype(BF16)
    router_b = router_bias.reshape(-1, 1)
    xs = [ctx.reshape(b * lc, d), x.reshape(b * l, d)]
    for i in range(depth):
        kind, j = i % 4, i // 4
        ctx_next = i < depth - 1
        mods = mods_all[i]
        gain = norm_mix[i].reshape(1, d)
        if not (ctx_next or kind >= 2):
            xs[0] = None
        if kind == 0:
            ys = _fourier_layer(streams, xs, mods, gain, fourier_w_out[j])
        elif kind == 1:
            ys = _conv_layer(streams, xs, mods, gain, conv_w_in[j], conv_w[j], conv_w_out[j])
        elif kind == 2:
            ys = _mla_layer(streams, xs, mods, gain,
                            (mla_w_in[j], mla_q_norm[j], mla_kv_norm[j], mla_w_uq[j], mla_w_ukv[j],
                             mla_q_gain[j], mla_k_gain[j], mla_w_out[j]), rope, ctx_next)
        else:
            ys = _gqa_layer(streams, xs, mods, gain,
                            (gqa_w_in[j], gqa_q_gain[j], gqa_k_gain[j], gqa_sink[j], gqa_w_out[j]),
                            rope, ctx_next)
        if not ctx_next:
            ys[0] = None
        gain2 = norm_ffn[i].reshape(1, d)
        wg, wu, wd = (_cast_experts_call(w, i) for w in (moe_w_gate, moe_w_up, moe_w_down))
        xs = [None if y is None else _moe_call(st, y, mods, gain2, router_wt, router_b, wg, wu, wd)
              for st, y in zip(streams, ys)]
    return xs[1].reshape(b, l, d)
```

```python
import functools

import numpy as np
import jax
import jax.numpy as jnp
from jax import lax
from jax.experimental import pallas as pl
from jax.experimental.pallas import tpu as pltpu

F32, BF16 = jnp.float32, jnp.bfloat16
SDS = jax.ShapeDtypeStruct

EPS = 1e-6
GRID_W = 64
ROPE_THETA = 10000.0
N_FOURIER_GROUPS = 4
MLA_HEADS, MLA_Q_RANK, MLA_KV_RANK = 16, 256, 128
MLA_NOPE, MLA_ROPE, MLA_V = 128, 64, 128
MLA_SCALE = (MLA_NOPE + MLA_ROPE) ** -0.5
GQA_Q_HEADS, GQA_KV_HEADS, GQA_HEAD_DIM = 16, 4, 64
GQA_GROUP = GQA_Q_HEADS // GQA_KV_HEADS
GQA_SCALE = GQA_HEAD_DIM ** -0.5
WINDOW = 128
BLOCK = 128
N_EXPERTS, N_EXPERT_GROUPS, EXPERT_FF = 16, 4, 256
EXPERTS_PER_GROUP = N_EXPERTS // N_EXPERT_GROUPS
ROPE_DIM = 64
LOG2E = 1.4426950408889634
MAX_FIXED_SHIFT = 50.0
SHIFT_MARGIN = 1.02

V7X_VMEM_BYTES = 64 * 1024 * 1024
LANES = 128
SH1, SC1, G1, SH2, SC2, G2 = range(6)


def _params(sem, vmem_mb):
    return pltpu.CompilerParams(dimension_semantics=sem, vmem_limit_bytes=vmem_mb * 1024 * 1024)


def _sigmoid(v):
    return 1.0 / (1.0 + jnp.exp(-v))


def _modnorm(x, gain, sc, sh):
    ms = jnp.mean(x * x, axis=-1, keepdims=True)
    return x * lax.rsqrt(ms + EPS) * (gain * (1.0 + sc)) + sh


def _nt(a, b):
    return lax.dot_general(a, b, (((1,), (1,)), ((), ())), preferred_element_type=F32)


def _dot(a, b):
    return jnp.dot(a, b, preferred_element_type=F32)


class Stream:
    def __init__(self, n, seg, tile, batch, is_ctx):
        self.n, self.seg, self.tile, self.batch, self.is_ctx = n, seg, tile, batch, is_ctx
        self.tiles = n // tile
        self.tiles_per_seg = seg // tile

    def mod_row(self, t):
        return self.batch if self.is_ctx else t // self.tiles_per_seg

    def seg_row(self, b):
        return self.batch if self.is_ctx else b


def _mod_spec(st, chunk, d):
    return pl.BlockSpec((None, 1, d), lambda t: (st.mod_row(t), 0, chunk))


def _full(shape):
    nd = len(shape)
    return pl.BlockSpec(shape, lambda *_: (0,) * nd)


def _resident(shape):
    nd = len(shape)
    return pl.BlockSpec(shape, lambda *_: (0,) * nd, pipeline_mode=pl.Buffered(1))


def _ada_call(cvec, w_ada, b_ada):
    depth, d, d6 = w_ada.shape
    r8 = cvec.shape[0]
    tn = d6 // 4

    def kern(c_ref, w_ref, b_ref, o_ref):
        c = c_ref[...]
        s = (c * _sigmoid(c)).astype(BF16)
        o_ref[...] = _dot(s, w_ref[...].astype(BF16)) + b_ref[...]

    return pl.pallas_call(
        kern, grid=(depth, d6 // tn),
        in_specs=[pl.BlockSpec((r8, d), lambda i, j: (0, 0)),
                  pl.BlockSpec((None, d, tn), lambda i, j: (i, 0, j)),
                  pl.BlockSpec((None, 1, tn), lambda i, j: (i, 0, j))],
        out_specs=pl.BlockSpec((None, r8, tn), lambda i, j: (i, 0, j)),
        out_shape=SDS((depth, r8, d6), F32), name="ada",
        compiler_params=_params(("arbitrary", "arbitrary"), 40))(cvec, w_ada, b_ada.reshape(depth, 1, d6))


def _proj_call(st, x, mods, gain, w, name):
    n, d = x.shape
    nout = w.shape[1]
    t = st.tile

    def kern(x_ref, g_ref, sc_ref, sh_ref, w_ref, o_ref):
        h = _modnorm(x_ref[...], g_ref[...], sc_ref[...], sh_ref[...]).astype(BF16)
        o_ref[...] = _dot(h, w_ref[...]).astype(BF16)

    return pl.pallas_call(
        kern, grid=(st.tiles,),
        in_specs=[pl.BlockSpec((t, d), lambda i: (i, 0)), _full((1, d)),
                  _mod_spec(st, SC1, d), _mod_spec(st, SH1, d), _resident((d, nout))],
        out_specs=pl.BlockSpec((t, nout), lambda i: (i, 0)),
        out_shape=SDS((n, nout), BF16), name=name,
        compiler_params=_params(("parallel",), 40))(x, gain, mods, mods, w)


def _outproj_call(st, a, x, mods, w, name):
    n, d = x.shape
    k = a.shape[1]
    t = st.tile

    def kern(a_ref, x_ref, g1_ref, w_ref, o_ref):
        o_ref[...] = x_ref[...] + g1_ref[...] * _dot(a_ref[...], w_ref[...])

    return pl.pallas_call(
        kern, grid=(st.tiles,),
        in_specs=[pl.BlockSpec((t, k), lambda i: (i, 0)), pl.BlockSpec((t, d), lambda i: (i, 0)),
                  _mod_spec(st, G1, d), _resident((k, d))],
        out_specs=pl.BlockSpec((t, d), lambda i: (i, 0)),
        out_shape=SDS((n, d), F32), name=name,
        compiler_params=_params(("parallel",), 40))(a, x, mods, w)


def _dft_tables(length, radix):
    lr = length // radix
    m = np.arange(lr)[None, :, None]
    j = np.arange(radix)[:, None, None]
    nn = np.arange(lr)[None, None, :]
    ang = 2.0 * np.pi * (((radix * m + j) * nn) % length) / length
    e = np.concatenate([np.cos(ang), np.sin(ang)], axis=-1) / np.sqrt(length)
    return jnp.asarray(e, dtype=F32).astype(BF16)


def _group_dft_tables(group):
    k = np.arange(group)
    ang = 2.0 * np.pi * ((k[:, None] * k[None, :]) % group) / group
    return (jnp.asarray(np.cos(ang) / np.sqrt(group), dtype=F32),
            jnp.asarray(np.sin(ang) / np.sqrt(group), dtype=F32))


def _fourier_weight_call(w_out):
    d = w_out.shape[0]
    grp = d // N_FOURIER_GROUPS
    cg, sg = _group_dft_tables(grp)

    def kern(cg_ref, sg_ref, w_ref, o_ref):
        w = w_ref[...]
        o_ref[:, :d] = jnp.dot(cg_ref[...], w, preferred_element_type=F32,
                               precision=lax.Precision.HIGHEST).astype(BF16)
        o_ref[:, d:] = jnp.dot(sg_ref[...], w, preferred_element_type=F32,
                               precision=lax.Precision.HIGHEST).astype(BF16)

    return pl.pallas_call(
        kern, grid=(N_FOURIER_GROUPS,),
        in_specs=[_full((grp, grp)), _full((grp, grp)), pl.BlockSpec((grp, d), lambda g: (g, 0))],
        out_specs=pl.BlockSpec((grp, 2 * d), lambda g: (g, 0)),
        out_shape=SDS((d, 2 * d), BF16), name="fourier_w",
        compiler_params=_params(("arbitrary",), 32))(cg, sg, w_out)


def _radix_terms(radix, j):
    real, imag = [], []
    for q in range(radix):
        k = (j * q * (4 // radix)) % 4 if radix > 1 else 0
        if k == 0:
            real.append((1, 0, q)); imag.append((-1, 1, q))
        elif k == 1:
            real.append((-1, 1, q)); imag.append((-1, 0, q))
        elif k == 2:
            real.append((-1, 0, q)); imag.append((1, 1, q))
        else:
            real.append((1, 1, q)); imag.append((1, 0, q))
    return real, imag


def _seq_dft_call(st, p, x, mods, radix):
    n, d = x.shape
    seg = st.seg
    lr = seg // radix
    e = _dft_tables(seg, radix)
    nb = n // seg
    cblk = 2 * LANES
    ncb = d // cblk

    def kern(pc_ref, ps_ref, x_ref, e_ref, g1_ref, o_ref, v_ref, *z_refs):
        parts = (pc_ref, ps_ref)
        for j in range(radix):
            real, imag = _radix_terms(radix, j)

            def comb(terms):
                acc = None
                for sgn, part, q in terms:
                    v = parts[part][q * lr:(q + 1) * lr, :].astype(F32)
                    if acc is None:
                        acc = v if sgn > 0 else -v
                    else:
                        acc = acc + v if sgn > 0 else acc - v
                return acc

            v_ref[0:lr, :] = comb(real).astype(BF16)
            v_ref[lr:2 * lr, :] = comb(imag).astype(BF16)
            z = _dot(e_ref[j], v_ref[...])
            for k, z_ref in enumerate(z_refs):
                z_ref[pl.ds(j, lr, stride=radix), :] = z[:, k * LANES:(k + 1) * LANES]
        for k, z_ref in enumerate(z_refs):
            sl = slice(k * LANES, (k + 1) * LANES)
            o_ref[:, sl] = x_ref[:, sl] + g1_ref[:, sl] * z_ref[...]

    return pl.pallas_call(
        kern, grid=(nb, ncb),
        in_specs=[pl.BlockSpec((seg, cblk), lambda b, c: (b, c)),
                  pl.BlockSpec((seg, cblk), lambda b, c: (b, ncb + c)),
                  pl.BlockSpec((seg, cblk), lambda b, c: (b, c)),
                  _resident((radix, lr, 2 * lr)),
                  pl.BlockSpec((None, 1, cblk), lambda b, c: (st.seg_row(b), 0, G1 * ncb + c))],
        out_specs=pl.BlockSpec((seg, cblk), lambda b, c: (b, c)),
        out_shape=SDS((n, d), F32),
        scratch_shapes=[pltpu.VMEM((2 * lr, cblk), BF16)] + [pltpu.VMEM((seg, LANES), F32)] * (cblk // LANES),
        name=f"seq_dft_r{radix}",
        compiler_params=_params(("parallel", "parallel"), 40))(p, p, x, e, mods)


def _fourier_layer(streams, xs, mods, gain, w_out):
    wcs = _fourier_weight_call(w_out)
    outs = []
    for st, x in zip(streams, xs):
        if x is None:
            outs.append(None)
            continue
        p = _proj_call(st, x, mods, gain, wcs, "fourier_proj")
        outs.append(_seq_dft_call(st, p, x, mods, 4 if st.seg >= 1024 else 1))
    return outs


def _conv_in_call(st, x, mods, gain, w_in):
    n, d = x.shape
    t = st.tile

    def kern(x_ref, g_ref, sc_ref, sh_ref, w_ref, bg_ref, u_ref):
        h = _modnorm(x_ref[...], g_ref[...], sc_ref[...], sh_ref[...]).astype(BF16)
        bg_ref[...] = _dot(h, w_ref[:, 0:d]).astype(BF16)
        u_ref[...] = (_dot(h, w_ref[:, d:2 * d]) * _dot(h, w_ref[:, 2 * d:3 * d])).astype(BF16)

    return pl.pallas_call(
        kern, grid=(st.tiles,),
        in_specs=[pl.BlockSpec((t, d), lambda i: (i, 0)), _full((1, d)),
                  _mod_spec(st, SC1, d), _mod_spec(st, SH1, d), _resident((d, 3 * d))],
        out_specs=[pl.BlockSpec((t, d), lambda i: (i, 0))] * 2,
        out_shape=[SDS((n, d), BF16)] * 2, name="conv_in",
        compiler_params=_params(("parallel",), 48))(x, gain, mods, mods, w_in)


def _conv_out_call(st, bg, u, x, mods, conv_w, w_out):
    n, d = x.shape
    t = st.tile
    halo = 16
    hb = t // halo
    nhalo = n // halo
    seg = st.seg

    def kern(u_ref, up_ref, un_ref, bg_ref, x_ref, cw_ref, g1_ref, w_ref, o_ref):
        i = pl.program_id(0)
        u = u_ref[...].astype(F32)
        row = lax.broadcasted_iota(jnp.int32, (t, 1), 0)
        pos = (i * t + row) & (seg - 1)
        prev_row = up_ref[...].astype(F32)[halo - 1:halo, :]
        next_row = un_ref[...].astype(F32)[0:1, :]
        um = jnp.where(row == 0, prev_row, pltpu.roll(u, 1, axis=0))
        um = jnp.where(pos == 0, 0.0, um)
        up = jnp.where(row == t - 1, next_row, pltpu.roll(u, t - 1, axis=0))
        up = jnp.where(pos == seg - 1, 0.0, up)
        z = cw_ref[0:1, :] * um + cw_ref[1:2, :] * u + cw_ref[2:3, :] * up
        a = (bg_ref[...].astype(F32) * z).astype(BF16)
        o_ref[...] = x_ref[...] + g1_ref[...] * _dot(a, w_ref[...])

    tile_spec = pl.BlockSpec((t, d), lambda i: (i, 0))
    return pl.pallas_call(
        kern, grid=(st.tiles,),
        in_specs=[tile_spec,
                  pl.BlockSpec((halo, d), lambda i: (jnp.maximum(i * hb - 1, 0), 0)),
                  pl.BlockSpec((halo, d), lambda i: (jnp.minimum((i + 1) * hb, nhalo - 1), 0)),
                  tile_spec, tile_spec, _full((3, d)), _mod_spec(st, G1, d), _resident((d, d))],
        out_specs=tile_spec, out_shape=SDS((n, d), F32), name="conv_out",
        compiler_params=_params(("parallel",), 40))(u, u, u, bg, x, conv_w, mods, w_out)


def _conv_layer(streams, xs, mods, gain, w_in, conv_w, w_out):
    w_in_b, w_out_b = w_in.astype(BF16), w_out.astype(BF16)
    outs = []
    for st, x in zip(streams, xs):
        if x is None:
            outs.append(None)
            continue
        bg, u = _conv_in_call(st, x, mods, gain, w_in_b)
        outs.append(_conv_out_call(st, bg, u, x, mods, conv_w, w_out_b))
    return outs


def _rope_tables(length):
    q = ROPE_DIM // 4
    pos = np.arange(length)
    rc = np.stack([pos // GRID_W, pos % GRID_W], axis=1).astype(np.float32)
    inv = (ROPE_THETA ** (-np.arange(q, dtype=np.float32) / q)).astype(np.float32)
    lane = np.arange(ROPE_DIM)
    ang = rc[:, lane >> 5] * inv[lane & (q - 1)][None, :]
    sign = np.where((lane & q) == 0, -1.0, 1.0)[None, :]
    cos = np.cos(ang.astype(np.float32)).astype(np.float32)
    sin = (np.sin(ang.astype(np.float32)) * sign).astype(np.float32)
    reps = LANES // ROPE_DIM
    return jnp.asarray(np.tile(cos, (1, reps))), jnp.asarray(np.tile(sin, (1, reps)))


def _rope_partner_perm(n_heads):
    lane = np.arange(n_heads * ROPE_DIM)
    return lane ^ (ROPE_DIM // 4)


def _seg_rinv(x, seg):
    t, w = x.shape
    cols = []
    for c in range(w // LANES):
        blk = x[:, c * LANES:(c + 1) * LANES]
        sq = blk * blk
        if seg == LANES:
            cols.append(jnp.broadcast_to(
                lax.rsqrt(jnp.mean(sq, axis=-1, keepdims=True) + EPS), (t, LANES)))
        else:
            lane = lax.broadcasted_iota(jnp.int32, (t, LANES), 1)
            low = lane < seg
            lo = jnp.sum(jnp.where(low, sq, 0.0), axis=-1, keepdims=True) * (1.0 / seg)
            hi = jnp.sum(jnp.where(low, 0.0, sq), axis=-1, keepdims=True) * (1.0 / seg)
            cols.append(jnp.where(low, lax.rsqrt(lo + EPS), lax.rsqrt(hi + EPS)))
    return cols[0] if len(cols) == 1 else jnp.concatenate(cols, axis=-1)


def _tile_lanes(v, w):
    reps = w // LANES
    return v if reps == 1 else jnp.concatenate([v] * reps, axis=-1)


def _mla_proj_call(st, x, mods, gain, wts, rope, need_q):
    n, d = x.shape
    t = st.tile
    h = MLA_HEADS
    wn, wr = h * MLA_NOPE, h * MLA_ROPE
    (w_in, q_norm, kv_norm, w_uq, w_ukv, qg_n, qg_r, qg_rp, kg_n, kg_r, kg_rp) = wts
    use_rope = rope is not None

    def kern(*refs):
        refs = list(refs)
        x_ref, g_ref, sc_ref, sh_ref, win_ref, qn_ref_, kvn_ref_, wuq_ref, wukv_ref = refs[:9]
        qgn_ref, qgr_ref, qgrp_ref, kgn_ref, kgr_ref, kgrp_ref = refs[9:15]
        rest = refs[15:]
        if use_rope:
            cos_ref, sin_ref = rest[:2]
            rest = rest[2:]
        if need_q:
            oqn_ref, oqr_ref = rest[:2]
            rest = rest[2:]
        okn_ref, okr_ref, ov_ref = rest
        hm = _modnorm(x_ref[...], g_ref[...], sc_ref[...], sh_ref[...]).astype(BF16)
        ck = _dot(hm, win_ref[...])
        kr2 = ck[:, MLA_Q_RANK + MLA_KV_RANK:]
        kr_rinv = _seg_rinv(kr2, MLA_ROPE)[:, 0:MLA_ROPE]
        kr_raw = kr2[:, 0:MLA_ROPE] * kgr_ref[...]
        if use_rope:
            kr_par = kr2[:, MLA_ROPE:] * kgrp_ref[...]
            kr = kr_rinv * (kr_raw * cos_ref[:, 0:MLA_ROPE] + kr_par * sin_ref[:, 0:MLA_ROPE])
        else:
            kr = kr_rinv * kr_raw
        okr_ref[...] = kr.astype(BF16)
        ckv = ck[:, MLA_Q_RANK:MLA_Q_RANK + MLA_KV_RANK]
        ckv = (ckv * _seg_rinv(ckv, LANES) * kvn_ref_[...]).astype(BF16)
        kv = _dot(ckv, wukv_ref[...])
        kn = kv[:, 0:wn]
        okn_ref[...] = (kn * _seg_rinv(kn, MLA_NOPE) * kgn_ref[...]).astype(BF16)
        ov_ref[...] = kv[:, wn:].astype(BF16)
        if need_q:
            cq = ck[:, 0:MLA_Q_RANK]
            rq = lax.rsqrt(jnp.mean(cq * cq, axis=-1, keepdims=True) + EPS)
            cq = (cq * rq * qn_ref_[...]).astype(BF16)
            q = _dot(cq, wuq_ref[...])
            qn = q[:, 0:wn]
            oqn_ref[...] = (qn * _seg_rinv(qn, MLA_NOPE) * (qgn_ref[...] * (MLA_SCALE * LOG2E))).astype(BF16)
            qr_raw = q[:, wn:wn + wr]
            rinv = _seg_rinv(qr_raw, MLA_ROPE) * (MLA_SCALE * LOG2E)
            if use_rope:
                qr_par = q[:, wn + wr:]
                cos = _tile_lanes(cos_ref[...], wr)
                sin = _tile_lanes(sin_ref[...], wr)
                qr = rinv * (qr_raw * qgr_ref[...] * cos + qr_par * qgrp_ref[...] * sin)
            else:
                qr = rinv * (qr_raw * qgr_ref[...])
            oqr_ref[...] = qr.astype(BF16)

    tile = lambda w: pl.BlockSpec((t, w), lambda i: (i, 0))
    in_specs = [tile(d), _full((1, d)), _mod_spec(st, SC1, d), _mod_spec(st, SH1, d),
                _resident(w_in.shape), _full(q_norm.shape), _full(kv_norm.shape),
                _resident(w_uq.shape), _resident(w_ukv.shape),
                _full(qg_n.shape), _full(qg_r.shape), _full(qg_rp.shape),
                _full(kg_n.shape), _full(kg_r.shape), _full(kg_rp.shape)]
    args = [x, gain, mods, mods, w_in, q_norm, kv_norm, w_uq, w_ukv, qg_n, qg_r, qg_rp, kg_n, kg_r, kg_rp]
    if use_rope:
        tps = st.tiles_per_seg
        in_specs += [pl.BlockSpec((t, LANES), lambda i: (i % tps, 0))] * 2
        args += list(rope)
    out_specs, out_shape = [], []
    if need_q:
        out_specs += [tile(wn), tile(wr)]
        out_shape += [SDS((n, wn), BF16), SDS((n, wr), BF16)]
    out_specs += [tile(wn), tile(MLA_ROPE), tile(h * MLA_V)]
    out_shape += [SDS((n, wn), BF16), SDS((n, MLA_ROPE), BF16), SDS((n, h * MLA_V), BF16)]
    outs = pl.pallas_call(
        kern, grid=(st.tiles,), in_specs=in_specs, out_specs=out_specs, out_shape=out_shape,
        name="mla_proj", compiler_params=_params(("parallel",), 48))(*args)
    if need_q:
        return tuple(outs)
    return (None, None) + tuple(outs)


def _mla_attn_fast_call(batch, lq, tq, q, kv_sets, shift):
    qn, qr = q
    hp = 2
    n_hp = MLA_HEADS // hp
    nq = lq // tq
    nsets = len(kv_sets)
    lengths = [s[3] for s in kv_sets]
    lk = sum(lengths)
    kw = 2 * LANES
    pad = kw - MLA_NOPE - MLA_ROPE

    def kern(*refs):
        sh_ref, qn_ref, qr_ref = refs[:3]
        sets = [refs[3 + 3 * s:6 + 3 * s] for s in range(nsets)]
        o_ref, kq_ref, vq_ref = refs[-3:]

        @pl.when(pl.program_id(2) == 0)
        def _():
            lane_k = lax.broadcasted_iota(jnp.int32, (lk, pad), 1)
            lane_v = lax.broadcasted_iota(jnp.int32, (lk, kw - MLA_V), 1)
            for hh in range(hp):
                off = 0
                for (kn_ref, kr_ref, v_ref), length in zip(sets, lengths):
                    kq_ref[hh, off:off + length, 0:MLA_NOPE] = kn_ref[:, hh * MLA_NOPE:(hh + 1) * MLA_NOPE]
                    kq_ref[hh, off:off + length, MLA_NOPE:MLA_NOPE + MLA_ROPE] = kr_ref[...]
                    vq_ref[hh, off:off + length, 0:MLA_V] = v_ref[:, hh * MLA_V:(hh + 1) * MLA_V]
                    off += length
                kq_ref[hh, :, MLA_NOPE + MLA_ROPE:] = jnp.where(lane_k == 0, -sh_ref[:, 0:pad], 0.0).astype(BF16)
                vq_ref[hh, :, MLA_V:] = jnp.where(lane_v == 0, 1.0, 0.0).astype(BF16)

        lane_q = lax.broadcasted_iota(jnp.int32, (tq, pad), 1)
        one = jnp.where(lane_q == 0, 1.0, 0.0).astype(BF16)
        for hh in range(hp):
            qc = jnp.concatenate([qn_ref[:, hh * MLA_NOPE:(hh + 1) * MLA_NOPE],
                                  qr_ref[:, hh * MLA_ROPE:(hh + 1) * MLA_ROPE], one], axis=1)
            p = jnp.exp2(_nt(qc, kq_ref[hh])).astype(BF16)
            acc = _dot(p, vq_ref[hh])
            o_ref[:, hh * MLA_V:(hh + 1) * MLA_V] = (acc[:, 0:MLA_V] / acc[:, MLA_V:MLA_V + 1]).astype(BF16)

    in_specs = [_full((1, LANES)),
                pl.BlockSpec((tq, hp * MLA_NOPE), lambda b, h, i: (b * nq + i, h)),
                pl.BlockSpec((tq, hp * MLA_ROPE), lambda b, h, i: (b * nq + i, h))]
    args = [shift, qn, qr]
    for kn, kr, v, length in kv_sets:
        in_specs += [pl.BlockSpec((length, hp * MLA_NOPE), lambda b, h, i: (b, h)),
                     pl.BlockSpec((length, MLA_ROPE), lambda b, h, i: (b, 0)),
                     pl.BlockSpec((length, hp * MLA_V), lambda b, h, i: (b, h))]
        args += [kn, kr, v]
    return pl.pallas_call(
        kern, grid=(batch, n_hp, nq), in_specs=in_specs,
        out_specs=pl.BlockSpec((tq, hp * MLA_V), lambda b, h, i: (b * nq + i, h)),
        out_shape=SDS((batch * lq, MLA_HEADS * MLA_V), BF16),
        scratch_shapes=[pltpu.VMEM((hp, lk, kw), BF16), pltpu.VMEM((hp, lk, kw), BF16)],
        name="mla_attn_fast",
        compiler_params=_params(("parallel", "parallel", "arbitrary"), 56))(*args)


def _mla_attn_call(batch, lq, tq, q, kv_sets):
    qn, qr = q
    hp = 2
    n_hp = MLA_HEADS // hp
    nq = lq // tq
    nsets = len(kv_sets)

    def kern(*refs):
        qn_ref, qr_ref = refs[:2]
        o_ref = refs[-1]
        sets = [refs[2 + 3 * s:5 + 3 * s] for s in range(nsets)]
        for hh in range(hp):
            qn_h = qn_ref[:, hh * MLA_NOPE:(hh + 1) * MLA_NOPE]
            qr_h = qr_ref[:, hh * MLA_ROPE:(hh + 1) * MLA_ROPE]
            scores = [_nt(qn_h, kn_ref[:, hh * MLA_NOPE:(hh + 1) * MLA_NOPE]) + _nt(qr_h, kr_ref[...])
                      for kn_ref, kr_ref, _ in sets]
            m = functools.reduce(jnp.maximum, [jnp.max(s, axis=-1, keepdims=True) for s in scores])
            ps = [jnp.exp2(s - m) for s in scores]
            den = functools.reduce(lambda a, b: a + b, [jnp.sum(p, axis=-1, keepdims=True) for p in ps])
            acc = functools.reduce(lambda a, b: a + b, [
                _dot(p.astype(BF16), v_ref[:, hh * MLA_V:(hh + 1) * MLA_V])
                for p, (_, _, v_ref) in zip(ps, sets)])
            o_ref[:, hh * MLA_V:(hh + 1) * MLA_V] = (acc / den).astype(BF16)

    in_specs = [pl.BlockSpec((tq, hp * MLA_NOPE), lambda b, h, i: (b * nq + i, h)),
                pl.BlockSpec((tq, hp * MLA_ROPE), lambda b, h, i: (b * nq + i, h))]
    args = [qn, qr]
    for kn, kr, v, length in kv_sets:
        in_specs += [pl.BlockSpec((length, hp * MLA_NOPE), lambda b, h, i: (b, h)),
                     pl.BlockSpec((length, MLA_ROPE), lambda b, h, i: (b, 0)),
                     pl.BlockSpec((length, hp * MLA_V), lambda b, h, i: (b, h))]
        args += [kn, kr, v]
    return pl.pallas_call(
        kern, grid=(batch, n_hp, nq), in_specs=in_specs,
        out_specs=pl.BlockSpec((tq, hp * MLA_V), lambda b, h, i: (b * nq + i, h)),
        out_shape=SDS((batch * lq, MLA_HEADS * MLA_V), BF16), name="mla_attn",
        compiler_params=_params(("parallel", "parallel", "arbitrary"), 56))(*args)


def _mla_layer(streams, xs, mods, gain, p, rope, ctx_next):
    st_c, st_x = streams
    xc, xx = xs
    h = MLA_HEADS
    w_in, q_norm, kv_norm, w_uq, w_ukv, q_gain, k_gain, w_out = p
    par = _rope_partner_perm(1)
    kr_cols = w_in[:, MLA_Q_RANK + MLA_KV_RANK:]
    w_in_p = jnp.concatenate([w_in, kr_cols[:, par]], axis=1).astype(BF16)
    wq = w_uq.reshape(MLA_Q_RANK, h, MLA_NOPE + MLA_ROPE)
    wq_n = wq[:, :, :MLA_NOPE].reshape(MLA_Q_RANK, h * MLA_NOPE)
    wq_r = wq[:, :, MLA_NOPE:]
    w_uq_p = jnp.concatenate([wq_n, wq_r.reshape(MLA_Q_RANK, h * MLA_ROPE),
                              wq_r[:, :, par].reshape(MLA_Q_RANK, h * MLA_ROPE)], axis=1).astype(BF16)
    wkv = w_ukv.reshape(MLA_KV_RANK, h, MLA_NOPE + MLA_V)
    w_ukv_p = jnp.concatenate([wkv[:, :, :MLA_NOPE].reshape(MLA_KV_RANK, h * MLA_NOPE),
                               wkv[:, :, MLA_NOPE:].reshape(MLA_KV_RANK, h * MLA_V)], axis=1).astype(BF16)
    qg_r = q_gain[MLA_NOPE:]
    kg_r = k_gain[MLA_NOPE:]
    wts = (w_in_p, q_norm.reshape(1, -1), kv_norm.reshape(1, -1), w_uq_p, w_ukv_p,
           jnp.tile(q_gain[:MLA_NOPE], h).reshape(1, -1), jnp.tile(qg_r, h).reshape(1, -1),
           jnp.tile(qg_r[par], h).reshape(1, -1), jnp.tile(k_gain[:MLA_NOPE], h).reshape(1, -1),
           kg_r.reshape(1, -1), kg_r[par].reshape(1, -1))
    w_out_b = w_out.astype(BF16)
    qn_c, qr_c, kn_c, kr_c, v_c = _mla_proj_call(st_c, xc, mods, gain, wts, None, ctx_next)
    qn_x, qr_x, kn_x, kr_x, v_x = _mla_proj_call(st_x, xx, mods, gain, wts, rope, True)
    b = st_x.batch
    amax = lambda g: jnp.max(jnp.abs(g))
    bound = LOG2E * MLA_SCALE * (MLA_NOPE * amax(q_gain[:MLA_NOPE]) * amax(k_gain[:MLA_NOPE])
                                 + MLA_ROPE * amax(qg_r) * amax(kg_r))
    shift = bound * SHIFT_MARGIN + 1.0
    shift_row = jnp.full((1, LANES), shift, F32)

    def attend(lq, tq, q, kv_sets):
        return lax.cond(shift <= MAX_FIXED_SHIFT,
                        lambda: _mla_attn_fast_call(b, lq, tq, q, kv_sets, shift_row),
                        lambda: _mla_attn_call(b, lq, tq, q, kv_sets))

    a_x = attend(st_x.seg, 512, (qn_x, qr_x), [(kn_c, kr_c, v_c, st_c.seg), (kn_x, kr_x, v_x, st_x.seg)])
    out_x = _outproj_call(st_x, a_x, xx, mods, w_out_b, "mla_out")
    out_c = None
    if ctx_next:
        a_c = attend(st_c.seg, st_c.seg, (qn_c, qr_c), [(kn_c, kr_c, v_c, st_c.seg)])
        out_c = _outproj_call(st_c, a_c, xc, mods, w_out_b, "mla_out")
    return [out_c, out_x]


def _gqa_proj_call(st, x, mods, gain, wts, rope, need_q):
    n, d = x.shape
    t = st.tile
    wq, wk = GQA_Q_HEADS * GQA_HEAD_DIM, GQA_KV_HEADS * GQA_HEAD_DIM
    w_in, qg, qgp, kg, kgp = wts
    use_rope = rope is not None

    def kern(*refs):
        refs = list(refs)
        x_ref, g_ref, sc_ref, sh_ref, w_ref, qg_ref, qgp_ref, kg_ref, kgp_ref = refs[:9]
        rest = refs[9:]
        if use_rope:
            cos_ref, sin_ref = rest[:2]
            rest = rest[2:]
        if need_q:
            oq_ref = rest[0]
            rest = rest[1:]
        ok_ref, ov_ref = rest
        hm = _modnorm(x_ref[...], g_ref[...], sc_ref[...], sh_ref[...]).astype(BF16)
        kvp = _dot(hm, w_ref[:, 0:3 * wk])
        k_raw = kvp[:, 0:wk]
        k_rinv = _seg_rinv(k_raw, GQA_HEAD_DIM)
        ov_ref[...] = kvp[:, wk:2 * wk].astype(BF16)
        if use_rope:
            cos_k, sin_k = _tile_lanes(cos_ref[...], wk), _tile_lanes(sin_ref[...], wk)
            k = k_rinv * (k_raw * kg_ref[...] * cos_k + kvp[:, 2 * wk:] * kgp_ref[...] * sin_k)
        else:
            k = k_rinv * (k_raw * kg_ref[...])
        ok_ref[...] = k.astype(BF16)
        if need_q:
            qp = _dot(hm, w_ref[:, 3 * wk:])
            q_raw = qp[:, 0:wq]
            rinv = _seg_rinv(q_raw, GQA_HEAD_DIM) * (GQA_SCALE * LOG2E)
            if use_rope:
                cos_q, sin_q = _tile_lanes(cos_ref[...], wq), _tile_lanes(sin_ref[...], wq)
                q = rinv * (q_raw * qg_ref[...] * cos_q + qp[:, wq:] * qgp_ref[...] * sin_q)
            else:
                q = rinv * (q_raw * qg_ref[...])
            oq_ref[...] = q.astype(BF16)

    tile = lambda w: pl.BlockSpec((t, w), lambda i: (i, 0))
    in_specs = [tile(d), _full((1, d)), _mod_spec(st, SC1, d), _mod_spec(st, SH1, d),
                _resident(w_in.shape), _full(qg.shape), _full(qgp.shape), _full(kg.shape), _full(kgp.shape)]
    args = [x, gain, mods, mods, w_in, qg, qgp, kg, kgp]
    if use_rope:
        tps = st.tiles_per_seg
        in_specs += [pl.BlockSpec((t, LANES), lambda i: (i % tps, 0))] * 2
        args += list(rope)
    out_specs, out_shape = [], []
    if need_q:
        out_specs.append(tile(wq))
        out_shape.append(SDS((n, wq), BF16))
    out_specs += [tile(wk), tile(wk)]
    out_shape += [SDS((n, wk), BF16)] * 2
    outs = pl.pallas_call(
        kern, grid=(st.tiles,), in_specs=in_specs, out_specs=out_specs, out_shape=out_shape,
        name="gqa_proj", compiler_params=_params(("parallel",), 48))(*args)
    return tuple(outs) if need_q else (None,) + tuple(outs)


def _gqa_window_call(batch, length, lc, q, k, v, kc, vc, sink):
    nb = length // BLOCK
    hd = GQA_HEAD_DIM
    wq, wk = GQA_Q_HEADS * hd, GQA_KV_HEADS * hd
    rows = GQA_GROUP * BLOCK

    def kern(sink_ref, q_ref, kp_ref, k0_ref, kn_ref, vp_ref, v0_ref, vn_ref, kc_ref, vc_ref, o_ref):
        nblk = pl.program_id(1)
        r = lax.broadcasted_iota(jnp.int32, (rows, 3 * BLOCK), 0) & (BLOCK - 1)
        c = lax.broadcasted_iota(jnp.int32, (rows, 3 * BLOCK), 1)
        valid = (c >= r + BLOCK - WINDOW) & (c <= r + BLOCK + WINDOW)
        valid = valid & ((c >= BLOCK) | (nblk > 0)) & ((c < 2 * BLOCK) | (nblk < nb - 1))
        hrow = lax.broadcasted_iota(jnp.int32, (rows, 1), 0) // BLOCK
        for g in range(GQA_KV_HEADS):
            sl = slice(g * hd, (g + 1) * hd)
            qg = jnp.concatenate([q_ref[:, (g * GQA_GROUP + j) * hd:(g * GQA_GROUP + j + 1) * hd]
                                  for j in range(GQA_GROUP)], axis=0)
            kband = jnp.concatenate([kp_ref[:, sl], k0_ref[:, sl], kn_ref[:, sl]], axis=0)
            vband = jnp.concatenate([vp_ref[:, sl], v0_ref[:, sl], vn_ref[:, sl]], axis=0)
            s_c = _nt(qg, kc_ref[:, sl])
            s_b = jnp.where(valid, _nt(qg, kband), -1e30)
            snk = jnp.zeros((rows, 1), F32)
            for j in range(GQA_GROUP):
                snk = jnp.where(hrow == j, sink_ref[g * GQA_GROUP + j], snk)
            m = jnp.maximum(jnp.maximum(jnp.max(s_c, axis=-1, keepdims=True),
                                        jnp.max(s_b, axis=-1, keepdims=True)), snk)
            p_c = jnp.exp2(s_c - m)
            p_b = jnp.exp2(s_b - m)
            den = (jnp.sum(p_c, axis=-1, keepdims=True) + jnp.sum(p_b, axis=-1, keepdims=True)
                   + jnp.exp2(snk - m))
            o = (_dot(p_c.astype(BF16), vc_ref[:, sl]) + _dot(p_b.astype(BF16), vband)) / den
            o_ref[:, g * GQA_GROUP * hd:(g + 1) * GQA_GROUP * hd] = jnp.concatenate(
                [o[j * BLOCK:(j + 1) * BLOCK, :] for j in range(GQA_GROUP)], axis=-1).astype(BF16)

    blk = lambda f: pl.BlockSpec((BLOCK, wk), f)
    prev_ = lambda b, i: (b * nb + jnp.maximum(i - 1, 0), 0)
    cur_ = lambda b, i: (b * nb + i, 0)
    next_ = lambda b, i: (b * nb + jnp.minimum(i + 1, nb - 1), 0)
    ctx_spec = pl.BlockSpec((lc, wk), lambda b, i: (b, 0))
    return pl.pallas_call(
        kern, grid=(batch, nb),
        in_specs=[pl.BlockSpec(memory_space=pltpu.SMEM),
                  pl.BlockSpec((BLOCK, wq), cur_), blk(prev_), blk(cur_), blk(next_),
                  blk(prev_), blk(cur_), blk(next_), ctx_spec, ctx_spec],
        out_specs=pl.BlockSpec((BLOCK, wq), cur_),
        out_shape=SDS((batch * length, wq), BF16), name="gqa_window",
        compiler_params=_params(("parallel", "arbitrary"), 40))(sink, q, k, k, k, v, v, v, kc, vc)


def _gqa_window_fast_call(batch, length, lc, q, k, v, kc, vc, sink2, shift):
    nb = length // BLOCK
    hd = GQA_HEAD_DIM
    wq, wk = GQA_Q_HEADS * hd, GQA_KV_HEADS * hd
    rows = GQA_GROUP * BLOCK

    def kern(sink_ref, shift_ref, q_ref, kp_ref, k0_ref, kn_ref, vp_ref, v0_ref, vn_ref, kc_ref, vc_ref, o_ref):
        nblk = pl.program_id(1)
        r = lax.broadcasted_iota(jnp.int32, (rows, 3 * BLOCK), 0) & (BLOCK - 1)
        c = lax.broadcasted_iota(jnp.int32, (rows, 3 * BLOCK), 1)
        valid = (c >= r + BLOCK - WINDOW) & (c <= r + BLOCK + WINDOW)
        valid = valid & ((c >= BLOCK) | (nblk > 0)) & ((c < 2 * BLOCK) | (nblk < nb - 1))
        hrow = lax.broadcasted_iota(jnp.int32, (rows, 1), 0) // BLOCK
        lane0_q = lax.broadcasted_iota(jnp.int32, (BLOCK, hd), 1) == 0

        def with_one(x):
            lane0 = lax.broadcasted_iota(jnp.int32, (x.shape[0], hd), 1) == 0
            return jnp.concatenate([x, jnp.where(lane0, 1.0, 0.0).astype(BF16)], axis=1)

        for g in range(GQA_KV_HEADS):
            sl = slice(g * hd, (g + 1) * hd)
            heads = [g * GQA_GROUP + j for j in range(GQA_GROUP)]
            qg = jnp.concatenate(
                [jnp.concatenate([q_ref[:, h * hd:(h + 1) * hd],
                                  jnp.where(lane0_q, -shift_ref[h], 0.0).astype(BF16)], axis=1)
                 for h in heads], axis=0)
            kband = with_one(jnp.concatenate([kp_ref[:, sl], k0_ref[:, sl], kn_ref[:, sl]], axis=0))
            vband = with_one(jnp.concatenate([vp_ref[:, sl], v0_ref[:, sl], vn_ref[:, sl]], axis=0))
            p_c = jnp.exp2(_nt(qg, with_one(kc_ref[:, sl]))).astype(BF16)
            p_b = jnp.where(valid, jnp.exp2(_nt(qg, kband)), 0.0).astype(BF16)
            acc = _dot(p_c, with_one(vc_ref[:, sl])) + _dot(p_b, vband)
            snk = jnp.zeros((rows, 1), F32)
            for j, h in enumerate(heads):
                snk = jnp.where(hrow == j, sink_ref[h] - shift_ref[h], snk)
            o = acc[:, 0:hd] / (acc[:, hd:hd + 1] + jnp.exp2(snk))
            o_ref[:, g * GQA_GROUP * hd:(g + 1) * GQA_GROUP * hd] = jnp.concatenate(
                [o[j * BLOCK:(j + 1) * BLOCK, :] for j in range(GQA_GROUP)], axis=-1).astype(BF16)

    blk = lambda f: pl.BlockSpec((BLOCK, wk), f)
    prev_ = lambda b, i: (b * nb + jnp.maximum(i - 1, 0), 0)
    cur_ = lambda b, i: (b * nb + i, 0)
    next_ = lambda b, i: (b * nb + jnp.minimum(i + 1, nb - 1), 0)
    ctx_spec = pl.BlockSpec((lc, wk), lambda b, i: (b, 0))
    smem = pl.BlockSpec(memory_space=pltpu.SMEM)
    return pl.pallas_call(
        kern, grid=(batch, nb),
        in_specs=[smem, smem, pl.BlockSpec((BLOCK, wq), cur_), blk(prev_), blk(cur_), blk(next_),
                  blk(prev_), blk(cur_), blk(next_), ctx_spec, ctx_spec],
        out_specs=pl.BlockSpec((BLOCK, wq), cur_),
        out_shape=SDS((batch * length, wq), BF16), name="gqa_window_fast",
        compiler_params=_params(("parallel", "arbitrary"), 40))(sink2, shift, q, k, k, k, v, v, v, kc, vc)


def _gqa_layer(streams, xs, mods, gain, p, rope, ctx_next):
    assert not ctx_next, "the windowed-GQA mixer is only implemented as the last layer"
    st_c, st_x = streams
    xc, xx = xs
    w_in, q_gain, k_gain, sink, w_out = p
    wq, wk = GQA_Q_HEADS * GQA_HEAD_DIM, GQA_KV_HEADS * GQA_HEAD_DIM
    w_q, w_k, w_v = w_in[:, :wq], w_in[:, wq:wq + wk], w_in[:, wq + wk:]
    w_in_p = jnp.concatenate([w_k, w_v, w_k[:, _rope_partner_perm(GQA_KV_HEADS)],
                              w_q, w_q[:, _rope_partner_perm(GQA_Q_HEADS)]], axis=1).astype(BF16)
    par = _rope_partner_perm(1)
    wts = (w_in_p, jnp.tile(q_gain, GQA_Q_HEADS).reshape(1, -1), jnp.tile(q_gain[par], GQA_Q_HEADS).reshape(1, -1),
           jnp.tile(k_gain, GQA_KV_HEADS).reshape(1, -1), jnp.tile(k_gain[par], GQA_KV_HEADS).reshape(1, -1))
    _, kc, vc = _gqa_proj_call(st_c, xc, mods, gain, wts, None, False)
    q, k, v = _gqa_proj_call(st_x, xx, mods, gain, wts, rope, True)
    sink2 = sink * LOG2E
    bound = LOG2E * GQA_SCALE * GQA_HEAD_DIM * jnp.max(jnp.abs(q_gain)) * jnp.max(jnp.abs(k_gain))
    shift = jnp.maximum(bound * SHIFT_MARGIN + 1.0, sink2).astype(BF16).astype(F32)
    args = (st_x.batch, st_x.seg, st_c.seg, q, k, v, kc, vc, sink2)
    a = lax.cond(jnp.max(shift) <= MAX_FIXED_SHIFT,
                 lambda: _gqa_window_fast_call(*args, shift),
                 lambda: _gqa_window_call(*args))
    return [None, _outproj_call(st_x, a, xx, mods, w_out.astype(BF16), "gqa_out")]


def _cast_experts_call(w, layer):
    _, ne, a, b = w.shape
    eb = 4

    def kern(w_ref, o_ref):
        o_ref[...] = w_ref[...].astype(BF16)

    return pl.pallas_call(
        kern, grid=(ne // eb,),
        in_specs=[pl.BlockSpec((None, eb, a, b), lambda i: (layer, i, 0, 0))],
        out_specs=pl.BlockSpec((eb, a, b), lambda i: (i, 0, 0)),
        out_shape=SDS((ne, a, b), BF16), name="cast_experts",
        compiler_params=_params(("parallel",), 32))(w)


def _route(logits_t, bias_col):
    scores = _sigmoid(logits_t)
    biased = scores + bias_col
    rows = [biased[e:e + 1, :] for e in range(N_EXPERTS)]
    srow = [scores[e:e + 1, :] for e in range(N_EXPERTS)]
    epg = EXPERTS_PER_GROUP
    gscore = []
    for g in range(N_EXPERT_GROUPS):
        v = rows[g * epg:(g + 1) * epg]
        pair = [v[a] + v[b] for a in range(epg) for b in range(a + 1, epg)]
        gscore.append(functools.reduce(jnp.maximum, pair))
    ind, wloc = [], [None] * epg
    for g in range(N_EXPERT_GROUPS):
        best = None
        for g2 in range(N_EXPERT_GROUPS):
            if g2 == g:
                continue
            cnd = gscore[g] > gscore[g2] if g2 < g else gscore[g] >= gscore[g2]
            best = cnd if best is None else best & cnd
        ind.append(jnp.where(best, 1.0, 0.0))
        for j in range(epg):
            e = g * epg + j
            rank = None
            for e2 in range(g * epg, (g + 1) * epg):
                if e2 == e:
                    continue
                ahead = rows[e2] >= rows[e] if e2 < e else rows[e2] > rows[e]
                one = jnp.where(ahead, 1.0, 0.0)
                rank = one if rank is None else rank + one
            w = jnp.where(best & (rank < 2.0), srow[e], 0.0)
            wloc[j] = w if wloc[j] is None else wloc[j] + w
    den = functools.reduce(lambda a, b: a + b, wloc)
    return ind, [w / den for w in wloc]


MOE_TILE = 1024
MOE_CHUNK = 256


def _moe_call(st, x, mods, gain, router_wt, router_bias, w_gate, w_up, w_down):
    n, d = x.shape
    t = min(MOE_TILE, n)
    r = MOE_CHUNK
    ne, _, ff = w_gate.shape
    epg, ng = EXPERTS_PER_GROUP, N_EXPERT_GROUPS
    nch = t // r + ng - 1
    tiles = n // t
    tri = jnp.asarray(np.triu(np.ones((t, t), np.float32), 1), dtype=BF16)
    wd2 = w_down.reshape(ne * ff, d)

    def kern(x_ref, g_ref, sc_ref, sh_ref, g2_ref, rw_ref, rb_ref, tri_ref, wg_ref, wu_ref, wd_ref, o_ref,
             hb_ref, y_ref, gt_ref):
        xin = x_ref[...]
        hb_ref[...] = _modnorm(xin, g_ref[...], sc_ref[...], sh_ref[...]).astype(BF16)
        ind, wloc = _route(_nt(rw_ref[...], hb_ref[...]), rb_ref[...])
        ind8 = jnp.concatenate(ind + [jnp.zeros((8 - ng, t), F32)], axis=0)
        pos = _dot(ind8.astype(BF16), tri_ref[...])
        count = [jnp.sum(ind[g]).astype(jnp.int32) for g in range(ng)]
        first = [jnp.int32(0)]
        for g in range(ng):
            first.append(first[g] + (count[g] + (r - 1)) // r)
        slot = functools.reduce(lambda a, b: a + b, [
            ind[g] * (pos[g:g + 1, :] + (first[g] * r).astype(F32)) for g in range(ng)])
        w_hi = [w.astype(BF16) for w in wloc]
        w_lo = [(w - h.astype(F32)).astype(BF16) for w, h in zip(wloc, w_hi)]
        wst = jnp.concatenate(w_hi + w_lo, axis=0)
        row_id = lax.broadcasted_iota(jnp.int32, (r, t), 0).astype(F32)
        for c in range(nch):
            rows = slice(c * r, (c + 1) * r)

            @pl.when(c < first[ng])
            def _():
                grp = functools.reduce(lambda a, b: a + b,
                                       [(c >= first[g]).astype(jnp.int32) for g in range(1, ng)])
                onehot = jnp.where(slot == row_id + float(c * r), 1.0, 0.0).astype(BF16)
                xg = _dot(onehot, hb_ref[...]).astype(BF16)
                wr = _nt(onehot, wst)
                acts = []
                for j in range(epg):
                    e = grp * epg + j
                    gt = _dot(xg, wg_ref[e])
                    up = _dot(xg, wu_ref[e])
                    acts.append((gt * _sigmoid(gt) * up * (wr[:, j:j + 1] + wr[:, epg + j:epg + j + 1])).astype(BF16))
                wd_g = wd_ref[pl.ds(pl.multiple_of(grp * (epg * ff), epg * ff), epg * ff), :]
                y_ref[rows, :] = _dot(jnp.concatenate(acts, axis=1), wd_g).astype(BF16)

            @pl.when(c >= first[ng])
            def _():
                y_ref[rows, :] = jnp.zeros((r, d), BF16)

        slot_col = jnp.concatenate([slot, jnp.zeros((7, t), F32)], axis=0).T[:, 0:1]
        for c in range(nch):
            col_id = lax.broadcasted_iota(jnp.int32, (t, r), 1).astype(F32) + float(c * r)
            gt_ref[:, c * r:(c + 1) * r] = jnp.where(slot_col == col_id, 1.0, 0.0).astype(BF16)
        o_ref[...] = xin + g2_ref[...] * _dot(gt_ref[...], y_ref[...])

    row = (lambda i: st.batch) if st.is_ctx else (lambda i: i // (st.seg // t))
    mod = lambda chunk: pl.BlockSpec((None, 1, d), lambda i: (row(i), 0, chunk))
    tile_spec = pl.BlockSpec((t, d), lambda i: (i, 0))
    return pl.pallas_call(
        kern, grid=(tiles,),
        in_specs=[pl.BlockSpec((t, d), lambda i: (i, 0), pipeline_mode=pl.Buffered(1)),
                  _full((1, d)), mod(SC2), mod(SH2), mod(G2), _full((ne, d)), _full((ne, 1)),
                  _resident((t, t)), _resident((ne, d, ff)), _resident((ne, d, ff)), _resident((ne * ff, d))],
        out_specs=tile_spec, out_shape=SDS((n, d), F32),
        scratch_shapes=[pltpu.VMEM((t, d), BF16), pltpu.VMEM((nch * r, d), BF16), pltpu.VMEM((t, nch * r), BF16)],
        name="moe", compiler_params=_params(("parallel",), 58))(
            x, gain, mods, mods, mods, router_wt, router_bias, tri, w_gate, w_up, wd2)


def kernel(x, c, ctx, c_ctx, w_ada, b_ada, norm_mix, norm_ffn, fourier_w_out, conv_w_in, conv_w, conv_w_out,
           mla_w_in, mla_q_norm, mla_kv_norm, mla_w_uq, mla_w_ukv, mla_q_gain, mla_k_gain, mla_w_out,
           gqa_w_in, gqa_q_gain, gqa_k_gain, gqa_sink, gqa_w_out, router_w, router_bias,
           moe_w_gate, moe_w_up, moe_w_down):
    b, l, d = x.shape
    lc = ctx.shape[1]
    depth = w_ada.shape[0]
    st_c = Stream(b * lc, lc, min(lc, 256), b, True)
    st_x = Stream(b * l, l, 512, b, False)
    streams = [st_c, st_x]
    r8 = -(-(b + 1) // 8) * 8
    cvec = jnp.concatenate([c, c_ctx[None, :], jnp.zeros((r8 - b - 1, d), F32)], axis=0)
    mods_all = _ada_call(cvec, w_ada, b_ada).reshape(depth, r8, 1, 6 * d)
    rope = _rope_tables(l)
    router_wt = router_w.T.astype(BF16)
    router_b = router_bias.reshape(-1, 1)
    xs = [ctx.reshape(b * lc, d), x.reshape(b * l, d)]
    for i in range(depth):
        kind, j = i % 4, i // 4
        ctx_next = i < depth - 1
        mods = mods_all[i]
        gain = norm_mix[i].reshape(1, d)
        if not (ctx_next or kind >= 2):
            xs[0] = None
        if kind == 0:
            ys = _fourier_layer(streams, xs, mods, gain, fourier_w_out[j])
        elif kind == 1:
            ys = _conv_layer(streams, xs, mods, gain, conv_w_in[j], conv_w[j], conv_w_out[j])
        elif kind == 2:
            ys = _mla_layer(streams, xs, mods, gain,
                            (mla_w_in[j], mla_q_norm[j], mla_kv_norm[j], mla_w_uq[j], mla_w_ukv[j],
                             mla_q_gain[j], mla_k_gain[j], mla_w_out[j]), rope, ctx_next)
        else:
            ys = _gqa_layer(streams, xs, mods, gain,
                            (gqa_w_in[j], gqa_q_gain[j], gqa_k_gain[j], gqa_sink[j], gqa_w_out[j]),
                            rope, ctx_next)
        if not ctx_next:
            ys[0] = None
        gain2 = norm_ffn[i].reshape(1, d)
        wg, wu, wd = (_cast_experts_call(w, i) for w in (moe_w_gate, moe_w_up, moe_w_down))
        xs = [None if y is None else _moe_call(st, y, mods, gain2, router_wt, router_b, wg, wu, wd)
              for st, y in zip(streams, ys)]
    return xs[1].reshape(b, l, d)
```

```python
import functools

import numpy as np
import jax
import jax.numpy as jnp
from jax import lax
from jax.experimental import pallas as pl
from jax.experimental.pallas import tpu as pltpu

F32, BF16 = jnp.float32, jnp.bfloat16
SDS = jax.ShapeDtypeStruct

EPS = 1e-6
GRID_W = 64
ROPE_THETA = 10000.0
N_FOURIER_GROUPS = 4
MLA_HEADS, MLA_Q_RANK, MLA_KV_RANK = 16, 256, 128
MLA_NOPE, MLA_ROPE, MLA_V = 128, 64, 128
MLA_SCALE = (MLA_NOPE + MLA_ROPE) ** -0.5
GQA_Q_HEADS, GQA_KV_HEADS, GQA_HEAD_DIM = 16, 4, 64
GQA_GROUP = GQA_Q_HEADS // GQA_KV_HEADS
GQA_SCALE = GQA_HEAD_DIM ** -0.5
WINDOW = 128
BLOCK = 128
N_EXPERTS, N_EXPERT_GROUPS, EXPERT_FF = 16, 4, 256
EXPERTS_PER_GROUP = N_EXPERTS // N_EXPERT_GROUPS
ROPE_DIM = 64
LOG2E = 1.4426950408889634
MAX_FIXED_SHIFT = 50.0
SHIFT_MARGIN = 1.02
TOKEN_TILE = 1024
PROJ_TILE = 512

V7X_VMEM_BYTES = 64 * 1024 * 1024
LANES = 128
SH1, SC1, G1, SH2, SC2, G2 = range(6)


def _params(sem, vmem_mb):
    return pltpu.CompilerParams(dimension_semantics=sem, vmem_limit_bytes=vmem_mb * 1024 * 1024)


def _sigmoid(v):
    return 1.0 / (1.0 + jnp.exp(-v))


def _modnorm(x, gain, sc, sh):
    ms = jnp.mean(x * x, axis=-1, keepdims=True)
    return x * lax.rsqrt(ms + EPS) * (gain * (1.0 + sc)) + sh


def _nt(a, b):
    return lax.dot_general(a, b, (((1,), (1,)), ((), ())), preferred_element_type=F32)


def _dot(a, b):
    return jnp.dot(a, b, preferred_element_type=F32)


class Stream:
    def __init__(self, n, seg, tile, batch, is_ctx):
        self.n, self.seg, self.tile, self.batch, self.is_ctx = n, seg, tile, batch, is_ctx
        self.tiles = n // tile
        self.tiles_per_seg = seg // tile

    def mod_row(self, t):
        return self.batch if self.is_ctx else t // self.tiles_per_seg

    def seg_row(self, b):
        return self.batch if self.is_ctx else b

    def with_tile(self, tile):
        return Stream(self.n, self.seg, min(tile, self.tile), self.batch, self.is_ctx)


def _mod_spec(st, chunk, d):
    return pl.BlockSpec((None, 1, d), lambda t: (st.mod_row(t), 0, chunk))


def _full(shape):
    nd = len(shape)
    return pl.BlockSpec(shape, lambda *_: (0,) * nd)


def _resident(shape):
    nd = len(shape)
    return pl.BlockSpec(shape, lambda *_: (0,) * nd, pipeline_mode=pl.Buffered(1))


def _ada_call(cvec, w_ada, b_ada):
    depth, d, d6 = w_ada.shape
    r8 = cvec.shape[0]
    tn = d6 // 4

    def kern(c_ref, w_ref, b_ref, o_ref):
        c = c_ref[...]
        s = (c * _sigmoid(c)).astype(BF16)
        o_ref[...] = _dot(s, w_ref[...].astype(BF16)) + b_ref[...]

    return pl.pallas_call(
        kern, grid=(depth, d6 // tn),
        in_specs=[pl.BlockSpec((r8, d), lambda i, j: (0, 0)),
                  pl.BlockSpec((None, d, tn), lambda i, j: (i, 0, j)),
                  pl.BlockSpec((None, 1, tn), lambda i, j: (i, 0, j))],
        out_specs=pl.BlockSpec((None, r8, tn), lambda i, j: (i, 0, j)),
        out_shape=SDS((depth, r8, d6), F32), name="ada",
        compiler_params=_params(("arbitrary", "arbitrary"), 40))(cvec, w_ada, b_ada.reshape(depth, 1, d6))


def _proj_call(st, x, mods, gain, w, name):
    n, d = x.shape
    nout = w.shape[1]
    t = st.tile

    def kern(x_ref, g_ref, sc_ref, sh_ref, w_ref, o_ref):
        h = _modnorm(x_ref[...], g_ref[...], sc_ref[...], sh_ref[...]).astype(BF16)
        o_ref[...] = _dot(h, w_ref[...]).astype(BF16)

    return pl.pallas_call(
        kern, grid=(st.tiles,),
        in_specs=[pl.BlockSpec((t, d), lambda i: (i, 0)), _full((1, d)),
                  _mod_spec(st, SC1, d), _mod_spec(st, SH1, d), _resident((d, nout))],
        out_specs=pl.BlockSpec((t, nout), lambda i: (i, 0)),
        out_shape=SDS((n, nout), BF16), name=name,
        compiler_params=_params(("parallel",), 40))(x, gain, mods, mods, w)


def _outproj_call(st, a, x, mods, w, name):
    n, d = x.shape
    k = a.shape[1]
    t = st.tile

    def kern(a_ref, x_ref, g1_ref, w_ref, o_ref):
        o_ref[...] = x_ref[...] + g1_ref[...] * _dot(a_ref[...], w_ref[...])

    return pl.pallas_call(
        kern, grid=(st.tiles,),
        in_specs=[pl.BlockSpec((t, k), lambda i: (i, 0)), pl.BlockSpec((t, d), lambda i: (i, 0)),
                  _mod_spec(st, G1, d), _resident((k, d))],
        out_specs=pl.BlockSpec((t, d), lambda i: (i, 0)),
        out_shape=SDS((n, d), F32), name=name,
        compiler_params=_params(("parallel",), 40))(a, x, mods, w)


def _dft_tables(length, radix):
    lr = length // radix
    m = np.arange(lr)[None, :, None]
    j = np.arange(radix)[:, None, None]
    nn = np.arange(lr)[None, None, :]
    ang = 2.0 * np.pi * (((radix * m + j) * nn) % length) / length
    e = np.concatenate([np.cos(ang), np.sin(ang)], axis=-1) / np.sqrt(length)
    return jnp.asarray(e, dtype=F32).astype(BF16)


def _group_dft_tables(group):
    k = np.arange(group)
    ang = 2.0 * np.pi * ((k[:, None] * k[None, :]) % group) / group
    return (jnp.asarray(np.cos(ang) / np.sqrt(group), dtype=F32),
            jnp.asarray(np.sin(ang) / np.sqrt(group), dtype=F32))


def _fourier_weight_call(w_out):
    d = w_out.shape[0]
    grp = d // N_FOURIER_GROUPS
    cg, sg = _group_dft_tables(grp)

    def kern(cg_ref, sg_ref, w_ref, o_ref):
        w = w_ref[...]
        o_ref[:, :d] = jnp.dot(cg_ref[...], w, preferred_element_type=F32,
                               precision=lax.Precision.HIGHEST).astype(BF16)
        o_ref[:, d:] = jnp.dot(sg_ref[...], w, preferred_element_type=F32,
                               precision=lax.Precision.HIGHEST).astype(BF16)

    return pl.pallas_call(
        kern, grid=(N_FOURIER_GROUPS,),
        in_specs=[_full((grp, grp)), _full((grp, grp)), pl.BlockSpec((grp, d), lambda g: (g, 0))],
        out_specs=pl.BlockSpec((grp, 2 * d), lambda g: (g, 0)),
        out_shape=SDS((d, 2 * d), BF16), name="fourier_w",
        compiler_params=_params(("arbitrary",), 32))(cg, sg, w_out)


def _radix_terms(radix, j):
    real, imag = [], []
    for q in range(radix):
        k = (j * q * (4 // radix)) % 4 if radix > 1 else 0
        if k == 0:
            real.append((1, 0, q)); imag.append((-1, 1, q))
        elif k == 1:
            real.append((-1, 1, q)); imag.append((-1, 0, q))
        elif k == 2:
            real.append((-1, 0, q)); imag.append((1, 1, q))
        else:
            real.append((1, 1, q)); imag.append((1, 0, q))
    return real, imag


def _seq_dft_call(st, p, x, mods, radix):
    n, d = x.shape
    seg = st.seg
    lr = seg // radix
    e = _dft_tables(seg, radix)
    nb = n // seg
    cblk = 2 * LANES
    ncb = d // cblk

    def kern(pc_ref, ps_ref, x_ref, e_ref, g1_ref, o_ref, v_ref, *z_refs):
        parts = (pc_ref, ps_ref)
        for j in range(radix):
            real, imag = _radix_terms(radix, j)

            def comb(terms):
                acc = None
                for sgn, part, q in terms:
                    v = parts[part][q * lr:(q + 1) * lr, :].astype(F32)
                    if acc is None:
                        acc = v if sgn > 0 else -v
                    else:
                        acc = acc + v if sgn > 0 else acc - v
                return acc

            v_ref[0:lr, :] = comb(real).astype(BF16)
            v_ref[lr:2 * lr, :] = comb(imag).astype(BF16)
            z = _dot(e_ref[j], v_ref[...])
            for k, z_ref in enumerate(z_refs):
                z_ref[pl.ds(j, lr, stride=radix), :] = z[:, k * LANES:(k + 1) * LANES]
        for k, z_ref in enumerate(z_refs):
            sl = slice(k * LANES, (k + 1) * LANES)
            o_ref[:, sl] = x_ref[:, sl] + g1_ref[:, sl] * z_ref[...]

    return pl.pallas_call(
        kern, grid=(nb, ncb),
        in_specs=[pl.BlockSpec((seg, cblk), lambda b, c: (b, c)),
                  pl.BlockSpec((seg, cblk), lambda b, c: (b, ncb + c)),
                  pl.BlockSpec((seg, cblk), lambda b, c: (b, c)),
                  _resident((radix, lr, 2 * lr)),
                  pl.BlockSpec((None, 1, cblk), lambda b, c: (st.seg_row(b), 0, G1 * ncb + c))],
        out_specs=pl.BlockSpec((seg, cblk), lambda b, c: (b, c)),
        out_shape=SDS((n, d), F32),
        scratch_shapes=[pltpu.VMEM((2 * lr, cblk), BF16)] + [pltpu.VMEM((seg, LANES), F32)] * (cblk // LANES),
        name=f"seq_dft_r{radix}",
        compiler_params=_params(("parallel", "parallel"), 40))(p, p, x, e, mods)


def _fourier_layer(streams, xs, mods, gain, w_out):
    wcs = _fourier_weight_call(w_out)
    outs = []
    for st, x in zip(streams, xs):
        if x is None:
            outs.append(None)
            continue
        p = _proj_call(st, x, mods, gain, wcs, "fourier_proj")
        outs.append(_seq_dft_call(st, p, x, mods, 4 if st.seg >= 1024 else 1))
    return outs


def _conv_in_call(st, x, mods, gain, w_in):
    n, d = x.shape
    t = st.tile

    def kern(x_ref, g_ref, sc_ref, sh_ref, w_ref, bg_ref, u_ref):
        h = _modnorm(x_ref[...], g_ref[...], sc_ref[...], sh_ref[...]).astype(BF16)
        bg_ref[...] = _dot(h, w_ref[:, 0:d]).astype(BF16)
        u_ref[...] = (_dot(h, w_ref[:, d:2 * d]) * _dot(h, w_ref[:, 2 * d:3 * d])).astype(BF16)

    return pl.pallas_call(
        kern, grid=(st.tiles,),
        in_specs=[pl.BlockSpec((t, d), lambda i: (i, 0)), _full((1, d)),
                  _mod_spec(st, SC1, d), _mod_spec(st, SH1, d), _resident((d, 3 * d))],
        out_specs=[pl.BlockSpec((t, d), lambda i: (i, 0))] * 2,
        out_shape=[SDS((n, d), BF16)] * 2, name="conv_in",
        compiler_params=_params(("parallel",), 48))(x, gain, mods, mods, w_in)


def _conv_out_call(st, bg, u, x, mods, conv_w, w_out):
    n, d = x.shape
    t = st.tile
    halo = 16
    hb = t // halo
    nhalo = n // halo
    seg = st.seg

    def kern(u_ref, up_ref, un_ref, bg_ref, x_ref, cw_ref, g1_ref, w_ref, o_ref):
        i = pl.program_id(0)
        u = u_ref[...].astype(F32)
        row = lax.broadcasted_iota(jnp.int32, (t, 1), 0)
        pos = (i * t + row) & (seg - 1)
        prev_row = up_ref[...].astype(F32)[halo - 1:halo, :]
        next_row = un_ref[...].astype(F32)[0:1, :]
        um = jnp.where(row == 0, prev_row, pltpu.roll(u, 1, axis=0))
        um = jnp.where(pos == 0, 0.0, um)
        up = jnp.where(row == t - 1, next_row, pltpu.roll(u, t - 1, axis=0))
        up = jnp.where(pos == seg - 1, 0.0, up)
        z = cw_ref[0:1, :] * um + cw_ref[1:2, :] * u + cw_ref[2:3, :] * up
        a = (bg_ref[...].astype(F32) * z).astype(BF16)
        o_ref[...] = x_ref[...] + g1_ref[...] * _dot(a, w_ref[...])

    tile_spec = pl.BlockSpec((t, d), lambda i: (i, 0))
    return pl.pallas_call(
        kern, grid=(st.tiles,),
        in_specs=[tile_spec,
                  pl.BlockSpec((halo, d), lambda i: (jnp.maximum(i * hb - 1, 0), 0)),
                  pl.BlockSpec((halo, d), lambda i: (jnp.minimum((i + 1) * hb, nhalo - 1), 0)),
                  tile_spec, tile_spec, _full((3, d)), _mod_spec(st, G1, d), _resident((d, d))],
        out_specs=tile_spec, out_shape=SDS((n, d), F32), name="conv_out",
        compiler_params=_params(("parallel",), 40))(u, u, u, bg, x, conv_w, mods, w_out)


def _conv_layer(streams, xs, mods, gain, w_in, conv_w, w_out):
    w_in_b, w_out_b = w_in.astype(BF16), w_out.astype(BF16)
    outs = []
    for st, x in zip(streams, xs):
        if x is None:
            outs.append(None)
            continue
        bg, u = _conv_in_call(st, x, mods, gain, w_in_b)
        outs.append(_conv_out_call(st, bg, u, x, mods, conv_w, w_out_b))
    return outs


def _rope_tables(length):
    q = ROPE_DIM // 4
    pos = np.arange(length)
    rc = np.stack([pos // GRID_W, pos % GRID_W], axis=1).astype(np.float32)
    inv = (ROPE_THETA ** (-np.arange(q, dtype=np.float32) / q)).astype(np.float32)
    lane = np.arange(ROPE_DIM)
    ang = rc[:, lane >> 5] * inv[lane & (q - 1)][None, :]
    sign = np.where((lane & q) == 0, -1.0, 1.0)[None, :]
    cos = np.cos(ang.astype(np.float32)).astype(np.float32)
    sin = (np.sin(ang.astype(np.float32)) * sign).astype(np.float32)
    reps = LANES // ROPE_DIM
    return jnp.asarray(np.tile(cos, (1, reps))), jnp.asarray(np.tile(sin, (1, reps)))


def _rope_partner_perm(n_heads):
    lane = np.arange(n_heads * ROPE_DIM)
    return lane ^ (ROPE_DIM // 4)


def _seg_rinv(x, seg):
    t, w = x.shape
    cols = []
    for c in range(w // LANES):
        blk = x[:, c * LANES:(c + 1) * LANES]
        sq = blk * blk
        if seg == LANES:
            cols.append(jnp.broadcast_to(
                lax.rsqrt(jnp.mean(sq, axis=-1, keepdims=True) + EPS), (t, LANES)))
        else:
            lane = lax.broadcasted_iota(jnp.int32, (t, LANES), 1)
            low = lane < seg
            lo = jnp.sum(jnp.where(low, sq, 0.0), axis=-1, keepdims=True) * (1.0 / seg)
            hi = jnp.sum(jnp.where(low, 0.0, sq), axis=-1, keepdims=True) * (1.0 / seg)
            cols.append(jnp.where(low, lax.rsqrt(lo + EPS), lax.rsqrt(hi + EPS)))
    return cols[0] if len(cols) == 1 else jnp.concatenate(cols, axis=-1)


def _tile_lanes(v, w):
    reps = w // LANES
    return v if reps == 1 else jnp.concatenate([v] * reps, axis=-1)


def _mla_proj_call(st, x, mods, gain, wts, rope, need_q):
    n, d = x.shape
    t = st.tile
    h = MLA_HEADS
    wn, wr = h * MLA_NOPE, h * MLA_ROPE
    (w_in, q_norm, kv_norm, w_uq, w_ukv, qg_n, qg_r, qg_rp, kg_n, kg_r, kg_rp) = wts
    use_rope = rope is not None

    def kern(*refs):
        refs = list(refs)
        x_ref, g_ref, sc_ref, sh_ref, win_ref, qn_ref_, kvn_ref_, wuq_ref, wukv_ref = refs[:9]
        qgn_ref, qgr_ref, qgrp_ref, kgn_ref, kgr_ref, kgrp_ref = refs[9:15]
        rest = refs[15:]
        if use_rope:
            cos_ref, sin_ref = rest[:2]
            rest = rest[2:]
        if need_q:
            oqn_ref, oqr_ref = rest[:2]
            rest = rest[2:]
        okn_ref, okr_ref, ov_ref = rest
        hm = _modnorm(x_ref[...], g_ref[...], sc_ref[...], sh_ref[...]).astype(BF16)
        ck = _dot(hm, win_ref[...])
        kr2 = ck[:, MLA_Q_RANK + MLA_KV_RANK:]
        kr_rinv = _seg_rinv(kr2, MLA_ROPE)[:, 0:MLA_ROPE]
        kr_raw = kr2[:, 0:MLA_ROPE] * kgr_ref[...]
        if use_rope:
            kr_par = kr2[:, MLA_ROPE:] * kgrp_ref[...]
            kr = kr_rinv * (kr_raw * cos_ref[:, 0:MLA_ROPE] + kr_par * sin_ref[:, 0:MLA_ROPE])
        else:
            kr = kr_rinv * kr_raw
        okr_ref[...] = kr.astype(BF16)
        ckv = ck[:, MLA_Q_RANK:MLA_Q_RANK + MLA_KV_RANK]
        ckv = (ckv * _seg_rinv(ckv, LANES) * kvn_ref_[...]).astype(BF16)
        kv = _dot(ckv, wukv_ref[...])
        kn = kv[:, 0:wn]
        okn_ref[...] = (kn * _seg_rinv(kn, MLA_NOPE) * kgn_ref[...]).astype(BF16)
        ov_ref[...] = kv[:, wn:].astype(BF16)
        if need_q:
            cq = ck[:, 0:MLA_Q_RANK]
            rq = lax.rsqrt(jnp.mean(cq * cq, axis=-1, keepdims=True) + EPS)
            cq = (cq * rq * qn_ref_[...]).astype(BF16)
            q = _dot(cq, wuq_ref[...])
            qn = q[:, 0:wn]
            oqn_ref[...] = (qn * _seg_rinv(qn, MLA_NOPE) * (qgn_ref[...] * (MLA_SCALE * LOG2E))).astype(BF16)
            qr_raw = q[:, wn:wn + wr]
            rinv = _seg_rinv(qr_raw, MLA_ROPE) * (MLA_SCALE * LOG2E)
            if use_rope:
                qr_par = q[:, wn + wr:]
                cos = _tile_lanes(cos_ref[...], wr)
                sin = _tile_lanes(sin_ref[...], wr)
                qr = rinv * (qr_raw * qgr_ref[...] * cos + qr_par * qgrp_ref[...] * sin)
            else:
                qr = rinv * (qr_raw * qgr_ref[...])
            oqr_ref[...] = qr.astype(BF16)

    tile = lambda w: pl.BlockSpec((t, w), lambda i: (i, 0))
    in_specs = [tile(d), _full((1, d)), _mod_spec(st, SC1, d), _mod_spec(st, SH1, d),
                _resident(w_in.shape), _full(q_norm.shape), _full(kv_norm.shape),
                _resident(w_uq.shape), _resident(w_ukv.shape),
                _full(qg_n.shape), _full(qg_r.shape), _full(qg_rp.shape),
                _full(kg_n.shape), _full(kg_r.shape), _full(kg_rp.shape)]
    args = [x, gain, mods, mods, w_in, q_norm, kv_norm, w_uq, w_ukv, qg_n, qg_r, qg_rp, kg_n, kg_r, kg_rp]
    if use_rope:
        tps = st.tiles_per_seg
        in_specs += [pl.BlockSpec((t, LANES), lambda i: (i % tps, 0))] * 2
        args += list(rope)
    out_specs, out_shape = [], []
    if need_q:
        out_specs += [tile(wn), tile(wr)]
        out_shape += [SDS((n, wn), BF16), SDS((n, wr), BF16)]
    out_specs += [tile(wn), tile(MLA_ROPE), tile(h * MLA_V)]
    out_shape += [SDS((n, wn), BF16), SDS((n, MLA_ROPE), BF16), SDS((n, h * MLA_V), BF16)]
    outs = pl.pallas_call(
        kern, grid=(st.tiles,), in_specs=in_specs, out_specs=out_specs, out_shape=out_shape,
        name="mla_proj", compiler_params=_params(("parallel",), 48))(*args)
    if need_q:
        return tuple(outs)
    return (None, None) + tuple(outs)


def _mla_attn_fast_call(batch, lq, tq, q, kv_sets, shift):
    qn, qr = q
    hp = 2
    n_hp = MLA_HEADS // hp
    nq = lq // tq
    nsets = len(kv_sets)
    lengths = [s[3] for s in kv_sets]
    lk = sum(lengths)
    kw = 2 * LANES
    pad = kw - MLA_NOPE - MLA_ROPE

    def kern(*refs):
        sh_ref, qn_ref, qr_ref = refs[:3]
        sets = [refs[3 + 3 * s:6 + 3 * s] for s in range(nsets)]
        o_ref, kq_ref, vq_ref = refs[-3:]

        @pl.when(pl.program_id(2) == 0)
        def _():
            lane_k = lax.broadcasted_iota(jnp.int32, (lk, pad), 1)
            lane_v = lax.broadcasted_iota(jnp.int32, (lk, kw - MLA_V), 1)
            for hh in range(hp):
                off = 0
                for (kn_ref, kr_ref, v_ref), length in zip(sets, lengths):
                    kq_ref[hh, off:off + length, 0:MLA_NOPE] = kn_ref[:, hh * MLA_NOPE:(hh + 1) * MLA_NOPE]
                    kq_ref[hh, off:off + length, MLA_NOPE:MLA_NOPE + MLA_ROPE] = kr_ref[...]
                    vq_ref[hh, off:off + length, 0:MLA_V] = v_ref[:, hh * MLA_V:(hh + 1) * MLA_V]
                    off += length
                kq_ref[hh, :, MLA_NOPE + MLA_ROPE:] = jnp.where(lane_k == 0, -sh_ref[:, 0:pad], 0.0).astype(BF16)
                vq_ref[hh, :, MLA_V:] = jnp.where(lane_v == 0, 1.0, 0.0).astype(BF16)

        lane_q = lax.broadcasted_iota(jnp.int32, (tq, pad), 1)
        one = jnp.where(lane_q == 0, 1.0, 0.0).astype(BF16)
        for hh in range(hp):
            qc = jnp.concatenate([qn_ref[:, hh * MLA_NOPE:(hh + 1) * MLA_NOPE],
                                  qr_ref[:, hh * MLA_ROPE:(hh + 1) * MLA_ROPE], one], axis=1)
            p = jnp.exp2(_nt(qc, kq_ref[hh])).astype(BF16)
            acc = _dot(p, vq_ref[hh])
            o_ref[:, hh * MLA_V:(hh + 1) * MLA_V] = (acc[:, 0:MLA_V] / acc[:, MLA_V:MLA_V + 1]).astype(BF16)

    in_specs = [_full((1, LANES)),
                pl.BlockSpec((tq, hp * MLA_NOPE), lambda b, h, i: (b * nq + i, h)),
                pl.BlockSpec((tq, hp * MLA_ROPE), lambda b, h, i: (b * nq + i, h))]
    args = [shift, qn, qr]
    for kn, kr, v, length in kv_sets:
        in_specs += [pl.BlockSpec((length, hp * MLA_NOPE), lambda b, h, i: (b, h)),
                     pl.BlockSpec((length, MLA_ROPE), lambda b, h, i: (b, 0)),
                     pl.BlockSpec((length, hp * MLA_V), lambda b, h, i: (b, h))]
        args += [kn, kr, v]
    return pl.pallas_call(
        kern, grid=(batch, n_hp, nq), in_specs=in_specs,
        out_specs=pl.BlockSpec((tq, hp * MLA_V), lambda b, h, i: (b * nq + i, h)),
        out_shape=SDS((batch * lq, MLA_HEADS * MLA_V), BF16),
        scratch_shapes=[pltpu.VMEM((hp, lk, kw), BF16), pltpu.VMEM((hp, lk, kw), BF16)],
        name="mla_attn_fast",
        compiler_params=_params(("parallel", "parallel", "arbitrary"), 56))(*args)


def _mla_attn_call(batch, lq, tq, q, kv_sets):
    qn, qr = q
    hp = 2
    n_hp = MLA_HEADS // hp
    nq = lq // tq
    nsets = len(kv_sets)

    def kern(*refs):
        qn_ref, qr_ref = refs[:2]
        o_ref = refs[-1]
        sets = [refs[2 + 3 * s:5 + 3 * s] for s in range(nsets)]
        for hh in range(hp):
            qn_h = qn_ref[:, hh * MLA_NOPE:(hh + 1) * MLA_NOPE]
            qr_h = qr_ref[:, hh * MLA_ROPE:(hh + 1) * MLA_ROPE]
            scores = [_nt(qn_h, kn_ref[:, hh * MLA_NOPE:(hh + 1) * MLA_NOPE]) + _nt(qr_h, kr_ref[...])
                      for kn_ref, kr_ref, _ in sets]
            m = functools.reduce(jnp.maximum, [jnp.max(s, axis=-1, keepdims=True) for s in scores])
            ps = [jnp.exp2(s - m) for s in scores]
            den = functools.reduce(lambda a, b: a + b, [jnp.sum(p, axis=-1, keepdims=True) for p in ps])
            acc = functools.reduce(lambda a, b: a + b, [
                _dot(p.astype(BF16), v_ref[:, hh * MLA_V:(hh + 1) * MLA_V])
                for p, (_, _, v_ref) in zip(ps, sets)])
            o_ref[:, hh * MLA_V:(hh + 1) * MLA_V] = (acc / den).astype(BF16)

    in_specs = [pl.BlockSpec((tq, hp * MLA_NOPE), lambda b, h, i: (b * nq + i, h)),
                pl.BlockSpec((tq, hp * MLA_ROPE), lambda b, h, i: (b * nq + i, h))]
    args = [qn, qr]
    for kn, kr, v, length in kv_sets:
        in_specs += [pl.BlockSpec((length, hp * MLA_NOPE), lambda b, h, i: (b, h)),
                     pl.BlockSpec((length, MLA_ROPE), lambda b, h, i: (b, 0)),
                     pl.BlockSpec((length, hp * MLA_V), lambda b, h, i: (b, h))]
        args += [kn, kr, v]
    return pl.pallas_call(
        kern, grid=(batch, n_hp, nq), in_specs=in_specs,
        out_specs=pl.BlockSpec((tq, hp * MLA_V), lambda b, h, i: (b * nq + i, h)),
        out_shape=SDS((batch * lq, MLA_HEADS * MLA_V), BF16), name="mla_attn",
        compiler_params=_params(("parallel", "parallel", "arbitrary"), 56))(*args)


def _mla_layer(streams, xs, mods, gain, p, rope, ctx_next):
    st_c, st_x = streams
    xc, xx = xs
    h = MLA_HEADS
    w_in, q_norm, kv_norm, w_uq, w_ukv, q_gain, k_gain, w_out = p
    par = _rope_partner_perm(1)
    kr_cols = w_in[:, MLA_Q_RANK + MLA_KV_RANK:]
    w_in_p = jnp.concatenate([w_in, kr_cols[:, par]], axis=1).astype(BF16)
    wq = w_uq.reshape(MLA_Q_RANK, h, MLA_NOPE + MLA_ROPE)
    wq_n = wq[:, :, :MLA_NOPE].reshape(MLA_Q_RANK, h * MLA_NOPE)
    wq_r = wq[:, :, MLA_NOPE:]
    w_uq_p = jnp.concatenate([wq_n, wq_r.reshape(MLA_Q_RANK, h * MLA_ROPE),
                              wq_r[:, :, par].reshape(MLA_Q_RANK, h * MLA_ROPE)], axis=1).astype(BF16)
    wkv = w_ukv.reshape(MLA_KV_RANK, h, MLA_NOPE + MLA_V)
    w_ukv_p = jnp.concatenate([wkv[:, :, :MLA_NOPE].reshape(MLA_KV_RANK, h * MLA_NOPE),
                               wkv[:, :, MLA_NOPE:].reshape(MLA_KV_RANK, h * MLA_V)], axis=1).astype(BF16)
    qg_r = q_gain[MLA_NOPE:]
    kg_r = k_gain[MLA_NOPE:]
    wts = (w_in_p, q_norm.reshape(1, -1), kv_norm.reshape(1, -1), w_uq_p, w_ukv_p,
           jnp.tile(q_gain[:MLA_NOPE], h).reshape(1, -1), jnp.tile(qg_r, h).reshape(1, -1),
           jnp.tile(qg_r[par], h).reshape(1, -1), jnp.tile(k_gain[:MLA_NOPE], h).reshape(1, -1),
           kg_r.reshape(1, -1), kg_r[par].reshape(1, -1))
    w_out_b = w_out.astype(BF16)
    qn_c, qr_c, kn_c, kr_c, v_c = _mla_proj_call(st_c.with_tile(PROJ_TILE), xc, mods, gain, wts, None, ctx_next)
    qn_x, qr_x, kn_x, kr_x, v_x = _mla_proj_call(st_x.with_tile(PROJ_TILE), xx, mods, gain, wts, rope, True)
    b = st_x.batch
    amax = lambda g: jnp.max(jnp.abs(g))
    bound = LOG2E * MLA_SCALE * (MLA_NOPE * amax(q_gain[:MLA_NOPE]) * amax(k_gain[:MLA_NOPE])
                                 + MLA_ROPE * amax(qg_r) * amax(kg_r))
    shift = bound * SHIFT_MARGIN + 1.0
    shift_row = jnp.full((1, LANES), shift, F32)

    def attend(lq, tq, q, kv_sets):
        return lax.cond(shift <= MAX_FIXED_SHIFT,
                        lambda: _mla_attn_fast_call(b, lq, tq, q, kv_sets, shift_row),
                        lambda: _mla_attn_call(b, lq, tq, q, kv_sets))

    a_x = attend(st_x.seg, min(st_x.seg, 1024), (qn_x, qr_x), [(kn_c, kr_c, v_c, st_c.seg), (kn_x, kr_x, v_x, st_x.seg)])
    out_x = _outproj_call(st_x, a_x, xx, mods, w_out_b, "mla_out")
    out_c = None
    if ctx_next:
        a_c = attend(st_c.seg, st_c.seg, (qn_c, qr_c), [(kn_c, kr_c, v_c, st_c.seg)])
        out_c = _outproj_call(st_c, a_c, xc, mods, w_out_b, "mla_out")
    return [out_c, out_x]


def _gqa_proj_call(st, x, mods, gain, wts, rope, need_q):
    n, d = x.shape
    t = st.tile
    wq, wk = GQA_Q_HEADS * GQA_HEAD_DIM, GQA_KV_HEADS * GQA_HEAD_DIM
    w_in, qg, qgp, kg, kgp = wts
    use_rope = rope is not None

    def kern(*refs):
        refs = list(refs)
        x_ref, g_ref, sc_ref, sh_ref, w_ref, qg_ref, qgp_ref, kg_ref, kgp_ref = refs[:9]
        rest = refs[9:]
        if use_rope:
            cos_ref, sin_ref = rest[:2]
            rest = rest[2:]
        if need_q:
            oq_ref = rest[0]
            rest = rest[1:]
        ok_ref, ov_ref = rest
        hm = _modnorm(x_ref[...], g_ref[...], sc_ref[...], sh_ref[...]).astype(BF16)
        kvp = _dot(hm, w_ref[:, 0:3 * wk])
        k_raw = kvp[:, 0:wk]
        k_rinv = _seg_rinv(k_raw, GQA_HEAD_DIM)
        ov_ref[...] = kvp[:, wk:2 * wk].astype(BF16)
        if use_rope:
            cos_k, sin_k = _tile_lanes(cos_ref[...], wk), _tile_lanes(sin_ref[...], wk)
            k = k_rinv * (k_raw * kg_ref[...] * cos_k + kvp[:, 2 * wk:] * kgp_ref[...] * sin_k)
        else:
            k = k_rinv * (k_raw * kg_ref[...])
        ok_ref[...] = k.astype(BF16)
        if need_q:
            qp = _dot(hm, w_ref[:, 3 * wk:])
            q_raw = qp[:, 0:wq]
            rinv = _seg_rinv(q_raw, GQA_HEAD_DIM) * (GQA_SCALE * LOG2E)
            if use_rope:
                cos_q, sin_q = _tile_lanes(cos_ref[...], wq), _tile_lanes(sin_ref[...], wq)
                q = rinv * (q_raw * qg_ref[...] * cos_q + qp[:, wq:] * qgp_ref[...] * sin_q)
            else:
                q = rinv * (q_raw * qg_ref[...])
            oq_ref[...] = q.astype(BF16)

    tile = lambda w: pl.BlockSpec((t, w), lambda i: (i, 0))
    in_specs = [tile(d), _full((1, d)), _mod_spec(st, SC1, d), _mod_spec(st, SH1, d),
                _resident(w_in.shape), _full(qg.shape), _full(qgp.shape), _full(kg.shape), _full(kgp.shape)]
    args = [x, gain, mods, mods, w_in, qg, qgp, kg, kgp]
    if use_rope:
        tps = st.tiles_per_seg
        in_specs += [pl.BlockSpec((t, LANES), lambda i: (i % tps, 0))] * 2
        args += list(rope)
    out_specs, out_shape = [], []
    if need_q:
        out_specs.append(tile(wq))
        out_shape.append(SDS((n, wq), BF16))
    out_specs += [tile(wk), tile(wk)]
    out_shape += [SDS((n, wk), BF16)] * 2
    outs = pl.pallas_call(
        kern, grid=(st.tiles,), in_specs=in_specs, out_specs=out_specs, out_shape=out_shape,
        name="gqa_proj", compiler_params=_params(("parallel",), 48))(*args)
    return tuple(outs) if need_q else (None,) + tuple(outs)


def _gqa_window_call(batch, length, lc, q, k, v, kc, vc, sink):
    nb = length // BLOCK
    hd = GQA_HEAD_DIM
    wq, wk = GQA_Q_HEADS * hd, GQA_KV_HEADS * hd
    rows = GQA_GROUP * BLOCK

    def kern(sink_ref, q_ref, kp_ref, k0_ref, kn_ref, vp_ref, v0_ref, vn_ref, kc_ref, vc_ref, o_ref):
        nblk = pl.program_id(1)
        r = lax.broadcasted_iota(jnp.int32, (rows, 3 * BLOCK), 0) & (BLOCK - 1)
        c = lax.broadcasted_iota(jnp.int32, (rows, 3 * BLOCK), 1)
        valid = (c >= r + BLOCK - WINDOW) & (c <= r + BLOCK + WINDOW)
        valid = valid & ((c >= BLOCK) | (nblk > 0)) & ((c < 2 * BLOCK) | (nblk < nb - 1))
        hrow = lax.broadcasted_iota(jnp.int32, (rows, 1), 0) // BLOCK
        for g in range(GQA_KV_HEADS):
            sl = slice(g * hd, (g + 1) * hd)
            qg = jnp.concatenate([q_ref[:, (g * GQA_GROUP + j) * hd:(g * GQA_GROUP + j + 1) * hd]
                                  for j in range(GQA_GROUP)], axis=0)
            kband = jnp.concatenate([kp_ref[:, sl], k0_ref[:, sl], kn_ref[:, sl]], axis=0)
            vband = jnp.concatenate([vp_ref[:, sl], v0_ref[:, sl], vn_ref[:, sl]], axis=0)
            s_c = _nt(qg, kc_ref[:, sl])
            s_b = jnp.where(valid, _nt(qg, kband), -1e30)
            snk = jnp.zeros((rows, 1), F32)
            for j in range(GQA_GROUP):
                snk = jnp.where(hrow == j, sink_ref[g * GQA_GROUP + j], snk)
            m = jnp.maximum(jnp.maximum(jnp.max(s_c, axis=-1, keepdims=True),
                                        jnp.max(s_b, axis=-1, keepdims=True)), snk)
            p_c = jnp.exp2(s_c - m)
            p_b = jnp.exp2(s_b - m)
            den = (jnp.sum(p_c, axis=-1, keepdims=True) + jnp.sum(p_b, axis=-1, keepdims=True)
                   + jnp.exp2(snk - m))
            o = (_dot(p_c.astype(BF16), vc_ref[:, sl]) + _dot(p_b.astype(BF16), vband)) / den
            o_ref[:, g * GQA_GROUP * hd:(g + 1) * GQA_GROUP * hd] = jnp.concatenate(
                [o[j * BLOCK:(j + 1) * BLOCK, :] for j in range(GQA_GROUP)], axis=-1).astype(BF16)

    blk = lambda f: pl.BlockSpec((BLOCK, wk), f)
    prev_ = lambda b, i: (b * nb + jnp.maximum(i - 1, 0), 0)
    cur_ = lambda b, i: (b * nb + i, 0)
    next_ = lambda b, i: (b * nb + jnp.minimum(i + 1, nb - 1), 0)
    ctx_spec = pl.BlockSpec((lc, wk), lambda b, i: (b, 0))
    return pl.pallas_call(
        kern, grid=(batch, nb),
        in_specs=[pl.BlockSpec(memory_space=pltpu.SMEM),
                  pl.BlockSpec((BLOCK, wq), cur_), blk(prev_), blk(cur_), blk(next_),
                  blk(prev_), blk(cur_), blk(next_), ctx_spec, ctx_spec],
        out_specs=pl.BlockSpec((BLOCK, wq), cur_),
        out_shape=SDS((batch * length, wq), BF16), name="gqa_window",
        compiler_params=_params(("parallel", "arbitrary"), 40))(sink, q, k, k, k, v, v, v, kc, vc)


def _gqa_window_fast_call(batch, length, lc, q, k, v, kc, vc, sink2, shift):
    nb = length // BLOCK
    hd = GQA_HEAD_DIM
    wq, wk = GQA_Q_HEADS * hd, GQA_KV_HEADS * hd
    rows = GQA_GROUP * BLOCK

    def kern(sink_ref, shift_ref, q_ref, kp_ref, k0_ref, kn_ref, vp_ref, v0_ref, vn_ref, kc_ref, vc_ref, o_ref):
        nblk = pl.program_id(1)
        r = lax.broadcasted_iota(jnp.int32, (rows, 3 * BLOCK), 0) & (BLOCK - 1)
        c = lax.broadcasted_iota(jnp.int32, (rows, 3 * BLOCK), 1)
        valid = (c >= r + BLOCK - WINDOW) & (c <= r + BLOCK + WINDOW)
        valid = valid & ((c >= BLOCK) | (nblk > 0)) & ((c < 2 * BLOCK) | (nblk < nb - 1))
        hrow = lax.broadcasted_iota(jnp.int32, (rows, 1), 0) // BLOCK
        lane0_q = lax.broadcasted_iota(jnp.int32, (BLOCK, hd), 1) == 0

        def with_one(x):
            lane0 = lax.broadcasted_iota(jnp.int32, (x.shape[0], hd), 1) == 0
            return jnp.concatenate([x, jnp.where(lane0, 1.0, 0.0).astype(BF16)], axis=1)

        for g in range(GQA_KV_HEADS):
            sl = slice(g * hd, (g + 1) * hd)
            heads = [g * GQA_GROUP + j for j in range(GQA_GROUP)]
            qg = jnp.concatenate(
                [jnp.concatenate([q_ref[:, h * hd:(h + 1) * hd],
                                  jnp.where(lane0_q, -shift_ref[h], 0.0).astype(BF16)], axis=1)
                 for h in heads], axis=0)
            kband = with_one(jnp.concatenate([kp_ref[:, sl], k0_ref[:, sl], kn_ref[:, sl]], axis=0))
            vband = with_one(jnp.concatenate([vp_ref[:, sl], v0_ref[:, sl], vn_ref[:, sl]], axis=0))
            p_c = jnp.exp2(_nt(qg, with_one(kc_ref[:, sl]))).astype(BF16)
            p_b = jnp.where(valid, jnp.exp2(_nt(qg, kband)), 0.0).astype(BF16)
            acc = _dot(p_c, with_one(vc_ref[:, sl])) + _dot(p_b, vband)
            snk = jnp.zeros((rows, 1), F32)
            for j, h in enumerate(heads):
                snk = jnp.where(hrow == j, sink_ref[h] - shift_ref[h], snk)
            o = acc[:, 0:hd] / (acc[:, hd:hd + 1] + jnp.exp2(snk))
            o_ref[:, g * GQA_GROUP * hd:(g + 1) * GQA_GROUP * hd] = jnp.concatenate(
                [o[j * BLOCK:(j + 1) * BLOCK, :] for j in range(GQA_GROUP)], axis=-1).astype(BF16)

    blk = lambda f: pl.BlockSpec((BLOCK, wk), f)
    prev_ = lambda b, i: (b * nb + jnp.maximum(i - 1, 0), 0)
    cur_ = lambda b, i: (b * nb + i, 0)
    next_ = lambda b, i: (b * nb + jnp.minimum(i + 1, nb - 1), 0)
    ctx_spec = pl.BlockSpec((lc, wk), lambda b, i: (b, 0))
    smem = pl.BlockSpec(memory_space=pltpu.SMEM)
    return pl.pallas_call(
        kern, grid=(batch, nb),
        in_specs=[smem, smem, pl.BlockSpec((BLOCK, wq), cur_), blk(prev_), blk(cur_), blk(next_),
                  blk(prev_), blk(cur_), blk(next_), ctx_spec, ctx_spec],
        out_specs=pl.BlockSpec((BLOCK, wq), cur_),
        out_shape=SDS((batch * length, wq), BF16), name="gqa_window_fast",
        compiler_params=_params(("parallel", "arbitrary"), 40))(sink2, shift, q, k, k, k, v, v, v, kc, vc)


def _gqa_layer(streams, xs, mods, gain, p, rope, ctx_next):
    assert not ctx_next, "the windowed-GQA mixer is only implemented as the last layer"
    st_c, st_x = streams
    xc, xx = xs
    w_in, q_gain, k_gain, sink, w_out = p
    wq, wk = GQA_Q_HEADS * GQA_HEAD_DIM, GQA_KV_HEADS * GQA_HEAD_DIM
    w_q, w_k, w_v = w_in[:, :wq], w_in[:, wq:wq + wk], w_in[:, wq + wk:]
    w_in_p = jnp.concatenate([w_k, w_v, w_k[:, _rope_partner_perm(GQA_KV_HEADS)],
                              w_q, w_q[:, _rope_partner_perm(GQA_Q_HEADS)]], axis=1).astype(BF16)
    par = _rope_partner_perm(1)
    wts = (w_in_p, jnp.tile(q_gain, GQA_Q_HEADS).reshape(1, -1), jnp.tile(q_gain[par], GQA_Q_HEADS).reshape(1, -1),
           jnp.tile(k_gain, GQA_KV_HEADS).reshape(1, -1), jnp.tile(k_gain[par], GQA_KV_HEADS).reshape(1, -1))
    _, kc, vc = _gqa_proj_call(st_c.with_tile(PROJ_TILE), xc, mods, gain, wts, None, False)
    q, k, v = _gqa_proj_call(st_x.with_tile(PROJ_TILE), xx, mods, gain, wts, rope, True)
    sink2 = sink * LOG2E
    bound = LOG2E * GQA_SCALE * GQA_HEAD_DIM * jnp.max(jnp.abs(q_gain)) * jnp.max(jnp.abs(k_gain))
    shift = jnp.maximum(bound * SHIFT_MARGIN + 1.0, sink2).astype(BF16).astype(F32)
    args = (st_x.batch, st_x.seg, st_c.seg, q, k, v, kc, vc, sink2)
    a = lax.cond(jnp.max(shift) <= MAX_FIXED_SHIFT,
                 lambda: _gqa_window_fast_call(*args, shift),
                 lambda: _gqa_window_call(*args))
    return [None, _outproj_call(st_x, a, xx, mods, w_out.astype(BF16), "gqa_out")]


def _cast_experts_call(w, layer):
    _, ne, a, b = w.shape
    eb = 4

    def kern(w_ref, o_ref):
        o_ref[...] = w_ref[...].astype(BF16)

    return pl.pallas_call(
        kern, grid=(ne // eb,),
        in_specs=[pl.BlockSpec((None, eb, a, b), lambda i: (layer, i, 0, 0))],
        out_specs=pl.BlockSpec((eb, a, b), lambda i: (i, 0, 0)),
        out_shape=SDS((ne, a, b), BF16), name="cast_experts",
        compiler_params=_params(("parallel",), 32))(w)


def _route(logits_t, bias_col):
    scores = _sigmoid(logits_t)
    biased = scores + bias_col
    rows = [biased[e:e + 1, :] for e in range(N_EXPERTS)]
    srow = [scores[e:e + 1, :] for e in range(N_EXPERTS)]
    epg = EXPERTS_PER_GROUP
    gscore = []
    for g in range(N_EXPERT_GROUPS):
        v = rows[g * epg:(g + 1) * epg]
        pair = [v[a] + v[b] for a in range(epg) for b in range(a + 1, epg)]
        gscore.append(functools.reduce(jnp.maximum, pair))
    ind, wloc = [], [None] * epg
    for g in range(N_EXPERT_GROUPS):
        best = None
        for g2 in range(N_EXPERT_GROUPS):
            if g2 == g:
                continue
            cnd = gscore[g] > gscore[g2] if g2 < g else gscore[g] >= gscore[g2]
            best = cnd if best is None else best & cnd
        ind.append(jnp.where(best, 1.0, 0.0))
        for j in range(epg):
            e = g * epg + j
            rank = None
            for e2 in range(g * epg, (g + 1) * epg):
                if e2 == e:
                    continue
                ahead = rows[e2] >= rows[e] if e2 < e else rows[e2] > rows[e]
                one = jnp.where(ahead, 1.0, 0.0)
                rank = one if rank is None else rank + one
            w = jnp.where(best & (rank < 2.0), srow[e], 0.0)
            wloc[j] = w if wloc[j] is None else wloc[j] + w
    den = functools.reduce(lambda a, b: a + b, wloc)
    return ind, [w / den for w in wloc]


MOE_TILE = 1024
MOE_CHUNK = 256


def _moe_call(st, x, mods, gain, router_wt, router_bias, w_gate, w_up, w_down):
    n, d = x.shape
    t = min(MOE_TILE, n)
    r = MOE_CHUNK
    ne, _, ff = w_gate.shape
    epg, ng = EXPERTS_PER_GROUP, N_EXPERT_GROUPS
    nch = t // r + ng - 1
    tiles = n // t
    wd2 = w_down.reshape(ne * ff, d)

    def kern(x_ref, g_ref, sc_ref, sh_ref, g2_ref, rw_ref, rb_ref, wg_ref, wu_ref, wd_ref, o_ref,
             hb_ref, oh_ref, xg_ref, y_ref):
        xin = x_ref[...]
        hb_ref[...] = _modnorm(xin, g_ref[...], sc_ref[...], sh_ref[...]).astype(BF16)
        ind, wloc = _route(_nt(rw_ref[...], hb_ref[...]), rb_ref[...])
        ind8 = jnp.concatenate(ind + [jnp.zeros((8 - ng, t), F32)], axis=0)
        lane = lax.broadcasted_iota(jnp.int32, (8, t), 1)
        csum, step = ind8, 1
        while step < t:
            csum = csum + jnp.where(lane >= step, pltpu.roll(csum, step, axis=1), 0.0)
            step *= 2
        pos = csum - ind8
        count = [jnp.sum(ind[g]).astype(jnp.int32) for g in range(ng)]
        first = [jnp.int32(0)]
        for g in range(ng):
            first.append(first[g] + (count[g] + (r - 1)) // r)
        slot = functools.reduce(lambda a, b: a + b, [
            ind[g] * (pos[g:g + 1, :] + (first[g] * r).astype(F32)) for g in range(ng)])
        w_hi = [w.astype(BF16) for w in wloc]
        w_lo = [(w - h.astype(F32)).astype(BF16) for w, h in zip(wloc, w_hi)]
        wst = jnp.concatenate(w_hi + w_lo, axis=0)
        row_id = lax.broadcasted_iota(jnp.int32, (r, t), 0).astype(F32)
        for c in range(nch):
            oh_ref[c * r:(c + 1) * r, :] = jnp.where(slot == row_id + float(c * r), 1.0, 0.0).astype(BF16)
        xg_ref[...] = _dot(oh_ref[...], hb_ref[...]).astype(BF16)
        wr_all = _nt(oh_ref[...], wst)
        for c in range(nch):
            rows = slice(c * r, (c + 1) * r)

            @pl.when(c < first[ng])
            def _():
                grp = functools.reduce(lambda a, b: a + b,
                                       [(c >= first[g]).astype(jnp.int32) for g in range(1, ng)])
                xg = xg_ref[rows, :]
                wr = wr_all[c * r:(c + 1) * r, :]
                acts = []
                for j in range(epg):
                    e = grp * epg + j
                    gt = _dot(xg, wg_ref[e])
                    up = _dot(xg, wu_ref[e])
                    acts.append((gt * _sigmoid(gt) * up * (wr[:, j:j + 1] + wr[:, epg + j:epg + j + 1])).astype(BF16))
                wd_g = wd_ref[pl.ds(pl.multiple_of(grp * (epg * ff), epg * ff), epg * ff), :]
                y_ref[rows, :] = _dot(jnp.concatenate(acts, axis=1), wd_g).astype(BF16)

            @pl.when(c >= first[ng])
            def _():
                y_ref[rows, :] = jnp.zeros((r, d), BF16)

        back = lax.dot_general(oh_ref[...], y_ref[...], (((0,), (0,)), ((), ())), preferred_element_type=F32)
        o_ref[...] = xin + g2_ref[...] * back

    row = (lambda i: st.batch) if st.is_ctx else (lambda i: i // (st.seg // t))
    mod = lambda chunk: pl.BlockSpec((None, 1, d), lambda i: (row(i), 0, chunk))
    return pl.pallas_call(
        kern, grid=(tiles,),
        in_specs=[pl.BlockSpec((t, d), lambda i: (i, 0), pipeline_mode=pl.Buffered(1)),
                  _full((1, d)), mod(SC2), mod(SH2), mod(G2), _full((ne, d)), _full((ne, 1)),
                  _resident((ne, d, ff)), _resident((ne, d, ff)), _resident((ne * ff, d))],
        out_specs=pl.BlockSpec((t, d), lambda i: (i, 0), pipeline_mode=pl.Buffered(1)),
        out_shape=SDS((n, d), F32),
        scratch_shapes=[pltpu.VMEM((t, d), BF16), pltpu.VMEM((nch * r, t), BF16),
                        pltpu.VMEM((nch * r, d), BF16), pltpu.VMEM((nch * r, d), BF16)],
        name="moe", compiler_params=_params(("parallel",), 58))(
            x, gain, mods, mods, mods, router_wt, router_bias, w_gate, w_up, wd2)


def kernel(x, c, ctx, c_ctx, w_ada, b_ada, norm_mix, norm_ffn, fourier_w_out, conv_w_in, conv_w, conv_w_out,
           mla_w_in, mla_q_norm, mla_kv_norm, mla_w_uq, mla_w_ukv, mla_q_gain, mla_k_gain, mla_w_out,
           gqa_w_in, gqa_q_gain, gqa_k_gain, gqa_sink, gqa_w_out, router_w, router_bias,
           moe_w_gate, moe_w_up, moe_w_down):
    b, l, d = x.shape
    lc = ctx.shape[1]
    depth = w_ada.shape[0]
    st_c = Stream(b * lc, lc, min(b * lc, TOKEN_TILE), b, True)
    st_x = Stream(b * l, l, min(l, TOKEN_TILE), b, False)
    streams = [st_c, st_x]
    r8 = -(-(b + 1) // 8) * 8
    cvec = jnp.concatenate([c, c_ctx[None, :], jnp.zeros((r8 - b - 1, d), F32)], axis=0)
    mods_all = _ada_call(cvec, w_ada, b_ada).reshape(depth, r8, 1, 6 * d)
    rope = _rope_tables(l)
    router_wt = router_w.T.astype(BF16)
    router_b = router_bias.reshape(-1, 1)
    xs = [ctx.reshape(b * lc, d), x.reshape(b * l, d)]
    for i in range(depth):
        kind, j = i % 4, i // 4
        ctx_next = i < depth - 1
        mods = mods_all[i]
        gain = norm_mix[i].reshape(1, d)
        if not (ctx_next or kind >= 2):
            xs[0] = None
        if kind == 0:
            ys = _fourier_layer(streams, xs, mods, gain, fourier_w_out[j])
        elif kind == 1:
            ys = _conv_layer(streams, xs, mods, gain, conv_w_in[j], conv_w[j], conv_w_out[j])
        elif kind == 2:
            ys = _mla_layer(streams, xs, mods, gain,
                            (mla_w_in[j], mla_q_norm[j], mla_kv_norm[j], mla_w_uq[j], mla_w_ukv[j],
                             mla_q_gain[j], mla_k_gain[j], mla_w_out[j]), rope, ctx_next)
        else:
            ys = _gqa_layer(streams, xs, mods, gain,
                            (gqa_w_in[j], gqa_q_gain[j], gqa_k_gain[j], gqa_sink[j], gqa_w_out[j]),
                            rope, ctx_next)
        if not ctx_next:
            ys[0] = None
        gain2 = norm_ffn[i].reshape(1, d)
        wg, wu, wd = (_cast_experts_call(w, i) for w in (moe_w_gate, moe_w_up, moe_w_down))
        xs = [None if y is None else _moe_call(st, y, mods, gain2, router_wt, router_b, wg, wu, wd)
              for st, y in zip(streams, ys)]
    return xs[1].reshape(b, l, d)
```

```python
import functools

import numpy as np
import jax
import jax.numpy as jnp
from jax import lax
from jax.experimental import pallas as pl
from jax.experimental.pallas import tpu as pltpu

F32, BF16 = jnp.float32, jnp.bfloat16
SDS = jax.ShapeDtypeStruct

EPS = 1e-6
GRID_W = 64
ROPE_THETA = 10000.0
N_FOURIER_GROUPS = 4
MLA_HEADS, MLA_Q_RANK, MLA_KV_RANK = 16, 256, 128
MLA_NOPE, MLA_ROPE, MLA_V = 128, 64, 128
MLA_SCALE = (MLA_NOPE + MLA_ROPE) ** -0.5
GQA_Q_HEADS, GQA_KV_HEADS, GQA_HEAD_DIM = 16, 4, 64
GQA_GROUP = GQA_Q_HEADS // GQA_KV_HEADS
GQA_SCALE = GQA_HEAD_DIM ** -0.5
WINDOW = 128
BLOCK = 128
N_EXPERTS, N_EXPERT_GROUPS, EXPERT_FF = 16, 4, 256
EXPERTS_PER_GROUP = N_EXPERTS // N_EXPERT_GROUPS
ROPE_DIM = 64
LOG2E = 1.4426950408889634
MAX_FIXED_SHIFT = 50.0
SHIFT_MARGIN = 1.02
TOKEN_TILE = 1024
PROJ_TILE = 512

V7X_VMEM_BYTES = 64 * 1024 * 1024
LANES = 128
SH1, SC1, G1, SH2, SC2, G2 = range(6)


def _params(sem, vmem_mb):
    return pltpu.CompilerParams(dimension_semantics=sem, vmem_limit_bytes=vmem_mb * 1024 * 1024)


def _sigmoid(v):
    return 1.0 / (1.0 + jnp.exp(-v))


def _modnorm(x, gain, sc, sh):
    ms = jnp.mean(x * x, axis=-1, keepdims=True)
    return x * lax.rsqrt(ms + EPS) * (gain * (1.0 + sc)) + sh


def _nt(a, b):
    return lax.dot_general(a, b, (((1,), (1,)), ((), ())), preferred_element_type=F32)


def _dot(a, b):
    return jnp.dot(a, b, preferred_element_type=F32)


class Stream:
    def __init__(self, n, seg, tile, batch, is_ctx):
        self.n, self.seg, self.tile, self.batch, self.is_ctx = n, seg, tile, batch, is_ctx
        self.tiles = n // tile
        self.tiles_per_seg = seg // tile

    def mod_row(self, t):
        return self.batch if self.is_ctx else t // self.tiles_per_seg

    def seg_row(self, b):
        return self.batch if self.is_ctx else b

    def with_tile(self, tile):
        return Stream(self.n, self.seg, min(tile, self.tile), self.batch, self.is_ctx)


def _mod_spec(st, chunk, d):
    return pl.BlockSpec((None, 1, d), lambda t: (st.mod_row(t), 0, chunk))


def _full(shape):
    nd = len(shape)
    return pl.BlockSpec(shape, lambda *_: (0,) * nd)


def _resident(shape):
    nd = len(shape)
    return pl.BlockSpec(shape, lambda *_: (0,) * nd, pipeline_mode=pl.Buffered(1))


def _ada_call(cvec, w_ada, b_ada):
    depth, d, d6 = w_ada.shape
    r8 = cvec.shape[0]
    tn = d6 // 4

    def kern(c_ref, w_ref, b_ref, o_ref):
        c = c_ref[...]
        s = (c * _sigmoid(c)).astype(BF16)
        o_ref[...] = _dot(s, w_ref[...].astype(BF16)) + b_ref[...]

    return pl.pallas_call(
        kern, grid=(depth, d6 // tn),
        in_specs=[pl.BlockSpec((r8, d), lambda i, j: (0, 0)),
                  pl.BlockSpec((None, d, tn), lambda i, j: (i, 0, j)),
                  pl.BlockSpec((None, 1, tn), lambda i, j: (i, 0, j))],
        out_specs=pl.BlockSpec((None, r8, tn), lambda i, j: (i, 0, j)),
        out_shape=SDS((depth, r8, d6), F32), name="ada",
        compiler_params=_params(("arbitrary", "arbitrary"), 40))(cvec, w_ada, b_ada.reshape(depth, 1, d6))


def _proj_call(st, x, mods, gain, w, name):
    n, d = x.shape
    nout = w.shape[1]
    t = st.tile

    def kern(x_ref, g_ref, sc_ref, sh_ref, w_ref, o_ref):
        h = _modnorm(x_ref[...], g_ref[...], sc_ref[...], sh_ref[...]).astype(BF16)
        o_ref[...] = _dot(h, w_ref[...]).astype(BF16)

    return pl.pallas_call(
        kern, grid=(st.tiles,),
        in_specs=[pl.BlockSpec((t, d), lambda i: (i, 0)), _full((1, d)),
                  _mod_spec(st, SC1, d), _mod_spec(st, SH1, d), _resident((d, nout))],
        out_specs=pl.BlockSpec((t, nout), lambda i: (i, 0)),
        out_shape=SDS((n, nout), BF16), name=name,
        compiler_params=_params(("parallel",), 40))(x, gain, mods, mods, w)


def _outproj_call(st, a, x, mods, w, name):
    n, d = x.shape
    k = a.shape[1]
    t = st.tile

    def kern(a_ref, x_ref, g1_ref, w_ref, o_ref):
        o_ref[...] = x_ref[...] + g1_ref[...] * _dot(a_ref[...], w_ref[...])

    return pl.pallas_call(
        kern, grid=(st.tiles,),
        in_specs=[pl.BlockSpec((t, k), lambda i: (i, 0)), pl.BlockSpec((t, d), lambda i: (i, 0)),
                  _mod_spec(st, G1, d), _resident((k, d))],
        out_specs=pl.BlockSpec((t, d), lambda i: (i, 0)),
        out_shape=SDS((n, d), F32), name=name,
        compiler_params=_params(("parallel",), 40))(a, x, mods, w)


def _dft_tables(length, radix):
    lr = length // radix
    m = np.arange(lr)[None, :, None]
    j = np.arange(radix)[:, None, None]
    nn = np.arange(lr)[None, None, :]
    ang = 2.0 * np.pi * (((radix * m + j) * nn) % length) / length
    e = np.concatenate([np.cos(ang), np.sin(ang)], axis=-1) / np.sqrt(length)
    return jnp.asarray(e, dtype=F32).astype(BF16)


def _group_dft_tables(group):
    k = np.arange(group)
    ang = 2.0 * np.pi * ((k[:, None] * k[None, :]) % group) / group
    return (jnp.asarray(np.cos(ang) / np.sqrt(group), dtype=F32),
            jnp.asarray(np.sin(ang) / np.sqrt(group), dtype=F32))


def _fourier_weight_call(w_out):
    d = w_out.shape[0]
    grp = d // N_FOURIER_GROUPS
    cg, sg = _group_dft_tables(grp)

    def kern(cg_ref, sg_ref, w_ref, o_ref):
        w = w_ref[...]
        o_ref[:, :d] = jnp.dot(cg_ref[...], w, preferred_element_type=F32,
                               precision=lax.Precision.HIGHEST).astype(BF16)
        o_ref[:, d:] = jnp.dot(sg_ref[...], w, preferred_element_type=F32,
                               precision=lax.Precision.HIGHEST).astype(BF16)

    return pl.pallas_call(
        kern, grid=(N_FOURIER_GROUPS,),
        in_specs=[_full((grp, grp)), _full((grp, grp)), pl.BlockSpec((grp, d), lambda g: (g, 0))],
        out_specs=pl.BlockSpec((grp, 2 * d), lambda g: (g, 0)),
        out_shape=SDS((d, 2 * d), BF16), name="fourier_w",
        compiler_params=_params(("arbitrary",), 32))(cg, sg, w_out)


def _radix_terms(radix, j):
    real, imag = [], []
    for q in range(radix):
        k = (j * q * (4 // radix)) % 4 if radix > 1 else 0
        if k == 0:
            real.append((1, 0, q)); imag.append((-1, 1, q))
        elif k == 1:
            real.append((-1, 1, q)); imag.append((-1, 0, q))
        elif k == 2:
            real.append((-1, 0, q)); imag.append((1, 1, q))
        else:
            real.append((1, 1, q)); imag.append((1, 0, q))
    return real, imag


def _seq_dft_call(st, p, x, mods, radix):
    n, d = x.shape
    seg = st.seg
    lr = seg // radix
    e = _dft_tables(seg, radix)
    nb = n // seg
    cblk = 2 * LANES
    ncb = d // cblk

    def kern(pc_ref, ps_ref, x_ref, e_ref, g1_ref, o_ref, v_ref, *z_refs):
        parts = (pc_ref, ps_ref)
        for j in range(radix):
            real, imag = _radix_terms(radix, j)

            def comb(terms):
                acc = None
                for sgn, part, q in terms:
                    v = parts[part][q * lr:(q + 1) * lr, :].astype(F32)
                    if acc is None:
                        acc = v if sgn > 0 else -v
                    else:
                        acc = acc + v if sgn > 0 else acc - v
                return acc

            v_ref[0:lr, :] = comb(real).astype(BF16)
            v_ref[lr:2 * lr, :] = comb(imag).astype(BF16)
            z = _dot(e_ref[j], v_ref[...])
            for k, z_ref in enumerate(z_refs):
                z_ref[pl.ds(j, lr, stride=radix), :] = z[:, k * LANES:(k + 1) * LANES]
        for k, z_ref in enumerate(z_refs):
            sl = slice(k * LANES, (k + 1) * LANES)
            o_ref[:, sl] = x_ref[:, sl] + g1_ref[:, sl] * z_ref[...]

    return pl.pallas_call(
        kern, grid=(nb, ncb),
        in_specs=[pl.BlockSpec((seg, cblk), lambda b, c: (b, c)),
                  pl.BlockSpec((seg, cblk), lambda b, c: (b, ncb + c)),
                  pl.BlockSpec((seg, cblk), lambda b, c: (b, c)),
                  _resident((radix, lr, 2 * lr)),
                  pl.BlockSpec((None, 1, cblk), lambda b, c: (st.seg_row(b), 0, G1 * ncb + c))],
        out_specs=pl.BlockSpec((seg, cblk), lambda b, c: (b, c)),
        out_shape=SDS((n, d), F32),
        scratch_shapes=[pltpu.VMEM((2 * lr, cblk), BF16)] + [pltpu.VMEM((seg, LANES), F32)] * (cblk // LANES),
        name=f"seq_dft_r{radix}",
        compiler_params=_params(("parallel", "parallel"), 40))(p, p, x, e, mods)


def _fourier_layer(streams, xs, mods, gain, w_out):
    wcs = _fourier_weight_call(w_out)
    outs = []
    for st, x in zip(streams, xs):
        if x is None:
            outs.append(None)
            continue
        p = _proj_call(st, x, mods, gain, wcs, "fourier_proj")
        outs.append(_seq_dft_call(st, p, x, mods, 4 if st.seg >= 1024 else 1))
    return outs


def _conv_in_call(st, x, mods, gain, w_in):
    n, d = x.shape
    t = st.tile

    def kern(x_ref, g_ref, sc_ref, sh_ref, w_ref, bg_ref, u_ref):
        h = _modnorm(x_ref[...], g_ref[...], sc_ref[...], sh_ref[...]).astype(BF16)
        bg_ref[...] = _dot(h, w_ref[:, 0:d]).astype(BF16)
        u_ref[...] = (_dot(h, w_ref[:, d:2 * d]) * _dot(h, w_ref[:, 2 * d:3 * d])).astype(BF16)

    return pl.pallas_call(
        kern, grid=(st.tiles,),
        in_specs=[pl.BlockSpec((t, d), lambda i: (i, 0)), _full((1, d)),
                  _mod_spec(st, SC1, d), _mod_spec(st, SH1, d), _resident((d, 3 * d))],
        out_specs=[pl.BlockSpec((t, d), lambda i: (i, 0))] * 2,
        out_shape=[SDS((n, d), BF16)] * 2, name="conv_in",
        compiler_params=_params(("parallel",), 48))(x, gain, mods, mods, w_in)


def _conv_out_call(st, bg, u, x, mods, conv_w, w_out):
    n, d = x.shape
    t = st.tile
    halo = 16
    hb = t // halo
    nhalo = n // halo
    seg = st.seg

    def kern(u_ref, up_ref, un_ref, bg_ref, x_ref, cw_ref, g1_ref, w_ref, o_ref):
        i = pl.program_id(0)
        u = u_ref[...].astype(F32)
        row = lax.broadcasted_iota(jnp.int32, (t, 1), 0)
        pos = (i * t + row) & (seg - 1)
        prev_row = up_ref[...].astype(F32)[halo - 1:halo, :]
        next_row = un_ref[...].astype(F32)[0:1, :]
        um = jnp.where(row == 0, prev_row, pltpu.roll(u, 1, axis=0))
        um = jnp.where(pos == 0, 0.0, um)
        up = jnp.where(row == t - 1, next_row, pltpu.roll(u, t - 1, axis=0))
        up = jnp.where(pos == seg - 1, 0.0, up)
        z = cw_ref[0:1, :] * um + cw_ref[1:2, :] * u + cw_ref[2:3, :] * up
        a = (bg_ref[...].astype(F32) * z).astype(BF16)
        o_ref[...] = x_ref[...] + g1_ref[...] * _dot(a, w_ref[...])

    tile_spec = pl.BlockSpec((t, d), lambda i: (i, 0))
    return pl.pallas_call(
        kern, grid=(st.tiles,),
        in_specs=[tile_spec,
                  pl.BlockSpec((halo, d), lambda i: (jnp.maximum(i * hb - 1, 0), 0)),
                  pl.BlockSpec((halo, d), lambda i: (jnp.minimum((i + 1) * hb, nhalo - 1), 0)),
                  tile_spec, tile_spec, _full((3, d)), _mod_spec(st, G1, d), _resident((d, d))],
        out_specs=tile_spec, out_shape=SDS((n, d), F32), name="conv_out",
        compiler_params=_params(("parallel",), 40))(u, u, u, bg, x, conv_w, mods, w_out)


def _conv_layer(streams, xs, mods, gain, w_in, conv_w, w_out):
    w_in_b, w_out_b = w_in.astype(BF16), w_out.astype(BF16)
    outs = []
    for st, x in zip(streams, xs):
        if x is None:
            outs.append(None)
            continue
        bg, u = _conv_in_call(st, x, mods, gain, w_in_b)
        outs.append(_conv_out_call(st, bg, u, x, mods, conv_w, w_out_b))
    return outs


def _rope_tables(length):
    q = ROPE_DIM // 4
    pos = np.arange(length)
    rc = np.stack([pos // GRID_W, pos % GRID_W], axis=1).astype(np.float32)
    inv = (ROPE_THETA ** (-np.arange(q, dtype=np.float32) / q)).astype(np.float32)
    lane = np.arange(ROPE_DIM)
    ang = rc[:, lane >> 5] * inv[lane & (q - 1)][None, :]
    sign = np.where((lane & q) == 0, -1.0, 1.0)[None, :]
    cos = np.cos(ang.astype(np.float32)).astype(np.float32)
    sin = (np.sin(ang.astype(np.float32)) * sign).astype(np.float32)
    reps = LANES // ROPE_DIM
    return jnp.asarray(np.tile(cos, (1, reps))), jnp.asarray(np.tile(sin, (1, reps)))


def _rope_partner_perm(n_heads):
    lane = np.arange(n_heads * ROPE_DIM)
    return lane ^ (ROPE_DIM // 4)


def _seg_rinv(x, seg):
    t, w = x.shape
    cols = []
    for c in range(w // LANES):
        blk = x[:, c * LANES:(c + 1) * LANES]
        sq = blk * blk
        if seg == LANES:
            cols.append(jnp.broadcast_to(
                lax.rsqrt(jnp.mean(sq, axis=-1, keepdims=True) + EPS), (t, LANES)))
        else:
            lane = lax.broadcasted_iota(jnp.int32, (t, LANES), 1)
            low = lane < seg
            lo = jnp.sum(jnp.where(low, sq, 0.0), axis=-1, keepdims=True) * (1.0 / seg)
            hi = jnp.sum(jnp.where(low, 0.0, sq), axis=-1, keepdims=True) * (1.0 / seg)
            cols.append(jnp.where(low, lax.rsqrt(lo + EPS), lax.rsqrt(hi + EPS)))
    return cols[0] if len(cols) == 1 else jnp.concatenate(cols, axis=-1)


def _tile_lanes(v, w):
    reps = w // LANES
    return v if reps == 1 else jnp.concatenate([v] * reps, axis=-1)


def _mla_proj_call(st, x, mods, gain, wts, rope, need_q):
    n, d = x.shape
    t = st.tile
    h = MLA_HEADS
    wn, wr = h * MLA_NOPE, h * MLA_ROPE
    (w_in, q_norm, kv_norm, w_uq, w_ukv, qg_n, qg_r, qg_rp, kg_n, kg_r, kg_rp) = wts
    use_rope = rope is not None

    def kern(*refs):
        refs = list(refs)
        x_ref, g_ref, sc_ref, sh_ref, win_ref, qn_ref_, kvn_ref_, wuq_ref, wukv_ref = refs[:9]
        qgn_ref, qgr_ref, qgrp_ref, kgn_ref, kgr_ref, kgrp_ref = refs[9:15]
        rest = refs[15:]
        if use_rope:
            cos_ref, sin_ref = rest[:2]
            rest = rest[2:]
        if need_q:
            oqn_ref, oqr_ref = rest[:2]
            rest = rest[2:]
        okn_ref, okr_ref, ov_ref = rest
        hm = _modnorm(x_ref[...], g_ref[...], sc_ref[...], sh_ref[...]).astype(BF16)
        ck = _dot(hm, win_ref[...])
        kr2 = ck[:, MLA_Q_RANK + MLA_KV_RANK:]
        kr_rinv = _seg_rinv(kr2, MLA_ROPE)[:, 0:MLA_ROPE]
        kr_raw = kr2[:, 0:MLA_ROPE] * kgr_ref[...]
        if use_rope:
            kr_par = kr2[:, MLA_ROPE:] * kgrp_ref[...]
            kr = kr_rinv * (kr_raw * cos_ref[:, 0:MLA_ROPE] + kr_par * sin_ref[:, 0:MLA_ROPE])
        else:
            kr = kr_rinv * kr_raw
        okr_ref[...] = kr.astype(BF16)
        ckv = ck[:, MLA_Q_RANK:MLA_Q_RANK + MLA_KV_RANK]
        ckv = (ckv * _seg_rinv(ckv, LANES) * kvn_ref_[...]).astype(BF16)
        kv = _dot(ckv, wukv_ref[...])
        kn = kv[:, 0:wn]
        okn_ref[...] = (kn * _seg_rinv(kn, MLA_NOPE) * kgn_ref[...]).astype(BF16)
        ov_ref[...] = kv[:, wn:].astype(BF16)
        if need_q:
            cq = ck[:, 0:MLA_Q_RANK]
            rq = lax.rsqrt(jnp.mean(cq * cq, axis=-1, keepdims=True) + EPS)
            cq = (cq * rq * qn_ref_[...]).astype(BF16)
            q = _dot(cq, wuq_ref[...])
            qn = q[:, 0:wn]
            oqn_ref[...] = (qn * _seg_rinv(qn, MLA_NOPE) * (qgn_ref[...] * (MLA_SCALE * LOG2E))).astype(BF16)
            qr_raw = q[:, wn:wn + wr]
            rinv = _seg_rinv(qr_raw, MLA_ROPE) * (MLA_SCALE * LOG2E)
            if use_rope:
                qr_par = q[:, wn + wr:]
                cos = _tile_lanes(cos_ref[...], wr)
                sin = _tile_lanes(sin_ref[...], wr)
                qr = rinv * (qr_raw * qgr_ref[...] * cos + qr_par * qgrp_ref[...] * sin)
            else:
                qr = rinv * (qr_raw * qgr_ref[...])
            oqr_ref[...] = qr.astype(BF16)

    tile = lambda w: pl.BlockSpec((t, w), lambda i: (i, 0))
    in_specs = [tile(d), _full((1, d)), _mod_spec(st, SC1, d), _mod_spec(st, SH1, d),
                _resident(w_in.shape), _full(q_norm.shape), _full(kv_norm.shape),
                _resident(w_uq.shape), _resident(w_ukv.shape),
                _full(qg_n.shape), _full(qg_r.shape), _full(qg_rp.shape),
                _full(kg_n.shape), _full(kg_r.shape), _full(kg_rp.shape)]
    args = [x, gain, mods, mods, w_in, q_norm, kv_norm, w_uq, w_ukv, qg_n, qg_r, qg_rp, kg_n, kg_r, kg_rp]
    if use_rope:
        tps = st.tiles_per_seg
        in_specs += [pl.BlockSpec((t, LANES), lambda i: (i % tps, 0))] * 2
        args += list(rope)
    out_specs, out_shape = [], []
    if need_q:
        out_specs += [tile(wn), tile(wr)]
        out_shape += [SDS((n, wn), BF16), SDS((n, wr), BF16)]
    out_specs += [tile(wn), tile(MLA_ROPE), tile(h * MLA_V)]
    out_shape += [SDS((n, wn), BF16), SDS((n, MLA_ROPE), BF16), SDS((n, h * MLA_V), BF16)]
    outs = pl.pallas_call(
        kern, grid=(st.tiles,), in_specs=in_specs, out_specs=out_specs, out_shape=out_shape,
        name="mla_proj", compiler_params=_params(("parallel",), 48))(*args)
    if need_q:
        return tuple(outs)
    return (None, None) + tuple(outs)


def _mla_attn_fast_call(batch, lq, tq, q, kv_sets, shift):
    qn, qr = q
    hp = 2
    n_hp = MLA_HEADS // hp
    nq = lq // tq
    nsets = len(kv_sets)
    lengths = [s[3] for s in kv_sets]
    lk = sum(lengths)
    kw = 2 * LANES
    pad = kw - MLA_NOPE - MLA_ROPE

    def kern(*refs):
        sh_ref, qn_ref, qr_ref = refs[:3]
        sets = [refs[3 + 3 * s:6 + 3 * s] for s in range(nsets)]
        o_ref, kq_ref, vq_ref = refs[-3:]

        @pl.when(pl.program_id(2) == 0)
        def _():
            lane_k = lax.broadcasted_iota(jnp.int32, (lk, pad), 1)
            lane_v = lax.broadcasted_iota(jnp.int32, (lk, kw - MLA_V), 1)
            for hh in range(hp):
                off = 0
                for (kn_ref, kr_ref, v_ref), length in zip(sets, lengths):
                    kq_ref[hh, off:off + length, 0:MLA_NOPE] = kn_ref[:, hh * MLA_NOPE:(hh + 1) * MLA_NOPE]
                    kq_ref[hh, off:off + length, MLA_NOPE:MLA_NOPE + MLA_ROPE] = kr_ref[...]
                    vq_ref[hh, off:off + length, 0:MLA_V] = v_ref[:, hh * MLA_V:(hh + 1) * MLA_V]
                    off += length
                kq_ref[hh, :, MLA_NOPE + MLA_ROPE:] = jnp.where(lane_k == 0, -sh_ref[:, 0:pad], 0.0).astype(BF16)
                vq_ref[hh, :, MLA_V:] = jnp.where(lane_v == 0, 1.0, 0.0).astype(BF16)

        lane_q = lax.broadcasted_iota(jnp.int32, (tq, pad), 1)
        one = jnp.where(lane_q == 0, 1.0, 0.0).astype(BF16)
        for hh in range(hp):
            qc = jnp.concatenate([qn_ref[:, hh * MLA_NOPE:(hh + 1) * MLA_NOPE],
                                  qr_ref[:, hh * MLA_ROPE:(hh + 1) * MLA_ROPE], one], axis=1)
            p = jnp.exp2(_nt(qc, kq_ref[hh]).astype(BF16))
            acc = _dot(p, vq_ref[hh])
            o_ref[:, hh * MLA_V:(hh + 1) * MLA_V] = (acc[:, 0:MLA_V] / acc[:, MLA_V:MLA_V + 1]).astype(BF16)

    in_specs = [_full((1, LANES)),
                pl.BlockSpec((tq, hp * MLA_NOPE), lambda b, h, i: (b * nq + i, h)),
                pl.BlockSpec((tq, hp * MLA_ROPE), lambda b, h, i: (b * nq + i, h))]
    args = [shift, qn, qr]
    for kn, kr, v, length in kv_sets:
        in_specs += [pl.BlockSpec((length, hp * MLA_NOPE), lambda b, h, i: (b, h)),
                     pl.BlockSpec((length, MLA_ROPE), lambda b, h, i: (b, 0)),
                     pl.BlockSpec((length, hp * MLA_V), lambda b, h, i: (b, h))]
        args += [kn, kr, v]
    return pl.pallas_call(
        kern, grid=(batch, n_hp, nq), in_specs=in_specs,
        out_specs=pl.BlockSpec((tq, hp * MLA_V), lambda b, h, i: (b * nq + i, h)),
        out_shape=SDS((batch * lq, MLA_HEADS * MLA_V), BF16),
        scratch_shapes=[pltpu.VMEM((hp, lk, kw), BF16), pltpu.VMEM((hp, lk, kw), BF16)],
        name="mla_attn_fast",
        compiler_params=_params(("parallel", "parallel", "arbitrary"), 56))(*args)


def _mla_attn_call(batch, lq, tq, q, kv_sets):
    qn, qr = q
    hp = 2
    n_hp = MLA_HEADS // hp
    nq = lq // tq
    nsets = len(kv_sets)

    def kern(*refs):
        qn_ref, qr_ref = refs[:2]
        o_ref = refs[-1]
        sets = [refs[2 + 3 * s:5 + 3 * s] for s in range(nsets)]
        for hh in range(hp):
            qn_h = qn_ref[:, hh * MLA_NOPE:(hh + 1) * MLA_NOPE]
            qr_h = qr_ref[:, hh * MLA_ROPE:(hh + 1) * MLA_ROPE]
            scores = [_nt(qn_h, kn_ref[:, hh * MLA_NOPE:(hh + 1) * MLA_NOPE]) + _nt(qr_h, kr_ref[...])
                      for kn_ref, kr_ref, _ in sets]
            m = functools.reduce(jnp.maximum, [jnp.max(s, axis=-1, keepdims=True) for s in scores])
            ps = [jnp.exp2(s - m) for s in scores]
            den = functools.reduce(lambda a, b: a + b, [jnp.sum(p, axis=-1, keepdims=True) for p in ps])
            acc = functools.reduce(lambda a, b: a + b, [
                _dot(p.astype(BF16), v_ref[:, hh * MLA_V:(hh + 1) * MLA_V])
                for p, (_, _, v_ref) in zip(ps, sets)])
            o_ref[:, hh * MLA_V:(hh + 1) * MLA_V] = (acc / den).astype(BF16)

    in_specs = [pl.BlockSpec((tq, hp * MLA_NOPE), lambda b, h, i: (b * nq + i, h)),
                pl.BlockSpec((tq, hp * MLA_ROPE), lambda b, h, i: (b * nq + i, h))]
    args = [qn, qr]
    for kn, kr, v, length in kv_sets:
        in_specs += [pl.BlockSpec((length, hp * MLA_NOPE), lambda b, h, i: (b, h)),
                     pl.BlockSpec((length, MLA_ROPE), lambda b, h, i: (b, 0)),
                     pl.BlockSpec((length, hp * MLA_V), lambda b, h, i: (b, h))]
        args += [kn, kr, v]
    return pl.pallas_call(
        kern, grid=(batch, n_hp, nq), in_specs=in_specs,
        out_specs=pl.BlockSpec((tq, hp * MLA_V), lambda b, h, i: (b * nq + i, h)),
        out_shape=SDS((batch * lq, MLA_HEADS * MLA_V), BF16), name="mla_attn",
        compiler_params=_params(("parallel", "parallel", "arbitrary"), 56))(*args)


def _mla_layer(streams, xs, mods, gain, p, rope, ctx_next):
    st_c, st_x = streams
    xc, xx = xs
    h = MLA_HEADS
    w_in, q_norm, kv_norm, w_uq, w_ukv, q_gain, k_gain, w_out = p
    par = _rope_partner_perm(1)
    kr_cols = w_in[:, MLA_Q_RANK + MLA_KV_RANK:]
    w_in_p = jnp.concatenate([w_in, kr_cols[:, par]], axis=1).astype(BF16)
    wq = w_uq.reshape(MLA_Q_RANK, h, MLA_NOPE + MLA_ROPE)
    wq_n = wq[:, :, :MLA_NOPE].reshape(MLA_Q_RANK, h * MLA_NOPE)
    wq_r = wq[:, :, MLA_NOPE:]
    w_uq_p = jnp.concatenate([wq_n, wq_r.reshape(MLA_Q_RANK, h * MLA_ROPE),
                              wq_r[:, :, par].reshape(MLA_Q_RANK, h * MLA_ROPE)], axis=1).astype(BF16)
    wkv = w_ukv.reshape(MLA_KV_RANK, h, MLA_NOPE + MLA_V)
    w_ukv_p = jnp.concatenate([wkv[:, :, :MLA_NOPE].reshape(MLA_KV_RANK, h * MLA_NOPE),
                               wkv[:, :, MLA_NOPE:].reshape(MLA_KV_RANK, h * MLA_V)], axis=1).astype(BF16)
    qg_r = q_gain[MLA_NOPE:]
    kg_r = k_gain[MLA_NOPE:]
    wts = (w_in_p, q_norm.reshape(1, -1), kv_norm.reshape(1, -1), w_uq_p, w_ukv_p,
           jnp.tile(q_gain[:MLA_NOPE], h).reshape(1, -1), jnp.tile(qg_r, h).reshape(1, -1),
           jnp.tile(qg_r[par], h).reshape(1, -1), jnp.tile(k_gain[:MLA_NOPE], h).reshape(1, -1),
           kg_r.reshape(1, -1), kg_r[par].reshape(1, -1))
    w_out_b = w_out.astype(BF16)
    qn_c, qr_c, kn_c, kr_c, v_c = _mla_proj_call(st_c.with_tile(PROJ_TILE), xc, mods, gain, wts, None, ctx_next)
    qn_x, qr_x, kn_x, kr_x, v_x = _mla_proj_call(st_x.with_tile(PROJ_TILE), xx, mods, gain, wts, rope, True)
    b = st_x.batch
    amax = lambda g: jnp.max(jnp.abs(g))
    bound = LOG2E * MLA_SCALE * (MLA_NOPE * amax(q_gain[:MLA_NOPE]) * amax(k_gain[:MLA_NOPE])
                                 + MLA_ROPE * amax(qg_r) * amax(kg_r))
    shift = bound * SHIFT_MARGIN + 1.0
    shift_row = jnp.full((1, LANES), shift, F32)

    def attend(lq, tq, q, kv_sets):
        return lax.cond(shift <= MAX_FIXED_SHIFT,
                        lambda: _mla_attn_fast_call(b, lq, tq, q, kv_sets, shift_row),
                        lambda: _mla_attn_call(b, lq, tq, q, kv_sets))

    a_x = attend(st_x.seg, min(st_x.seg, 1024), (qn_x, qr_x), [(kn_c, kr_c, v_c, st_c.seg), (kn_x, kr_x, v_x, st_x.seg)])
    out_x = _outproj_call(st_x, a_x, xx, mods, w_out_b, "mla_out")
    out_c = None
    if ctx_next:
        a_c = attend(st_c.seg, st_c.seg, (qn_c, qr_c), [(kn_c, kr_c, v_c, st_c.seg)])
        out_c = _outproj_call(st_c, a_c, xc, mods, w_out_b, "mla_out")
    return [out_c, out_x]


def _gqa_proj_call(st, x, mods, gain, wts, rope, need_q):
    n, d = x.shape
    t = st.tile
    wq, wk = GQA_Q_HEADS * GQA_HEAD_DIM, GQA_KV_HEADS * GQA_HEAD_DIM
    w_in, qg, qgp, kg, kgp = wts
    use_rope = rope is not None

    def kern(*refs):
        refs = list(refs)
        x_ref, g_ref, sc_ref, sh_ref, w_ref, qg_ref, qgp_ref, kg_ref, kgp_ref = refs[:9]
        rest = refs[9:]
        if use_rope:
            cos_ref, sin_ref = rest[:2]
            rest = rest[2:]
        if need_q:
            oq_ref = rest[0]
            rest = rest[1:]
        ok_ref, ov_ref = rest
        hm = _modnorm(x_ref[...], g_ref[...], sc_ref[...], sh_ref[...]).astype(BF16)
        kvp = _dot(hm, w_ref[:, 0:3 * wk])
        k_raw = kvp[:, 0:wk]
        k_rinv = _seg_rinv(k_raw, GQA_HEAD_DIM)
        ov_ref[...] = kvp[:, wk:2 * wk].astype(BF16)
        if use_rope:
            cos_k, sin_k = _tile_lanes(cos_ref[...], wk), _tile_lanes(sin_ref[...], wk)
            k = k_rinv * (k_raw * kg_ref[...] * cos_k + kvp[:, 2 * wk:] * kgp_ref[...] * sin_k)
        else:
            k = k_rinv * (k_raw * kg_ref[...])
        ok_ref[...] = k.astype(BF16)
        if need_q:
            qp = _dot(hm, w_ref[:, 3 * wk:])
            q_raw = qp[:, 0:wq]
            rinv = _seg_rinv(q_raw, GQA_HEAD_DIM) * (GQA_SCALE * LOG2E)
            if use_rope:
                cos_q, sin_q = _tile_lanes(cos_ref[...], wq), _tile_lanes(sin_ref[...], wq)
                q = rinv * (q_raw * qg_ref[...] * cos_q + qp[:, wq:] * qgp_ref[...] * sin_q)
            else:
                q = rinv * (q_raw * qg_ref[...])
            oq_ref[...] = q.astype(BF16)

    tile = lambda w: pl.BlockSpec((t, w), lambda i: (i, 0))
    in_specs = [tile(d), _full((1, d)), _mod_spec(st, SC1, d), _mod_spec(st, SH1, d),
                _resident(w_in.shape), _full(qg.shape), _full(qgp.shape), _full(kg.shape), _full(kgp.shape)]
    args = [x, gain, mods, mods, w_in, qg, qgp, kg, kgp]
    if use_rope:
        tps = st.tiles_per_seg
        in_specs += [pl.BlockSpec((t, LANES), lambda i: (i % tps, 0))] * 2
        args += list(rope)
    out_specs, out_shape = [], []
    if need_q:
        out_specs.append(tile(wq))
        out_shape.append(SDS((n, wq), BF16))
    out_specs += [tile(wk), tile(wk)]
    out_shape += [SDS((n, wk), BF16)] * 2
    outs = pl.pallas_call(
        kern, grid=(st.tiles,), in_specs=in_specs, out_specs=out_specs, out_shape=out_shape,
        name="gqa_proj", compiler_params=_params(("parallel",), 48))(*args)
    return tuple(outs) if need_q else (None,) + tuple(outs)


def _gqa_window_call(batch, length, lc, q, k, v, kc, vc, sink):
    nb = length // BLOCK
    hd = GQA_HEAD_DIM
    wq, wk = GQA_Q_HEADS * hd, GQA_KV_HEADS * hd
    rows = GQA_GROUP * BLOCK

    def kern(sink_ref, q_ref, kp_ref, k0_ref, kn_ref, vp_ref, v0_ref, vn_ref, kc_ref, vc_ref, o_ref):
        nblk = pl.program_id(1)
        r = lax.broadcasted_iota(jnp.int32, (rows, 3 * BLOCK), 0) & (BLOCK - 1)
        c = lax.broadcasted_iota(jnp.int32, (rows, 3 * BLOCK), 1)
        valid = (c >= r + BLOCK - WINDOW) & (c <= r + BLOCK + WINDOW)
        valid = valid & ((c >= BLOCK) | (nblk > 0)) & ((c < 2 * BLOCK) | (nblk < nb - 1))
        hrow = lax.broadcasted_iota(jnp.int32, (rows, 1), 0) // BLOCK
        for g in range(GQA_KV_HEADS):
            sl = slice(g * hd, (g + 1) * hd)
            qg = jnp.concatenate([q_ref[:, (g * GQA_GROUP + j) * hd:(g * GQA_GROUP + j + 1) * hd]
                                  for j in range(GQA_GROUP)], axis=0)
            kband = jnp.concatenate([kp_ref[:, sl], k0_ref[:, sl], kn_ref[:, sl]], axis=0)
            vband = jnp.concatenate([vp_ref[:, sl], v0_ref[:, sl], vn_ref[:, sl]], axis=0)
            s_c = _nt(qg, kc_ref[:, sl])
            s_b = jnp.where(valid, _nt(qg, kband), -1e30)
            snk = jnp.zeros((rows, 1), F32)
            for j in range(GQA_GROUP):
                snk = jnp.where(hrow == j, sink_ref[g * GQA_GROUP + j], snk)
            m = jnp.maximum(jnp.maximum(jnp.max(s_c, axis=-1, keepdims=True),
                                        jnp.max(s_b, axis=-1, keepdims=True)), snk)
            p_c = jnp.exp2(s_c - m)
            p_b = jnp.exp2(s_b - m)
            den = (jnp.sum(p_c, axis=-1, keepdims=True) + jnp.sum(p_b, axis=-1, keepdims=True)
                   + jnp.exp2(snk - m))
            o = (_dot(p_c.astype(BF16), vc_ref[:, sl]) + _dot(p_b.astype(BF16), vband)) / den
            o_ref[:, g * GQA_GROUP * hd:(g + 1) * GQA_GROUP * hd] = jnp.concatenate(
                [o[j * BLOCK:(j + 1) * BLOCK, :] for j in range(GQA_GROUP)], axis=-1).astype(BF16)

    blk = lambda f: pl.BlockSpec((BLOCK, wk), f)
    prev_ = lambda b, i: (b * nb + jnp.maximum(i - 1, 0), 0)
    cur_ = lambda b, i: (b * nb + i, 0)
    next_ = lambda b, i: (b * nb + jnp.minimum(i + 1, nb - 1), 0)
    ctx_spec = pl.BlockSpec((lc, wk), lambda b, i: (b, 0))
    return pl.pallas_call(
        kern, grid=(batch, nb),
        in_specs=[pl.BlockSpec(memory_space=pltpu.SMEM),
                  pl.BlockSpec((BLOCK, wq), cur_), blk(prev_), blk(cur_), blk(next_),
                  blk(prev_), blk(cur_), blk(next_), ctx_spec, ctx_spec],
        out_specs=pl.BlockSpec((BLOCK, wq), cur_),
        out_shape=SDS((batch * length, wq), BF16), name="gqa_window",
        compiler_params=_params(("parallel", "arbitrary"), 40))(sink, q, k, k, k, v, v, v, kc, vc)


def _gqa_window_fast_call(batch, length, lc, q, k, v, kc, vc, sink2, shift):
    nb = length // BLOCK
    hd = GQA_HEAD_DIM
    wq, wk = GQA_Q_HEADS * hd, GQA_KV_HEADS * hd
    rows = GQA_GROUP * BLOCK

    def kern(sink_ref, shift_ref, q_ref, kp_ref, k0_ref, kn_ref, vp_ref, v0_ref, vn_ref, kc_ref, vc_ref, o_ref):
        nblk = pl.program_id(1)
        r = lax.broadcasted_iota(jnp.int32, (rows, 3 * BLOCK), 0) & (BLOCK - 1)
        c = lax.broadcasted_iota(jnp.int32, (rows, 3 * BLOCK), 1)
        valid = (c >= r + BLOCK - WINDOW) & (c <= r + BLOCK + WINDOW)
        valid = valid & ((c >= BLOCK) | (nblk > 0)) & ((c < 2 * BLOCK) | (nblk < nb - 1))
        hrow = lax.broadcasted_iota(jnp.int32, (rows, 1), 0) // BLOCK
        lane0_q = lax.broadcasted_iota(jnp.int32, (BLOCK, hd), 1) == 0

        def with_one(x):
            lane0 = lax.broadcasted_iota(jnp.int32, (x.shape[0], hd), 1) == 0
            return jnp.concatenate([x, jnp.where(lane0, 1.0, 0.0).astype(BF16)], axis=1)

        for g in range(GQA_KV_HEADS):
            sl = slice(g * hd, (g + 1) * hd)
            heads = [g * GQA_GROUP + j for j in range(GQA_GROUP)]
            qg = jnp.concatenate(
                [jnp.concatenate([q_ref[:, h * hd:(h + 1) * hd],
                                  jnp.where(lane0_q, -shift_ref[h], 0.0).astype(BF16)], axis=1)
                 for h in heads], axis=0)
            kband = with_one(jnp.concatenate([kp_ref[:, sl], k0_ref[:, sl], kn_ref[:, sl]], axis=0))
            vband = with_one(jnp.concatenate([vp_ref[:, sl], v0_ref[:, sl], vn_ref[:, sl]], axis=0))
            p_c = jnp.exp2(_nt(qg, with_one(kc_ref[:, sl])).astype(BF16))
            p_b = jnp.exp2(jnp.where(valid, _nt(qg, kband), -1e30).astype(BF16))
            acc = _dot(p_c, with_one(vc_ref[:, sl])) + _dot(p_b, vband)
            snk = jnp.zeros((rows, 1), F32)
            for j, h in enumerate(heads):
                snk = jnp.where(hrow == j, sink_ref[h] - shift_ref[h], snk)
            o = acc[:, 0:hd] / (acc[:, hd:hd + 1] + jnp.exp2(snk))
            o_ref[:, g * GQA_GROUP * hd:(g + 1) * GQA_GROUP * hd] = jnp.concatenate(
                [o[j * BLOCK:(j + 1) * BLOCK, :] for j in range(GQA_GROUP)], axis=-1).astype(BF16)

    blk = lambda f: pl.BlockSpec((BLOCK, wk), f)
    prev_ = lambda b, i: (b * nb + jnp.maximum(i - 1, 0), 0)
    cur_ = lambda b, i: (b * nb + i, 0)
    next_ = lambda b, i: (b * nb + jnp.minimum(i + 1, nb - 1), 0)
    ctx_spec = pl.BlockSpec((lc, wk), lambda b, i: (b, 0))
    smem = pl.BlockSpec(memory_space=pltpu.SMEM)
    return pl.pallas_call(
        kern, grid=(batch, nb),
        in_specs=[smem, smem, pl.BlockSpec((BLOCK, wq), cur_), blk(prev_), blk(cur_), blk(next_),
                  blk(prev_), blk(cur_), blk(next_), ctx_spec, ctx_spec],
        out_specs=pl.BlockSpec((BLOCK, wq), cur_),
        out_shape=SDS((batch * length, wq), BF16), name="gqa_window_fast",
        compiler_params=_params(("parallel", "arbitrary"), 40))(sink2, shift, q, k, k, k, v, v, v, kc, vc)


def _gqa_layer(streams, xs, mods, gain, p, rope, ctx_next):
    assert not ctx_next, "the windowed-GQA mixer is only implemented as the last layer"
    st_c, st_x = streams
    xc, xx = xs
    w_in, q_gain, k_gain, sink, w_out = p
    wq, wk = GQA_Q_HEADS * GQA_HEAD_DIM, GQA_KV_HEADS * GQA_HEAD_DIM
    w_q, w_k, w_v = w_in[:, :wq], w_in[:, wq:wq + wk], w_in[:, wq + wk:]
    w_in_p = jnp.concatenate([w_k, w_v, w_k[:, _rope_partner_perm(GQA_KV_HEADS)],
                              w_q, w_q[:, _rope_partner_perm(GQA_Q_HEADS)]], axis=1).astype(BF16)
    par = _rope_partner_perm(1)
    wts = (w_in_p, jnp.tile(q_gain, GQA_Q_HEADS).reshape(1, -1), jnp.tile(q_gain[par], GQA_Q_HEADS).reshape(1, -1),
           jnp.tile(k_gain, GQA_KV_HEADS).reshape(1, -1), jnp.tile(k_gain[par], GQA_KV_HEADS).reshape(1, -1))
    _, kc, vc = _gqa_proj_call(st_c.with_tile(PROJ_TILE), xc, mods, gain, wts, None, False)
    q, k, v = _gqa_proj_call(st_x.with_tile(PROJ_TILE), xx, mods, gain, wts, rope, True)
    sink2 = sink * LOG2E
    bound = LOG2E * GQA_SCALE * GQA_HEAD_DIM * jnp.max(jnp.abs(q_gain)) * jnp.max(jnp.abs(k_gain))
    shift = jnp.maximum(bound * SHIFT_MARGIN + 1.0, sink2).astype(BF16).astype(F32)
    args = (st_x.batch, st_x.seg, st_c.seg, q, k, v, kc, vc, sink2)
    a = lax.cond(jnp.max(shift) <= MAX_FIXED_SHIFT,
                 lambda: _gqa_window_fast_call(*args, shift),
                 lambda: _gqa_window_call(*args))
    return [None, _outproj_call(st_x, a, xx, mods, w_out.astype(BF16), "gqa_out")]


def _cast_experts_call(w, layer):
    _, ne, a, b = w.shape
    eb = 4

    def kern(w_ref, o_ref):
        o_ref[...] = w_ref[...].astype(BF16)

    return pl.pallas_call(
        kern, grid=(ne // eb,),
        in_specs=[pl.BlockSpec((None, eb, a, b), lambda i: (layer, i, 0, 0))],
        out_specs=pl.BlockSpec((eb, a, b), lambda i: (i, 0, 0)),
        out_shape=SDS((ne, a, b), BF16), name="cast_experts",
        compiler_params=_params(("parallel",), 32))(w)


def _route(logits_t, bias_col):
    scores = _sigmoid(logits_t)
    biased = scores + bias_col
    rows = [biased[e:e + 1, :] for e in range(N_EXPERTS)]
    srow = [scores[e:e + 1, :] for e in range(N_EXPERTS)]
    epg = EXPERTS_PER_GROUP
    gscore = []
    for g in range(N_EXPERT_GROUPS):
        v = rows[g * epg:(g + 1) * epg]
        pair = [v[a] + v[b] for a in range(epg) for b in range(a + 1, epg)]
        gscore.append(functools.reduce(jnp.maximum, pair))
    ind, wloc = [], [None] * epg
    for g in range(N_EXPERT_GROUPS):
        best = None
        for g2 in range(N_EXPERT_GROUPS):
            if g2 == g:
                continue
            cnd = gscore[g] > gscore[g2] if g2 < g else gscore[g] >= gscore[g2]
            best = cnd if best is None else best & cnd
        ind.append(jnp.where(best, 1.0, 0.0))
        for j in range(epg):
            e = g * epg + j
            rank = None
            for e2 in range(g * epg, (g + 1) * epg):
                if e2 == e:
                    continue
                ahead = rows[e2] >= rows[e] if e2 < e else rows[e2] > rows[e]
                one = jnp.where(ahead, 1.0, 0.0)
                rank = one if rank is None else rank + one
            w = jnp.where(best & (rank < 2.0), srow[e], 0.0)
            wloc[j] = w if wloc[j] is None else wloc[j] + w
    den = functools.reduce(lambda a, b: a + b, wloc)
    return ind, [w / den for w in wloc]


MOE_TILE = 1024
MOE_CHUNK = 256


def _moe_call(st, x, mods, gain, router_wt, router_bias, w_gate, w_up, w_down):
    n, d = x.shape
    t = min(MOE_TILE, n)
    r = MOE_CHUNK
    ne, _, ff = w_gate.shape
    epg, ng = EXPERTS_PER_GROUP, N_EXPERT_GROUPS
    nch = t // r + ng - 1
    tiles = n // t
    wd2 = w_down.reshape(ne * ff, d)

    def kern(x_ref, g_ref, sc_ref, sh_ref, g2_ref, rw_ref, rb_ref, wg_ref, wu_ref, wd_ref, o_ref,
             hb_ref, oh_ref, xg_ref):
        hb_ref[...] = _modnorm(x_ref[...], g_ref[...], sc_ref[...], sh_ref[...]).astype(BF16)
        ind, wloc = _route(_nt(rw_ref[...], hb_ref[...]), rb_ref[...])
        ind8 = jnp.concatenate(ind + [jnp.zeros((8 - ng, t), F32)], axis=0)
        lane = lax.broadcasted_iota(jnp.int32, (8, t), 1)
        csum, step = ind8, 1
        while step < t:
            csum = csum + jnp.where(lane >= step, pltpu.roll(csum, step, axis=1), 0.0)
            step *= 2
        pos = csum - ind8
        count = [jnp.sum(ind[g]).astype(jnp.int32) for g in range(ng)]
        first = [jnp.int32(0)]
        for g in range(ng):
            first.append(first[g] + (count[g] + (r - 1)) // r)
        slot = functools.reduce(lambda a, b: a + b, [
            ind[g] * (pos[g:g + 1, :] + (first[g] * r).astype(F32)) for g in range(ng)])
        w_hi = [w.astype(BF16) for w in wloc]
        w_lo = [(w - h.astype(F32)).astype(BF16) for w, h in zip(wloc, w_hi)]
        wst = jnp.concatenate(w_hi + w_lo, axis=0)
        row_id = lax.broadcasted_iota(jnp.int32, (r, t), 0).astype(F32)
        for c in range(nch):
            oh_ref[c * r:(c + 1) * r, :] = jnp.where(slot == row_id + float(c * r), 1.0, 0.0).astype(BF16)
        xg_ref[...] = _dot(oh_ref[...], hb_ref[...]).astype(BF16)
        wr_all = _nt(wst, oh_ref[...]).T
        for c in range(nch):
            rows = slice(c * r, (c + 1) * r)

            @pl.when(c < first[ng])
            def _():
                grp = functools.reduce(lambda a, b: a + b,
                                       [(c >= first[g]).astype(jnp.int32) for g in range(1, ng)])
                xg = xg_ref[rows, :]
                wr = wr_all[c * r:(c + 1) * r, :]
                acts = []
                for j in range(epg):
                    e = grp * epg + j
                    gt = _dot(xg, wg_ref[e])
                    up = _dot(xg, wu_ref[e])
                    acts.append((gt * _sigmoid(gt) * up * (wr[:, j:j + 1] + wr[:, epg + j:epg + j + 1])).astype(BF16))
                wd_g = wd_ref[pl.ds(pl.multiple_of(grp * (epg * ff), epg * ff), epg * ff), :]
                xg_ref[rows, :] = _dot(jnp.concatenate(acts, axis=1), wd_g).astype(BF16)

        back = lax.dot_general(oh_ref[...], xg_ref[...], (((0,), (0,)), ((), ())), preferred_element_type=F32)
        o_ref[...] = x_ref[...] + g2_ref[...] * back

    row = (lambda i: st.batch) if st.is_ctx else (lambda i: i // (st.seg // t))
    mod = lambda chunk: pl.BlockSpec((None, 1, d), lambda i: (row(i), 0, chunk))
    return pl.pallas_call(
        kern, grid=(tiles,),
        in_specs=[pl.BlockSpec((t, d), lambda i: (i, 0)),
                  _full((1, d)), mod(SC2), mod(SH2), mod(G2), _full((ne, d)), _full((ne, 1)),
                  _resident((ne, d, ff)), _resident((ne, d, ff)), _resident((ne * ff, d))],
        out_specs=pl.BlockSpec((t, d), lambda i: (i, 0)),
        out_shape=SDS((n, d), F32),
        scratch_shapes=[pltpu.VMEM((t, d), BF16), pltpu.VMEM((nch * r, t), BF16),
                        pltpu.VMEM((nch * r, d), BF16)],
        name="moe", compiler_params=_params(("parallel",), 58))(
            x, gain, mods, mods, mods, router_wt, router_bias, w_gate, w_up, wd2)


def kernel(x, c, ctx, c_ctx, w_ada, b_ada, norm_mix, norm_ffn, fourier_w_out, conv_w_in, conv_w, conv_w_out,
           mla_w_in, mla_q_norm, mla_kv_norm, mla_w_uq, mla_w_ukv, mla_q_gain, mla_k_gain, mla_w_out,
           gqa_w_in, gqa_q_gain, gqa_k_gain, gqa_sink, gqa_w_out, router_w, router_bias,
           moe_w_gate, moe_w_up, moe_w_down):
    b, l, d = x.shape
    lc = ctx.shape[1]
    depth = w_ada.shape[0]
    st_c = Stream(b * lc, lc, min(b * lc, TOKEN_TILE), b, True)
    st_x = Stream(b * l, l, min(l, TOKEN_TILE), b, False)
    streams = [st_c, st_x]
    r8 = -(-(b + 1) // 8) * 8
    cvec = jnp.concatenate([c, c_ctx[None, :], jnp.zeros((r8 - b - 1, d), F32)], axis=0)
    mods_all = _ada_call(cvec, w_ada, b_ada).reshape(depth, r8, 1, 6 * d)
    rope = _rope_tables(l)
    router_wt = router_w.T.astype(BF16)
    router_b = router_bias.reshape(-1, 1)
    xs = [ctx.reshape(b * lc, d), x.reshape(b * l, d)]
    for i in range(depth):
        kind, j = i % 4, i // 4
        ctx_next = i < depth - 1
        mods = mods_all[i]
        gain = norm_mix[i].reshape(1, d)
        if not (ctx_next or kind >= 2):
            xs[0] = None
        if kind == 0:
            ys = _fourier_layer(streams, xs, mods, gain, fourier_w_out[j])
        elif kind == 1:
            ys = _conv_layer(streams, xs, mods, gain, conv_w_in[j], conv_w[j], conv_w_out[j])
        elif kind == 2:
            ys = _mla_layer(streams, xs, mods, gain,
                            (mla_w_in[j], mla_q_norm[j], mla_kv_norm[j], mla_w_uq[j], mla_w_ukv[j],
                             mla_q_gain[j], mla_k_gain[j], mla_w_out[j]), rope, ctx_next)
        else:
            ys = _gqa_layer(streams, xs, mods, gain,
                            (gqa_w_in[j], gqa_q_gain[j], gqa_k_gain[j], gqa_sink[j], gqa_w_out[j]),
                            rope, ctx_next)
        if not ctx_next:
            ys[0] = None
        gain2 = norm_ffn[i].reshape(1, d)
        wg, wu, wd = (_cast_experts_call(w, i) for w in (moe_w_gate, moe_w_up, moe_w_down))
        xs = [None if y is None else _moe_call(st, y, mods, gain2, router_wt, router_b, wg, wu, wd)
              for st, y in zip(streams, ys)]
    return xs[1].reshape(b, l, d)
```

```python
import functools

import numpy as np
import jax
import jax.numpy as jnp
from jax import lax
from jax.experimental import pallas as pl
from jax.experimental.pallas import tpu as pltpu

F32, BF16 = jnp.float32, jnp.bfloat16
SDS = jax.ShapeDtypeStruct

EPS = 1e-6
GRID_W = 64
ROPE_THETA = 10000.0
N_FOURIER_GROUPS = 4
MLA_HEADS, MLA_Q_RANK, MLA_KV_RANK = 16, 256, 128
MLA_NOPE, MLA_ROPE, MLA_V = 128, 64, 128
MLA_SCALE = (MLA_NOPE + MLA_ROPE) ** -0.5
GQA_Q_HEADS, GQA_KV_HEADS, GQA_HEAD_DIM = 16, 4, 64
GQA_GROUP = GQA_Q_HEADS // GQA_KV_HEADS
GQA_SCALE = GQA_HEAD_DIM ** -0.5
WINDOW = 128
BLOCK = 128
N_EXPERTS, N_EXPERT_GROUPS, EXPERT_FF = 16, 4, 256
EXPERTS_PER_GROUP = N_EXPERTS // N_EXPERT_GROUPS
ROPE_DIM = 64
LOG2E = 1.4426950408889634
MAX_FIXED_SHIFT = 50.0
SHIFT_MARGIN = 1.02
TOKEN_TILE = 1024
PROJ_TILE = 512

V7X_VMEM_BYTES = 64 * 1024 * 1024
LANES = 128
SH1, SC1, G1, SH2, SC2, G2 = range(6)


def _params(sem, vmem_mb):
    return pltpu.CompilerParams(dimension_semantics=sem, vmem_limit_bytes=vmem_mb * 1024 * 1024)


def _sigmoid(v):
    return 1.0 / (1.0 + jnp.exp(-v))


def _modnorm(x, gain, sc, sh):
    ms = jnp.mean(x * x, axis=-1, keepdims=True)
    return x * lax.rsqrt(ms + EPS) * (gain * (1.0 + sc)) + sh


def _nt(a, b):
    return lax.dot_general(a, b, (((1,), (1,)), ((), ())), preferred_element_type=F32)


def _dot(a, b):
    return jnp.dot(a, b, preferred_element_type=F32)


class Stream:
    def __init__(self, n, seg, tile, batch, is_ctx):
        self.n, self.seg, self.tile, self.batch, self.is_ctx = n, seg, tile, batch, is_ctx
        self.tiles = n // tile
        self.tiles_per_seg = seg // tile

    def mod_row(self, t):
        return self.batch if self.is_ctx else t // self.tiles_per_seg

    def seg_row(self, b):
        return self.batch if self.is_ctx else b

    def with_tile(self, tile):
        return Stream(self.n, self.seg, min(tile, self.tile), self.batch, self.is_ctx)


def _mod_spec(st, chunk, d):
    return pl.BlockSpec((None, 1, d), lambda t: (st.mod_row(t), 0, chunk))


def _full(shape):
    nd = len(shape)
    return pl.BlockSpec(shape, lambda *_: (0,) * nd)


def _resident(shape):
    nd = len(shape)
    return pl.BlockSpec(shape, lambda *_: (0,) * nd, pipeline_mode=pl.Buffered(1))


def _ada_call(cvec, w_ada, b_ada):
    depth, d, d6 = w_ada.shape
    r8 = cvec.shape[0]
    tn = d6 // 4

    def kern(c_ref, w_ref, b_ref, o_ref):
        c = c_ref[...]
        s = (c * _sigmoid(c)).astype(BF16)
        o_ref[...] = _dot(s, w_ref[...].astype(BF16)) + b_ref[...]

    return pl.pallas_call(
        kern, grid=(depth, d6 // tn),
        in_specs=[pl.BlockSpec((r8, d), lambda i, j: (0, 0)),
                  pl.BlockSpec((None, d, tn), lambda i, j: (i, 0, j)),
                  pl.BlockSpec((None, 1, tn), lambda i, j: (i, 0, j))],
        out_specs=pl.BlockSpec((None, r8, tn), lambda i, j: (i, 0, j)),
        out_shape=SDS((depth, r8, d6), F32), name="ada",
        compiler_params=_params(("arbitrary", "arbitrary"), 40))(cvec, w_ada, b_ada.reshape(depth, 1, d6))


def _proj_call(st, x, mods, gain, w, name):
    n, d = x.shape
    nout = w.shape[1]
    t = st.tile

    def kern(x_ref, g_ref, sc_ref, sh_ref, w_ref, o_ref):
        h = _modnorm(x_ref[...], g_ref[...], sc_ref[...], sh_ref[...]).astype(BF16)
        o_ref[...] = _dot(h, w_ref[...]).astype(BF16)

    return pl.pallas_call(
        kern, grid=(st.tiles,),
        in_specs=[pl.BlockSpec((t, d), lambda i: (i, 0)), _full((1, d)),
                  _mod_spec(st, SC1, d), _mod_spec(st, SH1, d), _resident((d, nout))],
        out_specs=pl.BlockSpec((t, nout), lambda i: (i, 0)),
        out_shape=SDS((n, nout), BF16), name=name,
        compiler_params=_params(("parallel",), 40))(x, gain, mods, mods, w)


def _outproj_call(st, a, x, mods, w, name):
    n, d = x.shape
    k = a.shape[1]
    t = st.tile

    def kern(a_ref, x_ref, g1_ref, w_ref, o_ref):
        o_ref[...] = x_ref[...] + g1_ref[...] * _dot(a_ref[...], w_ref[...])

    return pl.pallas_call(
        kern, grid=(st.tiles,),
        in_specs=[pl.BlockSpec((t, k), lambda i: (i, 0)), pl.BlockSpec((t, d), lambda i: (i, 0)),
                  _mod_spec(st, G1, d), _resident((k, d))],
        out_specs=pl.BlockSpec((t, d), lambda i: (i, 0)),
        out_shape=SDS((n, d), F32), name=name,
        compiler_params=_params(("parallel",), 40))(a, x, mods, w)


def _dft_tables(length, radix):
    lr = length // radix
    m = np.arange(lr)[None, :, None]
    j = np.arange(radix)[:, None, None]
    nn = np.arange(lr)[None, None, :]
    ang = 2.0 * np.pi * (((radix * m + j) * nn) % length) / length
    e = np.concatenate([np.cos(ang), np.sin(ang)], axis=-1) / np.sqrt(length)
    return jnp.asarray(e, dtype=F32).astype(BF16)


def _group_dft_tables(group):
    k = np.arange(group)
    ang = 2.0 * np.pi * ((k[:, None] * k[None, :]) % group) / group
    return (jnp.asarray(np.cos(ang) / np.sqrt(group), dtype=F32),
            jnp.asarray(np.sin(ang) / np.sqrt(group), dtype=F32))


def _fourier_weight_call(w_out):
    d = w_out.shape[0]
    grp = d // N_FOURIER_GROUPS
    cg, sg = _group_dft_tables(grp)

    def kern(cg_ref, sg_ref, w_ref, o_ref):
        w = w_ref[...]
        o_ref[:, :d] = jnp.dot(cg_ref[...], w, preferred_element_type=F32,
                               precision=lax.Precision.HIGHEST).astype(BF16)
        o_ref[:, d:] = jnp.dot(sg_ref[...], w, preferred_element_type=F32,
                               precision=lax.Precision.HIGHEST).astype(BF16)

    return pl.pallas_call(
        kern, grid=(N_FOURIER_GROUPS,),
        in_specs=[_full((grp, grp)), _full((grp, grp)), pl.BlockSpec((grp, d), lambda g: (g, 0))],
        out_specs=pl.BlockSpec((grp, 2 * d), lambda g: (g, 0)),
        out_shape=SDS((d, 2 * d), BF16), name="fourier_w",
        compiler_params=_params(("arbitrary",), 32))(cg, sg, w_out)


def _radix_terms(radix, j):
    real, imag = [], []
    for q in range(radix):
        k = (j * q * (4 // radix)) % 4 if radix > 1 else 0
        if k == 0:
            real.append((1, 0, q)); imag.append((-1, 1, q))
        elif k == 1:
            real.append((-1, 1, q)); imag.append((-1, 0, q))
        elif k == 2:
            real.append((-1, 0, q)); imag.append((1, 1, q))
        else:
            real.append((1, 1, q)); imag.append((1, 0, q))
    return real, imag


def _seq_dft_call(st, p, x, mods, radix):
    n, d = x.shape
    seg = st.seg
    lr = seg // radix
    e = _dft_tables(seg, radix)
    nb = n // seg
    cblk = 2 * LANES if seg * d * 4 > 2 ** 22 else d
    ncb = d // cblk

    def kern(pc_ref, ps_ref, x_ref, e_ref, g1_ref, o_ref, v_ref, *z_refs):
        parts = (pc_ref, ps_ref)
        for j in range(radix):
            real, imag = _radix_terms(radix, j)

            def comb(terms):
                acc = None
                for sgn, part, q in terms:
                    v = parts[part][q * lr:(q + 1) * lr, :].astype(F32)
                    if acc is None:
                        acc = v if sgn > 0 else -v
                    else:
                        acc = acc + v if sgn > 0 else acc - v
                return acc

            v_ref[0:lr, :] = comb(real).astype(BF16)
            v_ref[lr:2 * lr, :] = comb(imag).astype(BF16)
            z = _dot(e_ref[j], v_ref[...])
            for k, z_ref in enumerate(z_refs):
                z_ref[pl.ds(j, lr, stride=radix), :] = z[:, k * LANES:(k + 1) * LANES]
        for k, z_ref in enumerate(z_refs):
            sl = slice(k * LANES, (k + 1) * LANES)
            o_ref[:, sl] = x_ref[:, sl] + g1_ref[:, sl] * z_ref[...]

    return pl.pallas_call(
        kern, grid=(nb, ncb),
        in_specs=[pl.BlockSpec((seg, cblk), lambda b, c: (b, c)),
                  pl.BlockSpec((seg, cblk), lambda b, c: (b, ncb + c)),
                  pl.BlockSpec((seg, cblk), lambda b, c: (b, c)),
                  _resident((radix, lr, 2 * lr)),
                  pl.BlockSpec((None, 1, cblk), lambda b, c: (st.seg_row(b), 0, G1 * ncb + c))],
        out_specs=pl.BlockSpec((seg, cblk), lambda b, c: (b, c)),
        out_shape=SDS((n, d), F32),
        scratch_shapes=[pltpu.VMEM((2 * lr, cblk), BF16)] + [pltpu.VMEM((seg, LANES), F32)] * (cblk // LANES),
        name=f"seq_dft_r{radix}",
        compiler_params=_params(("parallel", "parallel"), 40))(p, p, x, e, mods)


def _fourier_layer(streams, xs, mods, gain, w_out):
    wcs = _fourier_weight_call(w_out)
    outs = []
    for st, x in zip(streams, xs):
        if x is None:
            outs.append(None)
            continue
        p = _proj_call(st, x, mods, gain, wcs, "fourier_proj")
        outs.append(_seq_dft_call(st, p, x, mods, 4 if st.seg >= 1024 else 1))
    return outs


def _conv_in_call(st, x, mods, gain, w_in):
    n, d = x.shape
    t = st.tile

    def kern(x_ref, g_ref, sc_ref, sh_ref, w_ref, bg_ref, u_ref):
        h = _modnorm(x_ref[...], g_ref[...], sc_ref[...], sh_ref[...]).astype(BF16)
        bg_ref[...] = _dot(h, w_ref[:, 0:d]).astype(BF16)
        u_ref[...] = (_dot(h, w_ref[:, d:2 * d]) * _dot(h, w_ref[:, 2 * d:3 * d])).astype(BF16)

    return pl.pallas_call(
        kern, grid=(st.tiles,),
        in_specs=[pl.BlockSpec((t, d), lambda i: (i, 0)), _full((1, d)),
                  _mod_spec(st, SC1, d), _mod_spec(st, SH1, d), _resident((d, 3 * d))],
        out_specs=[pl.BlockSpec((t, d), lambda i: (i, 0))] * 2,
        out_shape=[SDS((n, d), BF16)] * 2, name="conv_in",
        compiler_params=_params(("parallel",), 48))(x, gain, mods, mods, w_in)


def _conv_out_call(st, bg, u, x, mods, conv_w, w_out):
    n, d = x.shape
    t = st.tile
    halo = 16
    hb = t // halo
    nhalo = n // halo
    seg = st.seg

    def kern(u_ref, up_ref, un_ref, bg_ref, x_ref, cw_ref, g1_ref, w_ref, o_ref):
        i = pl.program_id(0)
        u = u_ref[...].astype(F32)
        row = lax.broadcasted_iota(jnp.int32, (t, 1), 0)
        pos = (i * t + row) & (seg - 1)
        prev_row = up_ref[...].astype(F32)[halo - 1:halo, :]
        next_row = un_ref[...].astype(F32)[0:1, :]
        um = jnp.where(row == 0, prev_row, pltpu.roll(u, 1, axis=0))
        um = jnp.where(pos == 0, 0.0, um)
        up = jnp.where(row == t - 1, next_row, pltpu.roll(u, t - 1, axis=0))
        up = jnp.where(pos == seg - 1, 0.0, up)
        z = cw_ref[0:1, :] * um + cw_ref[1:2, :] * u + cw_ref[2:3, :] * up
        a = (bg_ref[...].astype(F32) * z).astype(BF16)
        o_ref[...] = x_ref[...] + g1_ref[...] * _dot(a, w_ref[...])

    tile_spec = pl.BlockSpec((t, d), lambda i: (i, 0))
    return pl.pallas_call(
        kern, grid=(st.tiles,),
        in_specs=[tile_spec,
                  pl.BlockSpec((halo, d), lambda i: (jnp.maximum(i * hb - 1, 0), 0)),
                  pl.BlockSpec((halo, d), lambda i: (jnp.minimum((i + 1) * hb, nhalo - 1), 0)),
                  tile_spec, tile_spec, _full((3, d)), _mod_spec(st, G1, d), _resident((d, d))],
        out_specs=tile_spec, out_shape=SDS((n, d), F32), name="conv_out",
        compiler_params=_params(("parallel",), 40))(u, u, u, bg, x, conv_w, mods, w_out)


def _conv_layer(streams, xs, mods, gain, w_in, conv_w, w_out):
    w_in_b, w_out_b = w_in.astype(BF16), w_out.astype(BF16)
    outs = []
    for st, x in zip(streams, xs):
        if x is None:
            outs.append(None)
            continue
        bg, u = _conv_in_call(st, x, mods, gain, w_in_b)
        outs.append(_conv_out_call(st, bg, u, x, mods, conv_w, w_out_b))
    return outs


def _rope_tables(length):
    q = ROPE_DIM // 4
    pos = np.arange(length)
    rc = np.stack([pos // GRID_W, pos % GRID_W], axis=1).astype(np.float32)
    inv = (ROPE_THETA ** (-np.arange(q, dtype=np.float32) / q)).astype(np.float32)
    lane = np.arange(ROPE_DIM)
    ang = rc[:, lane >> 5] * inv[lane & (q - 1)][None, :]
    sign = np.where((lane & q) == 0, -1.0, 1.0)[None, :]
    cos = np.cos(ang.astype(np.float32)).astype(np.float32)
    sin = (np.sin(ang.astype(np.float32)) * sign).astype(np.float32)
    reps = LANES // ROPE_DIM
    return jnp.asarray(np.tile(cos, (1, reps))), jnp.asarray(np.tile(sin, (1, reps)))


def _rope_partner_perm(n_heads):
    lane = np.arange(n_heads * ROPE_DIM)
    return lane ^ (ROPE_DIM // 4)


def _seg_rinv(x, seg):
    t, w = x.shape
    cols = []
    for c in range(w // LANES):
        blk = x[:, c * LANES:(c + 1) * LANES]
        sq = blk * blk
        if seg == LANES:
            cols.append(jnp.broadcast_to(
                lax.rsqrt(jnp.mean(sq, axis=-1, keepdims=True) + EPS), (t, LANES)))
        else:
            lane = lax.broadcasted_iota(jnp.int32, (t, LANES), 1)
            low = lane < seg
            lo = jnp.sum(jnp.where(low, sq, 0.0), axis=-1, keepdims=True) * (1.0 / seg)
            hi = jnp.sum(jnp.where(low, 0.0, sq), axis=-1, keepdims=True) * (1.0 / seg)
            cols.append(jnp.where(low, lax.rsqrt(lo + EPS), lax.rsqrt(hi + EPS)))
    return cols[0] if len(cols) == 1 else jnp.concatenate(cols, axis=-1)


def _tile_lanes(v, w):
    reps = w // LANES
    return v if reps == 1 else jnp.concatenate([v] * reps, axis=-1)


def _mla_proj_call(st, x, mods, gain, wts, rope, need_q):
    n, d = x.shape
    t = st.tile
    h = MLA_HEADS
    wn, wr = h * MLA_NOPE, h * MLA_ROPE
    (w_in, q_norm, kv_norm, w_uq, w_ukv, qg_n, qg_r, qg_rp, kg_n, kg_r, kg_rp) = wts
    use_rope = rope is not None

    def kern(*refs):
        refs = list(refs)
        x_ref, g_ref, sc_ref, sh_ref, win_ref, qn_ref_, kvn_ref_, wuq_ref, wukv_ref = refs[:9]
        qgn_ref, qgr_ref, qgrp_ref, kgn_ref, kgr_ref, kgrp_ref = refs[9:15]
        rest = refs[15:]
        if use_rope:
            cos_ref, sin_ref = rest[:2]
            rest = rest[2:]
        if need_q:
            oqn_ref, oqr_ref = rest[:2]
            rest = rest[2:]
        okn_ref, okr_ref, ov_ref = rest
        hm = _modnorm(x_ref[...], g_ref[...], sc_ref[...], sh_ref[...]).astype(BF16)
        ck = _dot(hm, win_ref[...])
        kr2 = ck[:, MLA_Q_RANK + MLA_KV_RANK:]
        kr_rinv = _seg_rinv(kr2, MLA_ROPE)[:, 0:MLA_ROPE]
        kr_raw = kr2[:, 0:MLA_ROPE] * kgr_ref[...]
        if use_rope:
            kr_par = kr2[:, MLA_ROPE:] * kgrp_ref[...]
            kr = kr_rinv * (kr_raw * cos_ref[:, 0:MLA_ROPE] + kr_par * sin_ref[:, 0:MLA_ROPE])
        else:
            kr = kr_rinv * kr_raw
        okr_ref[...] = kr.astype(BF16)
        ckv = ck[:, MLA_Q_RANK:MLA_Q_RANK + MLA_KV_RANK]
        ckv = (ckv * _seg_rinv(ckv, LANES) * kvn_ref_[...]).astype(BF16)
        kv = _dot(ckv, wukv_ref[...])
        kn = kv[:, 0:wn]
        okn_ref[...] = (kn * _seg_rinv(kn, MLA_NOPE) * kgn_ref[...]).astype(BF16)
        ov_ref[...] = kv[:, wn:].astype(BF16)
        if need_q:
            cq = ck[:, 0:MLA_Q_RANK]
            rq = lax.rsqrt(jnp.mean(cq * cq, axis=-1, keepdims=True) + EPS)
            cq = (cq * rq * qn_ref_[...]).astype(BF16)
            q = _dot(cq, wuq_ref[...])
            qn = q[:, 0:wn]
            oqn_ref[...] = (qn * _seg_rinv(qn, MLA_NOPE) * (qgn_ref[...] * (MLA_SCALE * LOG2E))).astype(BF16)
            qr_raw = q[:, wn:wn + wr]
            rinv = _seg_rinv(qr_raw, MLA_ROPE) * (MLA_SCALE * LOG2E)
            if use_rope:
                qr_par = q[:, wn + wr:]
                cos = _tile_lanes(cos_ref[...], wr)
                sin = _tile_lanes(sin_ref[...], wr)
                qr = rinv * (qr_raw * qgr_ref[...] * cos + qr_par * qgrp_ref[...] * sin)
            else:
                qr = rinv * (qr_raw * qgr_ref[...])
            oqr_ref[...] = qr.astype(BF16)

    tile = lambda w: pl.BlockSpec((t, w), lambda i: (i, 0))
    in_specs = [tile(d), _full((1, d)), _mod_spec(st, SC1, d), _mod_spec(st, SH1, d),
                _resident(w_in.shape), _full(q_norm.shape), _full(kv_norm.shape),
                _resident(w_uq.shape), _resident(w_ukv.shape),
                _full(qg_n.shape), _full(qg_r.shape), _full(qg_rp.shape),
                _full(kg_n.shape), _full(kg_r.shape), _full(kg_rp.shape)]
    args = [x, gain, mods, mods, w_in, q_norm, kv_norm, w_uq, w_ukv, qg_n, qg_r, qg_rp, kg_n, kg_r, kg_rp]
    if use_rope:
        tps = st.tiles_per_seg
        in_specs += [pl.BlockSpec((t, LANES), lambda i: (i % tps, 0))] * 2
        args += list(rope)
    out_specs, out_shape = [], []
    if need_q:
        out_specs += [tile(wn), tile(wr)]
        out_shape += [SDS((n, wn), BF16), SDS((n, wr), BF16)]
    out_specs += [tile(wn), tile(MLA_ROPE), tile(h * MLA_V)]
    out_shape += [SDS((n, wn), BF16), SDS((n, MLA_ROPE), BF16), SDS((n, h * MLA_V), BF16)]
    outs = pl.pallas_call(
        kern, grid=(st.tiles,), in_specs=in_specs, out_specs=out_specs, out_shape=out_shape,
        name="mla_proj", compiler_params=_params(("parallel",), 48))(*args)
    if need_q:
        return tuple(outs)
    return (None, None) + tuple(outs)


def _mla_attn_fast_call(batch, lq, tq, q, kv_sets, shift):
    qn, qr = q
    lk_total = sum(s[3] for s in kv_sets)
    hp = 2 if lk_total > 1024 else 8
    n_hp = MLA_HEADS // hp
    nq = lq // tq
    nsets = len(kv_sets)
    lengths = [s[3] for s in kv_sets]
    lk = sum(lengths)
    kw = 2 * LANES
    pad = kw - MLA_NOPE - MLA_ROPE

    def kern(*refs):
        sh_ref, qn_ref, qr_ref = refs[:3]
        sets = [refs[3 + 3 * s:6 + 3 * s] for s in range(nsets)]
        o_ref, kq_ref, vq_ref = refs[-3:]

        @pl.when(pl.program_id(2) == 0)
        def _():
            lane_k = lax.broadcasted_iota(jnp.int32, (lk, pad), 1)
            lane_v = lax.broadcasted_iota(jnp.int32, (lk, kw - MLA_V), 1)
            for hh in range(hp):
                off = 0
                for (kn_ref, kr_ref, v_ref), length in zip(sets, lengths):
                    kq_ref[hh, off:off + length, 0:MLA_NOPE] = kn_ref[:, hh * MLA_NOPE:(hh + 1) * MLA_NOPE]
                    kq_ref[hh, off:off + length, MLA_NOPE:MLA_NOPE + MLA_ROPE] = kr_ref[...]
                    vq_ref[hh, off:off + length, 0:MLA_V] = v_ref[:, hh * MLA_V:(hh + 1) * MLA_V]
                    off += length
                kq_ref[hh, :, MLA_NOPE + MLA_ROPE:] = jnp.where(lane_k == 0, -sh_ref[:, 0:pad], 0.0).astype(BF16)
                vq_ref[hh, :, MLA_V:] = jnp.where(lane_v == 0, 1.0, 0.0).astype(BF16)

        lane_q = lax.broadcasted_iota(jnp.int32, (tq, pad), 1)
        one = jnp.where(lane_q == 0, 1.0, 0.0).astype(BF16)
        for hh in range(hp):
            qc = jnp.concatenate([qn_ref[:, hh * MLA_NOPE:(hh + 1) * MLA_NOPE],
                                  qr_ref[:, hh * MLA_ROPE:(hh + 1) * MLA_ROPE], one], axis=1)
            p = jnp.exp2(_nt(qc, kq_ref[hh]).astype(BF16))
            acc = _dot(p, vq_ref[hh])
            o_ref[:, hh * MLA_V:(hh + 1) * MLA_V] = (acc[:, 0:MLA_V] / acc[:, MLA_V:MLA_V + 1]).astype(BF16)

    in_specs = [_full((1, LANES)),
                pl.BlockSpec((tq, hp * MLA_NOPE), lambda b, h, i: (b * nq + i, h)),
                pl.BlockSpec((tq, hp * MLA_ROPE), lambda b, h, i: (b * nq + i, h))]
    args = [shift, qn, qr]
    for kn, kr, v, length in kv_sets:
        in_specs += [pl.BlockSpec((length, hp * MLA_NOPE), lambda b, h, i: (b, h)),
                     pl.BlockSpec((length, MLA_ROPE), lambda b, h, i: (b, 0)),
                     pl.BlockSpec((length, hp * MLA_V), lambda b, h, i: (b, h))]
        args += [kn, kr, v]
    return pl.pallas_call(
        kern, grid=(batch, n_hp, nq), in_specs=in_specs,
        out_specs=pl.BlockSpec((tq, hp * MLA_V), lambda b, h, i: (b * nq + i, h)),
        out_shape=SDS((batch * lq, MLA_HEADS * MLA_V), BF16),
        scratch_shapes=[pltpu.VMEM((hp, lk, kw), BF16), pltpu.VMEM((hp, lk, kw), BF16)],
        name="mla_attn_fast",
        compiler_params=_params(("parallel", "parallel", "arbitrary"), 56))(*args)


def _mla_attn_call(batch, lq, tq, q, kv_sets):
    qn, qr = q
    hp = 2
    n_hp = MLA_HEADS // hp
    nq = lq // tq
    nsets = len(kv_sets)

    def kern(*refs):
        qn_ref, qr_ref = refs[:2]
        o_ref = refs[-1]
        sets = [refs[2 + 3 * s:5 + 3 * s] for s in range(nsets)]
        for hh in range(hp):
            qn_h = qn_ref[:, hh * MLA_NOPE:(hh + 1) * MLA_NOPE]
            qr_h = qr_ref[:, hh * MLA_ROPE:(hh + 1) * MLA_ROPE]
            scores = [_nt(qn_h, kn_ref[:, hh * MLA_NOPE:(hh + 1) * MLA_NOPE]) + _nt(qr_h, kr_ref[...])
                      for kn_ref, kr_ref, _ in sets]
            m = functools.reduce(jnp.maximum, [jnp.max(s, axis=-1, keepdims=True) for s in scores])
            ps = [jnp.exp2(s - m) for s in scores]
            den = functools.reduce(lambda a, b: a + b, [jnp.sum(p, axis=-1, keepdims=True) for p in ps])
            acc = functools.reduce(lambda a, b: a + b, [
                _dot(p.astype(BF16), v_ref[:, hh * MLA_V:(hh + 1) * MLA_V])
                for p, (_, _, v_ref) in zip(ps, sets)])
            o_ref[:, hh * MLA_V:(hh + 1) * MLA_V] = (acc / den).astype(BF16)

    in_specs = [pl.BlockSpec((tq, hp * MLA_NOPE), lambda b, h, i: (b * nq + i, h)),
                pl.BlockSpec((tq, hp * MLA_ROPE), lambda b, h, i: (b * nq + i, h))]
    args = [qn, qr]
    for kn, kr, v, length in kv_sets:
        in_specs += [pl.BlockSpec((length, hp * MLA_NOPE), lambda b, h, i: (b, h)),
                     pl.BlockSpec((length, MLA_ROPE), lambda b, h, i: (b, 0)),
                     pl.BlockSpec((length, hp * MLA_V), lambda b, h, i: (b, h))]
        args += [kn, kr, v]
    return pl.pallas_call(
        kern, grid=(batch, n_hp, nq), in_specs=in_specs,
        out_specs=pl.BlockSpec((tq, hp * MLA_V), lambda b, h, i: (b * nq + i, h)),
        out_shape=SDS((batch * lq, MLA_HEADS * MLA_V), BF16), name="mla_attn",
        compiler_params=_params(("parallel", "parallel", "arbitrary"), 56))(*args)


def _mla_layer(streams, xs, mods, gain, p, rope, ctx_next):
    st_c, st_x = streams
    xc, xx = xs
    h = MLA_HEADS
    w_in, q_norm, kv_norm, w_uq, w_ukv, q_gain, k_gain, w_out = p
    par = _rope_partner_perm(1)
    kr_cols = w_in[:, MLA_Q_RANK + MLA_KV_RANK:]
    w_in_p = jnp.concatenate([w_in, kr_cols[:, par]], axis=1).astype(BF16)
    wq = w_uq.reshape(MLA_Q_RANK, h, MLA_NOPE + MLA_ROPE)
    wq_n = wq[:, :, :MLA_NOPE].reshape(MLA_Q_RANK, h * MLA_NOPE)
    wq_r = wq[:, :, MLA_NOPE:]
    w_uq_p = jnp.concatenate([wq_n, wq_r.reshape(MLA_Q_RANK, h * MLA_ROPE),
                              wq_r[:, :, par].reshape(MLA_Q_RANK, h * MLA_ROPE)], axis=1).astype(BF16)
    wkv = w_ukv.reshape(MLA_KV_RANK, h, MLA_NOPE + MLA_V)
    w_ukv_p = jnp.concatenate([wkv[:, :, :MLA_NOPE].reshape(MLA_KV_RANK, h * MLA_NOPE),
                               wkv[:, :, MLA_NOPE:].reshape(MLA_KV_RANK, h * MLA_V)], axis=1).astype(BF16)
    qg_r = q_gain[MLA_NOPE:]
    kg_r = k_gain[MLA_NOPE:]
    wts = (w_in_p, q_norm.reshape(1, -1), kv_norm.reshape(1, -1), w_uq_p, w_ukv_p,
           jnp.tile(q_gain[:MLA_NOPE], h).reshape(1, -1), jnp.tile(qg_r, h).reshape(1, -1),
           jnp.tile(qg_r[par], h).reshape(1, -1), jnp.tile(k_gain[:MLA_NOPE], h).reshape(1, -1),
           kg_r.reshape(1, -1), kg_r[par].reshape(1, -1))
    w_out_b = w_out.astype(BF16)
    qn_c, qr_c, kn_c, kr_c, v_c = _mla_proj_call(st_c.with_tile(PROJ_TILE), xc, mods, gain, wts, None, ctx_next)
    qn_x, qr_x, kn_x, kr_x, v_x = _mla_proj_call(st_x.with_tile(PROJ_TILE), xx, mods, gain, wts, rope, True)
    b = st_x.batch
    amax = lambda g: jnp.max(jnp.abs(g))
    bound = LOG2E * MLA_SCALE * (MLA_NOPE * amax(q_gain[:MLA_NOPE]) * amax(k_gain[:MLA_NOPE])
                                 + MLA_ROPE * amax(qg_r) * amax(kg_r))
    shift = bound * SHIFT_MARGIN + 1.0
    shift_row = jnp.full((1, LANES), shift, F32)

    def attend(lq, tq, q, kv_sets):
        return lax.cond(shift <= MAX_FIXED_SHIFT,
                        lambda: _mla_attn_fast_call(b, lq, tq, q, kv_sets, shift_row),
                        lambda: _mla_attn_call(b, lq, tq, q, kv_sets))

    a_x = attend(st_x.seg, min(st_x.seg, 1024), (qn_x, qr_x), [(kn_c, kr_c, v_c, st_c.seg), (kn_x, kr_x, v_x, st_x.seg)])
    out_x = _outproj_call(st_x, a_x, xx, mods, w_out_b, "mla_out")
    out_c = None
    if ctx_next:
        a_c = attend(st_c.seg, st_c.seg, (qn_c, qr_c), [(kn_c, kr_c, v_c, st_c.seg)])
        out_c = _outproj_call(st_c, a_c, xc, mods, w_out_b, "mla_out")
    return [out_c, out_x]


def _gqa_proj_call(st, x, mods, gain, wts, rope, need_q):
    n, d = x.shape
    t = st.tile
    wq, wk = GQA_Q_HEADS * GQA_HEAD_DIM, GQA_KV_HEADS * GQA_HEAD_DIM
    w_in, qg, qgp, kg, kgp = wts
    use_rope = rope is not None

    def kern(*refs):
        refs = list(refs)
        x_ref, g_ref, sc_ref, sh_ref, w_ref, qg_ref, qgp_ref, kg_ref, kgp_ref = refs[:9]
        rest = refs[9:]
        if use_rope:
            cos_ref, sin_ref = rest[:2]
            rest = rest[2:]
        if need_q:
            oq_ref = rest[0]
            rest = rest[1:]
        ok_ref, ov_ref = rest
        hm = _modnorm(x_ref[...], g_ref[...], sc_ref[...], sh_ref[...]).astype(BF16)
        kvp = _dot(hm, w_ref[:, 0:3 * wk])
        k_raw = kvp[:, 0:wk]
        k_rinv = _seg_rinv(k_raw, GQA_HEAD_DIM)
        ov_ref[...] = kvp[:, wk:2 * wk].astype(BF16)
        if use_rope:
            cos_k, sin_k = _tile_lanes(cos_ref[...], wk), _tile_lanes(sin_ref[...], wk)
            k = k_rinv * (k_raw * kg_ref[...] * cos_k + kvp[:, 2 * wk:] * kgp_ref[...] * sin_k)
        else:
            k = k_rinv * (k_raw * kg_ref[...])
        ok_ref[...] = k.astype(BF16)
        if need_q:
            qp = _dot(hm, w_ref[:, 3 * wk:])
            q_raw = qp[:, 0:wq]
            rinv = _seg_rinv(q_raw, GQA_HEAD_DIM) * (GQA_SCALE * LOG2E)
            if use_rope:
                cos_q, sin_q = _tile_lanes(cos_ref[...], wq), _tile_lanes(sin_ref[...], wq)
                q = rinv * (q_raw * qg_ref[...] * cos_q + qp[:, wq:] * qgp_ref[...] * sin_q)
            else:
                q = rinv * (q_raw * qg_ref[...])
            oq_ref[...] = q.astype(BF16)

    tile = lambda w: pl.BlockSpec((t, w), lambda i: (i, 0))
    in_specs = [tile(d), _full((1, d)), _mod_spec(st, SC1, d), _mod_spec(st, SH1, d),
                _resident(w_in.shape), _full(qg.shape), _full(qgp.shape), _full(kg.shape), _full(kgp.shape)]
    args = [x, gain, mods, mods, w_in, qg, qgp, kg, kgp]
    if use_rope:
        tps = st.tiles_per_seg
        in_specs += [pl.BlockSpec((t, LANES), lambda i: (i % tps, 0))] * 2
        args += list(rope)
    out_specs, out_shape = [], []
    if need_q:
        out_specs.append(tile(wq))
        out_shape.append(SDS((n, wq), BF16))
    out_specs += [tile(wk), tile(wk)]
    out_shape += [SDS((n, wk), BF16)] * 2
    outs = pl.pallas_call(
        kern, grid=(st.tiles,), in_specs=in_specs, out_specs=out_specs, out_shape=out_shape,
        name="gqa_proj", compiler_params=_params(("parallel",), 48))(*args)
    return tuple(outs) if need_q else (None,) + tuple(outs)


def _gqa_window_call(batch, length, lc, q, k, v, kc, vc, sink):
    nb = length // BLOCK
    hd = GQA_HEAD_DIM
    wq, wk = GQA_Q_HEADS * hd, GQA_KV_HEADS * hd
    rows = GQA_GROUP * BLOCK

    def kern(sink_ref, q_ref, kp_ref, k0_ref, kn_ref, vp_ref, v0_ref, vn_ref, kc_ref, vc_ref, o_ref):
        nblk = pl.program_id(1)
        r = lax.broadcasted_iota(jnp.int32, (rows, 3 * BLOCK), 0) & (BLOCK - 1)
        c = lax.broadcasted_iota(jnp.int32, (rows, 3 * BLOCK), 1)
        valid = (c >= r + BLOCK - WINDOW) & (c <= r + BLOCK + WINDOW)
        valid = valid & ((c >= BLOCK) | (nblk > 0)) & ((c < 2 * BLOCK) | (nblk < nb - 1))
        hrow = lax.broadcasted_iota(jnp.int32, (rows, 1), 0) // BLOCK
        for g in range(GQA_KV_HEADS):
            sl = slice(g * hd, (g + 1) * hd)
            qg = jnp.concatenate([q_ref[:, (g * GQA_GROUP + j) * hd:(g * GQA_GROUP + j + 1) * hd]
                                  for j in range(GQA_GROUP)], axis=0)
            kband = jnp.concatenate([kp_ref[:, sl], k0_ref[:, sl], kn_ref[:, sl]], axis=0)
            vband = jnp.concatenate([vp_ref[:, sl], v0_ref[:, sl], vn_ref[:, sl]], axis=0)
            s_c = _nt(qg, kc_ref[:, sl])
            s_b = jnp.where(valid, _nt(qg, kband), -1e30)
            snk = jnp.zeros((rows, 1), F32)
            for j in range(GQA_GROUP):
                snk = jnp.where(hrow == j, sink_ref[g * GQA_GROUP + j], snk)
            m = jnp.maximum(jnp.maximum(jnp.max(s_c, axis=-1, keepdims=True),
                                        jnp.max(s_b, axis=-1, keepdims=True)), snk)
            p_c = jnp.exp2(s_c - m)
            p_b = jnp.exp2(s_b - m)
            den = (jnp.sum(p_c, axis=-1, keepdims=True) + jnp.sum(p_b, axis=-1, keepdims=True)
                   + jnp.exp2(snk - m))
            o = (_dot(p_c.astype(BF16), vc_ref[:, sl]) + _dot(p_b.astype(BF16), vband)) / den
            o_ref[:, g * GQA_GROUP * hd:(g + 1) * GQA_GROUP * hd] = jnp.concatenate(
                [o[j * BLOCK:(j + 1) * BLOCK, :] for j in range(GQA_GROUP)], axis=-1).astype(BF16)

    blk = lambda f: pl.BlockSpec((BLOCK, wk), f)
    prev_ = lambda b, i: (b * nb + jnp.maximum(i - 1, 0), 0)
    cur_ = lambda b, i: (b * nb + i, 0)
    next_ = lambda b, i: (b * nb + jnp.minimum(i + 1, nb - 1), 0)
    ctx_spec = pl.BlockSpec((lc, wk), lambda b, i: (b, 0))
    return pl.pallas_call(
        kern, grid=(batch, nb),
        in_specs=[pl.BlockSpec(memory_space=pltpu.SMEM),
                  pl.BlockSpec((BLOCK, wq), cur_), blk(prev_), blk(cur_), blk(next_),
                  blk(prev_), blk(cur_), blk(next_), ctx_spec, ctx_spec],
        out_specs=pl.BlockSpec((BLOCK, wq), cur_),
        out_shape=SDS((batch * length, wq), BF16), name="gqa_window",
        compiler_params=_params(("parallel", "arbitrary"), 40))(sink, q, k, k, k, v, v, v, kc, vc)


def _gqa_window_fast_call(batch, length, lc, q, k, v, kc, vc, sink2, shift):
    nb = length // BLOCK
    hd = GQA_HEAD_DIM
    wq, wk = GQA_Q_HEADS * hd, GQA_KV_HEADS * hd
    rows = GQA_GROUP * BLOCK

    def kern(sink_ref, shift_ref, q_ref, kp_ref, k0_ref, kn_ref, vp_ref, v0_ref, vn_ref, kc_ref, vc_ref, o_ref):
        nblk = pl.program_id(1)
        r = lax.broadcasted_iota(jnp.int32, (rows, 3 * BLOCK), 0) & (BLOCK - 1)
        c = lax.broadcasted_iota(jnp.int32, (rows, 3 * BLOCK), 1)
        valid = (c >= r + BLOCK - WINDOW) & (c <= r + BLOCK + WINDOW)
        valid = valid & ((c >= BLOCK) | (nblk > 0)) & ((c < 2 * BLOCK) | (nblk < nb - 1))
        hrow = lax.broadcasted_iota(jnp.int32, (rows, 1), 0) // BLOCK
        lane0_q = lax.broadcasted_iota(jnp.int32, (BLOCK, hd), 1) == 0

        def with_one(x):
            lane0 = lax.broadcasted_iota(jnp.int32, (x.shape[0], hd), 1) == 0
            return jnp.concatenate([x, jnp.where(lane0, 1.0, 0.0).astype(BF16)], axis=1)

        for g in range(GQA_KV_HEADS):
            sl = slice(g * hd, (g + 1) * hd)
            heads = [g * GQA_GROUP + j for j in range(GQA_GROUP)]
            qg = jnp.concatenate(
                [jnp.concatenate([q_ref[:, h * hd:(h + 1) * hd],
                                  jnp.where(lane0_q, -shift_ref[h], 0.0).astype(BF16)], axis=1)
                 for h in heads], axis=0)
            kband = with_one(jnp.concatenate([kp_ref[:, sl], k0_ref[:, sl], kn_ref[:, sl]], axis=0))
            vband = with_one(jnp.concatenate([vp_ref[:, sl], v0_ref[:, sl], vn_ref[:, sl]], axis=0))
            p_c = jnp.exp2(_nt(qg, with_one(kc_ref[:, sl])).astype(BF16))
            p_b = jnp.exp2(jnp.where(valid, _nt(qg, kband), -1e30).astype(BF16))
            acc = _dot(p_c, with_one(vc_ref[:, sl])) + _dot(p_b, vband)
            snk = jnp.zeros((rows, 1), F32)
            for j, h in enumerate(heads):
                snk = jnp.where(hrow == j, sink_ref[h] - shift_ref[h], snk)
            o = acc[:, 0:hd] / (acc[:, hd:hd + 1] + jnp.exp2(snk))
            o_ref[:, g * GQA_GROUP * hd:(g + 1) * GQA_GROUP * hd] = jnp.concatenate(
                [o[j * BLOCK:(j + 1) * BLOCK, :] for j in range(GQA_GROUP)], axis=-1).astype(BF16)

    blk = lambda f: pl.BlockSpec((BLOCK, wk), f)
    prev_ = lambda b, i: (b * nb + jnp.maximum(i - 1, 0), 0)
    cur_ = lambda b, i: (b * nb + i, 0)
    next_ = lambda b, i: (b * nb + jnp.minimum(i + 1, nb - 1), 0)
    ctx_spec = pl.BlockSpec((lc, wk), lambda b, i: (b, 0))
    smem = pl.BlockSpec(memory_space=pltpu.SMEM)
    return pl.pallas_call(
        kern, grid=(batch, nb),
        in_specs=[smem, smem, pl.BlockSpec((BLOCK, wq), cur_), blk(prev_), blk(cur_), blk(next_),
                  blk(prev_), blk(cur_), blk(next_), ctx_spec, ctx_spec],
        out_specs=pl.BlockSpec((BLOCK, wq), cur_),
        out_shape=SDS((batch * length, wq), BF16), name="gqa_window_fast",
        compiler_params=_params(("parallel", "arbitrary"), 40))(sink2, shift, q, k, k, k, v, v, v, kc, vc)


def _gqa_layer(streams, xs, mods, gain, p, rope, ctx_next):
    assert not ctx_next, "the windowed-GQA mixer is only implemented as the last layer"
    st_c, st_x = streams
    xc, xx = xs
    w_in, q_gain, k_gain, sink, w_out = p
    wq, wk = GQA_Q_HEADS * GQA_HEAD_DIM, GQA_KV_HEADS * GQA_HEAD_DIM
    w_q, w_k, w_v = w_in[:, :wq], w_in[:, wq:wq + wk], w_in[:, wq + wk:]
    w_in_p = jnp.concatenate([w_k, w_v, w_k[:, _rope_partner_perm(GQA_KV_HEADS)],
                              w_q, w_q[:, _rope_partner_perm(GQA_Q_HEADS)]], axis=1).astype(BF16)
    par = _rope_partner_perm(1)
    wts = (w_in_p, jnp.tile(q_gain, GQA_Q_HEADS).reshape(1, -1), jnp.tile(q_gain[par], GQA_Q_HEADS).reshape(1, -1),
           jnp.tile(k_gain, GQA_KV_HEADS).reshape(1, -1), jnp.tile(k_gain[par], GQA_KV_HEADS).reshape(1, -1))
    _, kc, vc = _gqa_proj_call(st_c.with_tile(PROJ_TILE), xc, mods, gain, wts, None, False)
    q, k, v = _gqa_proj_call(st_x.with_tile(PROJ_TILE), xx, mods, gain, wts, rope, True)
    sink2 = sink * LOG2E
    bound = LOG2E * GQA_SCALE * GQA_HEAD_DIM * jnp.max(jnp.abs(q_gain)) * jnp.max(jnp.abs(k_gain))
    shift = jnp.maximum(bound * SHIFT_MARGIN + 1.0, sink2).astype(BF16).astype(F32)
    args = (st_x.batch, st_x.seg, st_c.seg, q, k, v, kc, vc, sink2)
    a = lax.cond(jnp.max(shift) <= MAX_FIXED_SHIFT,
                 lambda: _gqa_window_fast_call(*args, shift),
                 lambda: _gqa_window_call(*args))
    return [None, _outproj_call(st_x, a, xx, mods, w_out.astype(BF16), "gqa_out")]


def _cast_experts_call(w, layer):
    _, ne, a, b = w.shape
    eb = 4

    def kern(w_ref, o_ref):
        o_ref[...] = w_ref[...].astype(BF16)

    return pl.pallas_call(
        kern, grid=(ne // eb,),
        in_specs=[pl.BlockSpec((None, eb, a, b), lambda i: (layer, i, 0, 0))],
        out_specs=pl.BlockSpec((eb, a, b), lambda i: (i, 0, 0)),
        out_shape=SDS((ne, a, b), BF16), name="cast_experts",
        compiler_params=_params(("parallel",), 32))(w)


def _route(logits_t, bias_col):
    scores = _sigmoid(logits_t)
    biased = scores + bias_col
    rows = [biased[e:e + 1, :] for e in range(N_EXPERTS)]
    srow = [scores[e:e + 1, :] for e in range(N_EXPERTS)]
    epg = EXPERTS_PER_GROUP
    gscore = []
    for g in range(N_EXPERT_GROUPS):
        v = rows[g * epg:(g + 1) * epg]
        pair = [v[a] + v[b] for a in range(epg) for b in range(a + 1, epg)]
        gscore.append(functools.reduce(jnp.maximum, pair))
    ind, wloc = [], [None] * epg
    for g in range(N_EXPERT_GROUPS):
        best = None
        for g2 in range(N_EXPERT_GROUPS):
            if g2 == g:
                continue
            cnd = gscore[g] > gscore[g2] if g2 < g else gscore[g] >= gscore[g2]
            best = cnd if best is None else best & cnd
        ind.append(jnp.where(best, 1.0, 0.0))
        for j in range(epg):
            e = g * epg + j
            rank = None
            for e2 in range(g * epg, (g + 1) * epg):
                if e2 == e:
                    continue
                ahead = rows[e2] >= rows[e] if e2 < e else rows[e2] > rows[e]
                one = jnp.where(ahead, 1.0, 0.0)
                rank = one if rank is None else rank + one
            w = jnp.where(best & (rank < 2.0), srow[e], 0.0)
            wloc[j] = w if wloc[j] is None else wloc[j] + w
    den = functools.reduce(lambda a, b: a + b, wloc)
    return ind, [w / den for w in wloc]


MOE_TILE = 1024
MOE_CHUNK = 128


def _moe_call(st, x, mods, gain, router_wt, router_bias, w_gate, w_up, w_down):
    n, d = x.shape
    t = min(MOE_TILE, n)
    r = MOE_CHUNK
    ne, _, ff = w_gate.shape
    epg, ng = EXPERTS_PER_GROUP, N_EXPERT_GROUPS
    nch = t // r + ng - 1
    tiles = n // t
    wd2 = w_down.reshape(ne * ff, d)

    def kern(x_ref, g_ref, sc_ref, sh_ref, g2_ref, rw_ref, rb_ref, wg_ref, wu_ref, wd_ref, o_ref,
             hb_ref, oh_ref, xg_ref):
        hb_ref[...] = _modnorm(x_ref[...], g_ref[...], sc_ref[...], sh_ref[...]).astype(BF16)
        ind, wloc = _route(_nt(rw_ref[...], hb_ref[...]), rb_ref[...])
        ind8 = jnp.concatenate(ind + [jnp.zeros((8 - ng, t), F32)], axis=0)
        lane = lax.broadcasted_iota(jnp.int32, (8, t), 1)
        csum, step = ind8, 1
        while step < t:
            csum = csum + jnp.where(lane >= step, pltpu.roll(csum, step, axis=1), 0.0)
            step *= 2
        pos = csum - ind8
        count = [jnp.sum(ind[g]).astype(jnp.int32) for g in range(ng)]
        first = [jnp.int32(0)]
        for g in range(ng):
            first.append(first[g] + (count[g] + (r - 1)) // r)
        slot = functools.reduce(lambda a, b: a + b, [
            ind[g] * (pos[g:g + 1, :] + (first[g] * r).astype(F32)) for g in range(ng)])
        w_hi = [w.astype(BF16) for w in wloc]
        w_lo = [(w - h.astype(F32)).astype(BF16) for w, h in zip(wloc, w_hi)]
        wst = jnp.concatenate(w_hi + w_lo, axis=0)
        row_id = lax.broadcasted_iota(jnp.int32, (r, t), 0).astype(F32)
        for c in range(nch):
            oh_ref[c * r:(c + 1) * r, :] = jnp.where(slot == row_id + float(c * r), 1.0, 0.0).astype(BF16)
        xg_ref[...] = _dot(oh_ref[...], hb_ref[...]).astype(BF16)
        wr_all = _nt(wst, oh_ref[...]).T
        for c in range(nch):
            rows = slice(c * r, (c + 1) * r)

            @pl.when(c < first[ng])
            def _():
                grp = functools.reduce(lambda a, b: a + b,
                                       [(c >= first[g]).astype(jnp.int32) for g in range(1, ng)])
                xg = xg_ref[rows, :]
                wr = wr_all[c * r:(c + 1) * r, :]
                acts = []
                for j in range(epg):
                    e = grp * epg + j
                    gt = _dot(xg, wg_ref[e])
                    up = _dot(xg, wu_ref[e])
                    acts.append((gt * _sigmoid(gt) * up * (wr[:, j:j + 1] + wr[:, epg + j:epg + j + 1])).astype(BF16))
                wd_g = wd_ref[pl.ds(pl.multiple_of(grp * (epg * ff), epg * ff), epg * ff), :]
                xg_ref[rows, :] = _dot(jnp.concatenate(acts, axis=1), wd_g).astype(BF16)

        back = lax.dot_general(oh_ref[...], xg_ref[...], (((0,), (0,)), ((), ())), preferred_element_type=F32)
        o_ref[...] = x_ref[...] + g2_ref[...] * back

    row = (lambda i: st.batch) if st.is_ctx else (lambda i: i // (st.seg // t))
    mod = lambda chunk: pl.BlockSpec((None, 1, d), lambda i: (row(i), 0, chunk))
    return pl.pallas_call(
        kern, grid=(tiles,),
        in_specs=[pl.BlockSpec((t, d), lambda i: (i, 0)),
                  _full((1, d)), mod(SC2), mod(SH2), mod(G2), _full((ne, d)), _full((ne, 1)),
                  _resident((ne, d, ff)), _resident((ne, d, ff)), _resident((ne * ff, d))],
        out_specs=pl.BlockSpec((t, d), lambda i: (i, 0)),
        out_shape=SDS((n, d), F32),
        scratch_shapes=[pltpu.VMEM((t, d), BF16), pltpu.VMEM((nch * r, t), BF16),
                        pltpu.VMEM((nch * r, d), BF16)],
        name="moe", compiler_params=_params(("parallel",), 58))(
            x, gain, mods, mods, mods, router_wt, router_bias, w_gate, w_up, wd2)


def kernel(x, c, ctx, c_ctx, w_ada, b_ada, norm_mix, norm_ffn, fourier_w_out, conv_w_in, conv_w, conv_w_out,
           mla_w_in, mla_q_norm, mla_kv_norm, mla_w_uq, mla_w_ukv, mla_q_gain, mla_k_gain, mla_w_out,
           gqa_w_in, gqa_q_gain, gqa_k_gain, gqa_sink, gqa_w_out, router_w, router_bias,
           moe_w_gate, moe_w_up, moe_w_down):
    b, l, d = x.shape
    lc = ctx.shape[1]
    depth = w_ada.shape[0]
    st_c = Stream(b * lc, lc, min(b * lc, TOKEN_TILE), b, True)
    st_x = Stream(b * l, l, min(l, TOKEN_TILE), b, False)
    streams = [st_c, st_x]
    r8 = -(-(b + 1) // 8) * 8
    cvec = jnp.concatenate([c, c_ctx[None, :], jnp.zeros((r8 - b - 1, d), F32)], axis=0)
    mods_all = _ada_call(cvec, w_ada, b_ada).reshape(depth, r8, 1, 6 * d)
    rope = _rope_tables(l)
    router_wt = router_w.T.astype(BF16)
    router_b = router_bias.reshape(-1, 1)
    xs = [ctx.reshape(b * lc, d), x.reshape(b * l, d)]
    for i in range(depth):
        kind, j = i % 4, i // 4
        ctx_next = i < depth - 1
        mods = mods_all[i]
        gain = norm_mix[i].reshape(1, d)
        if not (ctx_next or kind >= 2):
            xs[0] = None
        if kind == 0:
            ys = _fourier_layer(streams, xs, mods, gain, fourier_w_out[j])
        elif kind == 1:
            ys = _conv_layer(streams, xs, mods, gain, conv_w_in[j], conv_w[j], conv_w_out[j])
        elif kind == 2:
            ys = _mla_layer(streams, xs, mods, gain,
                            (mla_w_in[j], mla_q_norm[j], mla_kv_norm[j], mla_w_uq[j], mla_w_ukv[j],
                             mla_q_gain[j], mla_k_gain[j], mla_w_out[j]), rope, ctx_next)
        else:
            ys = _gqa_layer(streams, xs, mods, gain,
                            (gqa_w_in[j], gqa_q_gain[j], gqa_k_gain[j], gqa_sink[j], gqa_w_out[j]),
                            rope, ctx_next)
        if not ctx_next:
            ys[0] = None
        gain2 = norm_ffn[i].reshape(1, d)
        wg, wu, wd = (_cast_experts_call(w, i) for w in (moe_w_gate, moe_w_up, moe_w_down))
        xs = [None if y is None else _moe_call(st, y, mods, gain2, router_wt, router_b, wg, wu, wd)
              for st, y in zip(streams, ys)]
    return xs[1].reshape(b, l, d)
```

```python
import functools

import numpy as np
import jax
import jax.numpy as jnp
from jax import lax
from jax.experimental import pallas as pl
from jax.experimental.pallas import tpu as pltpu

F32, BF16 = jnp.float32, jnp.bfloat16
SDS = jax.ShapeDtypeStruct

EPS = 1e-6
GRID_W = 64
ROPE_THETA = 10000.0
N_FOURIER_GROUPS = 4
MLA_HEADS, MLA_Q_RANK, MLA_KV_RANK = 16, 256, 128
MLA_NOPE, MLA_ROPE, MLA_V = 128, 64, 128
MLA_SCALE = (MLA_NOPE + MLA_ROPE) ** -0.5
GQA_Q_HEADS, GQA_KV_HEADS, GQA_HEAD_DIM = 16, 4, 64
GQA_GROUP = GQA_Q_HEADS // GQA_KV_HEADS
GQA_SCALE = GQA_HEAD_DIM ** -0.5
WINDOW = 128
BLOCK = 128
N_EXPERTS, N_EXPERT_GROUPS, EXPERT_FF = 16, 4, 256
EXPERTS_PER_GROUP = N_EXPERTS // N_EXPERT_GROUPS
ROPE_DIM = 64
LOG2E = 1.4426950408889634
MAX_FIXED_SHIFT = 50.0
SHIFT_MARGIN = 1.02
TOKEN_TILE = 1024
PROJ_TILE = 512

V7X_VMEM_BYTES = 64 * 1024 * 1024
LANES = 128
SH1, SC1, G1, SH2, SC2, G2 = range(6)


def _params(sem, vmem_mb):
    return pltpu.CompilerParams(dimension_semantics=sem, vmem_limit_bytes=vmem_mb * 1024 * 1024)


def _sigmoid(v):
    return 1.0 / (1.0 + jnp.exp(-v))


def _modnorm(x, gain, sc, sh):
    ms = jnp.mean(x * x, axis=-1, keepdims=True)
    return x * lax.rsqrt(ms + EPS) * (gain * (1.0 + sc)) + sh


def _nt(a, b):
    return lax.dot_general(a, b, (((1,), (1,)), ((), ())), preferred_element_type=F32)


def _dot(a, b):
    return jnp.dot(a, b, preferred_element_type=F32)


class Stream:
    def __init__(self, n, seg, tile, batch, is_ctx):
        self.n, self.seg, self.tile, self.batch, self.is_ctx = n, seg, tile, batch, is_ctx
        self.tiles = n // tile
        self.tiles_per_seg = seg // tile

    def mod_row(self, t):
        return self.batch if self.is_ctx else t // self.tiles_per_seg

    def seg_row(self, b):
        return self.batch if self.is_ctx else b

    def with_tile(self, tile):
        return Stream(self.n, self.seg, min(tile, self.tile), self.batch, self.is_ctx)


def _mod_spec(st, chunk, d):
    return pl.BlockSpec((None, 1, d), lambda t: (st.mod_row(t), 0, chunk))


def _full(shape):
    nd = len(shape)
    return pl.BlockSpec(shape, lambda *_: (0,) * nd)


def _resident(shape):
    nd = len(shape)
    return pl.BlockSpec(shape, lambda *_: (0,) * nd, pipeline_mode=pl.Buffered(1))


def _ada_call(cvec, w_ada, b_ada):
    depth, d, d6 = w_ada.shape
    r8 = cvec.shape[0]
    tn = d6 // 4

    def kern(c_ref, w_ref, b_ref, o_ref):
        c = c_ref[...]
        s = (c * _sigmoid(c)).astype(BF16)
        o_ref[...] = _dot(s, w_ref[...].astype(BF16)) + b_ref[...]

    return pl.pallas_call(
        kern, grid=(depth, d6 // tn),
        in_specs=[pl.BlockSpec((r8, d), lambda i, j: (0, 0)),
                  pl.BlockSpec((None, d, tn), lambda i, j: (i, 0, j)),
                  pl.BlockSpec((None, 1, tn), lambda i, j: (i, 0, j))],
        out_specs=pl.BlockSpec((None, r8, tn), lambda i, j: (i, 0, j)),
        out_shape=SDS((depth, r8, d6), F32), name="ada",
        compiler_params=_params(("arbitrary", "arbitrary"), 40))(cvec, w_ada, b_ada.reshape(depth, 1, d6))


def _proj_call(st, x, mods, gain, w, name):
    n, d = x.shape
    nout = w.shape[1]
    t = st.tile

    def kern(x_ref, g_ref, sc_ref, sh_ref, w_ref, o_ref):
        h = _modnorm(x_ref[...], g_ref[...], sc_ref[...], sh_ref[...]).astype(BF16)
        o_ref[...] = _dot(h, w_ref[...]).astype(BF16)

    return pl.pallas_call(
        kern, grid=(st.tiles,),
        in_specs=[pl.BlockSpec((t, d), lambda i: (i, 0)), _full((1, d)),
                  _mod_spec(st, SC1, d), _mod_spec(st, SH1, d), _resident((d, nout))],
        out_specs=pl.BlockSpec((t, nout), lambda i: (i, 0)),
        out_shape=SDS((n, nout), BF16), name=name,
        compiler_params=_params(("parallel",), 40))(x, gain, mods, mods, w)


def _outproj_call(st, a, x, mods, w, name):
    n, d = x.shape
    k = a.shape[1]
    t = st.tile

    def kern(a_ref, x_ref, g1_ref, w_ref, o_ref):
        o_ref[...] = x_ref[...] + g1_ref[...] * _dot(a_ref[...], w_ref[...])

    return pl.pallas_call(
        kern, grid=(st.tiles,),
        in_specs=[pl.BlockSpec((t, k), lambda i: (i, 0)), pl.BlockSpec((t, d), lambda i: (i, 0)),
                  _mod_spec(st, G1, d), _resident((k, d))],
        out_specs=pl.BlockSpec((t, d), lambda i: (i, 0)),
        out_shape=SDS((n, d), F32), name=name,
        compiler_params=_params(("parallel",), 40))(a, x, mods, w)


def _dft_tables(length, radix):
    lr = length // radix
    m = np.arange(lr)[None, :, None]
    j = np.arange(radix)[:, None, None]
    nn = np.arange(lr)[None, None, :]
    ang = 2.0 * np.pi * (((radix * m + j) * nn) % length) / length
    e = np.concatenate([np.cos(ang), np.sin(ang)], axis=-1) / np.sqrt(length)
    return jnp.asarray(e, dtype=F32).astype(BF16)


def _group_dft_tables(group):
    k = np.arange(group)
    ang = 2.0 * np.pi * ((k[:, None] * k[None, :]) % group) / group
    return (jnp.asarray(np.cos(ang) / np.sqrt(group), dtype=F32),
            jnp.asarray(np.sin(ang) / np.sqrt(group), dtype=F32))


def _fourier_weight_call(w_out):
    d = w_out.shape[0]
    grp = d // N_FOURIER_GROUPS
    cg, sg = _group_dft_tables(grp)

    def kern(cg_ref, sg_ref, w_ref, o_ref):
        w = w_ref[...]
        o_ref[:, :d] = jnp.dot(cg_ref[...], w, preferred_element_type=F32,
                               precision=lax.Precision.HIGHEST).astype(BF16)
        o_ref[:, d:] = jnp.dot(sg_ref[...], w, preferred_element_type=F32,
                               precision=lax.Precision.HIGHEST).astype(BF16)

    return pl.pallas_call(
        kern, grid=(N_FOURIER_GROUPS,),
        in_specs=[_full((grp, grp)), _full((grp, grp)), pl.BlockSpec((grp, d), lambda g: (g, 0))],
        out_specs=pl.BlockSpec((grp, 2 * d), lambda g: (g, 0)),
        out_shape=SDS((d, 2 * d), BF16), name="fourier_w",
        compiler_params=_params(("arbitrary",), 32))(cg, sg, w_out)


def _radix_terms(radix, j):
    real, imag = [], []
    for q in range(radix):
        k = (j * q * (4 // radix)) % 4 if radix > 1 else 0
        if k == 0:
            real.append((1, 0, q)); imag.append((-1, 1, q))
        elif k == 1:
            real.append((-1, 1, q)); imag.append((-1, 0, q))
        elif k == 2:
            real.append((-1, 0, q)); imag.append((1, 1, q))
        else:
            real.append((1, 1, q)); imag.append((1, 0, q))
    return real, imag


def _seq_dft_call(st, p, x, mods, radix):
    n, d = x.shape
    seg = st.seg
    lr = seg // radix
    e = _dft_tables(seg, radix)
    nb = n // seg
    cblk = 2 * LANES if seg * d * 4 > 2 ** 22 else d
    ncb = d // cblk

    def kern(pc_ref, ps_ref, x_ref, e_ref, g1_ref, o_ref, v_ref, *z_refs):
        parts = (pc_ref, ps_ref)
        for j in range(radix):
            real, imag = _radix_terms(radix, j)

            def comb(terms):
                acc = None
                for sgn, part, q in terms:
                    v = parts[part][q * lr:(q + 1) * lr, :].astype(F32)
                    if acc is None:
                        acc = v if sgn > 0 else -v
                    else:
                        acc = acc + v if sgn > 0 else acc - v
                return acc

            v_ref[0:lr, :] = comb(real).astype(BF16)
            v_ref[lr:2 * lr, :] = comb(imag).astype(BF16)
            z = _dot(e_ref[j], v_ref[...])
            for k, z_ref in enumerate(z_refs):
                z_ref[pl.ds(j, lr, stride=radix), :] = z[:, k * LANES:(k + 1) * LANES]
        for k, z_ref in enumerate(z_refs):
            sl = slice(k * LANES, (k + 1) * LANES)
            o_ref[:, sl] = x_ref[:, sl] + g1_ref[:, sl] * z_ref[...]

    return pl.pallas_call(
        kern, grid=(nb, ncb),
        in_specs=[pl.BlockSpec((seg, cblk), lambda b, c: (b, c)),
                  pl.BlockSpec((seg, cblk), lambda b, c: (b, ncb + c)),
                  pl.BlockSpec((seg, cblk), lambda b, c: (b, c)),
                  _resident((radix, lr, 2 * lr)),
                  pl.BlockSpec((None, 1, cblk), lambda b, c: (st.seg_row(b), 0, G1 * ncb + c))],
        out_specs=pl.BlockSpec((seg, cblk), lambda b, c: (b, c)),
        out_shape=SDS((n, d), F32),
        scratch_shapes=[pltpu.VMEM((2 * lr, cblk), BF16)] + [pltpu.VMEM((seg, LANES), F32)] * (cblk // LANES),
        name=f"seq_dft_r{radix}",
        compiler_params=_params(("parallel", "parallel"), 40))(p, p, x, e, mods)


def _fourier_layer(streams, xs, mods, gain, w_out):
    wcs = _fourier_weight_call(w_out)
    outs = []
    for st, x in zip(streams, xs):
        if x is None:
            outs.append(None)
            continue
        p = _proj_call(st, x, mods, gain, wcs, "fourier_proj")
        outs.append(_seq_dft_call(st, p, x, mods, 4 if st.seg >= 1024 else 1))
    return outs


def _conv_in_call(st, x, mods, gain, w_in):
    n, d = x.shape
    t = st.tile

    def kern(x_ref, g_ref, sc_ref, sh_ref, w_ref, bg_ref, u_ref):
        h = _modnorm(x_ref[...], g_ref[...], sc_ref[...], sh_ref[...]).astype(BF16)
        bg_ref[...] = _dot(h, w_ref[:, 0:d]).astype(BF16)
        u_ref[...] = (_dot(h, w_ref[:, d:2 * d]) * _dot(h, w_ref[:, 2 * d:3 * d])).astype(BF16)

    return pl.pallas_call(
        kern, grid=(st.tiles,),
        in_specs=[pl.BlockSpec((t, d), lambda i: (i, 0)), _full((1, d)),
                  _mod_spec(st, SC1, d), _mod_spec(st, SH1, d), _resident((d, 3 * d))],
        out_specs=[pl.BlockSpec((t, d), lambda i: (i, 0))] * 2,
        out_shape=[SDS((n, d), BF16)] * 2, name="conv_in",
        compiler_params=_params(("parallel",), 48))(x, gain, mods, mods, w_in)


def _conv_out_call(st, bg, u, x, mods, conv_w, w_out):
    n, d = x.shape
    t = st.tile
    halo = 16
    hb = t // halo
    nhalo = n // halo
    seg = st.seg

    def kern(u_ref, up_ref, un_ref, bg_ref, x_ref, cw_ref, g1_ref, w_ref, o_ref):
        i = pl.program_id(0)
        u = u_ref[...].astype(F32)
        row = lax.broadcasted_iota(jnp.int32, (t, 1), 0)
        pos = (i * t + row) & (seg - 1)
        prev_row = up_ref[...].astype(F32)[halo - 1:halo, :]
        next_row = un_ref[...].astype(F32)[0:1, :]
        um = jnp.where(row == 0, prev_row, pltpu.roll(u, 1, axis=0))
        um = jnp.where(pos == 0, 0.0, um)
        up = jnp.where(row == t - 1, next_row, pltpu.roll(u, t - 1, axis=0))
        up = jnp.where(pos == seg - 1, 0.0, up)
        z = cw_ref[0:1, :] * um + cw_ref[1:2, :] * u + cw_ref[2:3, :] * up
        a = (bg_ref[...].astype(F32) * z).astype(BF16)
        o_ref[...] = x_ref[...] + g1_ref[...] * _dot(a, w_ref[...])

    tile_spec = pl.BlockSpec((t, d), lambda i: (i, 0))
    return pl.pallas_call(
        kern, grid=(st.tiles,),
        in_specs=[tile_spec,
                  pl.BlockSpec((halo, d), lambda i: (jnp.maximum(i * hb - 1, 0), 0)),
                  pl.BlockSpec((halo, d), lambda i: (jnp.minimum((i + 1) * hb, nhalo - 1), 0)),
                  tile_spec, tile_spec, _full((3, d)), _mod_spec(st, G1, d), _resident((d, d))],
        out_specs=tile_spec, out_shape=SDS((n, d), F32), name="conv_out",
        compiler_params=_params(("parallel",), 40))(u, u, u, bg, x, conv_w, mods, w_out)


def _conv_layer(streams, xs, mods, gain, w_in, conv_w, w_out):
    w_in_b, w_out_b = w_in.astype(BF16), w_out.astype(BF16)
    outs = []
    for st, x in zip(streams, xs):
        if x is None:
            outs.append(None)
            continue
        bg, u = _conv_in_call(st, x, mods, gain, w_in_b)
        outs.append(_conv_out_call(st, bg, u, x, mods, conv_w, w_out_b))
    return outs


def _rope_tables(length):
    q = ROPE_DIM // 4
    pos = np.arange(length)
    rc = np.stack([pos // GRID_W, pos % GRID_W], axis=1).astype(np.float32)
    inv = (ROPE_THETA ** (-np.arange(q, dtype=np.float32) / q)).astype(np.float32)
    lane = np.arange(ROPE_DIM)
    ang = rc[:, lane >> 5] * inv[lane & (q - 1)][None, :]
    sign = np.where((lane & q) == 0, -1.0, 1.0)[None, :]
    cos = np.cos(ang.astype(np.float32)).astype(np.float32)
    sin = (np.sin(ang.astype(np.float32)) * sign).astype(np.float32)
    reps = LANES // ROPE_DIM
    return jnp.asarray(np.tile(cos, (1, reps))), jnp.asarray(np.tile(sin, (1, reps)))


def _rope_partner_perm(n_heads):
    lane = np.arange(n_heads * ROPE_DIM)
    return lane ^ (ROPE_DIM // 4)


def _seg_rinv(x, seg):
    t, w = x.shape
    cols = []
    for c in range(w // LANES):
        blk = x[:, c * LANES:(c + 1) * LANES]
        sq = blk * blk
        if seg == LANES:
            cols.append(jnp.broadcast_to(
                lax.rsqrt(jnp.mean(sq, axis=-1, keepdims=True) + EPS), (t, LANES)))
        else:
            lane = lax.broadcasted_iota(jnp.int32, (t, LANES), 1)
            low = lane < seg
            lo = jnp.sum(jnp.where(low, sq, 0.0), axis=-1, keepdims=True) * (1.0 / seg)
            hi = jnp.sum(jnp.where(low, 0.0, sq), axis=-1, keepdims=True) * (1.0 / seg)
            cols.append(jnp.where(low, lax.rsqrt(lo + EPS), lax.rsqrt(hi + EPS)))
    return cols[0] if len(cols) == 1 else jnp.concatenate(cols, axis=-1)


def _tile_lanes(v, w):
    reps = w // LANES
    return v if reps == 1 else jnp.concatenate([v] * reps, axis=-1)


def _mla_proj_call(st, x, mods, gain, wts, rope, need_q):
    n, d = x.shape
    t = st.tile
    h = MLA_HEADS
    wn, wr = h * MLA_NOPE, h * MLA_ROPE
    (w_in, q_norm, kv_norm, w_uq, w_ukv, qg_n, qg_r, qg_rp, kg_n, kg_r, kg_rp) = wts
    use_rope = rope is not None

    def kern(*refs):
        refs = list(refs)
        x_ref, g_ref, sc_ref, sh_ref, win_ref, qn_ref_, kvn_ref_, wuq_ref, wukv_ref = refs[:9]
        qgn_ref, qgr_ref, qgrp_ref, kgn_ref, kgr_ref, kgrp_ref = refs[9:15]
        rest = refs[15:]
        if use_rope:
            cos_ref, sin_ref = rest[:2]
            rest = rest[2:]
        if need_q:
            oqn_ref, oqr_ref = rest[:2]
            rest = rest[2:]
        okn_ref, okr_ref, ov_ref = rest
        hm = _modnorm(x_ref[...], g_ref[...], sc_ref[...], sh_ref[...]).astype(BF16)
        ck = _dot(hm, win_ref[...])
        kr2 = ck[:, MLA_Q_RANK + MLA_KV_RANK:]
        kr_rinv = _seg_rinv(kr2, MLA_ROPE)[:, 0:MLA_ROPE]
        kr_raw = kr2[:, 0:MLA_ROPE] * kgr_ref[...]
        if use_rope:
            kr_par = kr2[:, MLA_ROPE:] * kgrp_ref[...]
            kr = kr_rinv * (kr_raw * cos_ref[:, 0:MLA_ROPE] + kr_par * sin_ref[:, 0:MLA_ROPE])
        else:
            kr = kr_rinv * kr_raw
        okr_ref[...] = kr.astype(BF16)
        ckv = ck[:, MLA_Q_RANK:MLA_Q_RANK + MLA_KV_RANK]
        ckv = (ckv * _seg_rinv(ckv, LANES) * kvn_ref_[...]).astype(BF16)
        kv = _dot(ckv, wukv_ref[...])
        kn = kv[:, 0:wn]
        okn_ref[...] = (kn * _seg_rinv(kn, MLA_NOPE) * kgn_ref[...]).astype(BF16)
        ov_ref[...] = kv[:, wn:].astype(BF16)
        if need_q:
            cq = ck[:, 0:MLA_Q_RANK]
            rq = lax.rsqrt(jnp.mean(cq * cq, axis=-1, keepdims=True) + EPS)
            cq = (cq * rq * qn_ref_[...]).astype(BF16)
            q = _dot(cq, wuq_ref[...])
            qn = q[:, 0:wn]
            oqn_ref[...] = (qn * _seg_rinv(qn, MLA_NOPE) * (qgn_ref[...] * (MLA_SCALE * LOG2E))).astype(BF16)
            qr_raw = q[:, wn:wn + wr]
            rinv = _seg_rinv(qr_raw, MLA_ROPE) * (MLA_SCALE * LOG2E)
            if use_rope:
                qr_par = q[:, wn + wr:]
                cos = _tile_lanes(cos_ref[...], wr)
                sin = _tile_lanes(sin_ref[...], wr)
                qr = rinv * (qr_raw * qgr_ref[...] * cos + qr_par * qgrp_ref[...] * sin)
            else:
                qr = rinv * (qr_raw * qgr_ref[...])
            oqr_ref[...] = qr.astype(BF16)

    tile = lambda w: pl.BlockSpec((t, w), lambda i: (i, 0))
    in_specs = [tile(d), _full((1, d)), _mod_spec(st, SC1, d), _mod_spec(st, SH1, d),
                _resident(w_in.shape), _full(q_norm.shape), _full(kv_norm.shape),
                _resident(w_uq.shape), _resident(w_ukv.shape),
                _full(qg_n.shape), _full(qg_r.shape), _full(qg_rp.shape),
                _full(kg_n.shape), _full(kg_r.shape), _full(kg_rp.shape)]
    args = [x, gain, mods, mods, w_in, q_norm, kv_norm, w_uq, w_ukv, qg_n, qg_r, qg_rp, kg_n, kg_r, kg_rp]
    if use_rope:
        tps = st.tiles_per_seg
        in_specs += [pl.BlockSpec((t, LANES), lambda i: (i % tps, 0))] * 2
        args += list(rope)
    out_specs, out_shape = [], []
    if need_q:
        out_specs += [tile(wn), tile(wr)]
        out_shape += [SDS((n, wn), BF16), SDS((n, wr), BF16)]
    out_specs += [tile(wn), tile(MLA_ROPE), tile(h * MLA_V)]
    out_shape += [SDS((n, wn), BF16), SDS((n, MLA_ROPE), BF16), SDS((n, h * MLA_V), BF16)]
    outs = pl.pallas_call(
        kern, grid=(st.tiles,), in_specs=in_specs, out_specs=out_specs, out_shape=out_shape,
        name="mla_proj", compiler_params=_params(("parallel",), 48))(*args)
    if need_q:
        return tuple(outs)
    return (None, None) + tuple(outs)


def _mla_attn_fast_call(batch, lq, tq, q, kv_sets, shift):
    qn, qr = q
    lk_total = sum(s[3] for s in kv_sets)
    hp = 2 if lk_total > 1024 else 8
    n_hp = MLA_HEADS // hp
    nq = lq // tq
    nsets = len(kv_sets)
    lengths = [s[3] for s in kv_sets]
    lk = sum(lengths)
    kw = 2 * LANES
    pad = kw - MLA_NOPE - MLA_ROPE

    def kern(*refs):
        sh_ref, qn_ref, qr_ref = refs[:3]
        sets = [refs[3 + 3 * s:6 + 3 * s] for s in range(nsets)]
        o_ref, kq_ref, vq_ref = refs[-3:]

        @pl.when(pl.program_id(2) == 0)
        def _():
            lane_k = lax.broadcasted_iota(jnp.int32, (lk, pad), 1)
            lane_v = lax.broadcasted_iota(jnp.int32, (lk, kw - MLA_V), 1)
            for hh in range(hp):
                off = 0
                for (kn_ref, kr_ref, v_ref), length in zip(sets, lengths):
                    kq_ref[hh, off:off + length, 0:MLA_NOPE] = kn_ref[:, hh * MLA_NOPE:(hh + 1) * MLA_NOPE]
                    kq_ref[hh, off:off + length, MLA_NOPE:MLA_NOPE + MLA_ROPE] = kr_ref[...]
                    vq_ref[hh, off:off + length, 0:MLA_V] = v_ref[:, hh * MLA_V:(hh + 1) * MLA_V]
                    off += length
                kq_ref[hh, :, MLA_NOPE + MLA_ROPE:] = jnp.where(lane_k == 0, -sh_ref[:, 0:pad], 0.0).astype(BF16)
                vq_ref[hh, :, MLA_V:] = jnp.where(lane_v == 0, 1.0, 0.0).astype(BF16)

        lane_q = lax.broadcasted_iota(jnp.int32, (tq, pad), 1)
        one = jnp.where(lane_q == 0, 1.0, 0.0).astype(BF16)
        for hh in range(hp):
            qc = jnp.concatenate([qn_ref[:, hh * MLA_NOPE:(hh + 1) * MLA_NOPE],
                                  qr_ref[:, hh * MLA_ROPE:(hh + 1) * MLA_ROPE], one], axis=1)
            p = jnp.exp2(_nt(qc, kq_ref[hh]).astype(BF16))
            acc = _dot(p, vq_ref[hh])
            o_ref[:, hh * MLA_V:(hh + 1) * MLA_V] = (acc[:, 0:MLA_V] / acc[:, MLA_V:MLA_V + 1]).astype(BF16)

    in_specs = [_full((1, LANES)),
                pl.BlockSpec((tq, hp * MLA_NOPE), lambda b, h, i: (b * nq + i, h)),
                pl.BlockSpec((tq, hp * MLA_ROPE), lambda b, h, i: (b * nq + i, h))]
    args = [shift, qn, qr]
    for kn, kr, v, length in kv_sets:
        in_specs += [pl.BlockSpec((length, hp * MLA_NOPE), lambda b, h, i: (b, h)),
                     pl.BlockSpec((length, MLA_ROPE), lambda b, h, i: (b, 0)),
                     pl.BlockSpec((length, hp * MLA_V), lambda b, h, i: (b, h))]
        args += [kn, kr, v]
    return pl.pallas_call(
        kern, grid=(batch, n_hp, nq), in_specs=in_specs,
        out_specs=pl.BlockSpec((tq, hp * MLA_V), lambda b, h, i: (b * nq + i, h)),
        out_shape=SDS((batch * lq, MLA_HEADS * MLA_V), BF16),
        scratch_shapes=[pltpu.VMEM((hp, lk, kw), BF16), pltpu.VMEM((hp, lk, kw), BF16)],
        name="mla_attn_fast",
        compiler_params=_params(("parallel", "parallel", "arbitrary"), 56))(*args)


def _mla_attn_call(batch, lq, tq, q, kv_sets):
    qn, qr = q
    hp = 2
    n_hp = MLA_HEADS // hp
    nq = lq // tq
    nsets = len(kv_sets)

    def kern(*refs):
        qn_ref, qr_ref = refs[:2]
        o_ref = refs[-1]
        sets = [refs[2 + 3 * s:5 + 3 * s] for s in range(nsets)]
        for hh in range(hp):
            qn_h = qn_ref[:, hh * MLA_NOPE:(hh + 1) * MLA_NOPE]
            qr_h = qr_ref[:, hh * MLA_ROPE:(hh + 1) * MLA_ROPE]
            scores = [_nt(qn_h, kn_ref[:, hh * MLA_NOPE:(hh + 1) * MLA_NOPE]) + _nt(qr_h, kr_ref[...])
                      for kn_ref, kr_ref, _ in sets]
            m = functools.reduce(jnp.maximum, [jnp.max(s, axis=-1, keepdims=True) for s in scores])
            ps = [jnp.exp2(s - m) for s in scores]
            den = functools.reduce(lambda a, b: a + b, [jnp.sum(p, axis=-1, keepdims=True) for p in ps])
            acc = functools.reduce(lambda a, b: a + b, [
                _dot(p.astype(BF16), v_ref[:, hh * MLA_V:(hh + 1) * MLA_V])
                for p, (_, _, v_ref) in zip(ps, sets)])
            o_ref[:, hh * MLA_V:(hh + 1) * MLA_V] = (acc / den).astype(BF16)

    in_specs = [pl.BlockSpec((tq, hp * MLA_NOPE), lambda b, h, i: (b * nq + i, h)),
                pl.BlockSpec((tq, hp * MLA_ROPE), lambda b, h, i: (b * nq + i, h))]
    args = [qn, qr]
    for kn, kr, v, length in kv_sets:
        in_specs += [pl.BlockSpec((length, hp * MLA_NOPE), lambda b, h, i: (b, h)),
                     pl.BlockSpec((length, MLA_ROPE), lambda b, h, i: (b, 0)),
                     pl.BlockSpec((length, hp * MLA_V), lambda b, h, i: (b, h))]
        args += [kn, kr, v]
    return pl.pallas_call(
        kern, grid=(batch, n_hp, nq), in_specs=in_specs,
        out_specs=pl.BlockSpec((tq, hp * MLA_V), lambda b, h, i: (b * nq + i, h)),
        out_shape=SDS((batch * lq, MLA_HEADS * MLA_V), BF16), name="mla_attn",
        compiler_params=_params(("parallel", "parallel", "arbitrary"), 56))(*args)


def _mla_layer(streams, xs, mods, gain, p, rope, ctx_next):
    st_c, st_x = streams
    xc, xx = xs
    h = MLA_HEADS
    w_in, q_norm, kv_norm, w_uq, w_ukv, q_gain, k_gain, w_out = p
    par = _rope_partner_perm(1)
    kr_cols = w_in[:, MLA_Q_RANK + MLA_KV_RANK:]
    w_in_p = jnp.concatenate([w_in, kr_cols[:, par]], axis=1).astype(BF16)
    wq = w_uq.reshape(MLA_Q_RANK, h, MLA_NOPE + MLA_ROPE)
    wq_n = wq[:, :, :MLA_NOPE].reshape(MLA_Q_RANK, h * MLA_NOPE)
    wq_r = wq[:, :, MLA_NOPE:]
    w_uq_p = jnp.concatenate([wq_n, wq_r.reshape(MLA_Q_RANK, h * MLA_ROPE),
                              wq_r[:, :, par].reshape(MLA_Q_RANK, h * MLA_ROPE)], axis=1).astype(BF16)
    wkv = w_ukv.reshape(MLA_KV_RANK, h, MLA_NOPE + MLA_V)
    w_ukv_p = jnp.concatenate([wkv[:, :, :MLA_NOPE].reshape(MLA_KV_RANK, h * MLA_NOPE),
                               wkv[:, :, MLA_NOPE:].reshape(MLA_KV_RANK, h * MLA_V)], axis=1).astype(BF16)
    qg_r = q_gain[MLA_NOPE:]
    kg_r = k_gain[MLA_NOPE:]
    wts = (w_in_p, q_norm.reshape(1, -1), kv_norm.reshape(1, -1), w_uq_p, w_ukv_p,
           jnp.tile(q_gain[:MLA_NOPE], h).reshape(1, -1), jnp.tile(qg_r, h).reshape(1, -1),
           jnp.tile(qg_r[par], h).reshape(1, -1), jnp.tile(k_gain[:MLA_NOPE], h).reshape(1, -1),
           kg_r.reshape(1, -1), kg_r[par].reshape(1, -1))
    w_out_b = w_out.astype(BF16)
    qn_c, qr_c, kn_c, kr_c, v_c = _mla_proj_call(st_c.with_tile(PROJ_TILE), xc, mods, gain, wts, None, ctx_next)
    qn_x, qr_x, kn_x, kr_x, v_x = _mla_proj_call(st_x.with_tile(PROJ_TILE), xx, mods, gain, wts, rope, True)
    b = st_x.batch
    amax = lambda g: jnp.max(jnp.abs(g))
    bound = LOG2E * MLA_SCALE * (MLA_NOPE * amax(q_gain[:MLA_NOPE]) * amax(k_gain[:MLA_NOPE])
                                 + MLA_ROPE * amax(qg_r) * amax(kg_r))
    shift = bound * SHIFT_MARGIN + 1.0
    shift_row = jnp.full((1, LANES), shift, F32)

    def attend(lq, tq, q, kv_sets):
        return lax.cond(shift <= MAX_FIXED_SHIFT,
                        lambda: _mla_attn_fast_call(b, lq, tq, q, kv_sets, shift_row),
                        lambda: _mla_attn_call(b, lq, min(tq, 512), q, kv_sets))

    a_x = attend(st_x.seg, min(st_x.seg, 2048), (qn_x, qr_x), [(kn_c, kr_c, v_c, st_c.seg), (kn_x, kr_x, v_x, st_x.seg)])
    out_x = _outproj_call(st_x, a_x, xx, mods, w_out_b, "mla_out")
    out_c = None
    if ctx_next:
        a_c = attend(st_c.seg, st_c.seg, (qn_c, qr_c), [(kn_c, kr_c, v_c, st_c.seg)])
        out_c = _outproj_call(st_c, a_c, xc, mods, w_out_b, "mla_out")
    return [out_c, out_x]


def _gqa_proj_call(st, x, mods, gain, wts, rope, need_q):
    n, d = x.shape
    t = st.tile
    wq, wk = GQA_Q_HEADS * GQA_HEAD_DIM, GQA_KV_HEADS * GQA_HEAD_DIM
    w_in, qg, qgp, kg, kgp = wts
    use_rope = rope is not None

    def kern(*refs):
        refs = list(refs)
        x_ref, g_ref, sc_ref, sh_ref, w_ref, qg_ref, qgp_ref, kg_ref, kgp_ref = refs[:9]
        rest = refs[9:]
        if use_rope:
            cos_ref, sin_ref = rest[:2]
            rest = rest[2:]
        if need_q:
            oq_ref = rest[0]
            rest = rest[1:]
        ok_ref, ov_ref = rest
        hm = _modnorm(x_ref[...], g_ref[...], sc_ref[...], sh_ref[...]).astype(BF16)
        kvp = _dot(hm, w_ref[:, 0:3 * wk])
        k_raw = kvp[:, 0:wk]
        k_rinv = _seg_rinv(k_raw, GQA_HEAD_DIM)
        ov_ref[...] = kvp[:, wk:2 * wk].astype(BF16)
        if use_rope:
            cos_k, sin_k = _tile_lanes(cos_ref[...], wk), _tile_lanes(sin_ref[...], wk)
            k = k_rinv * (k_raw * kg_ref[...] * cos_k + kvp[:, 2 * wk:] * kgp_ref[...] * sin_k)
        else:
            k = k_rinv * (k_raw * kg_ref[...])
        ok_ref[...] = k.astype(BF16)
        if need_q:
            qp = _dot(hm, w_ref[:, 3 * wk:])
            q_raw = qp[:, 0:wq]
            rinv = _seg_rinv(q_raw, GQA_HEAD_DIM) * (GQA_SCALE * LOG2E)
            if use_rope:
                cos_q, sin_q = _tile_lanes(cos_ref[...], wq), _tile_lanes(sin_ref[...], wq)
                q = rinv * (q_raw * qg_ref[...] * cos_q + qp[:, wq:] * qgp_ref[...] * sin_q)
            else:
                q = rinv * (q_raw * qg_ref[...])
            oq_ref[...] = q.astype(BF16)

    tile = lambda w: pl.BlockSpec((t, w), lambda i: (i, 0))
    in_specs = [tile(d), _full((1, d)), _mod_spec(st, SC1, d), _mod_spec(st, SH1, d),
                _resident(w_in.shape), _full(qg.shape), _full(qgp.shape), _full(kg.shape), _full(kgp.shape)]
    args = [x, gain, mods, mods, w_in, qg, qgp, kg, kgp]
    if use_rope:
        tps = st.tiles_per_seg
        in_specs += [pl.BlockSpec((t, LANES), lambda i: (i % tps, 0))] * 2
        args += list(rope)
    out_specs, out_shape = [], []
    if need_q:
        out_specs.append(tile(wq))
        out_shape.append(SDS((n, wq), BF16))
    out_specs += [tile(wk), tile(wk)]
    out_shape += [SDS((n, wk), BF16)] * 2
    outs = pl.pallas_call(
        kern, grid=(st.tiles,), in_specs=in_specs, out_specs=out_specs, out_shape=out_shape,
        name="gqa_proj", compiler_params=_params(("parallel",), 48))(*args)
    return tuple(outs) if need_q else (None,) + tuple(outs)


def _gqa_window_call(batch, length, lc, q, k, v, kc, vc, sink):
    nb = length // BLOCK
    hd = GQA_HEAD_DIM
    wq, wk = GQA_Q_HEADS * hd, GQA_KV_HEADS * hd
    rows = GQA_GROUP * BLOCK

    def kern(sink_ref, q_ref, kp_ref, k0_ref, kn_ref, vp_ref, v0_ref, vn_ref, kc_ref, vc_ref, o_ref):
        nblk = pl.program_id(1)
        r = lax.broadcasted_iota(jnp.int32, (rows, 3 * BLOCK), 0) & (BLOCK - 1)
        c = lax.broadcasted_iota(jnp.int32, (rows, 3 * BLOCK), 1)
        valid = (c >= r + BLOCK - WINDOW) & (c <= r + BLOCK + WINDOW)
        valid = valid & ((c >= BLOCK) | (nblk > 0)) & ((c < 2 * BLOCK) | (nblk < nb - 1))
        hrow = lax.broadcasted_iota(jnp.int32, (rows, 1), 0) // BLOCK
        for g in range(GQA_KV_HEADS):
            sl = slice(g * hd, (g + 1) * hd)
            qg = jnp.concatenate([q_ref[:, (g * GQA_GROUP + j) * hd:(g * GQA_GROUP + j + 1) * hd]
                                  for j in range(GQA_GROUP)], axis=0)
            kband = jnp.concatenate([kp_ref[:, sl], k0_ref[:, sl], kn_ref[:, sl]], axis=0)
            vband = jnp.concatenate([vp_ref[:, sl], v0_ref[:, sl], vn_ref[:, sl]], axis=0)
            s_c = _nt(qg, kc_ref[:, sl])
            s_b = jnp.where(valid, _nt(qg, kband), -1e30)
            snk = jnp.zeros((rows, 1), F32)
            for j in range(GQA_GROUP):
                snk = jnp.where(hrow == j, sink_ref[g * GQA_GROUP + j], snk)
            m = jnp.maximum(jnp.maximum(jnp.max(s_c, axis=-1, keepdims=True),
                                        jnp.max(s_b, axis=-1, keepdims=True)), snk)
            p_c = jnp.exp2(s_c - m)
            p_b = jnp.exp2(s_b - m)
            den = (jnp.sum(p_c, axis=-1, keepdims=True) + jnp.sum(p_b, axis=-1, keepdims=True)
                   + jnp.exp2(snk - m))
            o = (_dot(p_c.astype(BF16), vc_ref[:, sl]) + _dot(p_b.astype(BF16), vband)) / den
            o_ref[:, g * GQA_GROUP * hd:(g + 1) * GQA_GROUP * hd] = jnp.concatenate(
                [o[j * BLOCK:(j + 1) * BLOCK, :] for j in range(GQA_GROUP)], axis=-1).astype(BF16)

    blk = lambda f: pl.BlockSpec((BLOCK, wk), f)
    prev_ = lambda b, i: (b * nb + jnp.maximum(i - 1, 0), 0)
    cur_ = lambda b, i: (b * nb + i, 0)
    next_ = lambda b, i: (b * nb + jnp.minimum(i + 1, nb - 1), 0)
    ctx_spec = pl.BlockSpec((lc, wk), lambda b, i: (b, 0))
    return pl.pallas_call(
        kern, grid=(batch, nb),
        in_specs=[pl.BlockSpec(memory_space=pltpu.SMEM),
                  pl.BlockSpec((BLOCK, wq), cur_), blk(prev_), blk(cur_), blk(next_),
                  blk(prev_), blk(cur_), blk(next_), ctx_spec, ctx_spec],
        out_specs=pl.BlockSpec((BLOCK, wq), cur_),
        out_shape=SDS((batch * length, wq), BF16), name="gqa_window",
        compiler_params=_params(("parallel", "arbitrary"), 40))(sink, q, k, k, k, v, v, v, kc, vc)


def _gqa_window_fast_call(batch, length, lc, q, k, v, kc, vc, sink2, shift):
    nb = length // BLOCK
    hd = GQA_HEAD_DIM
    wq, wk = GQA_Q_HEADS * hd, GQA_KV_HEADS * hd
    rows = GQA_GROUP * BLOCK

    def kern(sink_ref, shift_ref, q_ref, kp_ref, k0_ref, kn_ref, vp_ref, v0_ref, vn_ref, kc_ref, vc_ref, o_ref):
        nblk = pl.program_id(1)
        r = lax.broadcasted_iota(jnp.int32, (rows, 3 * BLOCK), 0) & (BLOCK - 1)
        c = lax.broadcasted_iota(jnp.int32, (rows, 3 * BLOCK), 1)
        valid = (c >= r + BLOCK - WINDOW) & (c <= r + BLOCK + WINDOW)
        valid = valid & ((c >= BLOCK) | (nblk > 0)) & ((c < 2 * BLOCK) | (nblk < nb - 1))
        hrow = lax.broadcasted_iota(jnp.int32, (rows, 1), 0) // BLOCK
        lane0_q = lax.broadcasted_iota(jnp.int32, (BLOCK, hd), 1) == 0

        def with_one(x):
            lane0 = lax.broadcasted_iota(jnp.int32, (x.shape[0], hd), 1) == 0
            return jnp.concatenate([x, jnp.where(lane0, 1.0, 0.0).astype(BF16)], axis=1)

        for g in range(GQA_KV_HEADS):
            sl = slice(g * hd, (g + 1) * hd)
            heads = [g * GQA_GROUP + j for j in range(GQA_GROUP)]
            qg = jnp.concatenate(
                [jnp.concatenate([q_ref[:, h * hd:(h + 1) * hd],
                                  jnp.where(lane0_q, -shift_ref[h], 0.0).astype(BF16)], axis=1)
                 for h in heads], axis=0)
            kband = with_one(jnp.concatenate([kp_ref[:, sl], k0_ref[:, sl], kn_ref[:, sl]], axis=0))
            vband = with_one(jnp.concatenate([vp_ref[:, sl], v0_ref[:, sl], vn_ref[:, sl]], axis=0))
            p_c = jnp.exp2(_nt(qg, with_one(kc_ref[:, sl])).astype(BF16))
            p_b = jnp.exp2(jnp.where(valid, _nt(qg, kband), -1e30).astype(BF16))
            acc = _dot(p_c, with_one(vc_ref[:, sl])) + _dot(p_b, vband)
            snk = jnp.zeros((rows, 1), F32)
            for j, h in enumerate(heads):
                snk = jnp.where(hrow == j, sink_ref[h] - shift_ref[h], snk)
            o = acc[:, 0:hd] / (acc[:, hd:hd + 1] + jnp.exp2(snk))
            o_ref[:, g * GQA_GROUP * hd:(g + 1) * GQA_GROUP * hd] = jnp.concatenate(
                [o[j * BLOCK:(j + 1) * BLOCK, :] for j in range(GQA_GROUP)], axis=-1).astype(BF16)

    blk = lambda f: pl.BlockSpec((BLOCK, wk), f)
    prev_ = lambda b, i: (b * nb + jnp.maximum(i - 1, 0), 0)
    cur_ = lambda b, i: (b * nb + i, 0)
    next_ = lambda b, i: (b * nb + jnp.minimum(i + 1, nb - 1), 0)
    ctx_spec = pl.BlockSpec((lc, wk), lambda b, i: (b, 0))
    smem = pl.BlockSpec(memory_space=pltpu.SMEM)
    return pl.pallas_call(
        kern, grid=(batch, nb),
        in_specs=[smem, smem, pl.BlockSpec((BLOCK, wq), cur_), blk(prev_), blk(cur_), blk(next_),
                  blk(prev_), blk(cur_), blk(next_), ctx_spec, ctx_spec],
        out_specs=pl.BlockSpec((BLOCK, wq), cur_),
        out_shape=SDS((batch * length, wq), BF16), name="gqa_window_fast",
        compiler_params=_params(("parallel", "arbitrary"), 40))(sink2, shift, q, k, k, k, v, v, v, kc, vc)


def _gqa_layer(streams, xs, mods, gain, p, rope, ctx_next):
    assert not ctx_next, "the windowed-GQA mixer is only implemented as the last layer"
    st_c, st_x = streams
    xc, xx = xs
    w_in, q_gain, k_gain, sink, w_out = p
    wq, wk = GQA_Q_HEADS * GQA_HEAD_DIM, GQA_KV_HEADS * GQA_HEAD_DIM
    w_q, w_k, w_v = w_in[:, :wq], w_in[:, wq:wq + wk], w_in[:, wq + wk:]
    w_in_p = jnp.concatenate([w_k, w_v, w_k[:, _rope_partner_perm(GQA_KV_HEADS)],
                              w_q, w_q[:, _rope_partner_perm(GQA_Q_HEADS)]], axis=1).astype(BF16)
    par = _rope_partner_perm(1)
    wts = (w_in_p, jnp.tile(q_gain, GQA_Q_HEADS).reshape(1, -1), jnp.tile(q_gain[par], GQA_Q_HEADS).reshape(1, -1),
           jnp.tile(k_gain, GQA_KV_HEADS).reshape(1, -1), jnp.tile(k_gain[par], GQA_KV_HEADS).reshape(1, -1))
    _, kc, vc = _gqa_proj_call(st_c.with_tile(PROJ_TILE), xc, mods, gain, wts, None, False)
    q, k, v = _gqa_proj_call(st_x.with_tile(PROJ_TILE), xx, mods, gain, wts, rope, True)
    sink2 = sink * LOG2E
    bound = LOG2E * GQA_SCALE * GQA_HEAD_DIM * jnp.max(jnp.abs(q_gain)) * jnp.max(jnp.abs(k_gain))
    shift = jnp.maximum(bound * SHIFT_MARGIN + 1.0, sink2).astype(BF16).astype(F32)
    args = (st_x.batch, st_x.seg, st_c.seg, q, k, v, kc, vc, sink2)
    a = lax.cond(jnp.max(shift) <= MAX_FIXED_SHIFT,
                 lambda: _gqa_window_fast_call(*args, shift),
                 lambda: _gqa_window_call(*args))
    return [None, _outproj_call(st_x, a, xx, mods, w_out.astype(BF16), "gqa_out")]


def _cast_experts_call(w, layer):
    _, ne, a, b = w.shape
    eb = 4

    def kern(w_ref, o_ref):
        o_ref[...] = w_ref[...].astype(BF16)

    return pl.pallas_call(
        kern, grid=(ne // eb,),
        in_specs=[pl.BlockSpec((None, eb, a, b), lambda i: (layer, i, 0, 0))],
        out_specs=pl.BlockSpec((eb, a, b), lambda i: (i, 0, 0)),
        out_shape=SDS((ne, a, b), BF16), name="cast_experts",
        compiler_params=_params(("parallel",), 32))(w)


def _route(logits_t, bias_col):
    scores = _sigmoid(logits_t)
    biased = scores + bias_col
    rows = [biased[e:e + 1, :] for e in range(N_EXPERTS)]
    srow = [scores[e:e + 1, :] for e in range(N_EXPERTS)]
    epg = EXPERTS_PER_GROUP
    gscore = []
    for g in range(N_EXPERT_GROUPS):
        v = rows[g * epg:(g + 1) * epg]
        pair = [v[a] + v[b] for a in range(epg) for b in range(a + 1, epg)]
        gscore.append(functools.reduce(jnp.maximum, pair))
    ind, wloc = [], [None] * epg
    for g in range(N_EXPERT_GROUPS):
        best = None
        for g2 in range(N_EXPERT_GROUPS):
            if g2 == g:
                continue
            cnd = gscore[g] > gscore[g2] if g2 < g else gscore[g] >= gscore[g2]
            best = cnd if best is None else best & cnd
        ind.append(jnp.where(best, 1.0, 0.0))
        for j in range(epg):
            e = g * epg + j
            rank = None
            for e2 in range(g * epg, (g + 1) * epg):
                if e2 == e:
                    continue
                ahead = rows[e2] >= rows[e] if e2 < e else rows[e2] > rows[e]
                one = jnp.where(ahead, 1.0, 0.0)
                rank = one if rank is None else rank + one
            w = jnp.where(best & (rank < 2.0), srow[e], 0.0)
            wloc[j] = w if wloc[j] is None else wloc[j] + w
    den = functools.reduce(lambda a, b: a + b, wloc)
    return ind, [w / den for w in wloc]


MOE_TILE = 1024
MOE_CHUNK = 128


def _moe_call(st, x, mods, gain, router_wt, router_bias, w_gate, w_up, w_down):
    n, d = x.shape
    t = min(MOE_TILE, n)
    r = MOE_CHUNK
    ne, _, ff = w_gate.shape
    epg, ng = EXPERTS_PER_GROUP, N_EXPERT_GROUPS
    nch = t // r + ng - 1
    tiles = n // t
    wd2 = w_down.reshape(ne * ff, d)

    def kern(x_ref, g_ref, sc_ref, sh_ref, g2_ref, rw_ref, rb_ref, wg_ref, wu_ref, wd_ref, o_ref,
             hb_ref, oh_ref, xg_ref):
        hb_ref[...] = _modnorm(x_ref[...], g_ref[...], sc_ref[...], sh_ref[...]).astype(BF16)
        ind, wloc = _route(_nt(rw_ref[...], hb_ref[...]), rb_ref[...])
        ind8 = jnp.concatenate(ind + [jnp.zeros((8 - ng, t), F32)], axis=0)
        lane = lax.broadcasted_iota(jnp.int32, (8, t), 1)
        csum, step = ind8, 1
        while step < t:
            csum = csum + jnp.where(lane >= step, pltpu.roll(csum, step, axis=1), 0.0)
            step *= 2
        pos = csum - ind8
        count = [jnp.sum(ind[g]).astype(jnp.int32) for g in range(ng)]
        first = [jnp.int32(0)]
        for g in range(ng):
            first.append(first[g] + (count[g] + (r - 1)) // r)
        slot = functools.reduce(lambda a, b: a + b, [
            ind[g] * (pos[g:g + 1, :] + (first[g] * r).astype(F32)) for g in range(ng)])
        w_hi = [w.astype(BF16) for w in wloc]
        w_lo = [(w - h.astype(F32)).astype(BF16) for w, h in zip(wloc, w_hi)]
        wst = jnp.concatenate(w_hi + w_lo, axis=0)
        row_id = lax.broadcasted_iota(jnp.int32, (r, t), 0).astype(F32)
        for c in range(nch):
            oh_ref[c * r:(c + 1) * r, :] = jnp.where(slot == row_id + float(c * r), 1.0, 0.0).astype(BF16)
        xg_ref[...] = _dot(oh_ref[...], hb_ref[...]).astype(BF16)
        wr_all = _nt(wst, oh_ref[...]).T
        for c in range(nch):
            rows = slice(c * r, (c + 1) * r)

            def chunk_ffn(c=c, rows=rows):
                grp = functools.reduce(lambda a, b: a + b,
                                       [(c >= first[g]).astype(jnp.int32) for g in range(1, ng)])
                xg = xg_ref[rows, :]
                wr = wr_all[c * r:(c + 1) * r, :]
                acts = []
                for j in range(epg):
                    e = grp * epg + j
                    gt = _dot(xg, wg_ref[e])
                    up = _dot(xg, wu_ref[e])
                    acts.append((gt * _sigmoid(gt) * up * (wr[:, j:j + 1] + wr[:, epg + j:epg + j + 1])).astype(BF16))
                wd_g = wd_ref[pl.ds(pl.multiple_of(grp * (epg * ff), epg * ff), epg * ff), :]
                xg_ref[rows, :] = _dot(jnp.concatenate(acts, axis=1), wd_g).astype(BF16)

            if c < t // r:
                chunk_ffn()
            else:
                pl.when(c < first[ng])(chunk_ffn)

        back = lax.dot_general(oh_ref[...], xg_ref[...], (((0,), (0,)), ((), ())), preferred_element_type=F32)
        o_ref[...] = x_ref[...] + g2_ref[...] * back

    row = (lambda i: st.batch) if st.is_ctx else (lambda i: i // (st.seg // t))
    mod = lambda chunk: pl.BlockSpec((None, 1, d), lambda i: (row(i), 0, chunk))
    return pl.pallas_call(
        kern, grid=(tiles,),
        in_specs=[pl.BlockSpec((t, d), lambda i: (i, 0)),
                  _full((1, d)), mod(SC2), mod(SH2), mod(G2), _full((ne, d)), _full((ne, 1)),
                  _resident((ne, d, ff)), _resident((ne, d, ff)), _resident((ne * ff, d))],
        out_specs=pl.BlockSpec((t, d), lambda i: (i, 0)),
        out_shape=SDS((n, d), F32),
        scratch_shapes=[pltpu.VMEM((t, d), BF16), pltpu.VMEM((nch * r, t), BF16),
                        pltpu.VMEM((nch * r, d), BF16)],
        name="moe", compiler_params=_params(("parallel",), 58))(
            x, gain, mods, mods, mods, router_wt, router_bias, w_gate, w_up, wd2)


def kernel(x, c, ctx, c_ctx, w_ada, b_ada, norm_mix, norm_ffn, fourier_w_out, conv_w_in, conv_w, conv_w_out,
           mla_w_in, mla_q_norm, mla_kv_norm, mla_w_uq, mla_w_ukv, mla_q_gain, mla_k_gain, mla_w_out,
           gqa_w_in, gqa_q_gain, gqa_k_gain, gqa_sink, gqa_w_out, router_w, router_bias,
           moe_w_gate, moe_w_up, moe_w_down):
    b, l, d = x.shape
    lc = ctx.shape[1]
    depth = w_ada.shape[0]
    st_c = Stream(b * lc, lc, min(b * lc, TOKEN_TILE), b, True)
    st_x = Stream(b * l, l, min(l, TOKEN_TILE), b, False)
    streams = [st_c, st_x]
    r8 = -(-(b + 1) // 8) * 8
    cvec = jnp.concatenate([c, c_ctx[None, :], jnp.zeros((r8 - b - 1, d), F32)], axis=0)
    mods_all = _ada_call(cvec, w_ada, b_ada).reshape(depth, r8, 1, 6 * d)
    rope = _rope_tables(l)
    router_wt = router_w.T.astype(BF16)
    router_b = router_bias.reshape(-1, 1)
    xs = [ctx.reshape(b * lc, d), x.reshape(b * l, d)]
    for i in range(depth):
        kind, j = i % 4, i // 4
        ctx_next = i < depth - 1
        mods = mods_all[i]
        gain = norm_mix[i].reshape(1, d)
        if not (ctx_next or kind >= 2):
            xs[0] = None
        if kind == 0:
            ys = _fourier_layer(streams, xs, mods, gain, fourier_w_out[j])
        elif kind == 1:
            ys = _conv_layer(streams, xs, mods, gain, conv_w_in[j], conv_w[j], conv_w_out[j])
        elif kind == 2:
            ys = _mla_layer(streams, xs, mods, gain,
                            (mla_w_in[j], mla_q_norm[j], mla_kv_norm[j], mla_w_uq[j], mla_w_ukv[j],
                             mla_q_gain[j], mla_k_gain[j], mla_w_out[j]), rope, ctx_next)
        else:
            ys = _gqa_layer(streams, xs, mods, gain,
                            (gqa_w_in[j], gqa_q_gain[j], gqa_k_gain[j], gqa_sink[j], gqa_w_out[j]),
                            rope, ctx_next)
        if not ctx_next:
            ys[0] = None
        gain2 = norm_ffn[i].reshape(1, d)
        wg, wu, wd = (_cast_experts_call(w, i) for w in (moe_w_gate, moe_w_up, moe_w_down))
        xs = [None if y is None else _moe_call(st, y, mods, gain2, router_wt, router_b, wg, wu, wd)
              for st, y in zip(streams, ys)]
    return xs[1].reshape(b, l, d)
```

```python
import functools

import numpy as np
import jax
import jax.numpy as jnp
from jax import lax
from jax.experimental import pallas as pl
from jax.experimental.pallas import tpu as pltpu

F32, BF16 = jnp.float32, jnp.bfloat16
SDS = jax.ShapeDtypeStruct

EPS = 1e-6
GRID_W = 64
ROPE_THETA = 10000.0
N_FOURIER_GROUPS = 4
MLA_HEADS, MLA_Q_RANK, MLA_KV_RANK = 16, 256, 128
MLA_NOPE, MLA_ROPE, MLA_V = 128, 64, 128
MLA_SCALE = (MLA_NOPE + MLA_ROPE) ** -0.5
GQA_Q_HEADS, GQA_KV_HEADS, GQA_HEAD_DIM = 16, 4, 64
GQA_GROUP = GQA_Q_HEADS // GQA_KV_HEADS
GQA_SCALE = GQA_HEAD_DIM ** -0.5
WINDOW = 128
BLOCK = 128
N_EXPERTS, N_EXPERT_GROUPS, EXPERT_FF = 16, 4, 256
EXPERTS_PER_GROUP = N_EXPERTS // N_EXPERT_GROUPS
ROPE_DIM = 64
LOG2E = 1.4426950408889634
MAX_FIXED_SHIFT = 50.0
SHIFT_MARGIN = 1.02
TOKEN_TILE = 1024
PROJ_TILE = 512

V7X_VMEM_BYTES = 64 * 1024 * 1024
LANES = 128
SH1, SC1, G1, SH2, SC2, G2 = range(6)


def _params(sem, vmem_mb):
    return pltpu.CompilerParams(dimension_semantics=sem, vmem_limit_bytes=vmem_mb * 1024 * 1024)


def _sigmoid(v):
    return 1.0 / (1.0 + jnp.exp(-v))


def _modnorm(x, gain, sc, sh):
    ms = jnp.mean(x * x, axis=-1, keepdims=True)
    return x * lax.rsqrt(ms + EPS) * (gain * (1.0 + sc)) + sh


def _nt(a, b):
    return lax.dot_general(a, b, (((1,), (1,)), ((), ())), preferred_element_type=F32)


def _dot(a, b):
    return jnp.dot(a, b, preferred_element_type=F32)


class Stream:
    def __init__(self, n, seg, tile, batch, is_ctx):
        self.n, self.seg, self.tile, self.batch, self.is_ctx = n, seg, tile, batch, is_ctx
        self.tiles = n // tile
        self.tiles_per_seg = seg // tile

    def mod_row(self, t):
        return self.batch if self.is_ctx else t // self.tiles_per_seg

    def seg_row(self, b):
        return self.batch if self.is_ctx else b

    def with_tile(self, tile):
        return Stream(self.n, self.seg, min(tile, self.tile), self.batch, self.is_ctx)


def _mod_spec(st, chunk, d):
    return pl.BlockSpec((None, 1, d), lambda t: (st.mod_row(t), 0, chunk))


def _full(shape):
    nd = len(shape)
    return pl.BlockSpec(shape, lambda *_: (0,) * nd)


def _resident(shape):
    nd = len(shape)
    return pl.BlockSpec(shape, lambda *_: (0,) * nd, pipeline_mode=pl.Buffered(1))


def _ada_call(cvec, w_ada, b_ada):
    depth, d, d6 = w_ada.shape
    r8 = cvec.shape[0]
    tn = d6 // 4

    def kern(c_ref, w_ref, b_ref, o_ref):
        c = c_ref[...]
        s = (c * _sigmoid(c)).astype(BF16)
        o_ref[...] = _dot(s, w_ref[...].astype(BF16)) + b_ref[...]

    return pl.pallas_call(
        kern, grid=(depth, d6 // tn),
        in_specs=[pl.BlockSpec((r8, d), lambda i, j: (0, 0)),
                  pl.BlockSpec((None, d, tn), lambda i, j: (i, 0, j)),
                  pl.BlockSpec((None, 1, tn), lambda i, j: (i, 0, j))],
        out_specs=pl.BlockSpec((None, r8, tn), lambda i, j: (i, 0, j)),
        out_shape=SDS((depth, r8, d6), F32), name="ada",
        compiler_params=_params(("arbitrary", "arbitrary"), 40))(cvec, w_ada, b_ada.reshape(depth, 1, d6))


def _proj_call(st, x, mods, gain, w, name):
    n, d = x.shape
    nout = w.shape[1]
    t = st.tile

    def kern(x_ref, g_ref, sc_ref, sh_ref, w_ref, o_ref):
        h = _modnorm(x_ref[...], g_ref[...], sc_ref[...], sh_ref[...]).astype(BF16)
        o_ref[...] = _dot(h, w_ref[...]).astype(BF16)

    return pl.pallas_call(
        kern, grid=(st.tiles,),
        in_specs=[pl.BlockSpec((t, d), lambda i: (i, 0)), _full((1, d)),
                  _mod_spec(st, SC1, d), _mod_spec(st, SH1, d), _resident((d, nout))],
        out_specs=pl.BlockSpec((t, nout), lambda i: (i, 0)),
        out_shape=SDS((n, nout), BF16), name=name,
        compiler_params=_params(("parallel",), 40))(x, gain, mods, mods, w)


def _outproj_call(st, a, mods, w, name):
    n, k = a.shape
    d = w.shape[1]
    t = st.tile

    def kern(a_ref, g1_ref, w_ref, o_ref):
        o_ref[...] = (g1_ref[...] * _dot(a_ref[...], w_ref[...])).astype(BF16)

    return pl.pallas_call(
        kern, grid=(st.tiles,),
        in_specs=[pl.BlockSpec((t, k), lambda i: (i, 0)), _mod_spec(st, G1, d), _resident((k, d))],
        out_specs=pl.BlockSpec((t, d), lambda i: (i, 0)),
        out_shape=SDS((n, d), BF16), name=name,
        compiler_params=_params(("parallel",), 40))(a, mods, w)


def _dft_tables(length, radix):
    lr = length // radix
    m = np.arange(lr)[None, :, None]
    j = np.arange(radix)[:, None, None]
    nn = np.arange(lr)[None, None, :]
    ang = 2.0 * np.pi * (((radix * m + j) * nn) % length) / length
    e = np.concatenate([np.cos(ang), np.sin(ang)], axis=-1) / np.sqrt(length)
    return jnp.asarray(e, dtype=F32).astype(BF16)


def _group_dft_tables(group):
    k = np.arange(group)
    ang = 2.0 * np.pi * ((k[:, None] * k[None, :]) % group) / group
    return (jnp.asarray(np.cos(ang) / np.sqrt(group), dtype=F32),
            jnp.asarray(np.sin(ang) / np.sqrt(group), dtype=F32))


def _fourier_weight_call(w_out):
    d = w_out.shape[0]
    grp = d // N_FOURIER_GROUPS
    cg, sg = _group_dft_tables(grp)

    def kern(cg_ref, sg_ref, w_ref, o_ref):
        w = w_ref[...]
        o_ref[:, :d] = jnp.dot(cg_ref[...], w, preferred_element_type=F32,
                               precision=lax.Precision.HIGHEST).astype(BF16)
        o_ref[:, d:] = jnp.dot(sg_ref[...], w, preferred_element_type=F32,
                               precision=lax.Precision.HIGHEST).astype(BF16)

    return pl.pallas_call(
        kern, grid=(N_FOURIER_GROUPS,),
        in_specs=[_full((grp, grp)), _full((grp, grp)), pl.BlockSpec((grp, d), lambda g: (g, 0))],
        out_specs=pl.BlockSpec((grp, 2 * d), lambda g: (g, 0)),
        out_shape=SDS((d, 2 * d), BF16), name="fourier_w",
        compiler_params=_params(("arbitrary",), 32))(cg, sg, w_out)


def _radix_terms(radix, j):
    real, imag = [], []
    for q in range(radix):
        k = (j * q * (4 // radix)) % 4 if radix > 1 else 0
        if k == 0:
            real.append((1, 0, q)); imag.append((-1, 1, q))
        elif k == 1:
            real.append((-1, 1, q)); imag.append((-1, 0, q))
        elif k == 2:
            real.append((-1, 0, q)); imag.append((1, 1, q))
        else:
            real.append((1, 1, q)); imag.append((1, 0, q))
    return real, imag


def _seq_dft_call(st, p, mods, radix):
    n, d = p.shape[0], p.shape[1] // 2
    seg = st.seg
    lr = seg // radix
    e = _dft_tables(seg, radix)
    nb = n // seg
    cblk = 2 * LANES if seg * d * 4 > 2 ** 22 else d
    ncb = d // cblk

    def kern(pc_ref, ps_ref, e_ref, g1_ref, o_ref, v_ref, *z_refs):
        parts = (pc_ref, ps_ref)
        for j in range(radix):
            real, imag = _radix_terms(radix, j)

            def comb(terms):
                acc = None
                for sgn, part, q in terms:
                    v = parts[part][q * lr:(q + 1) * lr, :].astype(F32)
                    if acc is None:
                        acc = v if sgn > 0 else -v
                    else:
                        acc = acc + v if sgn > 0 else acc - v
                return acc

            v_ref[0:lr, :] = comb(real).astype(BF16)
            v_ref[lr:2 * lr, :] = comb(imag).astype(BF16)
            z = _dot(e_ref[j], v_ref[...])
            for k, z_ref in enumerate(z_refs):
                z_ref[pl.ds(j, lr, stride=radix), :] = z[:, k * LANES:(k + 1) * LANES]
        for k, z_ref in enumerate(z_refs):
            sl = slice(k * LANES, (k + 1) * LANES)
            o_ref[:, sl] = (g1_ref[:, sl] * z_ref[...]).astype(BF16)

    return pl.pallas_call(
        kern, grid=(nb, ncb),
        in_specs=[pl.BlockSpec((seg, cblk), lambda b, c: (b, c)),
                  pl.BlockSpec((seg, cblk), lambda b, c: (b, ncb + c)),
                  _resident((radix, lr, 2 * lr)),
                  pl.BlockSpec((None, 1, cblk), lambda b, c: (st.seg_row(b), 0, G1 * ncb + c))],
        out_specs=pl.BlockSpec((seg, cblk), lambda b, c: (b, c)),
        out_shape=SDS((n, d), BF16),
        scratch_shapes=[pltpu.VMEM((2 * lr, cblk), BF16)] + [pltpu.VMEM((seg, LANES), F32)] * (cblk // LANES),
        name=f"seq_dft_r{radix}",
        compiler_params=_params(("parallel", "parallel"), 40))(p, p, e, mods)


def _fourier_layer(streams, xs, mods, gain, w_out):
    wcs = _fourier_weight_call(w_out)
    outs = []
    for st, x in zip(streams, xs):
        if x is None:
            outs.append(None)
            continue
        p = _proj_call(st, x, mods, gain, wcs, "fourier_proj")
        outs.append(_seq_dft_call(st, p, mods, 4 if st.seg >= 1024 else 1))
    return outs


def _conv_in_call(st, x, mods, gain, w_in):
    n, d = x.shape
    t = st.tile

    def kern(x_ref, g_ref, sc_ref, sh_ref, w_ref, bg_ref, u_ref):
        h = _modnorm(x_ref[...], g_ref[...], sc_ref[...], sh_ref[...]).astype(BF16)
        bg_ref[...] = _dot(h, w_ref[:, 0:d]).astype(BF16)
        u_ref[...] = (_dot(h, w_ref[:, d:2 * d]) * _dot(h, w_ref[:, 2 * d:3 * d])).astype(BF16)

    return pl.pallas_call(
        kern, grid=(st.tiles,),
        in_specs=[pl.BlockSpec((t, d), lambda i: (i, 0)), _full((1, d)),
                  _mod_spec(st, SC1, d), _mod_spec(st, SH1, d), _resident((d, 3 * d))],
        out_specs=[pl.BlockSpec((t, d), lambda i: (i, 0))] * 2,
        out_shape=[SDS((n, d), BF16)] * 2, name="conv_in",
        compiler_params=_params(("parallel",), 48))(x, gain, mods, mods, w_in)


def _conv_out_call(st, bg, u, mods, conv_w, w_out):
    n, d = u.shape
    t = st.tile
    halo = 16
    hb = t // halo
    nhalo = n // halo
    seg = st.seg

    def kern(u_ref, up_ref, un_ref, bg_ref, cw_ref, g1_ref, w_ref, o_ref):
        i = pl.program_id(0)
        u = u_ref[...].astype(F32)
        row = lax.broadcasted_iota(jnp.int32, (t, 1), 0)
        pos = (i * t + row) & (seg - 1)
        prev_row = up_ref[...].astype(F32)[halo - 1:halo, :]
        next_row = un_ref[...].astype(F32)[0:1, :]
        um = jnp.where(row == 0, prev_row, pltpu.roll(u, 1, axis=0))
        um = jnp.where(pos == 0, 0.0, um)
        up = jnp.where(row == t - 1, next_row, pltpu.roll(u, t - 1, axis=0))
        up = jnp.where(pos == seg - 1, 0.0, up)
        z = cw_ref[0:1, :] * um + cw_ref[1:2, :] * u + cw_ref[2:3, :] * up
        a = (bg_ref[...].astype(F32) * z).astype(BF16)
        o_ref[...] = (g1_ref[...] * _dot(a, w_ref[...])).astype(BF16)

    tile_spec = pl.BlockSpec((t, d), lambda i: (i, 0))
    return pl.pallas_call(
        kern, grid=(st.tiles,),
        in_specs=[tile_spec,
                  pl.BlockSpec((halo, d), lambda i: (jnp.maximum(i * hb - 1, 0), 0)),
                  pl.BlockSpec((halo, d), lambda i: (jnp.minimum((i + 1) * hb, nhalo - 1), 0)),
                  tile_spec, _full((3, d)), _mod_spec(st, G1, d), _resident((d, d))],
        out_specs=tile_spec, out_shape=SDS((n, d), BF16), name="conv_out",
        compiler_params=_params(("parallel",), 40))(u, u, u, bg, conv_w, mods, w_out)


def _conv_layer(streams, xs, mods, gain, w_in, conv_w, w_out):
    w_in_b, w_out_b = w_in.astype(BF16), w_out.astype(BF16)
    outs = []
    for st, x in zip(streams, xs):
        if x is None:
            outs.append(None)
            continue
        bg, u = _conv_in_call(st, x, mods, gain, w_in_b)
        outs.append(_conv_out_call(st, bg, u, mods, conv_w, w_out_b))
    return outs


def _rope_tables(length):
    q = ROPE_DIM // 4
    pos = np.arange(length)
    rc = np.stack([pos // GRID_W, pos % GRID_W], axis=1).astype(np.float32)
    inv = (ROPE_THETA ** (-np.arange(q, dtype=np.float32) / q)).astype(np.float32)
    lane = np.arange(ROPE_DIM)
    ang = rc[:, lane >> 5] * inv[lane & (q - 1)][None, :]
    sign = np.where((lane & q) == 0, -1.0, 1.0)[None, :]
    cos = np.cos(ang.astype(np.float32)).astype(np.float32)
    sin = (np.sin(ang.astype(np.float32)) * sign).astype(np.float32)
    reps = LANES // ROPE_DIM
    return jnp.asarray(np.tile(cos, (1, reps))), jnp.asarray(np.tile(sin, (1, reps)))


def _rope_partner_perm(n_heads):
    lane = np.arange(n_heads * ROPE_DIM)
    return lane ^ (ROPE_DIM // 4)


def _seg_rinv(x, seg):
    t, w = x.shape
    cols = []
    for c in range(w // LANES):
        blk = x[:, c * LANES:(c + 1) * LANES]
        sq = blk * blk
        if seg == LANES:
            cols.append(jnp.broadcast_to(
                lax.rsqrt(jnp.mean(sq, axis=-1, keepdims=True) + EPS), (t, LANES)))
        else:
            lane = lax.broadcasted_iota(jnp.int32, (t, LANES), 1)
            low = lane < seg
            lo = jnp.sum(jnp.where(low, sq, 0.0), axis=-1, keepdims=True) * (1.0 / seg)
            hi = jnp.sum(jnp.where(low, 0.0, sq), axis=-1, keepdims=True) * (1.0 / seg)
            cols.append(jnp.where(low, lax.rsqrt(lo + EPS), lax.rsqrt(hi + EPS)))
    return cols[0] if len(cols) == 1 else jnp.concatenate(cols, axis=-1)


def _tile_lanes(v, w):
    reps = w // LANES
    return v if reps == 1 else jnp.concatenate([v] * reps, axis=-1)


def _mla_proj_call(st, x, mods, gain, wts, rope, need_q):
    n, d = x.shape
    t = st.tile
    h = MLA_HEADS
    wn, wr = h * MLA_NOPE, h * MLA_ROPE
    (w_in, q_norm, kv_norm, w_uq, w_ukv, qg_n, qg_r, qg_rp, kg_n, kg_r, kg_rp) = wts
    use_rope = rope is not None

    def kern(*refs):
        refs = list(refs)
        x_ref, g_ref, sc_ref, sh_ref, win_ref, qn_ref_, kvn_ref_, wuq_ref, wukv_ref = refs[:9]
        qgn_ref, qgr_ref, qgrp_ref, kgn_ref, kgr_ref, kgrp_ref = refs[9:15]
        rest = refs[15:]
        if use_rope:
            cos_ref, sin_ref = rest[:2]
            rest = rest[2:]
        if need_q:
            oqn_ref, oqr_ref = rest[:2]
            rest = rest[2:]
        okn_ref, okr_ref, ov_ref = rest
        hm = _modnorm(x_ref[...], g_ref[...], sc_ref[...], sh_ref[...]).astype(BF16)
        ck = _dot(hm, win_ref[...])
        kr2 = ck[:, MLA_Q_RANK + MLA_KV_RANK:]
        kr_rinv = _seg_rinv(kr2, MLA_ROPE)[:, 0:MLA_ROPE]
        kr_raw = kr2[:, 0:MLA_ROPE] * kgr_ref[...]
        if use_rope:
            kr_par = kr2[:, MLA_ROPE:] * kgrp_ref[...]
            kr = kr_rinv * (kr_raw * cos_ref[:, 0:MLA_ROPE] + kr_par * sin_ref[:, 0:MLA_ROPE])
        else:
            kr = kr_rinv * kr_raw
        okr_ref[...] = kr.astype(BF16)
        ckv = ck[:, MLA_Q_RANK:MLA_Q_RANK + MLA_KV_RANK]
        ckv = (ckv * _seg_rinv(ckv, LANES) * kvn_ref_[...]).astype(BF16)
        kv = _dot(ckv, wukv_ref[...])
        kn = kv[:, 0:wn]
        okn_ref[...] = (kn * _seg_rinv(kn, MLA_NOPE) * kgn_ref[...]).astype(BF16)
        ov_ref[...] = kv[:, wn:].astype(BF16)
        if need_q:
            cq = ck[:, 0:MLA_Q_RANK]
            rq = lax.rsqrt(jnp.mean(cq * cq, axis=-1, keepdims=True) + EPS)
            cq = (cq * rq * qn_ref_[...]).astype(BF16)
            q = _dot(cq, wuq_ref[...])
            qn = q[:, 0:wn]
            oqn_ref[...] = (qn * _seg_rinv(qn, MLA_NOPE) * (qgn_ref[...] * (MLA_SCALE * LOG2E))).astype(BF16)
            qr_raw = q[:, wn:wn + wr]
            rinv = _seg_rinv(qr_raw, MLA_ROPE) * (MLA_SCALE * LOG2E)
            if use_rope:
                qr_par = q[:, wn + wr:]
                cos = _tile_lanes(cos_ref[...], wr)
                sin = _tile_lanes(sin_ref[...], wr)
                qr = rinv * (qr_raw * qgr_ref[...] * cos + qr_par * qgrp_ref[...] * sin)
            else:
                qr = rinv * (qr_raw * qgr_ref[...])
            oqr_ref[...] = qr.astype(BF16)

    tile = lambda w: pl.BlockSpec((t, w), lambda i: (i, 0))
    in_specs = [tile(d), _full((1, d)), _mod_spec(st, SC1, d), _mod_spec(st, SH1, d),
                _resident(w_in.shape), _full(q_norm.shape), _full(kv_norm.shape),
                _resident(w_uq.shape), _resident(w_ukv.shape),
                _full(qg_n.shape), _full(qg_r.shape), _full(qg_rp.shape),
                _full(kg_n.shape), _full(kg_r.shape), _full(kg_rp.shape)]
    args = [x, gain, mods, mods, w_in, q_norm, kv_norm, w_uq, w_ukv, qg_n, qg_r, qg_rp, kg_n, kg_r, kg_rp]
    if use_rope:
        tps = st.tiles_per_seg
        in_specs += [pl.BlockSpec((t, LANES), lambda i: (i % tps, 0))] * 2
        args += list(rope)
    out_specs, out_shape = [], []
    if need_q:
        out_specs += [tile(wn), tile(wr)]
        out_shape += [SDS((n, wn), BF16), SDS((n, wr), BF16)]
    out_specs += [tile(wn), tile(MLA_ROPE), tile(h * MLA_V)]
    out_shape += [SDS((n, wn), BF16), SDS((n, MLA_ROPE), BF16), SDS((n, h * MLA_V), BF16)]
    outs = pl.pallas_call(
        kern, grid=(st.tiles,), in_specs=in_specs, out_specs=out_specs, out_shape=out_shape,
        name="mla_proj", compiler_params=_params(("parallel",), 48))(*args)
    if need_q:
        return tuple(outs)
    return (None, None) + tuple(outs)


def _mla_attn_fast_call(batch, lq, tq, q, kv_sets, shift):
    qn, qr = q
    lk_total = sum(s[3] for s in kv_sets)
    hp = 2 if lk_total > 1024 else 8
    n_hp = MLA_HEADS // hp
    nq = lq // tq
    nsets = len(kv_sets)
    lengths = [s[3] for s in kv_sets]
    lk = sum(lengths)
    kw = 2 * LANES
    pad = kw - MLA_NOPE - MLA_ROPE

    def kern(*refs):
        sh_ref, qn_ref, qr_ref = refs[:3]
        sets = [refs[3 + 3 * s:6 + 3 * s] for s in range(nsets)]
        o_ref, kq_ref, vq_ref = refs[-3:]

        @pl.when(pl.program_id(2) == 0)
        def _():
            lane_k = lax.broadcasted_iota(jnp.int32, (lk, pad), 1)
            lane_v = lax.broadcasted_iota(jnp.int32, (lk, kw - MLA_V), 1)
            for hh in range(hp):
                off = 0
                for (kn_ref, kr_ref, v_ref), length in zip(sets, lengths):
                    kq_ref[hh, off:off + length, 0:MLA_NOPE] = kn_ref[:, hh * MLA_NOPE:(hh + 1) * MLA_NOPE]
                    kq_ref[hh, off:off + length, MLA_NOPE:MLA_NOPE + MLA_ROPE] = kr_ref[...]
                    vq_ref[hh, off:off + length, 0:MLA_V] = v_ref[:, hh * MLA_V:(hh + 1) * MLA_V]
                    off += length
                kq_ref[hh, :, MLA_NOPE + MLA_ROPE:] = jnp.where(lane_k == 0, -sh_ref[:, 0:pad], 0.0).astype(BF16)
                vq_ref[hh, :, MLA_V:] = jnp.where(lane_v == 0, 1.0, 0.0).astype(BF16)

        lane_q = lax.broadcasted_iota(jnp.int32, (tq, pad), 1)
        one = jnp.where(lane_q == 0, 1.0, 0.0).astype(BF16)
        for hh in range(hp):
            qc = jnp.concatenate([qn_ref[:, hh * MLA_NOPE:(hh + 1) * MLA_NOPE],
                                  qr_ref[:, hh * MLA_ROPE:(hh + 1) * MLA_ROPE], one], axis=1)
            p = jnp.exp2(_nt(qc, kq_ref[hh]).astype(BF16))
            acc = _dot(p, vq_ref[hh])
            o_ref[:, hh * MLA_V:(hh + 1) * MLA_V] = (acc[:, 0:MLA_V] / acc[:, MLA_V:MLA_V + 1]).astype(BF16)

    in_specs = [_full((1, LANES)),
                pl.BlockSpec((tq, hp * MLA_NOPE), lambda b, h, i: (b * nq + i, h)),
                pl.BlockSpec((tq, hp * MLA_ROPE), lambda b, h, i: (b * nq + i, h))]
    args = [shift, qn, qr]
    for kn, kr, v, length in kv_sets:
        in_specs += [pl.BlockSpec((length, hp * MLA_NOPE), lambda b, h, i: (b, h)),
                     pl.BlockSpec((length, MLA_ROPE), lambda b, h, i: (b, 0)),
                     pl.BlockSpec((length, hp * MLA_V), lambda b, h, i: (b, h))]
        args += [kn, kr, v]
    return pl.pallas_call(
        kern, grid=(batch, n_hp, nq), in_specs=in_specs,
        out_specs=pl.BlockSpec((tq, hp * MLA_V), lambda b, h, i: (b * nq + i, h)),
        out_shape=SDS((batch * lq, MLA_HEADS * MLA_V), BF16),
        scratch_shapes=[pltpu.VMEM((hp, lk, kw), BF16), pltpu.VMEM((hp, lk, kw), BF16)],
        name="mla_attn_fast",
        compiler_params=_params(("parallel", "parallel", "arbitrary"), 56))(*args)


def _mla_attn_call(batch, lq, tq, q, kv_sets):
    qn, qr = q
    hp = 2
    n_hp = MLA_HEADS // hp
    nq = lq // tq
    nsets = len(kv_sets)

    def kern(*refs):
        qn_ref, qr_ref = refs[:2]
        o_ref = refs[-1]
        sets = [refs[2 + 3 * s:5 + 3 * s] for s in range(nsets)]
        for hh in range(hp):
            qn_h = qn_ref[:, hh * MLA_NOPE:(hh + 1) * MLA_NOPE]
            qr_h = qr_ref[:, hh * MLA_ROPE:(hh + 1) * MLA_ROPE]
            scores = [_nt(qn_h, kn_ref[:, hh * MLA_NOPE:(hh + 1) * MLA_NOPE]) + _nt(qr_h, kr_ref[...])
                      for kn_ref, kr_ref, _ in sets]
            m = functools.reduce(jnp.maximum, [jnp.max(s, axis=-1, keepdims=True) for s in scores])
            ps = [jnp.exp2(s - m) for s in scores]
            den = functools.reduce(lambda a, b: a + b, [jnp.sum(p, axis=-1, keepdims=True) for p in ps])
            acc = functools.reduce(lambda a, b: a + b, [
                _dot(p.astype(BF16), v_ref[:, hh * MLA_V:(hh + 1) * MLA_V])
                for p, (_, _, v_ref) in zip(ps, sets)])
            o_ref[:, hh * MLA_V:(hh + 1) * MLA_V] = (acc / den).astype(BF16)

    in_specs = [pl.BlockSpec((tq, hp * MLA_NOPE), lambda b, h, i: (b * nq + i, h)),
                pl.BlockSpec((tq, hp * MLA_ROPE), lambda b, h, i: (b * nq + i, h))]
    args = [qn, qr]
    for kn, kr, v, length in kv_sets:
        in_specs += [pl.BlockSpec((length, hp * MLA_NOPE), lambda b, h, i: (b, h)),
                     pl.BlockSpec((length, MLA_ROPE), lambda b, h, i: (b, 0)),
                     pl.BlockSpec((length, hp * MLA_V), lambda b, h, i: (b, h))]
        args += [kn, kr, v]
    return pl.pallas_call(
        kern, grid=(batch, n_hp, nq), in_specs=in_specs,
        out_specs=pl.BlockSpec((tq, hp * MLA_V), lambda b, h, i: (b * nq + i, h)),
        out_shape=SDS((batch * lq, MLA_HEADS * MLA_V), BF16), name="mla_attn",
        compiler_params=_params(("parallel", "parallel", "arbitrary"), 56))(*args)


def _mla_layer(streams, xs, mods, gain, p, rope, ctx_next):
    st_c, st_x = streams
    xc, xx = xs
    h = MLA_HEADS
    w_in, q_norm, kv_norm, w_uq, w_ukv, q_gain, k_gain, w_out = p
    par = _rope_partner_perm(1)
    kr_cols = w_in[:, MLA_Q_RANK + MLA_KV_RANK:]
    w_in_p = jnp.concatenate([w_in, kr_cols[:, par]], axis=1).astype(BF16)
    wq = w_uq.reshape(MLA_Q_RANK, h, MLA_NOPE + MLA_ROPE)
    wq_n = wq[:, :, :MLA_NOPE].reshape(MLA_Q_RANK, h * MLA_NOPE)
    wq_r = wq[:, :, MLA_NOPE:]
    w_uq_p = jnp.concatenate([wq_n, wq_r.reshape(MLA_Q_RANK, h * MLA_ROPE),
                              wq_r[:, :, par].reshape(MLA_Q_RANK, h * MLA_ROPE)], axis=1).astype(BF16)
    wkv = w_ukv.reshape(MLA_KV_RANK, h, MLA_NOPE + MLA_V)
    w_ukv_p = jnp.concatenate([wkv[:, :, :MLA_NOPE].reshape(MLA_KV_RANK, h * MLA_NOPE),
                               wkv[:, :, MLA_NOPE:].reshape(MLA_KV_RANK, h * MLA_V)], axis=1).astype(BF16)
    qg_r = q_gain[MLA_NOPE:]
    kg_r = k_gain[MLA_NOPE:]
    wts = (w_in_p, q_norm.reshape(1, -1), kv_norm.reshape(1, -1), w_uq_p, w_ukv_p,
           jnp.tile(q_gain[:MLA_NOPE], h).reshape(1, -1), jnp.tile(qg_r, h).reshape(1, -1),
           jnp.tile(qg_r[par], h).reshape(1, -1), jnp.tile(k_gain[:MLA_NOPE], h).reshape(1, -1),
           kg_r.reshape(1, -1), kg_r[par].reshape(1, -1))
    w_out_b = w_out.astype(BF16)
    qn_c, qr_c, kn_c, kr_c, v_c = _mla_proj_call(st_c.with_tile(PROJ_TILE), xc, mods, gain, wts, None, ctx_next)
    qn_x, qr_x, kn_x, kr_x, v_x = _mla_proj_call(st_x.with_tile(PROJ_TILE), xx, mods, gain, wts, rope, True)
    b = st_x.batch
    amax = lambda g: jnp.max(jnp.abs(g))
    bound = LOG2E * MLA_SCALE * (MLA_NOPE * amax(q_gain[:MLA_NOPE]) * amax(k_gain[:MLA_NOPE])
                                 + MLA_ROPE * amax(qg_r) * amax(kg_r))
    shift = bound * SHIFT_MARGIN + 1.0
    shift_row = jnp.full((1, LANES), shift, F32)

    def attend(lq, tq, q, kv_sets):
        return lax.cond(shift <= MAX_FIXED_SHIFT,
                        lambda: _mla_attn_fast_call(b, lq, tq, q, kv_sets, shift_row),
                        lambda: _mla_attn_call(b, lq, min(tq, 512), q, kv_sets))

    a_x = attend(st_x.seg, min(st_x.seg, 2048), (qn_x, qr_x), [(kn_c, kr_c, v_c, st_c.seg), (kn_x, kr_x, v_x, st_x.seg)])
    out_x = _outproj_call(st_x, a_x, mods, w_out_b, "mla_out")
    out_c = None
    if ctx_next:
        a_c = attend(st_c.seg, st_c.seg, (qn_c, qr_c), [(kn_c, kr_c, v_c, st_c.seg)])
        out_c = _outproj_call(st_c, a_c, mods, w_out_b, "mla_out")
    return [out_c, out_x]


def _gqa_proj_call(st, x, mods, gain, wts, rope, need_q):
    n, d = x.shape
    t = st.tile
    wq, wk = GQA_Q_HEADS * GQA_HEAD_DIM, GQA_KV_HEADS * GQA_HEAD_DIM
    w_in, qg, qgp, kg, kgp = wts
    use_rope = rope is not None

    def kern(*refs):
        refs = list(refs)
        x_ref, g_ref, sc_ref, sh_ref, w_ref, qg_ref, qgp_ref, kg_ref, kgp_ref = refs[:9]
        rest = refs[9:]
        if use_rope:
            cos_ref, sin_ref = rest[:2]
            rest = rest[2:]
        if need_q:
            oq_ref = rest[0]
            rest = rest[1:]
        ok_ref, ov_ref = rest
        hm = _modnorm(x_ref[...], g_ref[...], sc_ref[...], sh_ref[...]).astype(BF16)
        kvp = _dot(hm, w_ref[:, 0:3 * wk])
        k_raw = kvp[:, 0:wk]
        k_rinv = _seg_rinv(k_raw, GQA_HEAD_DIM)
        ov_ref[...] = kvp[:, wk:2 * wk].astype(BF16)
        if use_rope:
            cos_k, sin_k = _tile_lanes(cos_ref[...], wk), _tile_lanes(sin_ref[...], wk)
            k = k_rinv * (k_raw * kg_ref[...] * cos_k + kvp[:, 2 * wk:] * kgp_ref[...] * sin_k)
        else:
            k = k_rinv * (k_raw * kg_ref[...])
        ok_ref[...] = k.astype(BF16)
        if need_q:
            qp = _dot(hm, w_ref[:, 3 * wk:])
            q_raw = qp[:, 0:wq]
            rinv = _seg_rinv(q_raw, GQA_HEAD_DIM) * (GQA_SCALE * LOG2E)
            if use_rope:
                cos_q, sin_q = _tile_lanes(cos_ref[...], wq), _tile_lanes(sin_ref[...], wq)
                q = rinv * (q_raw * qg_ref[...] * cos_q + qp[:, wq:] * qgp_ref[...] * sin_q)
            else:
                q = rinv * (q_raw * qg_ref[...])
            oq_ref[...] = q.astype(BF16)

    tile = lambda w: pl.BlockSpec((t, w), lambda i: (i, 0))
    in_specs = [tile(d), _full((1, d)), _mod_spec(st, SC1, d), _mod_spec(st, SH1, d),
                _resident(w_in.shape), _full(qg.shape), _full(qgp.shape), _full(kg.shape), _full(kgp.shape)]
    args = [x, gain, mods, mods, w_in, qg, qgp, kg, kgp]
    if use_rope:
        tps = st.tiles_per_seg
        in_specs += [pl.BlockSpec((t, LANES), lambda i: (i % tps, 0))] * 2
        args += list(rope)
    out_specs, out_shape = [], []
    if need_q:
        out_specs.append(tile(wq))
        out_shape.append(SDS((n, wq), BF16))
    out_specs += [tile(wk), tile(wk)]
    out_shape += [SDS((n, wk), BF16)] * 2
    outs = pl.pallas_call(
        kern, grid=(st.tiles,), in_specs=in_specs, out_specs=out_specs, out_shape=out_shape,
        name="gqa_proj", compiler_params=_params(("parallel",), 48))(*args)
    return tuple(outs) if need_q else (None,) + tuple(outs)


def _gqa_window_call(batch, length, lc, q, k, v, kc, vc, sink):
    nb = length // BLOCK
    hd = GQA_HEAD_DIM
    wq, wk = GQA_Q_HEADS * hd, GQA_KV_HEADS * hd
    rows = GQA_GROUP * BLOCK

    def kern(sink_ref, q_ref, kp_ref, k0_ref, kn_ref, vp_ref, v0_ref, vn_ref, kc_ref, vc_ref, o_ref):
        nblk = pl.program_id(1)
        r = lax.broadcasted_iota(jnp.int32, (rows, 3 * BLOCK), 0) & (BLOCK - 1)
        c = lax.broadcasted_iota(jnp.int32, (rows, 3 * BLOCK), 1)
        valid = (c >= r + BLOCK - WINDOW) & (c <= r + BLOCK + WINDOW)
        valid = valid & ((c >= BLOCK) | (nblk > 0)) & ((c < 2 * BLOCK) | (nblk < nb - 1))
        hrow = lax.broadcasted_iota(jnp.int32, (rows, 1), 0) // BLOCK
        for g in range(GQA_KV_HEADS):
            sl = slice(g * hd, (g + 1) * hd)
            qg = jnp.concatenate([q_ref[:, (g * GQA_GROUP + j) * hd:(g * GQA_GROUP + j + 1) * hd]
                                  for j in range(GQA_GROUP)], axis=0)
            kband = jnp.concatenate([kp_ref[:, sl], k0_ref[:, sl], kn_ref[:, sl]], axis=0)
            vband = jnp.concatenate([vp_ref[:, sl], v0_ref[:, sl], vn_ref[:, sl]], axis=0)
            s_c = _nt(qg, kc_ref[:, sl])
            s_b = jnp.where(valid, _nt(qg, kband), -1e30)
            snk = jnp.zeros((rows, 1), F32)
            for j in range(GQA_GROUP):
                snk = jnp.where(hrow == j, sink_ref[g * GQA_GROUP + j], snk)
            m = jnp.maximum(jnp.maximum(jnp.max(s_c, axis=-1, keepdims=True),
                                        jnp.max(s_b, axis=-1, keepdims=True)), snk)
            p_c = jnp.exp2(s_c - m)
            p_b = jnp.exp2(s_b - m)
            den = (jnp.sum(p_c, axis=-1, keepdims=True) + jnp.sum(p_b, axis=-1, keepdims=True)
                   + jnp.exp2(snk - m))
            o = (_dot(p_c.astype(BF16), vc_ref[:, sl]) + _dot(p_b.astype(BF16), vband)) / den
            o_ref[:, g * GQA_GROUP * hd:(g + 1) * GQA_GROUP * hd] = jnp.concatenate(
                [o[j * BLOCK:(j + 1) * BLOCK, :] for j in range(GQA_GROUP)], axis=-1).astype(BF16)

    blk = lambda f: pl.BlockSpec((BLOCK, wk), f)
    prev_ = lambda b, i: (b * nb + jnp.maximum(i - 1, 0), 0)
    cur_ = lambda b, i: (b * nb + i, 0)
    next_ = lambda b, i: (b * nb + jnp.minimum(i + 1, nb - 1), 0)
    ctx_spec = pl.BlockSpec((lc, wk), lambda b, i: (b, 0))
    return pl.pallas_call(
        kern, grid=(batch, nb),
        in_specs=[pl.BlockSpec(memory_space=pltpu.SMEM),
                  pl.BlockSpec((BLOCK, wq), cur_), blk(prev_), blk(cur_), blk(next_),
                  blk(prev_), blk(cur_), blk(next_), ctx_spec, ctx_spec],
        out_specs=pl.BlockSpec((BLOCK, wq), cur_),
        out_shape=SDS((batch * length, wq), BF16), name="gqa_window",
        compiler_params=_params(("parallel", "arbitrary"), 40))(sink, q, k, k, k, v, v, v, kc, vc)


def _gqa_window_fast_call(batch, length, lc, q, k, v, kc, vc, sink2, shift):
    nb = length // BLOCK
    hd = GQA_HEAD_DIM
    wq, wk = GQA_Q_HEADS * hd, GQA_KV_HEADS * hd
    rows = GQA_GROUP * BLOCK

    def kern(sink_ref, shift_ref, q_ref, kp_ref, k0_ref, kn_ref, vp_ref, v0_ref, vn_ref, kc_ref, vc_ref, o_ref):
        nblk = pl.program_id(1)
        r = lax.broadcasted_iota(jnp.int32, (rows, 3 * BLOCK), 0) & (BLOCK - 1)
        c = lax.broadcasted_iota(jnp.int32, (rows, 3 * BLOCK), 1)
        valid = (c >= r + BLOCK - WINDOW) & (c <= r + BLOCK + WINDOW)
        valid = valid & ((c >= BLOCK) | (nblk > 0)) & ((c < 2 * BLOCK) | (nblk < nb - 1))
        hrow = lax.broadcasted_iota(jnp.int32, (rows, 1), 0) // BLOCK
        lane0_q = lax.broadcasted_iota(jnp.int32, (BLOCK, hd), 1) == 0

        def with_one(x):
            lane0 = lax.broadcasted_iota(jnp.int32, (x.shape[0], hd), 1) == 0
            return jnp.concatenate([x, jnp.where(lane0, 1.0, 0.0).astype(BF16)], axis=1)

        for g in range(GQA_KV_HEADS):
            sl = slice(g * hd, (g + 1) * hd)
            heads = [g * GQA_GROUP + j for j in range(GQA_GROUP)]
            qg = jnp.concatenate(
                [jnp.concatenate([q_ref[:, h * hd:(h + 1) * hd],
                                  jnp.where(lane0_q, -shift_ref[h], 0.0).astype(BF16)], axis=1)
                 for h in heads], axis=0)
            kband = with_one(jnp.concatenate([kp_ref[:, sl], k0_ref[:, sl], kn_ref[:, sl]], axis=0))
            vband = with_one(jnp.concatenate([vp_ref[:, sl], v0_ref[:, sl], vn_ref[:, sl]], axis=0))
            p_c = jnp.exp2(_nt(qg, with_one(kc_ref[:, sl])).astype(BF16))
            p_b = jnp.exp2(jnp.where(valid, _nt(qg, kband), -1e30).astype(BF16))
            acc = _dot(p_c, with_one(vc_ref[:, sl])) + _dot(p_b, vband)
            snk = jnp.zeros((rows, 1), F32)
            for j, h in enumerate(heads):
                snk = jnp.where(hrow == j, sink_ref[h] - shift_ref[h], snk)
            o = acc[:, 0:hd] / (acc[:, hd:hd + 1] + jnp.exp2(snk))
            o_ref[:, g * GQA_GROUP * hd:(g + 1) * GQA_GROUP * hd] = jnp.concatenate(
                [o[j * BLOCK:(j + 1) * BLOCK, :] for j in range(GQA_GROUP)], axis=-1).astype(BF16)

    blk = lambda f: pl.BlockSpec((BLOCK, wk), f)
    prev_ = lambda b, i: (b * nb + jnp.maximum(i - 1, 0), 0)
    cur_ = lambda b, i: (b * nb + i, 0)
    next_ = lambda b, i: (b * nb + jnp.minimum(i + 1, nb - 1), 0)
    ctx_spec = pl.BlockSpec((lc, wk), lambda b, i: (b, 0))
    smem = pl.BlockSpec(memory_space=pltpu.SMEM)
    return pl.pallas_call(
        kern, grid=(batch, nb),
        in_specs=[smem, smem, pl.BlockSpec((BLOCK, wq), cur_), blk(prev_), blk(cur_), blk(next_),
                  blk(prev_), blk(cur_), blk(next_), ctx_spec, ctx_spec],
        out_specs=pl.BlockSpec((BLOCK, wq), cur_),
        out_shape=SDS((batch * length, wq), BF16), name="gqa_window_fast",
        compiler_params=_params(("parallel", "arbitrary"), 40))(sink2, shift, q, k, k, k, v, v, v, kc, vc)


def _gqa_layer(streams, xs, mods, gain, p, rope, ctx_next):
    assert not ctx_next, "the windowed-GQA mixer is only implemented as the last layer"
    st_c, st_x = streams
    xc, xx = xs
    w_in, q_gain, k_gain, sink, w_out = p
    wq, wk = GQA_Q_HEADS * GQA_HEAD_DIM, GQA_KV_HEADS * GQA_HEAD_DIM
    w_q, w_k, w_v = w_in[:, :wq], w_in[:, wq:wq + wk], w_in[:, wq + wk:]
    w_in_p = jnp.concatenate([w_k, w_v, w_k[:, _rope_partner_perm(GQA_KV_HEADS)],
                              w_q, w_q[:, _rope_partner_perm(GQA_Q_HEADS)]], axis=1).astype(BF16)
    par = _rope_partner_perm(1)
    wts = (w_in_p, jnp.tile(q_gain, GQA_Q_HEADS).reshape(1, -1), jnp.tile(q_gain[par], GQA_Q_HEADS).reshape(1, -1),
           jnp.tile(k_gain, GQA_KV_HEADS).reshape(1, -1), jnp.tile(k_gain[par], GQA_KV_HEADS).reshape(1, -1))
    _, kc, vc = _gqa_proj_call(st_c.with_tile(PROJ_TILE), xc, mods, gain, wts, None, False)
    q, k, v = _gqa_proj_call(st_x.with_tile(PROJ_TILE), xx, mods, gain, wts, rope, True)
    sink2 = sink * LOG2E
    bound = LOG2E * GQA_SCALE * GQA_HEAD_DIM * jnp.max(jnp.abs(q_gain)) * jnp.max(jnp.abs(k_gain))
    shift = jnp.maximum(bound * SHIFT_MARGIN + 1.0, sink2).astype(BF16).astype(F32)
    args = (st_x.batch, st_x.seg, st_c.seg, q, k, v, kc, vc, sink2)
    a = lax.cond(jnp.max(shift) <= MAX_FIXED_SHIFT,
                 lambda: _gqa_window_fast_call(*args, shift),
                 lambda: _gqa_window_call(*args))
    return [None, _outproj_call(st_x, a, mods, w_out.astype(BF16), "gqa_out")]


def _cast_experts_call(w, layer):
    _, ne, a, b = w.shape
    eb = 4

    def kern(w_ref, o_ref):
        o_ref[...] = w_ref[...].astype(BF16)

    return pl.pallas_call(
        kern, grid=(ne // eb,),
        in_specs=[pl.BlockSpec((None, eb, a, b), lambda i: (layer, i, 0, 0))],
        out_specs=pl.BlockSpec((eb, a, b), lambda i: (i, 0, 0)),
        out_shape=SDS((ne, a, b), BF16), name="cast_experts",
        compiler_params=_params(("parallel",), 32))(w)


def _route(logits_t, bias_col):
    scores = _sigmoid(logits_t)
    biased = scores + bias_col
    rows = [biased[e:e + 1, :] for e in range(N_EXPERTS)]
    srow = [scores[e:e + 1, :] for e in range(N_EXPERTS)]
    epg = EXPERTS_PER_GROUP
    gscore = []
    for g in range(N_EXPERT_GROUPS):
        v = rows[g * epg:(g + 1) * epg]
        pair = [v[a] + v[b] for a in range(epg) for b in range(a + 1, epg)]
        gscore.append(functools.reduce(jnp.maximum, pair))
    ind, wloc = [], [None] * epg
    for g in range(N_EXPERT_GROUPS):
        best = None
        for g2 in range(N_EXPERT_GROUPS):
            if g2 == g:
                continue
            cnd = gscore[g] > gscore[g2] if g2 < g else gscore[g] >= gscore[g2]
            best = cnd if best is None else best & cnd
        ind.append(jnp.where(best, 1.0, 0.0))
        for j in range(epg):
            e = g * epg + j
            rank = None
            for e2 in range(g * epg, (g + 1) * epg):
                if e2 == e:
                    continue
                ahead = rows[e2] >= rows[e] if e2 < e else rows[e2] > rows[e]
                one = jnp.where(ahead, 1.0, 0.0)
                rank = one if rank is None else rank + one
            w = jnp.where(best & (rank < 2.0), srow[e], 0.0)
            wloc[j] = w if wloc[j] is None else wloc[j] + w
    den = functools.reduce(lambda a, b: a + b, wloc)
    return ind, [w / den for w in wloc]


MOE_TILE = 1024
MOE_CHUNK = 128


def _moe_call(st, x_in, y_in, mods, gain, router_wt, router_bias, w_gate, w_up, w_down):
    n, d = x_in.shape
    t = min(MOE_TILE, n)
    r = MOE_CHUNK
    ne, _, ff = w_gate.shape
    epg, ng = EXPERTS_PER_GROUP, N_EXPERT_GROUPS
    nch = t // r + ng - 1
    tiles = n // t
    wd2 = w_down.reshape(ne * ff, d)

    def kern(x_ref, y_ref, g_ref, sc_ref, sh_ref, g2_ref, rw_ref, rb_ref, wg_ref, wu_ref, wd_ref, o_ref,
             hb_ref, oh_ref, xg_ref):
        hb_ref[...] = _modnorm(x_ref[...] + y_ref[...].astype(F32), g_ref[...], sc_ref[...], sh_ref[...]).astype(BF16)
        ind, wloc = _route(_nt(rw_ref[...], hb_ref[...]), rb_ref[...])
        ind8 = jnp.concatenate(ind + [jnp.zeros((8 - ng, t), F32)], axis=0)
        lane = lax.broadcasted_iota(jnp.int32, (8, t), 1)
        csum, step = ind8, 1
        while step < t:
            csum = csum + jnp.where(lane >= step, pltpu.roll(csum, step, axis=1), 0.0)
            step *= 2
        pos = csum - ind8
        count = [jnp.sum(ind[g]).astype(jnp.int32) for g in range(ng)]
        first = [jnp.int32(0)]
        for g in range(ng):
            first.append(first[g] + (count[g] + (r - 1)) // r)
        slot = functools.reduce(lambda a, b: a + b, [
            ind[g] * (pos[g:g + 1, :] + (first[g] * r).astype(F32)) for g in range(ng)])
        w_hi = [w.astype(BF16) for w in wloc]
        w_lo = [(w - h.astype(F32)).astype(BF16) for w, h in zip(wloc, w_hi)]
        wst = jnp.concatenate(w_hi + w_lo, axis=0)
        row_id = lax.broadcasted_iota(jnp.int32, (r, t), 0).astype(F32)
        for c in range(nch):
            oh_ref[c * r:(c + 1) * r, :] = jnp.where(slot == row_id + float(c * r), 1.0, 0.0).astype(BF16)
        main = (nch - 1) * r
        xg_ref[0:main, :] = _dot(oh_ref[0:main, :], hb_ref[...]).astype(BF16)
        wr_all = _nt(wst, oh_ref[...]).T
        for c in range(nch):
            rows = slice(c * r, (c + 1) * r)

            def chunk_ffn(c=c, rows=rows):
                grp = functools.reduce(lambda a, b: a + b,
                                       [(c >= first[g]).astype(jnp.int32) for g in range(1, ng)])
                if c * r < main:
                    xg = xg_ref[rows, :]
                else:
                    xg = _dot(oh_ref[rows, :], hb_ref[...]).astype(BF16)
                wr = wr_all[c * r:(c + 1) * r, :]
                acts = []
                for j in range(epg):
                    e = grp * epg + j
                    gt = _dot(xg, wg_ref[e])
                    up = _dot(xg, wu_ref[e])
                    acts.append((gt * _sigmoid(gt) * up * (wr[:, j:j + 1] + wr[:, epg + j:epg + j + 1])).astype(BF16))
                wd_g = wd_ref[pl.ds(pl.multiple_of(grp * (epg * ff), epg * ff), epg * ff), :]
                xg_ref[rows, :] = _dot(jnp.concatenate(acts, axis=1), wd_g).astype(BF16)

            if c < t // r:
                chunk_ffn()
            else:
                pl.when(c < first[ng])(chunk_ffn)

        tdot = lambda a, b: lax.dot_general(a, b, (((0,), (0,)), ((), ())), preferred_element_type=F32)
        o_ref[...] = (x_ref[...] + y_ref[...].astype(F32)
                      + g2_ref[...] * tdot(oh_ref[0:main, :], xg_ref[0:main, :]))

        @pl.when(nch - 1 < first[ng])
        def _():
            o_ref[...] += g2_ref[...] * tdot(oh_ref[main:, :], xg_ref[main:, :])

    row = (lambda i: st.batch) if st.is_ctx else (lambda i: i // (st.seg // t))
    mod = lambda chunk: pl.BlockSpec((None, 1, d), lambda i: (row(i), 0, chunk))
    return pl.pallas_call(
        kern, grid=(tiles,),
        in_specs=[pl.BlockSpec((t, d), lambda i: (i, 0)),
                  pl.BlockSpec((t, d), lambda i: (i, 0), pipeline_mode=pl.Buffered(1)),
                  _full((1, d)), mod(SC2), mod(SH2), mod(G2), _full((ne, d)), _full((ne, 1)),
                  _resident((ne, d, ff)), _resident((ne, d, ff)), _resident((ne * ff, d))],
        out_specs=pl.BlockSpec((t, d), lambda i: (i, 0)),
        out_shape=SDS((n, d), F32),
        scratch_shapes=[pltpu.VMEM((t, d), BF16), pltpu.VMEM((nch * r, t), BF16),
                        pltpu.VMEM((nch * r, d), BF16)],
        name="moe", compiler_params=_params(("parallel",), 58))(
            x_in, y_in, gain, mods, mods, mods, router_wt, router_bias, w_gate, w_up, wd2)


def kernel(x, c, ctx, c_ctx, w_ada, b_ada, norm_mix, norm_ffn, fourier_w_out, conv_w_in, conv_w, conv_w_out,
           mla_w_in, mla_q_norm, mla_kv_norm, mla_w_uq, mla_w_ukv, mla_q_gain, mla_k_gain, mla_w_out,
           gqa_w_in, gqa_q_gain, gqa_k_gain, gqa_sink, gqa_w_out, router_w, router_bias,
           moe_w_gate, moe_w_up, moe_w_down):
    b, l, d = x.shape
    lc = ctx.shape[1]
    depth = w_ada.shape[0]
    st_c = Stream(b * lc, lc, min(b * lc, TOKEN_TILE), b, True)
    st_x = Stream(b * l, l, min(l, TOKEN_TILE), b, False)
    streams = [st_c, st_x]
    r8 = -(-(b + 1) // 8) * 8
    cvec = jnp.concatenate([c, c_ctx[None, :], jnp.zeros((r8 - b - 1, d), F32)], axis=0)
    mods_all = _ada_call(cvec, w_ada, b_ada).reshape(depth, r8, 1, 6 * d)
    rope = _rope_tables(l)
    router_wt = router_w.T.astype(BF16)
    router_b = router_bias.reshape(-1, 1)
    xs = [ctx.reshape(b * lc, d), x.reshape(b * l, d)]
    for i in range(depth):
        kind, j = i % 4, i // 4
        ctx_next = i < depth - 1
        mods = mods_all[i]
        gain = norm_mix[i].reshape(1, d)
        if not (ctx_next or kind >= 2):
            xs[0] = None
        if kind == 0:
            ys = _fourier_layer(streams, xs, mods, gain, fourier_w_out[j])
        elif kind == 1:
            ys = _conv_layer(streams, xs, mods, gain, conv_w_in[j], conv_w[j], conv_w_out[j])
        elif kind == 2:
            ys = _mla_layer(streams, xs, mods, gain,
                            (mla_w_in[j], mla_q_norm[j], mla_kv_norm[j], mla_w_uq[j], mla_w_ukv[j],
                             mla_q_gain[j], mla_k_gain[j], mla_w_out[j]), rope, ctx_next)
        else:
            ys = _gqa_layer(streams, xs, mods, gain,
                            (gqa_w_in[j], gqa_q_gain[j], gqa_k_gain[j], gqa_sink[j], gqa_w_out[j]),
                            rope, ctx_next)
        if not ctx_next:
            ys[0] = None
        gain2 = norm_ffn[i].reshape(1, d)
        wg, wu, wd = (_cast_experts_call(w, i) for w in (moe_w_gate, moe_w_up, moe_w_down))
        xs = [None if y is None else _moe_call(st, x, y, mods, gain2, router_wt, router_b, wg, wu, wd)
              for st, x, y in zip(streams, xs, ys)]
    return xs[1].reshape(b, l, d)
```

```python
import functools

import numpy as np
import jax
import jax.numpy as jnp
from jax import lax
from jax.experimental import pallas as pl
from jax.experimental.pallas import tpu as pltpu

F32, BF16 = jnp.float32, jnp.bfloat16
SDS = jax.ShapeDtypeStruct

EPS = 1e-6
GRID_W = 64
ROPE_THETA = 10000.0
N_FOURIER_GROUPS = 4
MLA_HEADS, MLA_Q_RANK, MLA_KV_RANK = 16, 256, 128
MLA_NOPE, MLA_ROPE, MLA_V = 128, 64, 128
MLA_SCALE = (MLA_NOPE + MLA_ROPE) ** -0.5
GQA_Q_HEADS, GQA_KV_HEADS, GQA_HEAD_DIM = 16, 4, 64
GQA_GROUP = GQA_Q_HEADS // GQA_KV_HEADS
GQA_SCALE = GQA_HEAD_DIM ** -0.5
WINDOW = 128
BLOCK = 128
N_EXPERTS, N_EXPERT_GROUPS, EXPERT_FF = 16, 4, 256
EXPERTS_PER_GROUP = N_EXPERTS // N_EXPERT_GROUPS
ROPE_DIM = 64
LOG2E = 1.4426950408889634
MAX_FIXED_SHIFT = 50.0
SHIFT_MARGIN = 1.02
TOKEN_TILE = 1024
PROJ_TILE = 512

V7X_VMEM_BYTES = 64 * 1024 * 1024
LANES = 128
SH1, SC1, G1, SH2, SC2, G2 = range(6)


def _params(sem, vmem_mb):
    return pltpu.CompilerParams(dimension_semantics=sem, vmem_limit_bytes=vmem_mb * 1024 * 1024)


def _sigmoid(v):
    return 1.0 / (1.0 + jnp.exp(-v))


def _modnorm(x, gain, sc, sh):
    ms = jnp.mean(x * x, axis=-1, keepdims=True)
    return x * lax.rsqrt(ms + EPS) * (gain * (1.0 + sc)) + sh


def _nt(a, b):
    return lax.dot_general(a, b, (((1,), (1,)), ((), ())), preferred_element_type=F32)


def _dot(a, b):
    return jnp.dot(a, b, preferred_element_type=F32)


class Stream:
    def __init__(self, n, seg, tile, batch, is_ctx):
        self.n, self.seg, self.tile, self.batch, self.is_ctx = n, seg, tile, batch, is_ctx
        self.tiles = n // tile
        self.tiles_per_seg = seg // tile

    def mod_row(self, t):
        return self.batch if self.is_ctx else t // self.tiles_per_seg

    def seg_row(self, b):
        return self.batch if self.is_ctx else b

    def with_tile(self, tile):
        return Stream(self.n, self.seg, min(tile, self.tile), self.batch, self.is_ctx)


def _mod_spec(st, chunk, d):
    return pl.BlockSpec((None, 1, d), lambda t: (st.mod_row(t), 0, chunk))


def _full(shape):
    nd = len(shape)
    return pl.BlockSpec(shape, lambda *_: (0,) * nd)


def _resident(shape):
    nd = len(shape)
    return pl.BlockSpec(shape, lambda *_: (0,) * nd, pipeline_mode=pl.Buffered(1))


def _ada_call(cvec, w_ada, b_ada):
    depth, d, d6 = w_ada.shape
    r8 = cvec.shape[0]
    tn = d6 // 4

    def kern(c_ref, w_ref, b_ref, o_ref):
        c = c_ref[...]
        s = (c * _sigmoid(c)).astype(BF16)
        o_ref[...] = _dot(s, w_ref[...].astype(BF16)) + b_ref[...]

    return pl.pallas_call(
        kern, grid=(depth, d6 // tn),
        in_specs=[pl.BlockSpec((r8, d), lambda i, j: (0, 0)),
                  pl.BlockSpec((None, d, tn), lambda i, j: (i, 0, j)),
                  pl.BlockSpec((None, 1, tn), lambda i, j: (i, 0, j))],
        out_specs=pl.BlockSpec((None, r8, tn), lambda i, j: (i, 0, j)),
        out_shape=SDS((depth, r8, d6), F32), name="ada",
        compiler_params=_params(("arbitrary", "arbitrary"), 40))(cvec, w_ada, b_ada.reshape(depth, 1, d6))


def _proj_call(st, x, mods, gain, w, name):
    n, d = x.shape
    nout = w.shape[1]
    t = st.tile

    def kern(x_ref, g_ref, sc_ref, sh_ref, w_ref, o_ref):
        h = _modnorm(x_ref[...], g_ref[...], sc_ref[...], sh_ref[...]).astype(BF16)
        o_ref[...] = _dot(h, w_ref[...]).astype(BF16)

    return pl.pallas_call(
        kern, grid=(st.tiles,),
        in_specs=[pl.BlockSpec((t, d), lambda i: (i, 0)), _full((1, d)),
                  _mod_spec(st, SC1, d), _mod_spec(st, SH1, d), _resident((d, nout))],
        out_specs=pl.BlockSpec((t, nout), lambda i: (i, 0)),
        out_shape=SDS((n, nout), BF16), name=name,
        compiler_params=_params(("parallel",), 40))(x, gain, mods, mods, w)


def _outproj_call(st, a, mods, w, name):
    n, k = a.shape
    d = w.shape[1]
    t = st.tile

    def kern(a_ref, g1_ref, w_ref, o_ref):
        o_ref[...] = (g1_ref[...] * _dot(a_ref[...], w_ref[...])).astype(BF16)

    return pl.pallas_call(
        kern, grid=(st.tiles,),
        in_specs=[pl.BlockSpec((t, k), lambda i: (i, 0)), _mod_spec(st, G1, d), _resident((k, d))],
        out_specs=pl.BlockSpec((t, d), lambda i: (i, 0)),
        out_shape=SDS((n, d), BF16), name=name,
        compiler_params=_params(("parallel",), 40))(a, mods, w)


def _dft_tables(length, radix):
    lr = length // radix
    m = np.arange(lr)[None, :, None]
    j = np.arange(radix)[:, None, None]
    nn = np.arange(lr)[None, None, :]
    ang = 2.0 * np.pi * (((radix * m + j) * nn) % length) / length
    e = np.concatenate([np.cos(ang), np.sin(ang)], axis=-1) / np.sqrt(length)
    return jnp.asarray(e, dtype=F32).astype(BF16)


def _group_dft_tables(group):
    k = np.arange(group)
    ang = 2.0 * np.pi * ((k[:, None] * k[None, :]) % group) / group
    return (jnp.asarray(np.cos(ang) / np.sqrt(group), dtype=F32),
            jnp.asarray(np.sin(ang) / np.sqrt(group), dtype=F32))


def _fourier_weight_call(w_out):
    d = w_out.shape[0]
    grp = d // N_FOURIER_GROUPS
    cg, sg = _group_dft_tables(grp)

    def kern(cg_ref, sg_ref, w_ref, o_ref):
        w = w_ref[...]
        o_ref[:, :d] = jnp.dot(cg_ref[...], w, preferred_element_type=F32,
                               precision=lax.Precision.HIGHEST).astype(BF16)
        o_ref[:, d:] = jnp.dot(sg_ref[...], w, preferred_element_type=F32,
                               precision=lax.Precision.HIGHEST).astype(BF16)

    return pl.pallas_call(
        kern, grid=(N_FOURIER_GROUPS,),
        in_specs=[_full((grp, grp)), _full((grp, grp)), pl.BlockSpec((grp, d), lambda g: (g, 0))],
        out_specs=pl.BlockSpec((grp, 2 * d), lambda g: (g, 0)),
        out_shape=SDS((d, 2 * d), BF16), name="fourier_w",
        compiler_params=_params(("arbitrary",), 32))(cg, sg, w_out)


def _radix_terms(radix, j):
    real, imag = [], []
    for q in range(radix):
        k = (j * q * (4 // radix)) % 4 if radix > 1 else 0
        if k == 0:
            real.append((1, 0, q)); imag.append((-1, 1, q))
        elif k == 1:
            real.append((-1, 1, q)); imag.append((-1, 0, q))
        elif k == 2:
            real.append((-1, 0, q)); imag.append((1, 1, q))
        else:
            real.append((1, 1, q)); imag.append((1, 0, q))
    return real, imag


def _seq_dft_call(st, p, mods, radix):
    n, d = p.shape[0], p.shape[1] // 2
    seg = st.seg
    lr = seg // radix
    e = _dft_tables(seg, radix)
    nb = n // seg
    cblk = 2 * LANES if seg * d * 4 > 2 ** 22 else d
    ncb = d // cblk

    def kern(pc_ref, ps_ref, e_ref, g1_ref, o_ref, v_ref, *z_refs):
        parts = (pc_ref, ps_ref)
        for j in range(radix):
            real, imag = _radix_terms(radix, j)

            def comb(terms):
                acc = None
                for sgn, part, q in terms:
                    v = parts[part][q * lr:(q + 1) * lr, :].astype(F32)
                    if acc is None:
                        acc = v if sgn > 0 else -v
                    else:
                        acc = acc + v if sgn > 0 else acc - v
                return acc

            v_ref[0:lr, :] = comb(real).astype(BF16)
            v_ref[lr:2 * lr, :] = comb(imag).astype(BF16)
            z = _dot(e_ref[j], v_ref[...])
            for k, z_ref in enumerate(z_refs):
                z_ref[pl.ds(j, lr, stride=radix), :] = z[:, k * LANES:(k + 1) * LANES]
        for k, z_ref in enumerate(z_refs):
            sl = slice(k * LANES, (k + 1) * LANES)
            o_ref[:, sl] = (g1_ref[:, sl] * z_ref[...]).astype(BF16)

    return pl.pallas_call(
        kern, grid=(nb, ncb),
        in_specs=[pl.BlockSpec((seg, cblk), lambda b, c: (b, c)),
                  pl.BlockSpec((seg, cblk), lambda b, c: (b, ncb + c)),
                  _resident((radix, lr, 2 * lr)),
                  pl.BlockSpec((None, 1, cblk), lambda b, c: (st.seg_row(b), 0, G1 * ncb + c))],
        out_specs=pl.BlockSpec((seg, cblk), lambda b, c: (b, c)),
        out_shape=SDS((n, d), BF16),
        scratch_shapes=[pltpu.VMEM((2 * lr, cblk), BF16)] + [pltpu.VMEM((seg, LANES), F32)] * (cblk // LANES),
        name=f"seq_dft_r{radix}",
        compiler_params=_params(("parallel", "parallel"), 40))(p, p, e, mods)


def _fourier_layer(streams, xs, mods, gain, w_out):
    wcs = _fourier_weight_call(w_out)
    outs = []
    for st, x in zip(streams, xs):
        if x is None:
            outs.append(None)
            continue
        p = _proj_call(st, x, mods, gain, wcs, "fourier_proj")
        outs.append(_seq_dft_call(st, p, mods, 4 if st.seg >= 1024 else 1))
    return outs


def _conv_in_call(st, x, mods, gain, w_in):
    n, d = x.shape
    t = st.tile

    def kern(x_ref, g_ref, sc_ref, sh_ref, w_ref, bg_ref, u_ref):
        h = _modnorm(x_ref[...], g_ref[...], sc_ref[...], sh_ref[...]).astype(BF16)
        bg_ref[...] = _dot(h, w_ref[:, 0:d]).astype(BF16)
        u_ref[...] = (_dot(h, w_ref[:, d:2 * d]) * _dot(h, w_ref[:, 2 * d:3 * d])).astype(BF16)

    return pl.pallas_call(
        kern, grid=(st.tiles,),
        in_specs=[pl.BlockSpec((t, d), lambda i: (i, 0)), _full((1, d)),
                  _mod_spec(st, SC1, d), _mod_spec(st, SH1, d), _resident((d, 3 * d))],
        out_specs=[pl.BlockSpec((t, d), lambda i: (i, 0))] * 2,
        out_shape=[SDS((n, d), BF16)] * 2, name="conv_in",
        compiler_params=_params(("parallel",), 48))(x, gain, mods, mods, w_in)


def _conv_out_call(st, bg, u, mods, conv_w, w_out):
    n, d = u.shape
    t = st.tile
    halo = 16
    hb = t // halo
    nhalo = n // halo
    seg = st.seg

    def kern(u_ref, up_ref, un_ref, bg_ref, cw_ref, g1_ref, w_ref, o_ref):
        i = pl.program_id(0)
        u = u_ref[...].astype(F32)
        row = lax.broadcasted_iota(jnp.int32, (t, 1), 0)
        pos = (i * t + row) & (seg - 1)
        prev_row = up_ref[...].astype(F32)[halo - 1:halo, :]
        next_row = un_ref[...].astype(F32)[0:1, :]
        um = jnp.where(row == 0, prev_row, pltpu.roll(u, 1, axis=0))
        um = jnp.where(pos == 0, 0.0, um)
        up = jnp.where(row == t - 1, next_row, pltpu.roll(u, t - 1, axis=0))
        up = jnp.where(pos == seg - 1, 0.0, up)
        z = cw_ref[0:1, :] * um + cw_ref[1:2, :] * u + cw_ref[2:3, :] * up
        a = (bg_ref[...].astype(F32) * z).astype(BF16)
        o_ref[...] = (g1_ref[...] * _dot(a, w_ref[...])).astype(BF16)

    tile_spec = pl.BlockSpec((t, d), lambda i: (i, 0))
    return pl.pallas_call(
        kern, grid=(st.tiles,),
        in_specs=[tile_spec,
                  pl.BlockSpec((halo, d), lambda i: (jnp.maximum(i * hb - 1, 0), 0)),
                  pl.BlockSpec((halo, d), lambda i: (jnp.minimum((i + 1) * hb, nhalo - 1), 0)),
                  tile_spec, _full((3, d)), _mod_spec(st, G1, d), _resident((d, d))],
        out_specs=tile_spec, out_shape=SDS((n, d), BF16), name="conv_out",
        compiler_params=_params(("parallel",), 40))(u, u, u, bg, conv_w, mods, w_out)


def _conv_layer(streams, xs, mods, gain, w_in, conv_w, w_out):
    w_in_b, w_out_b = w_in.astype(BF16), w_out.astype(BF16)
    outs = []
    for st, x in zip(streams, xs):
        if x is None:
            outs.append(None)
            continue
        bg, u = _conv_in_call(st, x, mods, gain, w_in_b)
        outs.append(_conv_out_call(st, bg, u, mods, conv_w, w_out_b))
    return outs


def _rope_tables(length):
    q = ROPE_DIM // 4
    pos = np.arange(length)
    rc = np.stack([pos // GRID_W, pos % GRID_W], axis=1).astype(np.float32)
    inv = (ROPE_THETA ** (-np.arange(q, dtype=np.float32) / q)).astype(np.float32)
    lane = np.arange(ROPE_DIM)
    ang = rc[:, lane >> 5] * inv[lane & (q - 1)][None, :]
    sign = np.where((lane & q) == 0, -1.0, 1.0)[None, :]
    cos = np.cos(ang.astype(np.float32)).astype(np.float32)
    sin = (np.sin(ang.astype(np.float32)) * sign).astype(np.float32)
    reps = LANES // ROPE_DIM
    return jnp.asarray(np.tile(cos, (1, reps))), jnp.asarray(np.tile(sin, (1, reps)))


def _rope_partner_perm(n_heads):
    lane = np.arange(n_heads * ROPE_DIM)
    return lane ^ (ROPE_DIM // 4)


def _seg_rinv(x, seg):
    t, w = x.shape
    cols = []
    for c in range(w // LANES):
        blk = x[:, c * LANES:(c + 1) * LANES]
        sq = blk * blk
        if seg == LANES:
            cols.append(jnp.broadcast_to(
                lax.rsqrt(jnp.mean(sq, axis=-1, keepdims=True) + EPS), (t, LANES)))
        else:
            lane = lax.broadcasted_iota(jnp.int32, (t, LANES), 1)
            low = lane < seg
            lo = jnp.sum(jnp.where(low, sq, 0.0), axis=-1, keepdims=True) * (1.0 / seg)
            hi = jnp.sum(jnp.where(low, 0.0, sq), axis=-1, keepdims=True) * (1.0 / seg)
            cols.append(jnp.where(low, lax.rsqrt(lo + EPS), lax.rsqrt(hi + EPS)))
    return cols[0] if len(cols) == 1 else jnp.concatenate(cols, axis=-1)


def _tile_lanes(v, w):
    reps = w // LANES
    return v if reps == 1 else jnp.concatenate([v] * reps, axis=-1)


def _mla_proj_call(st, x, mods, gain, wts, rope, need_q):
    n, d = x.shape
    t = st.tile
    h = MLA_HEADS
    wn, wr = h * MLA_NOPE, h * MLA_ROPE
    (w_in, q_norm, kv_norm, w_uq, w_ukv, qg_n, qg_r, qg_rp, kg_n, kg_r, kg_rp) = wts
    use_rope = rope is not None

    def kern(*refs):
        refs = list(refs)
        x_ref, g_ref, sc_ref, sh_ref, win_ref, qn_ref_, kvn_ref_, wuq_ref, wukv_ref = refs[:9]
        qgn_ref, qgr_ref, qgrp_ref, kgn_ref, kgr_ref, kgrp_ref = refs[9:15]
        rest = refs[15:]
        if use_rope:
            cos_ref, sin_ref = rest[:2]
            rest = rest[2:]
        if need_q:
            oqn_ref, oqr_ref = rest[:2]
            rest = rest[2:]
        okn_ref, okr_ref, ov_ref = rest
        hm = _modnorm(x_ref[...], g_ref[...], sc_ref[...], sh_ref[...]).astype(BF16)
        ck = _dot(hm, win_ref[...])
        kr2 = ck[:, MLA_Q_RANK + MLA_KV_RANK:]
        kr_rinv = _seg_rinv(kr2, MLA_ROPE)[:, 0:MLA_ROPE]
        kr_raw = kr2[:, 0:MLA_ROPE] * kgr_ref[...]
        if use_rope:
            kr_par = kr2[:, MLA_ROPE:] * kgrp_ref[...]
            kr = kr_rinv * (kr_raw * cos_ref[:, 0:MLA_ROPE] + kr_par * sin_ref[:, 0:MLA_ROPE])
        else:
            kr = kr_rinv * kr_raw
        okr_ref[...] = kr.astype(BF16)
        ckv = ck[:, MLA_Q_RANK:MLA_Q_RANK + MLA_KV_RANK]
        ckv = (ckv * _seg_rinv(ckv, LANES) * kvn_ref_[...]).astype(BF16)
        kv = _dot(ckv, wukv_ref[...])
        kn = kv[:, 0:wn]
        okn_ref[...] = (kn * _seg_rinv(kn, MLA_NOPE) * kgn_ref[...]).astype(BF16)
        ov_ref[...] = kv[:, wn:].astype(BF16)
        if need_q:
            cq = ck[:, 0:MLA_Q_RANK]
            rq = lax.rsqrt(jnp.mean(cq * cq, axis=-1, keepdims=True) + EPS)
            cq = (cq * rq * qn_ref_[...]).astype(BF16)
            q = _dot(cq, wuq_ref[...])
            qn = q[:, 0:wn]
            oqn_ref[...] = (qn * _seg_rinv(qn, MLA_NOPE) * (qgn_ref[...] * (MLA_SCALE * LOG2E))).astype(BF16)
            qr_raw = q[:, wn:wn + wr]
            rinv = _seg_rinv(qr_raw, MLA_ROPE) * (MLA_SCALE * LOG2E)
            if use_rope:
                qr_par = q[:, wn + wr:]
                cos = _tile_lanes(cos_ref[...], wr)
                sin = _tile_lanes(sin_ref[...], wr)
                qr = rinv * (qr_raw * qgr_ref[...] * cos + qr_par * qgrp_ref[...] * sin)
            else:
                qr = rinv * (qr_raw * qgr_ref[...])
            oqr_ref[...] = qr.astype(BF16)

    tile = lambda w: pl.BlockSpec((t, w), lambda i: (i, 0))
    in_specs = [tile(d), _full((1, d)), _mod_spec(st, SC1, d), _mod_spec(st, SH1, d),
                _resident(w_in.shape), _full(q_norm.shape), _full(kv_norm.shape),
                _resident(w_uq.shape), _resident(w_ukv.shape),
                _full(qg_n.shape), _full(qg_r.shape), _full(qg_rp.shape),
                _full(kg_n.shape), _full(kg_r.shape), _full(kg_rp.shape)]
    args = [x, gain, mods, mods, w_in, q_norm, kv_norm, w_uq, w_ukv, qg_n, qg_r, qg_rp, kg_n, kg_r, kg_rp]
    if use_rope:
        tps = st.tiles_per_seg
        in_specs += [pl.BlockSpec((t, LANES), lambda i: (i % tps, 0))] * 2
        args += list(rope)
    out_specs, out_shape = [], []
    if need_q:
        out_specs += [tile(wn), tile(wr)]
        out_shape += [SDS((n, wn), BF16), SDS((n, wr), BF16)]
    out_specs += [tile(wn), tile(MLA_ROPE), tile(h * MLA_V)]
    out_shape += [SDS((n, wn), BF16), SDS((n, MLA_ROPE), BF16), SDS((n, h * MLA_V), BF16)]
    outs = pl.pallas_call(
        kern, grid=(st.tiles,), in_specs=in_specs, out_specs=out_specs, out_shape=out_shape,
        name="mla_proj", compiler_params=_params(("parallel",), 48))(*args)
    if need_q:
        return tuple(outs)
    return (None, None) + tuple(outs)


def _mla_attn_fast_call(batch, lq, tq, q, kv_sets, shift):
    qn, qr = q
    lk_total = sum(s[3] for s in kv_sets)
    hp = 2 if lk_total > 1024 else 8
    n_hp = MLA_HEADS // hp
    nq = lq // tq
    nsets = len(kv_sets)
    lengths = [s[3] for s in kv_sets]
    lk = sum(lengths)
    kw = 2 * LANES
    pad = kw - MLA_NOPE - MLA_ROPE

    def kern(*refs):
        sh_ref, qn_ref, qr_ref = refs[:3]
        sets = [refs[3 + 3 * s:6 + 3 * s] for s in range(nsets)]
        o_ref, kq_ref, vq_ref = refs[-3:]

        @pl.when(pl.program_id(2) == 0)
        def _():
            lane_k = lax.broadcasted_iota(jnp.int32, (lk, pad), 1)
            lane_v = lax.broadcasted_iota(jnp.int32, (lk, kw - MLA_V), 1)
            for hh in range(hp):
                off = 0
                for (kn_ref, kr_ref, v_ref), length in zip(sets, lengths):
                    kq_ref[hh, off:off + length, 0:MLA_NOPE] = kn_ref[:, hh * MLA_NOPE:(hh + 1) * MLA_NOPE]
                    kq_ref[hh, off:off + length, MLA_NOPE:MLA_NOPE + MLA_ROPE] = kr_ref[...]
                    vq_ref[hh, off:off + length, 0:MLA_V] = v_ref[:, hh * MLA_V:(hh + 1) * MLA_V]
                    off += length
                kq_ref[hh, :, MLA_NOPE + MLA_ROPE:] = jnp.where(lane_k == 0, -sh_ref[:, 0:pad], 0.0).astype(BF16)
                vq_ref[hh, :, MLA_V:] = jnp.where(lane_v == 0, 1.0, 0.0).astype(BF16)

        lane_q = lax.broadcasted_iota(jnp.int32, (tq, pad), 1)
        one = jnp.where(lane_q == 0, 1.0, 0.0).astype(BF16)
        for hh in range(hp):
            qc = jnp.concatenate([qn_ref[:, hh * MLA_NOPE:(hh + 1) * MLA_NOPE],
                                  qr_ref[:, hh * MLA_ROPE:(hh + 1) * MLA_ROPE], one], axis=1)
            p = jnp.exp2(_nt(qc, kq_ref[hh]).astype(BF16))
            acc = _dot(p, vq_ref[hh])
            o_ref[:, hh * MLA_V:(hh + 1) * MLA_V] = (acc[:, 0:MLA_V] / acc[:, MLA_V:MLA_V + 1]).astype(BF16)

    in_specs = [_full((1, LANES)),
                pl.BlockSpec((tq, hp * MLA_NOPE), lambda b, h, i: (b * nq + i, h)),
                pl.BlockSpec((tq, hp * MLA_ROPE), lambda b, h, i: (b * nq + i, h))]
    args = [shift, qn, qr]
    for kn, kr, v, length in kv_sets:
        in_specs += [pl.BlockSpec((length, hp * MLA_NOPE), lambda b, h, i: (b, h)),
                     pl.BlockSpec((length, MLA_ROPE), lambda b, h, i: (b, 0)),
                     pl.BlockSpec((length, hp * MLA_V), lambda b, h, i: (b, h))]
        args += [kn, kr, v]
    return pl.pallas_call(
        kern, grid=(batch, n_hp, nq), in_specs=in_specs,
        out_specs=pl.BlockSpec((tq, hp * MLA_V), lambda b, h, i: (b * nq + i, h)),
        out_shape=SDS((batch * lq, MLA_HEADS * MLA_V), BF16),
        scratch_shapes=[pltpu.VMEM((hp, lk, kw), BF16), pltpu.VMEM((hp, lk, kw), BF16)],
        name="mla_attn_fast",
        compiler_params=_params(("parallel", "parallel", "arbitrary"), 56))(*args)


def _mla_attn_call(batch, lq, tq, q, kv_sets):
    qn, qr = q
    hp = 2
    n_hp = MLA_HEADS // hp
    nq = lq // tq
    nsets = len(kv_sets)

    def kern(*refs):
        qn_ref, qr_ref = refs[:2]
        o_ref = refs[-1]
        sets = [refs[2 + 3 * s:5 + 3 * s] for s in range(nsets)]
        for hh in range(hp):
            qn_h = qn_ref[:, hh * MLA_NOPE:(hh + 1) * MLA_NOPE]
            qr_h = qr_ref[:, hh * MLA_ROPE:(hh + 1) * MLA_ROPE]
            scores = [_nt(qn_h, kn_ref[:, hh * MLA_NOPE:(hh + 1) * MLA_NOPE]) + _nt(qr_h, kr_ref[...])
                      for kn_ref, kr_ref, _ in sets]
            m = functools.reduce(jnp.maximum, [jnp.max(s, axis=-1, keepdims=True) for s in scores])
            ps = [jnp.exp2(s - m) for s in scores]
            den = functools.reduce(lambda a, b: a + b, [jnp.sum(p, axis=-1, keepdims=True) for p in ps])
            acc = functools.reduce(lambda a, b: a + b, [
                _dot(p.astype(BF16), v_ref[:, hh * MLA_V:(hh + 1) * MLA_V])
                for p, (_, _, v_ref) in zip(ps, sets)])
            o_ref[:, hh * MLA_V:(hh + 1) * MLA_V] = (acc / den).astype(BF16)

    in_specs = [pl.BlockSpec((tq, hp * MLA_NOPE), lambda b, h, i: (b * nq + i, h)),
                pl.BlockSpec((tq, hp * MLA_ROPE), lambda b, h, i: (b * nq + i, h))]
    args = [qn, qr]
    for kn, kr, v, length in kv_sets:
        in_specs += [pl.BlockSpec((length, hp * MLA_NOPE), lambda b, h, i: (b, h)),
                     pl.BlockSpec((length, MLA_ROPE), lambda b, h, i: (b, 0)),
                     pl.BlockSpec((length, hp * MLA_V), lambda b, h, i: (b, h))]
        args += [kn, kr, v]
    return pl.pallas_call(
        kern, grid=(batch, n_hp, nq), in_specs=in_specs,
        out_specs=pl.BlockSpec((tq, hp * MLA_V), lambda b, h, i: (b * nq + i, h)),
        out_shape=SDS((batch * lq, MLA_HEADS * MLA_V), BF16), name="mla_attn",
        compiler_params=_params(("parallel", "parallel", "arbitrary"), 56))(*args)


def _mla_layer(streams, xs, mods, gain, p, rope, ctx_next):
    st_c, st_x = streams
    xc, xx = xs
    h = MLA_HEADS
    w_in, q_norm, kv_norm, w_uq, w_ukv, q_gain, k_gain, w_out = p
    par = _rope_partner_perm(1)
    kr_cols = w_in[:, MLA_Q_RANK + MLA_KV_RANK:]
    w_in_p = jnp.concatenate([w_in, kr_cols[:, par]], axis=1).astype(BF16)
    wq = w_uq.reshape(MLA_Q_RANK, h, MLA_NOPE + MLA_ROPE)
    wq_n = wq[:, :, :MLA_NOPE].reshape(MLA_Q_RANK, h * MLA_NOPE)
    wq_r = wq[:, :, MLA_NOPE:]
    w_uq_p = jnp.concatenate([wq_n, wq_r.reshape(MLA_Q_RANK, h * MLA_ROPE),
                              wq_r[:, :, par].reshape(MLA_Q_RANK, h * MLA_ROPE)], axis=1).astype(BF16)
    wkv = w_ukv.reshape(MLA_KV_RANK, h, MLA_NOPE + MLA_V)
    w_ukv_p = jnp.concatenate([wkv[:, :, :MLA_NOPE].reshape(MLA_KV_RANK, h * MLA_NOPE),
                               wkv[:, :, MLA_NOPE:].reshape(MLA_KV_RANK, h * MLA_V)], axis=1).astype(BF16)
    qg_r = q_gain[MLA_NOPE:]
    kg_r = k_gain[MLA_NOPE:]
    wts = (w_in_p, q_norm.reshape(1, -1), kv_norm.reshape(1, -1), w_uq_p, w_ukv_p,
           jnp.tile(q_gain[:MLA_NOPE], h).reshape(1, -1), jnp.tile(qg_r, h).reshape(1, -1),
           jnp.tile(qg_r[par], h).reshape(1, -1), jnp.tile(k_gain[:MLA_NOPE], h).reshape(1, -1),
           kg_r.reshape(1, -1), kg_r[par].reshape(1, -1))
    w_out_b = w_out.astype(BF16)
    qn_c, qr_c, kn_c, kr_c, v_c = _mla_proj_call(st_c.with_tile(PROJ_TILE), xc, mods, gain, wts, None, ctx_next)
    qn_x, qr_x, kn_x, kr_x, v_x = _mla_proj_call(st_x.with_tile(PROJ_TILE), xx, mods, gain, wts, rope, True)
    b = st_x.batch
    amax = lambda g: jnp.max(jnp.abs(g))
    bound = LOG2E * MLA_SCALE * (MLA_NOPE * amax(q_gain[:MLA_NOPE]) * amax(k_gain[:MLA_NOPE])
                                 + MLA_ROPE * amax(qg_r) * amax(kg_r))
    shift = bound * SHIFT_MARGIN + 1.0
    shift_row = jnp.full((1, LANES), shift, F32)

    def attend(lq, tq, q, kv_sets):
        return lax.cond(shift <= MAX_FIXED_SHIFT,
                        lambda: _mla_attn_fast_call(b, lq, tq, q, kv_sets, shift_row),
                        lambda: _mla_attn_call(b, lq, min(tq, 512), q, kv_sets))

    a_x = attend(st_x.seg, min(st_x.seg, 2048), (qn_x, qr_x), [(kn_c, kr_c, v_c, st_c.seg), (kn_x, kr_x, v_x, st_x.seg)])
    out_x = _outproj_call(st_x, a_x, mods, w_out_b, "mla_out")
    out_c = None
    if ctx_next:
        a_c = attend(st_c.seg, st_c.seg, (qn_c, qr_c), [(kn_c, kr_c, v_c, st_c.seg)])
        out_c = _outproj_call(st_c, a_c, mods, w_out_b, "mla_out")
    return [out_c, out_x]


def _gqa_proj_call(st, x, mods, gain, wts, rope, need_q):
    n, d = x.shape
    t = st.tile
    wq, wk = GQA_Q_HEADS * GQA_HEAD_DIM, GQA_KV_HEADS * GQA_HEAD_DIM
    w_in, qg, qgp, kg, kgp = wts
    use_rope = rope is not None

    def kern(*refs):
        refs = list(refs)
        x_ref, g_ref, sc_ref, sh_ref, w_ref, qg_ref, qgp_ref, kg_ref, kgp_ref = refs[:9]
        rest = refs[9:]
        if use_rope:
            cos_ref, sin_ref = rest[:2]
            rest = rest[2:]
        if need_q:
            oq_ref = rest[0]
            rest = rest[1:]
        ok_ref, ov_ref = rest
        hm = _modnorm(x_ref[...], g_ref[...], sc_ref[...], sh_ref[...]).astype(BF16)
        kvp = _dot(hm, w_ref[:, 0:3 * wk])
        k_raw = kvp[:, 0:wk]
        k_rinv = _seg_rinv(k_raw, GQA_HEAD_DIM)
        ov_ref[...] = kvp[:, wk:2 * wk].astype(BF16)
        if use_rope:
            cos_k, sin_k = _tile_lanes(cos_ref[...], wk), _tile_lanes(sin_ref[...], wk)
            k = k_rinv * (k_raw * kg_ref[...] * cos_k + kvp[:, 2 * wk:] * kgp_ref[...] * sin_k)
        else:
            k = k_rinv * (k_raw * kg_ref[...])
        ok_ref[...] = k.astype(BF16)
        if need_q:
            qp = _dot(hm, w_ref[:, 3 * wk:])
            q_raw = qp[:, 0:wq]
            rinv = _seg_rinv(q_raw, GQA_HEAD_DIM) * (GQA_SCALE * LOG2E)
            if use_rope:
                cos_q, sin_q = _tile_lanes(cos_ref[...], wq), _tile_lanes(sin_ref[...], wq)
                q = rinv * (q_raw * qg_ref[...] * cos_q + qp[:, wq:] * qgp_ref[...] * sin_q)
            else:
                q = rinv * (q_raw * qg_ref[...])
            oq_ref[...] = q.astype(BF16)

    tile = lambda w: pl.BlockSpec((t, w), lambda i: (i, 0))
    in_specs = [tile(d), _full((1, d)), _mod_spec(st, SC1, d), _mod_spec(st, SH1, d),
                _resident(w_in.shape), _full(qg.shape), _full(qgp.shape), _full(kg.shape), _full(kgp.shape)]
    args = [x, gain, mods, mods, w_in, qg, qgp, kg, kgp]
    if use_rope:
        tps = st.tiles_per_seg
        in_specs += [pl.BlockSpec((t, LANES), lambda i: (i % tps, 0))] * 2
        args += list(rope)
    out_specs, out_shape = [], []
    if need_q:
        out_specs.append(tile(wq))
        out_shape.append(SDS((n, wq), BF16))
    out_specs += [tile(wk), tile(wk)]
    out_shape += [SDS((n, wk), BF16)] * 2
    outs = pl.pallas_call(
        kern, grid=(st.tiles,), in_specs=in_specs, out_specs=out_specs, out_shape=out_shape,
        name="gqa_proj", compiler_params=_params(("parallel",), 48))(*args)
    return tuple(outs) if need_q else (None,) + tuple(outs)


def _gqa_window_call(batch, length, lc, q, k, v, kc, vc, sink):
    nb = length // BLOCK
    hd = GQA_HEAD_DIM
    wq, wk = GQA_Q_HEADS * hd, GQA_KV_HEADS * hd
    rows = GQA_GROUP * BLOCK

    def kern(sink_ref, q_ref, kp_ref, k0_ref, kn_ref, vp_ref, v0_ref, vn_ref, kc_ref, vc_ref, o_ref):
        nblk = pl.program_id(1)
        r = lax.broadcasted_iota(jnp.int32, (rows, 3 * BLOCK), 0) & (BLOCK - 1)
        c = lax.broadcasted_iota(jnp.int32, (rows, 3 * BLOCK), 1)
        valid = (c >= r + BLOCK - WINDOW) & (c <= r + BLOCK + WINDOW)
        valid = valid & ((c >= BLOCK) | (nblk > 0)) & ((c < 2 * BLOCK) | (nblk < nb - 1))
        hrow = lax.broadcasted_iota(jnp.int32, (rows, 1), 0) // BLOCK
        for g in range(GQA_KV_HEADS):
            sl = slice(g * hd, (g + 1) * hd)
            qg = jnp.concatenate([q_ref[:, (g * GQA_GROUP + j) * hd:(g * GQA_GROUP + j + 1) * hd]
                                  for j in range(GQA_GROUP)], axis=0)
            kband = jnp.concatenate([kp_ref[:, sl], k0_ref[:, sl], kn_ref[:, sl]], axis=0)
            vband = jnp.concatenate([vp_ref[:, sl], v0_ref[:, sl], vn_ref[:, sl]], axis=0)
            s_c = _nt(qg, kc_ref[:, sl])
            s_b = jnp.where(valid, _nt(qg, kband), -1e30)
            snk = jnp.zeros((rows, 1), F32)
            for j in range(GQA_GROUP):
                snk = jnp.where(hrow == j, sink_ref[g * GQA_GROUP + j], snk)
            m = jnp.maximum(jnp.maximum(jnp.max(s_c, axis=-1, keepdims=True),
                                        jnp.max(s_b, axis=-1, keepdims=True)), snk)
            p_c = jnp.exp2(s_c - m)
            p_b = jnp.exp2(s_b - m)
            den = (jnp.sum(p_c, axis=-1, keepdims=True) + jnp.sum(p_b, axis=-1, keepdims=True)
                   + jnp.exp2(snk - m))
            o = (_dot(p_c.astype(BF16), vc_ref[:, sl]) + _dot(p_b.astype(BF16), vband)) / den
            o_ref[:, g * GQA_GROUP * hd:(g + 1) * GQA_GROUP * hd] = jnp.concatenate(
                [o[j * BLOCK:(j + 1) * BLOCK, :] for j in range(GQA_GROUP)], axis=-1).astype(BF16)

    blk = lambda f: pl.BlockSpec((BLOCK, wk), f)
    prev_ = lambda b, i: (b * nb + jnp.maximum(i - 1, 0), 0)
    cur_ = lambda b, i: (b * nb + i, 0)
    next_ = lambda b, i: (b * nb + jnp.minimum(i + 1, nb - 1), 0)
    ctx_spec = pl.BlockSpec((lc, wk), lambda b, i: (b, 0))
    return pl.pallas_call(
        kern, grid=(batch, nb),
        in_specs=[pl.BlockSpec(memory_space=pltpu.SMEM),
                  pl.BlockSpec((BLOCK, wq), cur_), blk(prev_), blk(cur_), blk(next_),
                  blk(prev_), blk(cur_), blk(next_), ctx_spec, ctx_spec],
        out_specs=pl.BlockSpec((BLOCK, wq), cur_),
        out_shape=SDS((batch * length, wq), BF16), name="gqa_window",
        compiler_params=_params(("parallel", "arbitrary"), 40))(sink, q, k, k, k, v, v, v, kc, vc)


def _gqa_window_fast_call(batch, length, lc, q, k, v, kc, vc, sink2, shift):
    nb = length // BLOCK
    hd = GQA_HEAD_DIM
    wq, wk = GQA_Q_HEADS * hd, GQA_KV_HEADS * hd
    rows = GQA_GROUP * BLOCK

    def kern(sink_ref, shift_ref, q_ref, kp_ref, k0_ref, kn_ref, vp_ref, v0_ref, vn_ref, kc_ref, vc_ref, o_ref):
        nblk = pl.program_id(1)
        r = lax.broadcasted_iota(jnp.int32, (rows, 3 * BLOCK), 0) & (BLOCK - 1)
        c = lax.broadcasted_iota(jnp.int32, (rows, 3 * BLOCK), 1)
        valid = (c >= r + BLOCK - WINDOW) & (c <= r + BLOCK + WINDOW)
        valid = valid & ((c >= BLOCK) | (nblk > 0)) & ((c < 2 * BLOCK) | (nblk < nb - 1))
        hrow = lax.broadcasted_iota(jnp.int32, (rows, 1), 0) // BLOCK
        lane0_q = lax.broadcasted_iota(jnp.int32, (BLOCK, hd), 1) == 0

        def with_one(x):
            lane0 = lax.broadcasted_iota(jnp.int32, (x.shape[0], hd), 1) == 0
            return jnp.concatenate([x, jnp.where(lane0, 1.0, 0.0).astype(BF16)], axis=1)

        for g in range(GQA_KV_HEADS):
            sl = slice(g * hd, (g + 1) * hd)
            heads = [g * GQA_GROUP + j for j in range(GQA_GROUP)]
            qg = jnp.concatenate(
                [jnp.concatenate([q_ref[:, h * hd:(h + 1) * hd],
                                  jnp.where(lane0_q, -shift_ref[h], 0.0).astype(BF16)], axis=1)
                 for h in heads], axis=0)
            kband = with_one(jnp.concatenate([kp_ref[:, sl], k0_ref[:, sl], kn_ref[:, sl]], axis=0))
            vband = with_one(jnp.concatenate([vp_ref[:, sl], v0_ref[:, sl], vn_ref[:, sl]], axis=0))
            p_c = jnp.exp2(_nt(qg, with_one(kc_ref[:, sl])).astype(BF16))
            p_b = jnp.exp2(jnp.where(valid, _nt(qg, kband), -1e30).astype(BF16))
            acc = _dot(p_c, with_one(vc_ref[:, sl])) + _dot(p_b, vband)
            snk = jnp.zeros((rows, 1), F32)
            for j, h in enumerate(heads):
                snk = jnp.where(hrow == j, sink_ref[h] - shift_ref[h], snk)
            o = acc[:, 0:hd] / (acc[:, hd:hd + 1] + jnp.exp2(snk))
            o_ref[:, g * GQA_GROUP * hd:(g + 1) * GQA_GROUP * hd] = jnp.concatenate(
                [o[j * BLOCK:(j + 1) * BLOCK, :] for j in range(GQA_GROUP)], axis=-1).astype(BF16)

    blk = lambda f: pl.BlockSpec((BLOCK, wk), f)
    prev_ = lambda b, i: (b * nb + jnp.maximum(i - 1, 0), 0)
    cur_ = lambda b, i: (b * nb + i, 0)
    next_ = lambda b, i: (b * nb + jnp.minimum(i + 1, nb - 1), 0)
    ctx_spec = pl.BlockSpec((lc, wk), lambda b, i: (b, 0))
    smem = pl.BlockSpec(memory_space=pltpu.SMEM)
    return pl.pallas_call(
        kern, grid=(batch, nb),
        in_specs=[smem, smem, pl.BlockSpec((BLOCK, wq), cur_), blk(prev_), blk(cur_), blk(next_),
                  blk(prev_), blk(cur_), blk(next_), ctx_spec, ctx_spec],
        out_specs=pl.BlockSpec((BLOCK, wq), cur_),
        out_shape=SDS((batch * length, wq), BF16), name="gqa_window_fast",
        compiler_params=_params(("parallel", "arbitrary"), 40))(sink2, shift, q, k, k, k, v, v, v, kc, vc)


def _gqa_layer(streams, xs, mods, gain, p, rope, ctx_next):
    assert not ctx_next, "the windowed-GQA mixer is only implemented as the last layer"
    st_c, st_x = streams
    xc, xx = xs
    w_in, q_gain, k_gain, sink, w_out = p
    wq, wk = GQA_Q_HEADS * GQA_HEAD_DIM, GQA_KV_HEADS * GQA_HEAD_DIM
    w_q, w_k, w_v = w_in[:, :wq], w_in[:, wq:wq + wk], w_in[:, wq + wk:]
    w_in_p = jnp.concatenate([w_k, w_v, w_k[:, _rope_partner_perm(GQA_KV_HEADS)],
                              w_q, w_q[:, _rope_partner_perm(GQA_Q_HEADS)]], axis=1).astype(BF16)
    par = _rope_partner_perm(1)
    wts = (w_in_p, jnp.tile(q_gain, GQA_Q_HEADS).reshape(1, -1), jnp.tile(q_gain[par], GQA_Q_HEADS).reshape(1, -1),
           jnp.tile(k_gain, GQA_KV_HEADS).reshape(1, -1), jnp.tile(k_gain[par], GQA_KV_HEADS).reshape(1, -1))
    _, kc, vc = _gqa_proj_call(st_c.with_tile(PROJ_TILE), xc, mods, gain, wts, None, False)
    q, k, v = _gqa_proj_call(st_x.with_tile(PROJ_TILE), xx, mods, gain, wts, rope, True)
    sink2 = sink * LOG2E
    bound = LOG2E * GQA_SCALE * GQA_HEAD_DIM * jnp.max(jnp.abs(q_gain)) * jnp.max(jnp.abs(k_gain))
    shift = jnp.maximum(bound * SHIFT_MARGIN + 1.0, sink2).astype(BF16).astype(F32)
    args = (st_x.batch, st_x.seg, st_c.seg, q, k, v, kc, vc, sink2)
    a = lax.cond(jnp.max(shift) <= MAX_FIXED_SHIFT,
                 lambda: _gqa_window_fast_call(*args, shift),
                 lambda: _gqa_window_call(*args))
    return [None, _outproj_call(st_x, a, mods, w_out.astype(BF16), "gqa_out")]


def _cast_experts_call(w, layer):
    _, ne, a, b = w.shape
    eb = 4

    def kern(w_ref, o_ref):
        o_ref[...] = w_ref[...].astype(BF16)

    return pl.pallas_call(
        kern, grid=(ne // eb,),
        in_specs=[pl.BlockSpec((None, eb, a, b), lambda i: (layer, i, 0, 0))],
        out_specs=pl.BlockSpec((eb, a, b), lambda i: (i, 0, 0)),
        out_shape=SDS((ne, a, b), BF16), name="cast_experts",
        compiler_params=_params(("parallel",), 32))(w)


def _route(logits_t, bias_col):
    scores = _sigmoid(logits_t)
    biased = scores + bias_col
    rows = [biased[e:e + 1, :] for e in range(N_EXPERTS)]
    srow = [scores[e:e + 1, :] for e in range(N_EXPERTS)]
    epg = EXPERTS_PER_GROUP
    gscore = []
    for g in range(N_EXPERT_GROUPS):
        v = rows[g * epg:(g + 1) * epg]
        pair = [v[a] + v[b] for a in range(epg) for b in range(a + 1, epg)]
        gscore.append(functools.reduce(jnp.maximum, pair))
    ind, wloc = [], [None] * epg
    for g in range(N_EXPERT_GROUPS):
        best = None
        for g2 in range(N_EXPERT_GROUPS):
            if g2 == g:
                continue
            cnd = gscore[g] > gscore[g2] if g2 < g else gscore[g] >= gscore[g2]
            best = cnd if best is None else best & cnd
        ind.append(jnp.where(best, 1.0, 0.0))
        for j in range(epg):
            e = g * epg + j
            rank = None
            for e2 in range(g * epg, (g + 1) * epg):
                if e2 == e:
                    continue
                ahead = rows[e2] >= rows[e] if e2 < e else rows[e2] > rows[e]
                one = jnp.where(ahead, 1.0, 0.0)
                rank = one if rank is None else rank + one
            w = jnp.where(best & (rank < 2.0), srow[e], 0.0)
            wloc[j] = w if wloc[j] is None else wloc[j] + w
    den = functools.reduce(lambda a, b: a + b, wloc)
    return ind, [w / den for w in wloc]


MOE_TILE = 1024
MOE_CHUNK = 128


def _moe_call(st, x_in, y_in, mods, gain, router_wt, router_bias, w_gate, w_up, w_down):
    n, d = x_in.shape
    t = min(MOE_TILE, n)
    r = MOE_CHUNK
    ne, _, ff = w_gate.shape
    epg, ng = EXPERTS_PER_GROUP, N_EXPERT_GROUPS
    nch = t // r + ng - 1
    tiles = n // t
    wd2 = w_down.reshape(ne * ff, d)

    def kern(x_ref, y_ref, g_ref, sc_ref, sh_ref, g2_ref, rw_ref, rb_ref, wg_ref, wu_ref, wd_ref, o_ref,
             hb_ref, oh_ref, xg_ref):
        hb_ref[...] = _modnorm(x_ref[...] + y_ref[...].astype(F32), g_ref[...], sc_ref[...], sh_ref[...]).astype(BF16)
        ind, wloc = _route(_nt(rw_ref[...], hb_ref[...]), rb_ref[...])
        ind8 = jnp.concatenate(ind + [jnp.zeros((8 - ng, t), F32)], axis=0)
        lane = lax.broadcasted_iota(jnp.int32, (8, t), 1)
        csum, step = ind8, 1
        while step < t:
            csum = csum + jnp.where(lane >= step, pltpu.roll(csum, step, axis=1), 0.0)
            step *= 2
        pos = csum - ind8
        count = [jnp.sum(ind[g]).astype(jnp.int32) for g in range(ng)]
        first = [jnp.int32(0)]
        for g in range(ng):
            first.append(first[g] + (count[g] + (r - 1)) // r)
        slot = functools.reduce(lambda a, b: a + b, [
            ind[g] * (pos[g:g + 1, :] + (first[g] * r).astype(F32)) for g in range(ng)])
        w_hi = [w.astype(BF16) for w in wloc]
        w_lo = [(w - h.astype(F32)).astype(BF16) for w, h in zip(wloc, w_hi)]
        wst = jnp.concatenate(w_hi + w_lo, axis=0)
        row_id = lax.broadcasted_iota(jnp.int32, (r, t), 0).astype(F32)
        onehot = lambda c: jnp.where(slot == row_id + float(c * r), 1.0, 0.0).astype(BF16)
        last = nch - 1
        for c in range(last):
            oh_ref[c * r:(c + 1) * r, :] = onehot(c)
        xg_ref[...] = _dot(oh_ref[...], hb_ref[...]).astype(BF16)
        wr_all = _nt(wst, oh_ref[...]).T

        def chunk_ffn(c, xg, wr):
            grp = functools.reduce(lambda a, b: a + b,
                                   [(c >= first[g]).astype(jnp.int32) for g in range(1, ng)])
            acts = []
            for j in range(epg):
                e = grp * epg + j
                gt = _dot(xg, wg_ref[e])
                up = _dot(xg, wu_ref[e])
                acts.append((gt * _sigmoid(gt) * up * (wr[:, j:j + 1] + wr[:, epg + j:epg + j + 1])).astype(BF16))
            wd_g = wd_ref[pl.ds(pl.multiple_of(grp * (epg * ff), epg * ff), epg * ff), :]
            return _dot(jnp.concatenate(acts, axis=1), wd_g).astype(BF16)

        for c in range(last):
            rows = slice(c * r, (c + 1) * r)

            def run(c=c, rows=rows):
                xg_ref[rows, :] = chunk_ffn(c, xg_ref[rows, :], wr_all[c * r:(c + 1) * r, :])

            if c < t // r:
                run()
            else:
                pl.when(c < first[ng])(run)

        tdot = lambda a, b: lax.dot_general(a, b, (((0,), (0,)), ((), ())), preferred_element_type=F32)
        o_ref[...] = x_ref[...] + y_ref[...].astype(F32) + g2_ref[...] * tdot(oh_ref[...], xg_ref[...])

        @pl.when(last < first[ng])
        def _():
            oh = onehot(last)
            out = chunk_ffn(last, _dot(oh, hb_ref[...]).astype(BF16), _nt(wst, oh).T)
            o_ref[...] += g2_ref[...] * tdot(oh, out)

    row = (lambda i: st.batch) if st.is_ctx else (lambda i: i // (st.seg // t))
    mod = lambda chunk: pl.BlockSpec((None, 1, d), lambda i: (row(i), 0, chunk))
    return pl.pallas_call(
        kern, grid=(tiles,),
        in_specs=[pl.BlockSpec((t, d), lambda i: (i, 0)),
                  pl.BlockSpec((t, d), lambda i: (i, 0)),
                  _full((1, d)), mod(SC2), mod(SH2), mod(G2), _full((ne, d)), _full((ne, 1)),
                  _resident((ne, d, ff)), _resident((ne, d, ff)), _resident((ne * ff, d))],
        out_specs=pl.BlockSpec((t, d), lambda i: (i, 0)),
        out_shape=SDS((n, d), F32),
        scratch_shapes=[pltpu.VMEM((t, d), BF16), pltpu.VMEM(((nch - 1) * r, t), BF16),
                        pltpu.VMEM(((nch - 1) * r, d), BF16)],
        name="moe", compiler_params=_params(("parallel",), 58))(
            x_in, y_in, gain, mods, mods, mods, router_wt, router_bias, w_gate, w_up, wd2)


def kernel(x, c, ctx, c_ctx, w_ada, b_ada, norm_mix, norm_ffn, fourier_w_out, conv_w_in, conv_w, conv_w_out,
           mla_w_in, mla_q_norm, mla_kv_norm, mla_w_uq, mla_w_ukv, mla_q_gain, mla_k_gain, mla_w_out,
           gqa_w_in, gqa_q_gain, gqa_k_gain, gqa_sink, gqa_w_out, router_w, router_bias,
           moe_w_gate, moe_w_up, moe_w_down):
    b, l, d = x.shape
    lc = ctx.shape[1]
    depth = w_ada.shape[0]
    st_c = Stream(b * lc, lc, min(b * lc, TOKEN_TILE), b, True)
    st_x = Stream(b * l, l, min(l, TOKEN_TILE), b, False)
    streams = [st_c, st_x]
    r8 = -(-(b + 1) // 8) * 8
    cvec = jnp.concatenate([c, c_ctx[None, :], jnp.zeros((r8 - b - 1, d), F32)], axis=0)
    mods_all = _ada_call(cvec, w_ada, b_ada).reshape(depth, r8, 1, 6 * d)
    rope = _rope_tables(l)
    router_wt = router_w.T.astype(BF16)
    router_b = router_bias.reshape(-1, 1)
    xs = [ctx.reshape(b * lc, d), x.reshape(b * l, d)]
    for i in range(depth):
        kind, j = i % 4, i // 4
        ctx_next = i < depth - 1
        mods = mods_all[i]
        gain = norm_mix[i].reshape(1, d)
        if not (ctx_next or kind >= 2):
            xs[0] = None
        if kind == 0:
            ys = _fourier_layer(streams, xs, mods, gain, fourier_w_out[j])
        elif kind == 1:
            ys = _conv_layer(streams, xs, mods, gain, conv_w_in[j], conv_w[j], conv_w_out[j])
        elif kind == 2:
            ys = _mla_layer(streams, xs, mods, gain,
                            (mla_w_in[j], mla_q_norm[j], mla_kv_norm[j], mla_w_uq[j], mla_w_ukv[j],
                             mla_q_gain[j], mla_k_gain[j], mla_w_out[j]), rope, ctx_next)
        else:
            ys = _gqa_layer(streams, xs, mods, gain,
                            (gqa_w_in[j], gqa_q_gain[j], gqa_k_gain[j], gqa_sink[j], gqa_w_out[j]),
                            rope, ctx_next)
        if not ctx_next:
            ys[0] = None
        gain2 = norm_ffn[i].reshape(1, d)
        wg, wu, wd = (_cast_experts_call(w, i) for w in (moe_w_gate, moe_w_up, moe_w_down))
        xs = [None if y is None else _moe_call(st, x, y, mods, gain2, router_wt, router_b, wg, wu, wd)
              for st, x, y in zip(streams, xs, ys)]
    return xs[1].reshape(b, l, d)
```

```python
import functools

import numpy as np
import jax
import jax.numpy as jnp
from jax import lax
from jax.experimental import pallas as pl
from jax.experimental.pallas import tpu as pltpu

F32, BF16 = jnp.float32, jnp.bfloat16
SDS = jax.ShapeDtypeStruct

EPS = 1e-6
GRID_W = 64
ROPE_THETA = 10000.0
N_FOURIER_GROUPS = 4
MLA_HEADS, MLA_Q_RANK, MLA_KV_RANK = 16, 256, 128
MLA_NOPE, MLA_ROPE, MLA_V = 128, 64, 128
MLA_SCALE = (MLA_NOPE + MLA_ROPE) ** -0.5
GQA_Q_HEADS, GQA_KV_HEADS, GQA_HEAD_DIM = 16, 4, 64
GQA_GROUP = GQA_Q_HEADS // GQA_KV_HEADS
GQA_SCALE = GQA_HEAD_DIM ** -0.5
WINDOW = 128
BLOCK = 128
N_EXPERTS, N_EXPERT_GROUPS, EXPERT_FF = 16, 4, 256
EXPERTS_PER_GROUP = N_EXPERTS // N_EXPERT_GROUPS
ROPE_DIM = 64
LOG2E = 1.4426950408889634
MAX_FIXED_SHIFT = 50.0
SHIFT_MARGIN = 1.02
TOKEN_TILE = 1024
PROJ_TILE = 512

V7X_VMEM_BYTES = 64 * 1024 * 1024
LANES = 128
SH1, SC1, G1, SH2, SC2, G2 = range(6)


def _params(sem, vmem_mb):
    return pltpu.CompilerParams(dimension_semantics=sem, vmem_limit_bytes=vmem_mb * 1024 * 1024)


def _sigmoid(v):
    return 1.0 / (1.0 + jnp.exp(-v))


def _modnorm(x, gain, sc, sh):
    ms = jnp.mean(x * x, axis=-1, keepdims=True)
    return x * lax.rsqrt(ms + EPS) * (gain * (1.0 + sc)) + sh


def _nt(a, b):
    return lax.dot_general(a, b, (((1,), (1,)), ((), ())), preferred_element_type=F32)


def _dot(a, b):
    return jnp.dot(a, b, preferred_element_type=F32)


class Stream:
    def __init__(self, n, seg, tile, batch, is_ctx):
        self.n, self.seg, self.tile, self.batch, self.is_ctx = n, seg, tile, batch, is_ctx
        self.tiles = n // tile
        self.tiles_per_seg = seg // tile

    def mod_row(self, t):
        return self.batch if self.is_ctx else t // self.tiles_per_seg

    def seg_row(self, b):
        return self.batch if self.is_ctx else b

    def with_tile(self, tile):
        return Stream(self.n, self.seg, min(tile, self.tile), self.batch, self.is_ctx)


def _mod_spec(st, chunk, d):
    return pl.BlockSpec((None, 1, d), lambda t: (st.mod_row(t), 0, chunk))


def _full(shape):
    nd = len(shape)
    return pl.BlockSpec(shape, lambda *_: (0,) * nd)


def _resident(shape):
    nd = len(shape)
    return pl.BlockSpec(shape, lambda *_: (0,) * nd, pipeline_mode=pl.Buffered(1))


def _ada_call(cvec, w_ada, b_ada):
    depth, d, d6 = w_ada.shape
    r8 = cvec.shape[0]
    tn = d6 // 4

    def kern(c_ref, w_ref, b_ref, o_ref):
        c = c_ref[...]
        s = (c * _sigmoid(c)).astype(BF16)
        o_ref[...] = _dot(s, w_ref[...].astype(BF16)) + b_ref[...]

    return pl.pallas_call(
        kern, grid=(depth, d6 // tn),
        in_specs=[pl.BlockSpec((r8, d), lambda i, j: (0, 0)),
                  pl.BlockSpec((None, d, tn), lambda i, j: (i, 0, j)),
                  pl.BlockSpec((None, 1, tn), lambda i, j: (i, 0, j))],
        out_specs=pl.BlockSpec((None, r8, tn), lambda i, j: (i, 0, j)),
        out_shape=SDS((depth, r8, d6), F32), name="ada",
        compiler_params=_params(("arbitrary", "arbitrary"), 40))(cvec, w_ada, b_ada.reshape(depth, 1, d6))


def _outproj_call(st, a, mods, w, name):
    n, k = a.shape
    d = w.shape[1]
    t = st.tile

    def kern(a_ref, g1_ref, w_ref, o_ref):
        o_ref[...] = (g1_ref[...] * _dot(a_ref[...], w_ref[...])).astype(BF16)

    return pl.pallas_call(
        kern, grid=(st.tiles,),
        in_specs=[pl.BlockSpec((t, k), lambda i: (i, 0)), _mod_spec(st, G1, d), _resident((k, d))],
        out_specs=pl.BlockSpec((t, d), lambda i: (i, 0)),
        out_shape=SDS((n, d), BF16), name=name,
        compiler_params=_params(("parallel",), 40))(a, mods, w)


def _dft_tables(length, radix):
    lr = length // radix
    m = np.arange(lr)[None, :, None]
    j = np.arange(radix)[:, None, None]
    nn = np.arange(lr)[None, None, :]
    ang = 2.0 * np.pi * (((radix * m + j) * nn) % length) / length
    e = np.concatenate([np.cos(ang), np.sin(ang)], axis=-1) / np.sqrt(length)
    return jnp.asarray(e, dtype=F32).astype(BF16)


def _group_dft_tables(group):
    k = np.arange(group)
    ang = 2.0 * np.pi * ((k[:, None] * k[None, :]) % group) / group
    return (jnp.asarray(np.cos(ang) / np.sqrt(group), dtype=F32),
            jnp.asarray(np.sin(ang) / np.sqrt(group), dtype=F32))


def _fourier_weight_call(w_out):
    d = w_out.shape[0]
    grp = d // N_FOURIER_GROUPS
    cg, sg = _group_dft_tables(grp)

    def kern(cg_ref, sg_ref, w_ref, o_ref):
        w = w_ref[...]
        o_ref[:, :d] = jnp.dot(cg_ref[...], w, preferred_element_type=F32,
                               precision=lax.Precision.HIGHEST).astype(BF16)
        o_ref[:, d:] = jnp.dot(sg_ref[...], w, preferred_element_type=F32,
                               precision=lax.Precision.HIGHEST).astype(BF16)

    return pl.pallas_call(
        kern, grid=(N_FOURIER_GROUPS,),
        in_specs=[_full((grp, grp)), _full((grp, grp)), pl.BlockSpec((grp, d), lambda g: (g, 0))],
        out_specs=pl.BlockSpec((grp, 2 * d), lambda g: (g, 0)),
        out_shape=SDS((d, 2 * d), BF16), name="fourier_w",
        compiler_params=_params(("arbitrary",), 32))(cg, sg, w_out)


def _radix_terms(radix, j):
    real, imag = [], []
    for q in range(radix):
        k = (j * q * (4 // radix)) % 4 if radix > 1 else 0
        if k == 0:
            real.append((1, 0, q)); imag.append((-1, 1, q))
        elif k == 1:
            real.append((-1, 1, q)); imag.append((-1, 0, q))
        elif k == 2:
            real.append((-1, 0, q)); imag.append((1, 1, q))
        else:
            real.append((1, 1, q)); imag.append((1, 0, q))
    return real, imag


def _fourier_mix_call(st, x, mods, gain, wcs, radix):
    n, d = x.shape
    seg = st.seg
    lr = seg // radix
    e = _dft_tables(seg, radix)
    nb = n // seg
    rt = min(seg, 512)
    cblk = 2 * LANES

    def kern(x_ref, g_ref, sc_ref, sh_ref, g1_ref, w_ref, e_ref, o_ref, p_ref, v_ref, *z_refs):
        for i in range(seg // rt):
            rows = slice(i * rt, (i + 1) * rt)
            h = _modnorm(x_ref[rows, :], g_ref[...], sc_ref[...], sh_ref[...]).astype(BF16)
            p_ref[rows, :] = _dot(h, w_ref[...]).astype(BF16)
        for c in range(d // cblk):
            cols = (slice(c * cblk, (c + 1) * cblk), slice(d + c * cblk, d + (c + 1) * cblk))
            for j in range(radix):
                real, imag = _radix_terms(radix, j)

                def comb(terms):
                    acc = None
                    for sgn, part, q in terms:
                        v = p_ref[q * lr:(q + 1) * lr, cols[part]].astype(F32)
                        if acc is None:
                            acc = v if sgn > 0 else -v
                        else:
                            acc = acc + v if sgn > 0 else acc - v
                    return acc

                v_ref[0:lr, :] = comb(real).astype(BF16)
                v_ref[lr:2 * lr, :] = comb(imag).astype(BF16)
                z = _dot(e_ref[j], v_ref[...])
                for k, z_ref in enumerate(z_refs):
                    z_ref[pl.ds(j, lr, stride=radix), :] = z[:, k * LANES:(k + 1) * LANES]
            for k, z_ref in enumerate(z_refs):
                sl = slice(c * cblk + k * LANES, c * cblk + (k + 1) * LANES)
                o_ref[:, sl] = (g1_ref[:, sl] * z_ref[...]).astype(BF16)

    mod = lambda chunk: pl.BlockSpec((None, 1, d), lambda b: (st.seg_row(b), 0, chunk))
    return pl.pallas_call(
        kern, grid=(nb,),
        in_specs=[pl.BlockSpec((seg, d), lambda b: (b, 0)), _full((1, d)), mod(SC1), mod(SH1), mod(G1),
                  _resident((d, 2 * d)), _resident((radix, lr, 2 * lr))],
        out_specs=pl.BlockSpec((seg, d), lambda b: (b, 0)),
        out_shape=SDS((n, d), BF16),
        scratch_shapes=[pltpu.VMEM((seg, 2 * d), BF16), pltpu.VMEM((2 * lr, cblk), BF16)]
        + [pltpu.VMEM((seg, LANES), F32)] * (cblk // LANES),
        name=f"fourier_mix_r{radix}",
        compiler_params=_params(("parallel",), 56))(x, gain, mods, mods, mods, wcs, e)


def _fourier_layer(streams, xs, mods, gain, w_out):
    wcs = _fourier_weight_call(w_out)
    return [None if x is None else _fourier_mix_call(st, x, mods, gain, wcs, 4 if st.seg >= 1024 else 1)
            for st, x in zip(streams, xs)]


def _conv_mix_call(st, x, mods, gain, w_in, conv_w, w_out):
    n, d = x.shape
    seg = st.seg
    nb = n // seg
    rt = min(seg, 512)
    nrt = seg // rt
    halo = 16

    def kern(x_ref, g_ref, sc_ref, sh_ref, g1_ref, win_ref, cw_ref, wout_ref, o_ref, bg_ref, u_ref):
        for i in range(nrt):
            rows = slice(i * rt, (i + 1) * rt)
            h = _modnorm(x_ref[rows, :], g_ref[...], sc_ref[...], sh_ref[...]).astype(BF16)
            bg_ref[rows, :] = _dot(h, win_ref[:, 0:d]).astype(BF16)
            u_ref[rows, :] = (_dot(h, win_ref[:, d:2 * d]) * _dot(h, win_ref[:, 2 * d:3 * d])).astype(BF16)
        row = lax.broadcasted_iota(jnp.int32, (rt, 1), 0)
        zero_row = jnp.zeros((1, d), F32)
        for i in range(nrt):
            rows = slice(i * rt, (i + 1) * rt)
            u = u_ref[rows, :].astype(F32)
            prev_row = zero_row if i == 0 else u_ref[i * rt - halo:i * rt, :].astype(F32)[halo - 1:halo, :]
            next_row = (zero_row if i == nrt - 1
                        else u_ref[(i + 1) * rt:(i + 1) * rt + halo, :].astype(F32)[0:1, :])
            um = jnp.where(row == 0, prev_row, pltpu.roll(u, 1, axis=0))
            up = jnp.where(row == rt - 1, next_row, pltpu.roll(u, rt - 1, axis=0))
            z = cw_ref[0:1, :] * um + cw_ref[1:2, :] * u + cw_ref[2:3, :] * up
            a = (bg_ref[rows, :].astype(F32) * z).astype(BF16)
            o_ref[rows, :] = (g1_ref[...] * _dot(a, wout_ref[...])).astype(BF16)

    mod = lambda chunk: pl.BlockSpec((None, 1, d), lambda b: (st.seg_row(b), 0, chunk))
    seg_spec = pl.BlockSpec((seg, d), lambda b: (b, 0))
    return pl.pallas_call(
        kern, grid=(nb,),
        in_specs=[seg_spec, _full((1, d)), mod(SC1), mod(SH1), mod(G1),
                  _resident((d, 3 * d)), _full((3, d)), _resident((d, d))],
        out_specs=seg_spec, out_shape=SDS((n, d), BF16),
        scratch_shapes=[pltpu.VMEM((seg, d), BF16), pltpu.VMEM((seg, d), BF16)],
        name="conv_mix", compiler_params=_params(("parallel",), 56))(
            x, gain, mods, mods, mods, w_in, conv_w, w_out)


def _conv_layer(streams, xs, mods, gain, w_in, conv_w, w_out):
    w_in_b, w_out_b = w_in.astype(BF16), w_out.astype(BF16)
    return [None if x is None else _conv_mix_call(st, x, mods, gain, w_in_b, conv_w, w_out_b)
            for st, x in zip(streams, xs)]


def _rope_tables(length):
    q = ROPE_DIM // 4
    pos = np.arange(length)
    rc = np.stack([pos // GRID_W, pos % GRID_W], axis=1).astype(np.float32)
    inv = (ROPE_THETA ** (-np.arange(q, dtype=np.float32) / q)).astype(np.float32)
    lane = np.arange(ROPE_DIM)
    ang = rc[:, lane >> 5] * inv[lane & (q - 1)][None, :]
    sign = np.where((lane & q) == 0, -1.0, 1.0)[None, :]
    cos = np.cos(ang.astype(np.float32)).astype(np.float32)
    sin = (np.sin(ang.astype(np.float32)) * sign).astype(np.float32)
    reps = LANES // ROPE_DIM
    return jnp.asarray(np.tile(cos, (1, reps))), jnp.asarray(np.tile(sin, (1, reps)))


def _rope_partner_perm(n_heads):
    lane = np.arange(n_heads * ROPE_DIM)
    return lane ^ (ROPE_DIM // 4)


def _seg_rinv(x, seg):
    t, w = x.shape
    cols = []
    for c in range(w // LANES):
        blk = x[:, c * LANES:(c + 1) * LANES]
        sq = blk * blk
        if seg == LANES:
            cols.append(jnp.broadcast_to(
                lax.rsqrt(jnp.mean(sq, axis=-1, keepdims=True) + EPS), (t, LANES)))
        else:
            lane = lax.broadcasted_iota(jnp.int32, (t, LANES), 1)
            low = lane < seg
            lo = jnp.sum(jnp.where(low, sq, 0.0), axis=-1, keepdims=True) * (1.0 / seg)
            hi = jnp.sum(jnp.where(low, 0.0, sq), axis=-1, keepdims=True) * (1.0 / seg)
            cols.append(jnp.where(low, lax.rsqrt(lo + EPS), lax.rsqrt(hi + EPS)))
    return cols[0] if len(cols) == 1 else jnp.concatenate(cols, axis=-1)


def _tile_lanes(v, w):
    reps = w // LANES
    return v if reps == 1 else jnp.concatenate([v] * reps, axis=-1)


def _mla_proj_call(st, x, mods, gain, wts, rope, need_q):
    n, d = x.shape
    t = st.tile
    h = MLA_HEADS
    wn, wr = h * MLA_NOPE, h * MLA_ROPE
    (w_in, q_norm, kv_norm, w_uq, w_ukv, qg_n, qg_r, qg_rp, kg_n, kg_r, kg_rp) = wts
    use_rope = rope is not None

    def kern(*refs):
        refs = list(refs)
        x_ref, g_ref, sc_ref, sh_ref, win_ref, qn_ref_, kvn_ref_, wuq_ref, wukv_ref = refs[:9]
        qgn_ref, qgr_ref, qgrp_ref, kgn_ref, kgr_ref, kgrp_ref = refs[9:15]
        rest = refs[15:]
        if use_rope:
            cos_ref, sin_ref = rest[:2]
            rest = rest[2:]
        if need_q:
            oqn_ref, oqr_ref = rest[:2]
            rest = rest[2:]
        okn_ref, okr_ref, ov_ref = rest
        hm = _modnorm(x_ref[...], g_ref[...], sc_ref[...], sh_ref[...]).astype(BF16)
        ck = _dot(hm, win_ref[...])
        kr2 = ck[:, MLA_Q_RANK + MLA_KV_RANK:]
        kr_rinv = _seg_rinv(kr2, MLA_ROPE)[:, 0:MLA_ROPE]
        kr_raw = kr2[:, 0:MLA_ROPE] * kgr_ref[...]
        if use_rope:
            kr_par = kr2[:, MLA_ROPE:] * kgrp_ref[...]
            kr = kr_rinv * (kr_raw * cos_ref[:, 0:MLA_ROPE] + kr_par * sin_ref[:, 0:MLA_ROPE])
        else:
            kr = kr_rinv * kr_raw
        okr_ref[...] = kr.astype(BF16)
        ckv = ck[:, MLA_Q_RANK:MLA_Q_RANK + MLA_KV_RANK]
        ckv = (ckv * _seg_rinv(ckv, LANES) * kvn_ref_[...]).astype(BF16)
        kv = _dot(ckv, wukv_ref[...])
        kn = kv[:, 0:wn]
        okn_ref[...] = (kn * _seg_rinv(kn, MLA_NOPE) * kgn_ref[...]).astype(BF16)
        ov_ref[...] = kv[:, wn:].astype(BF16)
        if need_q:
            cq = ck[:, 0:MLA_Q_RANK]
            rq = lax.rsqrt(jnp.mean(cq * cq, axis=-1, keepdims=True) + EPS)
            cq = (cq * rq * qn_ref_[...]).astype(BF16)
            q = _dot(cq, wuq_ref[...])
            qn = q[:, 0:wn]
            oqn_ref[...] = (qn * _seg_rinv(qn, MLA_NOPE) * (qgn_ref[...] * (MLA_SCALE * LOG2E))).astype(BF16)
            qr_raw = q[:, wn:wn + wr]
            rinv = _seg_rinv(qr_raw, MLA_ROPE) * (MLA_SCALE * LOG2E)
            if use_rope:
                qr_par = q[:, wn + wr:]
                cos = _tile_lanes(cos_ref[...], wr)
                sin = _tile_lanes(sin_ref[...], wr)
                qr = rinv * (qr_raw * qgr_ref[...] * cos + qr_par * qgrp_ref[...] * sin)
            else:
                qr = rinv * (qr_raw * qgr_ref[...])
            oqr_ref[...] = qr.astype(BF16)

    tile = lambda w: pl.BlockSpec((t, w), lambda i: (i, 0))
    in_specs = [tile(d), _full((1, d)), _mod_spec(st, SC1, d), _mod_spec(st, SH1, d),
                _resident(w_in.shape), _full(q_norm.shape), _full(kv_norm.shape),
                _resident(w_uq.shape), _resident(w_ukv.shape),
                _full(qg_n.shape), _full(qg_r.shape), _full(qg_rp.shape),
                _full(kg_n.shape), _full(kg_r.shape), _full(kg_rp.shape)]
    args = [x, gain, mods, mods, w_in, q_norm, kv_norm, w_uq, w_ukv, qg_n, qg_r, qg_rp, kg_n, kg_r, kg_rp]
    if use_rope:
        tps = st.tiles_per_seg
        in_specs += [pl.BlockSpec((t, LANES), lambda i: (i % tps, 0))] * 2
        args += list(rope)
    out_specs, out_shape = [], []
    if need_q:
        out_specs += [tile(wn), tile(wr)]
        out_shape += [SDS((n, wn), BF16), SDS((n, wr), BF16)]
    out_specs += [tile(wn), tile(MLA_ROPE), tile(h * MLA_V)]
    out_shape += [SDS((n, wn), BF16), SDS((n, MLA_ROPE), BF16), SDS((n, h * MLA_V), BF16)]
    outs = pl.pallas_call(
        kern, grid=(st.tiles,), in_specs=in_specs, out_specs=out_specs, out_shape=out_shape,
        name="mla_proj", compiler_params=_params(("parallel",), 48))(*args)
    if need_q:
        return tuple(outs)
    return (None, None) + tuple(outs)


def _mla_attn_fast_call(batch, lq, tq, q, kv_sets, shift):
    qn, qr = q
    lk_total = sum(s[3] for s in kv_sets)
    hp = 2 if lk_total > 1024 else 8
    n_hp = MLA_HEADS // hp
    nq = lq // tq
    nsets = len(kv_sets)
    lengths = [s[3] for s in kv_sets]
    lk = sum(lengths)
    kw = 2 * LANES
    pad = kw - MLA_NOPE - MLA_ROPE

    def kern(*refs):
        sh_ref, qn_ref, qr_ref = refs[:3]
        sets = [refs[3 + 3 * s:6 + 3 * s] for s in range(nsets)]
        o_ref, kq_ref, vq_ref = refs[-3:]

        @pl.when(pl.program_id(2) == 0)
        def _():
            lane_k = lax.broadcasted_iota(jnp.int32, (lk, pad), 1)
            lane_v = lax.broadcasted_iota(jnp.int32, (lk, kw - MLA_V), 1)
            for hh in range(hp):
                off = 0
                for (kn_ref, kr_ref, v_ref), length in zip(sets, lengths):
                    kq_ref[hh, off:off + length, 0:MLA_NOPE] = kn_ref[:, hh * MLA_NOPE:(hh + 1) * MLA_NOPE]
                    kq_ref[hh, off:off + length, MLA_NOPE:MLA_NOPE + MLA_ROPE] = kr_ref[...]
                    vq_ref[hh, off:off + length, 0:MLA_V] = v_ref[:, hh * MLA_V:(hh + 1) * MLA_V]
                    off += length
                kq_ref[hh, :, MLA_NOPE + MLA_ROPE:] = jnp.where(lane_k == 0, -sh_ref[:, 0:pad], 0.0).astype(BF16)
                vq_ref[hh, :, MLA_V:] = jnp.where(lane_v == 0, 1.0, 0.0).astype(BF16)

        lane_q = lax.broadcasted_iota(jnp.int32, (tq, pad), 1)
        one = jnp.where(lane_q == 0, 1.0, 0.0).astype(BF16)
        for hh in range(hp):
            qc = jnp.concatenate([qn_ref[:, hh * MLA_NOPE:(hh + 1) * MLA_NOPE],
                                  qr_ref[:, hh * MLA_ROPE:(hh + 1) * MLA_ROPE], one], axis=1)
            p = jnp.exp2(_nt(qc, kq_ref[hh]).astype(BF16))
            acc = _dot(p, vq_ref[hh])
            o_ref[:, hh * MLA_V:(hh + 1) * MLA_V] = (acc[:, 0:MLA_V] / acc[:, MLA_V:MLA_V + 1]).astype(BF16)

    in_specs = [_full((1, LANES)),
                pl.BlockSpec((tq, hp * MLA_NOPE), lambda b, h, i: (b * nq + i, h)),
                pl.BlockSpec((tq, hp * MLA_ROPE), lambda b, h, i: (b * nq + i, h))]
    args = [shift, qn, qr]
    for kn, kr, v, length in kv_sets:
        in_specs += [pl.BlockSpec((length, hp * MLA_NOPE), lambda b, h, i: (b, h)),
                     pl.BlockSpec((length, MLA_ROPE), lambda b, h, i: (b, 0)),
                     pl.BlockSpec((length, hp * MLA_V), lambda b, h, i: (b, h))]
        args += [kn, kr, v]
    return pl.pallas_call(
        kern, grid=(batch, n_hp, nq), in_specs=in_specs,
        out_specs=pl.BlockSpec((tq, hp * MLA_V), lambda b, h, i: (b * nq + i, h)),
        out_shape=SDS((batch * lq, MLA_HEADS * MLA_V), BF16),
        scratch_shapes=[pltpu.VMEM((hp, lk, kw), BF16), pltpu.VMEM((hp, lk, kw), BF16)],
        name="mla_attn_fast",
        compiler_params=_params(("parallel", "parallel", "arbitrary"), 56))(*args)


def _mla_attn_call(batch, lq, tq, q, kv_sets):
    qn, qr = q
    hp = 2
    n_hp = MLA_HEADS // hp
    nq = lq // tq
    nsets = len(kv_sets)

    def kern(*refs):
        qn_ref, qr_ref = refs[:2]
        o_ref = refs[-1]
        sets = [refs[2 + 3 * s:5 + 3 * s] for s in range(nsets)]
        for hh in range(hp):
            qn_h = qn_ref[:, hh * MLA_NOPE:(hh + 1) * MLA_NOPE]
            qr_h = qr_ref[:, hh * MLA_ROPE:(hh + 1) * MLA_ROPE]
            scores = [_nt(qn_h, kn_ref[:, hh * MLA_NOPE:(hh + 1) * MLA_NOPE]) + _nt(qr_h, kr_ref[...])
                      for kn_ref, kr_ref, _ in sets]
            m = functools.reduce(jnp.maximum, [jnp.max(s, axis=-1, keepdims=True) for s in scores])
            ps = [jnp.exp2(s - m) for s in scores]
            den = functools.reduce(lambda a, b: a + b, [jnp.sum(p, axis=-1, keepdims=True) for p in ps])
            acc = functools.reduce(lambda a, b: a + b, [
                _dot(p.astype(BF16), v_ref[:, hh * MLA_V:(hh + 1) * MLA_V])
                for p, (_, _, v_ref) in zip(ps, sets)])
            o_ref[:, hh * MLA_V:(hh + 1) * MLA_V] = (acc / den).astype(BF16)

    in_specs = [pl.BlockSpec((tq, hp * MLA_NOPE), lambda b, h, i: (b * nq + i, h)),
                pl.BlockSpec((tq, hp * MLA_ROPE), lambda b, h, i: (b * nq + i, h))]
    args = [qn, qr]
    for kn, kr, v, length in kv_sets:
        in_specs += [pl.BlockSpec((length, hp * MLA_NOPE), lambda b, h, i: (b, h)),
                     pl.BlockSpec((length, MLA_ROPE), lambda b, h, i: (b, 0)),
                     pl.BlockSpec((length, hp * MLA_V), lambda b, h, i: (b, h))]
        args += [kn, kr, v]
    return pl.pallas_call(
        kern, grid=(batch, n_hp, nq), in_specs=in_specs,
        out_specs=pl.BlockSpec((tq, hp * MLA_V), lambda b, h, i: (b * nq + i, h)),
        out_shape=SDS((batch * lq, MLA_HEADS * MLA_V), BF16), name="mla_attn",
        compiler_params=_params(("parallel", "parallel", "arbitrary"), 56))(*args)


def _mla_layer(streams, xs, mods, gain, p, rope, ctx_next):
    st_c, st_x = streams
    xc, xx = xs
    h = MLA_HEADS
    w_in, q_norm, kv_norm, w_uq, w_ukv, q_gain, k_gain, w_out = p
    par = _rope_partner_perm(1)
    kr_cols = w_in[:, MLA_Q_RANK + MLA_KV_RANK:]
    w_in_p = jnp.concatenate([w_in, kr_cols[:, par]], axis=1).astype(BF16)
    wq = w_uq.reshape(MLA_Q_RANK, h, MLA_NOPE + MLA_ROPE)
    wq_n = wq[:, :, :MLA_NOPE].reshape(MLA_Q_RANK, h * MLA_NOPE)
    wq_r = wq[:, :, MLA_NOPE:]
    w_uq_p = jnp.concatenate([wq_n, wq_r.reshape(MLA_Q_RANK, h * MLA_ROPE),
                              wq_r[:, :, par].reshape(MLA_Q_RANK, h * MLA_ROPE)], axis=1).astype(BF16)
    wkv = w_ukv.reshape(MLA_KV_RANK, h, MLA_NOPE + MLA_V)
    w_ukv_p = jnp.concatenate([wkv[:, :, :MLA_NOPE].reshape(MLA_KV_RANK, h * MLA_NOPE),
                               wkv[:, :, MLA_NOPE:].reshape(MLA_KV_RANK, h * MLA_V)], axis=1).astype(BF16)
    qg_r = q_gain[MLA_NOPE:]
    kg_r = k_gain[MLA_NOPE:]
    wts = (w_in_p, q_norm.reshape(1, -1), kv_norm.reshape(1, -1), w_uq_p, w_ukv_p,
           jnp.tile(q_gain[:MLA_NOPE], h).reshape(1, -1), jnp.tile(qg_r, h).reshape(1, -1),
           jnp.tile(qg_r[par], h).reshape(1, -1), jnp.tile(k_gain[:MLA_NOPE], h).reshape(1, -1),
           kg_r.reshape(1, -1), kg_r[par].reshape(1, -1))
    w_out_b = w_out.astype(BF16)
    qn_c, qr_c, kn_c, kr_c, v_c = _mla_proj_call(st_c.with_tile(PROJ_TILE), xc, mods, gain, wts, None, ctx_next)
    qn_x, qr_x, kn_x, kr_x, v_x = _mla_proj_call(st_x.with_tile(PROJ_TILE), xx, mods, gain, wts, rope, True)
    b = st_x.batch
    amax = lambda g: jnp.max(jnp.abs(g))
    bound = LOG2E * MLA_SCALE * (MLA_NOPE * amax(q_gain[:MLA_NOPE]) * amax(k_gain[:MLA_NOPE])
                                 + MLA_ROPE * amax(qg_r) * amax(kg_r))
    shift = bound * SHIFT_MARGIN + 1.0
    shift_row = jnp.full((1, LANES), shift, F32)

    def attend(lq, tq, q, kv_sets):
        return lax.cond(shift <= MAX_FIXED_SHIFT,
                        lambda: _mla_attn_fast_call(b, lq, tq, q, kv_sets, shift_row),
                        lambda: _mla_attn_call(b, lq, min(tq, 512), q, kv_sets))

    a_x = attend(st_x.seg, min(st_x.seg, 2048), (qn_x, qr_x), [(kn_c, kr_c, v_c, st_c.seg), (kn_x, kr_x, v_x, st_x.seg)])
    out_x = _outproj_call(st_x, a_x, mods, w_out_b, "mla_out")
    out_c = None
    if ctx_next:
        a_c = attend(st_c.seg, st_c.seg, (qn_c, qr_c), [(kn_c, kr_c, v_c, st_c.seg)])
        out_c = _outproj_call(st_c, a_c, mods, w_out_b, "mla_out")
    return [out_c, out_x]


def _gqa_proj_call(st, x, mods, gain, wts, rope, need_q):
    n, d = x.shape
    t = st.tile
    wq, wk = GQA_Q_HEADS * GQA_HEAD_DIM, GQA_KV_HEADS * GQA_HEAD_DIM
    w_in, qg, qgp, kg, kgp = wts
    use_rope = rope is not None

    def kern(*refs):
        refs = list(refs)
        x_ref, g_ref, sc_ref, sh_ref, w_ref, qg_ref, qgp_ref, kg_ref, kgp_ref = refs[:9]
        rest = refs[9:]
        if use_rope:
            cos_ref, sin_ref = rest[:2]
            rest = rest[2:]
        if need_q:
            oq_ref = rest[0]
            rest = rest[1:]
        ok_ref, ov_ref = rest
        hm = _modnorm(x_ref[...], g_ref[...], sc_ref[...], sh_ref[...]).astype(BF16)
        kvp = _dot(hm, w_ref[:, 0:3 * wk])
        k_raw = kvp[:, 0:wk]
        k_rinv = _seg_rinv(k_raw, GQA_HEAD_DIM)
        ov_ref[...] = kvp[:, wk:2 * wk].astype(BF16)
        if use_rope:
            cos_k, sin_k = _tile_lanes(cos_ref[...], wk), _tile_lanes(sin_ref[...], wk)
            k = k_rinv * (k_raw * kg_ref[...] * cos_k + kvp[:, 2 * wk:] * kgp_ref[...] * sin_k)
        else:
            k = k_rinv * (k_raw * kg_ref[...])
        ok_ref[...] = k.astype(BF16)
        if need_q:
            qp = _dot(hm, w_ref[:, 3 * wk:])
            q_raw = qp[:, 0:wq]
            rinv = _seg_rinv(q_raw, GQA_HEAD_DIM) * (GQA_SCALE * LOG2E)
            if use_rope:
                cos_q, sin_q = _tile_lanes(cos_ref[...], wq), _tile_lanes(sin_ref[...], wq)
                q = rinv * (q_raw * qg_ref[...] * cos_q + qp[:, wq:] * qgp_ref[...] * sin_q)
            else:
                q = rinv * (q_raw * qg_ref[...])
            oq_ref[...] = q.astype(BF16)

    tile = lambda w: pl.BlockSpec((t, w), lambda i: (i, 0))
    in_specs = [tile(d), _full((1, d)), _mod_spec(st, SC1, d), _mod_spec(st, SH1, d),
                _resident(w_in.shape), _full(qg.shape), _full(qgp.shape), _full(kg.shape), _full(kgp.shape)]
    args = [x, gain, mods, mods, w_in, qg, qgp, kg, kgp]
    if use_rope:
        tps = st.tiles_per_seg
        in_specs += [pl.BlockSpec((t, LANES), lambda i: (i % tps, 0))] * 2
        args += list(rope)
    out_specs, out_shape = [], []
    if need_q:
        out_specs.append(tile(wq))
        out_shape.append(SDS((n, wq), BF16))
    out_specs += [tile(wk), tile(wk)]
    out_shape += [SDS((n, wk), BF16)] * 2
    outs = pl.pallas_call(
        kern, grid=(st.tiles,), in_specs=in_specs, out_specs=out_specs, out_shape=out_shape,
        name="gqa_proj", compiler_params=_params(("parallel",), 48))(*args)
    return tuple(outs) if need_q else (None,) + tuple(outs)


def _gqa_window_call(batch, length, lc, q, k, v, kc, vc, sink):
    nb = length // BLOCK
    hd = GQA_HEAD_DIM
    wq, wk = GQA_Q_HEADS * hd, GQA_KV_HEADS * hd
    rows = GQA_GROUP * BLOCK

    def kern(sink_ref, q_ref, kp_ref, k0_ref, kn_ref, vp_ref, v0_ref, vn_ref, kc_ref, vc_ref, o_ref):
        nblk = pl.program_id(1)
        r = lax.broadcasted_iota(jnp.int32, (rows, 3 * BLOCK), 0) & (BLOCK - 1)
        c = lax.broadcasted_iota(jnp.int32, (rows, 3 * BLOCK), 1)
        valid = (c >= r + BLOCK - WINDOW) & (c <= r + BLOCK + WINDOW)
        valid = valid & ((c >= BLOCK) | (nblk > 0)) & ((c < 2 * BLOCK) | (nblk < nb - 1))
        hrow = lax.broadcasted_iota(jnp.int32, (rows, 1), 0) // BLOCK
        for g in range(GQA_KV_HEADS):
            sl = slice(g * hd, (g + 1) * hd)
            qg = jnp.concatenate([q_ref[:, (g * GQA_GROUP + j) * hd:(g * GQA_GROUP + j + 1) * hd]
                                  for j in range(GQA_GROUP)], axis=0)
            kband = jnp.concatenate([kp_ref[:, sl], k0_ref[:, sl], kn_ref[:, sl]], axis=0)
            vband = jnp.concatenate([vp_ref[:, sl], v0_ref[:, sl], vn_ref[:, sl]], axis=0)
            s_c = _nt(qg, kc_ref[:, sl])
            s_b = jnp.where(valid, _nt(qg, kband), -1e30)
            snk = jnp.zeros((rows, 1), F32)
            for j in range(GQA_GROUP):
                snk = jnp.where(hrow == j, sink_ref[g * GQA_GROUP + j], snk)
            m = jnp.maximum(jnp.maximum(jnp.max(s_c, axis=-1, keepdims=True),
                                        jnp.max(s_b, axis=-1, keepdims=True)), snk)
            p_c = jnp.exp2(s_c - m)
            p_b = jnp.exp2(s_b - m)
            den = (jnp.sum(p_c, axis=-1, keepdims=True) + jnp.sum(p_b, axis=-1, keepdims=True)
                   + jnp.exp2(snk - m))
            o = (_dot(p_c.astype(BF16), vc_ref[:, sl]) + _dot(p_b.astype(BF16), vband)) / den
            o_ref[:, g * GQA_GROUP * hd:(g + 1) * GQA_GROUP * hd] = jnp.concatenate(
                [o[j * BLOCK:(j + 1) * BLOCK, :] for j in range(GQA_GROUP)], axis=-1).astype(BF16)

    blk = lambda f: pl.BlockSpec((BLOCK, wk), f)
    prev_ = lambda b, i: (b * nb + jnp.maximum(i - 1, 0), 0)
    cur_ = lambda b, i: (b * nb + i, 0)
    next_ = lambda b, i: (b * nb + jnp.minimum(i + 1, nb - 1), 0)
    ctx_spec = pl.BlockSpec((lc, wk), lambda b, i: (b, 0))
    return pl.pallas_call(
        kern, grid=(batch, nb),
        in_specs=[pl.BlockSpec(memory_space=pltpu.SMEM),
                  pl.BlockSpec((BLOCK, wq), cur_), blk(prev_), blk(cur_), blk(next_),
                  blk(prev_), blk(cur_), blk(next_), ctx_spec, ctx_spec],
        out_specs=pl.BlockSpec((BLOCK, wq), cur_),
        out_shape=SDS((batch * length, wq), BF16), name="gqa_window",
        compiler_params=_params(("parallel", "arbitrary"), 40))(sink, q, k, k, k, v, v, v, kc, vc)


def _gqa_window_fast_call(batch, length, lc, q, k, v, kc, vc, sink2, shift):
    nb = length // BLOCK
    hd = GQA_HEAD_DIM
    wq, wk = GQA_Q_HEADS * hd, GQA_KV_HEADS * hd
    rows = GQA_GROUP * BLOCK

    def kern(sink_ref, shift_ref, q_ref, kp_ref, k0_ref, kn_ref, vp_ref, v0_ref, vn_ref, kc_ref, vc_ref, o_ref):
        nblk = pl.program_id(1)
        r = lax.broadcasted_iota(jnp.int32, (rows, 3 * BLOCK), 0) & (BLOCK - 1)
        c = lax.broadcasted_iota(jnp.int32, (rows, 3 * BLOCK), 1)
        valid = (c >= r + BLOCK - WINDOW) & (c <= r + BLOCK + WINDOW)
        valid = valid & ((c >= BLOCK) | (nblk > 0)) & ((c < 2 * BLOCK) | (nblk < nb - 1))
        hrow = lax.broadcasted_iota(jnp.int32, (rows, 1), 0) // BLOCK
        lane0_q = lax.broadcasted_iota(jnp.int32, (BLOCK, hd), 1) == 0

        def with_one(x):
            lane0 = lax.broadcasted_iota(jnp.int32, (x.shape[0], hd), 1) == 0
            return jnp.concatenate([x, jnp.where(lane0, 1.0, 0.0).astype(BF16)], axis=1)

        for g in range(GQA_KV_HEADS):
            sl = slice(g * hd, (g + 1) * hd)
            heads = [g * GQA_GROUP + j for j in range(GQA_GROUP)]
            qg = jnp.concatenate(
                [jnp.concatenate([q_ref[:, h * hd:(h + 1) * hd],
                                  jnp.where(lane0_q, -shift_ref[h], 0.0).astype(BF16)], axis=1)
                 for h in heads], axis=0)
            kband = with_one(jnp.concatenate([kp_ref[:, sl], k0_ref[:, sl], kn_ref[:, sl]], axis=0))
            vband = with_one(jnp.concatenate([vp_ref[:, sl], v0_ref[:, sl], vn_ref[:, sl]], axis=0))
            p_c = jnp.exp2(_nt(qg, with_one(kc_ref[:, sl])).astype(BF16))
            p_b = jnp.exp2(jnp.where(valid, _nt(qg, kband), -1e30).astype(BF16))
            acc = _dot(p_c, with_one(vc_ref[:, sl])) + _dot(p_b, vband)
            snk = jnp.zeros((rows, 1), F32)
            for j, h in enumerate(heads):
                snk = jnp.where(hrow == j, sink_ref[h] - shift_ref[h], snk)
            o = acc[:, 0:hd] / (acc[:, hd:hd + 1] + jnp.exp2(snk))
            o_ref[:, g * GQA_GROUP * hd:(g + 1) * GQA_GROUP * hd] = jnp.concatenate(
                [o[j * BLOCK:(j + 1) * BLOCK, :] for j in range(GQA_GROUP)], axis=-1).astype(BF16)

    blk = lambda f: pl.BlockSpec((BLOCK, wk), f)
    prev_ = lambda b, i: (b * nb + jnp.maximum(i - 1, 0), 0)
    cur_ = lambda b, i: (b * nb + i, 0)
    next_ = lambda b, i: (b * nb + jnp.minimum(i + 1, nb - 1), 0)
    ctx_spec = pl.BlockSpec((lc, wk), lambda b, i: (b, 0))
    smem = pl.BlockSpec(memory_space=pltpu.SMEM)
    return pl.pallas_call(
        kern, grid=(batch, nb),
        in_specs=[smem, smem, pl.BlockSpec((BLOCK, wq), cur_), blk(prev_), blk(cur_), blk(next_),
                  blk(prev_), blk(cur_), blk(next_), ctx_spec, ctx_spec],
        out_specs=pl.BlockSpec((BLOCK, wq), cur_),
        out_shape=SDS((batch * length, wq), BF16), name="gqa_window_fast",
        compiler_params=_params(("parallel", "arbitrary"), 40))(sink2, shift, q, k, k, k, v, v, v, kc, vc)


def _gqa_layer(streams, xs, mods, gain, p, rope, ctx_next):
    assert not ctx_next, "the windowed-GQA mixer is only implemented as the last layer"
    st_c, st_x = streams
    xc, xx = xs
    w_in, q_gain, k_gain, sink, w_out = p
    wq, wk = GQA_Q_HEADS * GQA_HEAD_DIM, GQA_KV_HEADS * GQA_HEAD_DIM
    w_q, w_k, w_v = w_in[:, :wq], w_in[:, wq:wq + wk], w_in[:, wq + wk:]
    w_in_p = jnp.concatenate([w_k, w_v, w_k[:, _rope_partner_perm(GQA_KV_HEADS)],
                              w_q, w_q[:, _rope_partner_perm(GQA_Q_HEADS)]], axis=1).astype(BF16)
    par = _rope_partner_perm(1)
    wts = (w_in_p, jnp.tile(q_gain, GQA_Q_HEADS).reshape(1, -1), jnp.tile(q_gain[par], GQA_Q_HEADS).reshape(1, -1),
           jnp.tile(k_gain, GQA_KV_HEADS).reshape(1, -1), jnp.tile(k_gain[par], GQA_KV_HEADS).reshape(1, -1))
    _, kc, vc = _gqa_proj_call(st_c.with_tile(PROJ_TILE), xc, mods, gain, wts, None, False)
    q, k, v = _gqa_proj_call(st_x.with_tile(PROJ_TILE), xx, mods, gain, wts, rope, True)
    sink2 = sink * LOG2E
    bound = LOG2E * GQA_SCALE * GQA_HEAD_DIM * jnp.max(jnp.abs(q_gain)) * jnp.max(jnp.abs(k_gain))
    shift = jnp.maximum(bound * SHIFT_MARGIN + 1.0, sink2).astype(BF16).astype(F32)
    args = (st_x.batch, st_x.seg, st_c.seg, q, k, v, kc, vc, sink2)
    a = lax.cond(jnp.max(shift) <= MAX_FIXED_SHIFT,
                 lambda: _gqa_window_fast_call(*args, shift),
                 lambda: _gqa_window_call(*args))
    return [None, _outproj_call(st_x, a, mods, w_out.astype(BF16), "gqa_out")]


def _cast_experts_call(w, layer):
    _, ne, a, b = w.shape
    eb = 4

    def kern(w_ref, o_ref):
        o_ref[...] = w_ref[...].astype(BF16)

    return pl.pallas_call(
        kern, grid=(ne // eb,),
        in_specs=[pl.BlockSpec((None, eb, a, b), lambda i: (layer, i, 0, 0))],
        out_specs=pl.BlockSpec((eb, a, b), lambda i: (i, 0, 0)),
        out_shape=SDS((ne, a, b), BF16), name="cast_experts",
        compiler_params=_params(("parallel",), 32))(w)


def _route(logits_t, bias_col):
    scores = _sigmoid(logits_t)
    biased = scores + bias_col
    rows = [biased[e:e + 1, :] for e in range(N_EXPERTS)]
    srow = [scores[e:e + 1, :] for e in range(N_EXPERTS)]
    epg = EXPERTS_PER_GROUP
    gscore = []
    for g in range(N_EXPERT_GROUPS):
        v = rows[g * epg:(g + 1) * epg]
        pair = [v[a] + v[b] for a in range(epg) for b in range(a + 1, epg)]
        gscore.append(functools.reduce(jnp.maximum, pair))
    ind, wloc = [], [None] * epg
    for g in range(N_EXPERT_GROUPS):
        best = None
        for g2 in range(N_EXPERT_GROUPS):
            if g2 == g:
                continue
            cnd = gscore[g] > gscore[g2] if g2 < g else gscore[g] >= gscore[g2]
            best = cnd if best is None else best & cnd
        ind.append(jnp.where(best, 1.0, 0.0))
        for j in range(epg):
            e = g * epg + j
            rank = None
            for e2 in range(g * epg, (g + 1) * epg):
                if e2 == e:
                    continue
                ahead = rows[e2] >= rows[e] if e2 < e else rows[e2] > rows[e]
                one = jnp.where(ahead, 1.0, 0.0)
                rank = one if rank is None else rank + one
            w = jnp.where(best & (rank < 2.0), srow[e], 0.0)
            wloc[j] = w if wloc[j] is None else wloc[j] + w
    den = functools.reduce(lambda a, b: a + b, wloc)
    return ind, [w / den for w in wloc]


MOE_TILE = 1024
MOE_CHUNK = 128


def _moe_call(st, x_in, y_in, mods, gain, router_wt, router_bias, w_gate, w_up, w_down):
    n, d = x_in.shape
    t = min(MOE_TILE, n)
    r = MOE_CHUNK
    ne, _, ff = w_gate.shape
    epg, ng = EXPERTS_PER_GROUP, N_EXPERT_GROUPS
    nch = t // r + ng - 1
    tiles = n // t
    wd2 = w_down.reshape(ne * ff, d)

    def kern(x_ref, y_ref, g_ref, sc_ref, sh_ref, g2_ref, rw_ref, rb_ref, wg_ref, wu_ref, wd_ref, o_ref,
             hb_ref, oh_ref, xg_ref):
        hb_ref[...] = _modnorm(x_ref[...] + y_ref[...].astype(F32), g_ref[...], sc_ref[...], sh_ref[...]).astype(BF16)
        ind, wloc = _route(_nt(rw_ref[...], hb_ref[...]), rb_ref[...])
        ind8 = jnp.concatenate(ind + [jnp.zeros((8 - ng, t), F32)], axis=0)
        lane = lax.broadcasted_iota(jnp.int32, (8, t), 1)
        csum, step = ind8, 1
        while step < t:
            csum = csum + jnp.where(lane >= step, pltpu.roll(csum, step, axis=1), 0.0)
            step *= 2
        pos = csum - ind8
        count = [jnp.sum(ind[g]).astype(jnp.int32) for g in range(ng)]
        first = [jnp.int32(0)]
        for g in range(ng):
            first.append(first[g] + (count[g] + (r - 1)) // r)
        slot = functools.reduce(lambda a, b: a + b, [
            ind[g] * (pos[g:g + 1, :] + (first[g] * r).astype(F32)) for g in range(ng)])
        w_hi = [w.astype(BF16) for w in wloc]
        w_lo = [(w - h.astype(F32)).astype(BF16) for w, h in zip(wloc, w_hi)]
        wst = jnp.concatenate(w_hi + w_lo, axis=0)
        row_id = lax.broadcasted_iota(jnp.int32, (r, t), 0).astype(F32)
        onehot = lambda c: jnp.where(slot == row_id + float(c * r), 1.0, 0.0).astype(BF16)
        last = nch - 1
        for c in range(last):
            oh_ref[c * r:(c + 1) * r, :] = onehot(c)
        xg_ref[...] = _dot(oh_ref[...], hb_ref[...]).astype(BF16)
        wr_all = _nt(wst, oh_ref[...]).T

        def chunk_ffn(c, xg, wr):
            grp = functools.reduce(lambda a, b: a + b,
                                   [(c >= first[g]).astype(jnp.int32) for g in range(1, ng)])
            acts = []
            for j in range(epg):
                e = grp * epg + j
                gt = _dot(xg, wg_ref[e])
                up = _dot(xg, wu_ref[e])
                acts.append((gt * _sigmoid(gt) * up * (wr[:, j:j + 1] + wr[:, epg + j:epg + j + 1])).astype(BF16))
            wd_g = wd_ref[pl.ds(pl.multiple_of(grp * (epg * ff), epg * ff), epg * ff), :]
            return _dot(jnp.concatenate(acts, axis=1), wd_g).astype(BF16)

        for c in range(last):
            rows = slice(c * r, (c + 1) * r)

            def run(c=c, rows=rows):
                xg_ref[rows, :] = chunk_ffn(c, xg_ref[rows, :], wr_all[c * r:(c + 1) * r, :])

            if c < t // r:
                run()
            else:
                pl.when(c < first[ng])(run)

        tdot = lambda a, b: lax.dot_general(a, b, (((0,), (0,)), ((), ())), preferred_element_type=F32)
        o_ref[...] = x_ref[...] + y_ref[...].astype(F32) + g2_ref[...] * tdot(oh_ref[...], xg_ref[...])

        @pl.when(last < first[ng])
        def _():
            oh = onehot(last)
            out = chunk_ffn(last, _dot(oh, hb_ref[...]).astype(BF16), _nt(wst, oh).T)
            o_ref[...] += g2_ref[...] * tdot(oh, out)

    row = (lambda i: st.batch) if st.is_ctx else (lambda i: i // (st.seg // t))
    mod = lambda chunk: pl.BlockSpec((None, 1, d), lambda i: (row(i), 0, chunk))
    return pl.pallas_call(
        kern, grid=(tiles,),
        in_specs=[pl.BlockSpec((t, d), lambda i: (i, 0)),
                  pl.BlockSpec((t, d), lambda i: (i, 0)),
                  _full((1, d)), mod(SC2), mod(SH2), mod(G2), _full((ne, d)), _full((ne, 1)),
                  _resident((ne, d, ff)), _resident((ne, d, ff)), _resident((ne * ff, d))],
        out_specs=pl.BlockSpec((t, d), lambda i: (i, 0)),
        out_shape=SDS((n, d), F32),
        scratch_shapes=[pltpu.VMEM((t, d), BF16), pltpu.VMEM(((nch - 1) * r, t), BF16),
                        pltpu.VMEM(((nch - 1) * r, d), BF16)],
        name="moe", compiler_params=_params(("parallel",), 58))(
            x_in, y_in, gain, mods, mods, mods, router_wt, router_bias, w_gate, w_up, wd2)


def kernel(x, c, ctx, c_ctx, w_ada, b_ada, norm_mix, norm_ffn, fourier_w_out, conv_w_in, conv_w, conv_w_out,
           mla_w_in, mla_q_norm, mla_kv_norm, mla_w_uq, mla_w_ukv, mla_q_gain, mla_k_gain, mla_w_out,
           gqa_w_in, gqa_q_gain, gqa_k_gain, gqa_sink, gqa_w_out, router_w, router_bias,
           moe_w_gate, moe_w_up, moe_w_down):
    b, l, d = x.shape
    lc = ctx.shape[1]
    depth = w_ada.shape[0]
    st_c = Stream(b * lc, lc, min(b * lc, TOKEN_TILE), b, True)
    st_x = Stream(b * l, l, min(l, TOKEN_TILE), b, False)
    streams = [st_c, st_x]
    r8 = -(-(b + 1) // 8) * 8
    cvec = jnp.concatenate([c, c_ctx[None, :], jnp.zeros((r8 - b - 1, d), F32)], axis=0)
    mods_all = _ada_call(cvec, w_ada, b_ada).reshape(depth, r8, 1, 6 * d)
    rope = _rope_tables(l)
    router_wt = router_w.T.astype(BF16)
    router_b = router_bias.reshape(-1, 1)
    xs = [ctx.reshape(b * lc, d), x.reshape(b * l, d)]
    for i in range(depth):
        kind, j = i % 4, i // 4
        ctx_next = i < depth - 1
        mods = mods_all[i]
        gain = norm_mix[i].reshape(1, d)
        if not (ctx_next or kind >= 2):
            xs[0] = None
        if kind == 0:
            ys = _fourier_layer(streams, xs, mods, gain, fourier_w_out[j])
        elif kind == 1:
            ys = _conv_layer(streams, xs, mods, gain, conv_w_in[j], conv_w[j], conv_w_out[j])
        elif kind == 2:
            ys = _mla_layer(streams, xs, mods, gain,
                            (mla_w_in[j], mla_q_norm[j], mla_kv_norm[j], mla_w_uq[j], mla_w_ukv[j],
                             mla_q_gain[j], mla_k_gain[j], mla_w_out[j]), rope, ctx_next)
        else:
            ys = _gqa_layer(streams, xs, mods, gain,
                            (gqa_w_in[j], gqa_q_gain[j], gqa_k_gain[j], gqa_sink[j], gqa_w_out[j]),
                            rope, ctx_next)
        if not ctx_next:
            ys[0] = None
        gain2 = norm_ffn[i].reshape(1, d)
        wg, wu, wd = (_cast_experts_call(w, i) for w in (moe_w_gate, moe_w_up, moe_w_down))
        xs = [None if y is None else _moe_call(st, x, y, mods, gain2, router_wt, router_b, wg, wu, wd)
              for st, x, y in zip(streams, xs, ys)]
    return xs[1].reshape(b, l, d)
```

```python
import functools

import numpy as np
import jax
import jax.numpy as jnp
from jax import lax
from jax.experimental import pallas as pl
from jax.experimental.pallas import tpu as pltpu

F32, BF16 = jnp.float32, jnp.bfloat16
SDS = jax.ShapeDtypeStruct

EPS = 1e-6
GRID_W = 64
ROPE_THETA = 10000.0
N_FOURIER_GROUPS = 4
MLA_HEADS, MLA_Q_RANK, MLA_KV_RANK = 16, 256, 128
MLA_NOPE, MLA_ROPE, MLA_V = 128, 64, 128
MLA_SCALE = (MLA_NOPE + MLA_ROPE) ** -0.5
GQA_Q_HEADS, GQA_KV_HEADS, GQA_HEAD_DIM = 16, 4, 64
GQA_GROUP = GQA_Q_HEADS // GQA_KV_HEADS
GQA_SCALE = GQA_HEAD_DIM ** -0.5
WINDOW = 128
BLOCK = 128
N_EXPERTS, N_EXPERT_GROUPS, EXPERT_FF = 16, 4, 256
EXPERTS_PER_GROUP = N_EXPERTS // N_EXPERT_GROUPS
ROPE_DIM = 64
LOG2E = 1.4426950408889634
MAX_FIXED_SHIFT = 50.0
SHIFT_MARGIN = 1.02
TOKEN_TILE = 1024
PROJ_TILE = 512

V7X_VMEM_BYTES = 64 * 1024 * 1024
LANES = 128
SH1, SC1, G1, SH2, SC2, G2 = range(6)


def _params(sem, vmem_mb):
    return pltpu.CompilerParams(dimension_semantics=sem, vmem_limit_bytes=vmem_mb * 1024 * 1024)


def _sigmoid(v):
    return 1.0 / (1.0 + jnp.exp(-v))


def _modnorm(x, gain, sc, sh):
    ms = jnp.mean(x * x, axis=-1, keepdims=True)
    return x * lax.rsqrt(ms + EPS) * (gain * (1.0 + sc)) + sh


def _nt(a, b):
    return lax.dot_general(a, b, (((1,), (1,)), ((), ())), preferred_element_type=F32)


def _dot(a, b):
    return jnp.dot(a, b, preferred_element_type=F32)


class Stream:
    def __init__(self, n, seg, tile, batch, is_ctx):
        self.n, self.seg, self.tile, self.batch, self.is_ctx = n, seg, tile, batch, is_ctx
        self.tiles = n // tile
        self.tiles_per_seg = seg // tile

    def mod_row(self, t):
        return self.batch if self.is_ctx else t // self.tiles_per_seg

    def seg_row(self, b):
        return self.batch if self.is_ctx else b

    def with_tile(self, tile):
        return Stream(self.n, self.seg, min(tile, self.tile), self.batch, self.is_ctx)


def _mod_spec(st, chunk, d):
    return pl.BlockSpec((None, 1, d), lambda t: (st.mod_row(t), 0, chunk))


def _full(shape):
    nd = len(shape)
    return pl.BlockSpec(shape, lambda *_: (0,) * nd)


def _resident(shape):
    nd = len(shape)
    return pl.BlockSpec(shape, lambda *_: (0,) * nd, pipeline_mode=pl.Buffered(1))


def _ada_call(cvec, w_ada, b_ada):
    depth, d, d6 = w_ada.shape
    r8 = cvec.shape[0]
    tn = d6 // 4

    def kern(c_ref, w_ref, b_ref, o_ref):
        c = c_ref[...]
        s = (c * _sigmoid(c)).astype(BF16)
        o_ref[...] = _dot(s, w_ref[...].astype(BF16)) + b_ref[...]

    return pl.pallas_call(
        kern, grid=(depth, d6 // tn),
        in_specs=[pl.BlockSpec((r8, d), lambda i, j: (0, 0)),
                  pl.BlockSpec((None, d, tn), lambda i, j: (i, 0, j)),
                  pl.BlockSpec((None, 1, tn), lambda i, j: (i, 0, j))],
        out_specs=pl.BlockSpec((None, r8, tn), lambda i, j: (i, 0, j)),
        out_shape=SDS((depth, r8, d6), F32), name="ada",
        compiler_params=_params(("arbitrary", "arbitrary"), 40))(cvec, w_ada, b_ada.reshape(depth, 1, d6))


def _outproj_call(st, a, mods, w, name):
    n, k = a.shape
    d = w.shape[1]
    t = st.tile

    def kern(a_ref, g1_ref, w_ref, o_ref):
        o_ref[...] = (g1_ref[...] * _dot(a_ref[...], w_ref[...])).astype(BF16)

    return pl.pallas_call(
        kern, grid=(st.tiles,),
        in_specs=[pl.BlockSpec((t, k), lambda i: (i, 0)), _mod_spec(st, G1, d), _resident((k, d))],
        out_specs=pl.BlockSpec((t, d), lambda i: (i, 0)),
        out_shape=SDS((n, d), BF16), name=name,
        compiler_params=_params(("parallel",), 40))(a, mods, w)


def _dft_tables(length, radix):
    lr = length // radix
    m = np.arange(lr)[None, :, None]
    j = np.arange(radix)[:, None, None]
    nn = np.arange(lr)[None, None, :]
    ang = 2.0 * np.pi * (((radix * m + j) * nn) % length) / length
    e = np.concatenate([np.cos(ang), np.sin(ang)], axis=-1) / np.sqrt(length)
    return jnp.asarray(e, dtype=F32).astype(BF16)


def _group_dft_tables(group):
    k = np.arange(group)
    ang = 2.0 * np.pi * ((k[:, None] * k[None, :]) % group) / group
    return (jnp.asarray(np.cos(ang) / np.sqrt(group), dtype=F32),
            jnp.asarray(np.sin(ang) / np.sqrt(group), dtype=F32))


def _fourier_weight_call(w_out):
    d = w_out.shape[0]
    grp = d // N_FOURIER_GROUPS
    cg, sg = _group_dft_tables(grp)

    def kern(cg_ref, sg_ref, w_ref, o_ref):
        w = w_ref[...]
        o_ref[:, :d] = jnp.dot(cg_ref[...], w, preferred_element_type=F32,
                               precision=lax.Precision.HIGHEST).astype(BF16)
        o_ref[:, d:] = jnp.dot(sg_ref[...], w, preferred_element_type=F32,
                               precision=lax.Precision.HIGHEST).astype(BF16)

    return pl.pallas_call(
        kern, grid=(N_FOURIER_GROUPS,),
        in_specs=[_full((grp, grp)), _full((grp, grp)), pl.BlockSpec((grp, d), lambda g: (g, 0))],
        out_specs=pl.BlockSpec((grp, 2 * d), lambda g: (g, 0)),
        out_shape=SDS((d, 2 * d), BF16), name="fourier_w",
        compiler_params=_params(("arbitrary",), 32))(cg, sg, w_out)


def _radix_terms(radix, j):
    real, imag = [], []
    for q in range(radix):
        k = (j * q * (4 // radix)) % 4 if radix > 1 else 0
        if k == 0:
            real.append((1, 0, q)); imag.append((-1, 1, q))
        elif k == 1:
            real.append((-1, 1, q)); imag.append((-1, 0, q))
        elif k == 2:
            real.append((-1, 0, q)); imag.append((1, 1, q))
        else:
            real.append((1, 1, q)); imag.append((1, 0, q))
    return real, imag


def _fourier_mix_call(st, x, mods, gain, wcs, radix):
    n, d = x.shape
    seg = st.seg
    lr = seg // radix
    e = _dft_tables(seg, radix)
    nb = n // seg
    rt = min(seg, 512)
    cblk = 2 * LANES

    def kern(x_ref, g_ref, sc_ref, sh_ref, g1_ref, w_ref, e_ref, o_ref, p_ref, v_ref, *z_refs):
        for i in range(seg // rt):
            rows = slice(i * rt, (i + 1) * rt)
            h = _modnorm(x_ref[rows, :], g_ref[...], sc_ref[...], sh_ref[...]).astype(BF16)
            p_ref[rows, :] = _dot(h, w_ref[...]).astype(BF16)
        for c in range(d // cblk):
            cols = (slice(c * cblk, (c + 1) * cblk), slice(d + c * cblk, d + (c + 1) * cblk))
            for j in range(radix):
                real, imag = _radix_terms(radix, j)

                def comb(terms):
                    acc = None
                    for sgn, part, q in terms:
                        v = p_ref[q * lr:(q + 1) * lr, cols[part]].astype(F32)
                        if acc is None:
                            acc = v if sgn > 0 else -v
                        else:
                            acc = acc + v if sgn > 0 else acc - v
                    return acc

                v_ref[0:lr, :] = comb(real).astype(BF16)
                v_ref[lr:2 * lr, :] = comb(imag).astype(BF16)
                z = _dot(e_ref[j], v_ref[...])
                for k, z_ref in enumerate(z_refs):
                    z_ref[pl.ds(j, lr, stride=radix), :] = z[:, k * LANES:(k + 1) * LANES]
            for k, z_ref in enumerate(z_refs):
                sl = slice(c * cblk + k * LANES, c * cblk + (k + 1) * LANES)
                o_ref[:, sl] = (g1_ref[:, sl] * z_ref[...]).astype(BF16)

    mod = lambda chunk: pl.BlockSpec((None, 1, d), lambda b: (st.seg_row(b), 0, chunk))
    return pl.pallas_call(
        kern, grid=(nb,),
        in_specs=[pl.BlockSpec((seg, d), lambda b: (b, 0)), _full((1, d)), mod(SC1), mod(SH1), mod(G1),
                  _resident((d, 2 * d)), _resident((radix, lr, 2 * lr))],
        out_specs=pl.BlockSpec((seg, d), lambda b: (b, 0)),
        out_shape=SDS((n, d), BF16),
        scratch_shapes=[pltpu.VMEM((seg, 2 * d), BF16), pltpu.VMEM((2 * lr, cblk), BF16)]
        + [pltpu.VMEM((seg, LANES), F32)] * (cblk // LANES),
        name=f"fourier_mix_r{radix}",
        compiler_params=_params(("parallel",), 56))(x, gain, mods, mods, mods, wcs, e)


def _fourier_layer(streams, xs, mods, gain, w_out):
    wcs = _fourier_weight_call(w_out)
    return [None if x is None else _fourier_mix_call(st, x, mods, gain, wcs, 4 if st.seg >= 1024 else 1)
            for st, x in zip(streams, xs)]


def _conv_mix_call(st, x, mods, gain, w_in, conv_w, w_out):
    n, d = x.shape
    seg = st.seg
    nb = n // seg
    rt = min(seg, 512)
    nrt = seg // rt
    halo = 16

    def kern(x_ref, g_ref, sc_ref, sh_ref, g1_ref, win_ref, cw_ref, wout_ref, o_ref, bg_ref, u_ref):
        for i in range(nrt):
            rows = slice(i * rt, (i + 1) * rt)
            h = _modnorm(x_ref[rows, :], g_ref[...], sc_ref[...], sh_ref[...]).astype(BF16)
            bg_ref[rows, :] = _dot(h, win_ref[:, 0:d]).astype(BF16)
            u_ref[rows, :] = (_dot(h, win_ref[:, d:2 * d]) * _dot(h, win_ref[:, 2 * d:3 * d])).astype(BF16)
        row = lax.broadcasted_iota(jnp.int32, (rt, 1), 0)
        zero_row = jnp.zeros((1, d), F32)
        for i in range(nrt):
            rows = slice(i * rt, (i + 1) * rt)
            u = u_ref[rows, :].astype(F32)
            prev_row = zero_row if i == 0 else u_ref[i * rt - halo:i * rt, :].astype(F32)[halo - 1:halo, :]
            next_row = (zero_row if i == nrt - 1
                        else u_ref[(i + 1) * rt:(i + 1) * rt + halo, :].astype(F32)[0:1, :])
            um = jnp.where(row == 0, prev_row, pltpu.roll(u, 1, axis=0))
            up = jnp.where(row == rt - 1, next_row, pltpu.roll(u, rt - 1, axis=0))
            z = cw_ref[0:1, :] * um + cw_ref[1:2, :] * u + cw_ref[2:3, :] * up
            a = (bg_ref[rows, :].astype(F32) * z).astype(BF16)
            o_ref[rows, :] = (g1_ref[...] * _dot(a, wout_ref[...])).astype(BF16)

    mod = lambda chunk: pl.BlockSpec((None, 1, d), lambda b: (st.seg_row(b), 0, chunk))
    seg_spec = pl.BlockSpec((seg, d), lambda b: (b, 0))
    return pl.pallas_call(
        kern, grid=(nb,),
        in_specs=[seg_spec, _full((1, d)), mod(SC1), mod(SH1), mod(G1),
                  _resident((d, 3 * d)), _full((3, d)), _resident((d, d))],
        out_specs=seg_spec, out_shape=SDS((n, d), BF16),
        scratch_shapes=[pltpu.VMEM((seg, d), BF16), pltpu.VMEM((seg, d), BF16)],
        name="conv_mix", compiler_params=_params(("parallel",), 56))(
            x, gain, mods, mods, mods, w_in, conv_w, w_out)


def _conv_layer(streams, xs, mods, gain, w_in, conv_w, w_out):
    w_in_b, w_out_b = w_in.astype(BF16), w_out.astype(BF16)
    return [None if x is None else _conv_mix_call(st, x, mods, gain, w_in_b, conv_w, w_out_b)
            for st, x in zip(streams, xs)]


def _rope_tables(length):
    q = ROPE_DIM // 4
    pos = np.arange(length)
    rc = np.stack([pos // GRID_W, pos % GRID_W], axis=1).astype(np.float32)
    inv = (ROPE_THETA ** (-np.arange(q, dtype=np.float32) / q)).astype(np.float32)
    lane = np.arange(ROPE_DIM)
    ang = rc[:, lane >> 5] * inv[lane & (q - 1)][None, :]
    sign = np.where((lane & q) == 0, -1.0, 1.0)[None, :]
    cos = np.cos(ang.astype(np.float32)).astype(np.float32)
    sin = (np.sin(ang.astype(np.float32)) * sign).astype(np.float32)
    reps = LANES // ROPE_DIM
    return jnp.asarray(np.tile(cos, (1, reps))), jnp.asarray(np.tile(sin, (1, reps)))


def _rope_partner_perm(n_heads):
    lane = np.arange(n_heads * ROPE_DIM)
    return lane ^ (ROPE_DIM // 4)


def _seg_rinv(x, seg):
    t, w = x.shape
    cols = []
    for c in range(w // LANES):
        blk = x[:, c * LANES:(c + 1) * LANES]
        sq = blk * blk
        if seg == LANES:
            cols.append(jnp.broadcast_to(
                lax.rsqrt(jnp.mean(sq, axis=-1, keepdims=True) + EPS), (t, LANES)))
        else:
            lane = lax.broadcasted_iota(jnp.int32, (t, LANES), 1)
            low = lane < seg
            lo = jnp.sum(jnp.where(low, sq, 0.0), axis=-1, keepdims=True) * (1.0 / seg)
            hi = jnp.sum(jnp.where(low, 0.0, sq), axis=-1, keepdims=True) * (1.0 / seg)
            cols.append(jnp.where(low, lax.rsqrt(lo + EPS), lax.rsqrt(hi + EPS)))
    return cols[0] if len(cols) == 1 else jnp.concatenate(cols, axis=-1)


def _tile_lanes(v, w):
    reps = w // LANES
    return v if reps == 1 else jnp.concatenate([v] * reps, axis=-1)


def _mla_proj_call(st, x, mods, gain, wts, rope, need_q):
    n, d = x.shape
    t = st.tile
    h = MLA_HEADS
    wn, wr = h * MLA_NOPE, h * MLA_ROPE
    (w_in, q_norm, kv_norm, w_uq, w_ukv, qg_n, qg_r, qg_rp, kg_n, kg_r, kg_rp) = wts
    use_rope = rope is not None

    def kern(*refs):
        refs = list(refs)
        x_ref, g_ref, sc_ref, sh_ref, win_ref, qn_ref_, kvn_ref_, wuq_ref, wukv_ref = refs[:9]
        qgn_ref, qgr_ref, qgrp_ref, kgn_ref, kgr_ref, kgrp_ref = refs[9:15]
        rest = refs[15:]
        if use_rope:
            cos_ref, sin_ref = rest[:2]
            rest = rest[2:]
        if need_q:
            oqn_ref, oqr_ref = rest[:2]
            rest = rest[2:]
        okn_ref, okr_ref, ov_ref = rest
        hm = _modnorm(x_ref[...], g_ref[...], sc_ref[...], sh_ref[...]).astype(BF16)
        ck = _dot(hm, win_ref[...])
        kr2 = ck[:, MLA_Q_RANK + MLA_KV_RANK:]
        kr_rinv = _seg_rinv(kr2, MLA_ROPE)[:, 0:MLA_ROPE]
        kr_raw = kr2[:, 0:MLA_ROPE] * kgr_ref[...]
        if use_rope:
            kr_par = kr2[:, MLA_ROPE:] * kgrp_ref[...]
            kr = kr_rinv * (kr_raw * cos_ref[:, 0:MLA_ROPE] + kr_par * sin_ref[:, 0:MLA_ROPE])
        else:
            kr = kr_rinv * kr_raw
        okr_ref[...] = kr.astype(BF16)
        ckv = ck[:, MLA_Q_RANK:MLA_Q_RANK + MLA_KV_RANK]
        ckv = (ckv * _seg_rinv(ckv, LANES) * kvn_ref_[...]).astype(BF16)
        kv = _dot(ckv, wukv_ref[...])
        kn = kv[:, 0:wn]
        okn_ref[...] = (kn * _seg_rinv(kn, MLA_NOPE) * kgn_ref[...]).astype(BF16)
        ov_ref[...] = kv[:, wn:].astype(BF16)
        if need_q:
            cq = ck[:, 0:MLA_Q_RANK]
            rq = lax.rsqrt(jnp.mean(cq * cq, axis=-1, keepdims=True) + EPS)
            cq = (cq * rq * qn_ref_[...]).astype(BF16)
            q = _dot(cq, wuq_ref[...])
            qn = q[:, 0:wn]
            oqn_ref[...] = (qn * _seg_rinv(qn, MLA_NOPE) * (qgn_ref[...] * (MLA_SCALE * LOG2E))).astype(BF16)
            qr_raw = q[:, wn:wn + wr]
            rinv = _seg_rinv(qr_raw, MLA_ROPE) * (MLA_SCALE * LOG2E)
            if use_rope:
                qr_par = q[:, wn + wr:]
                cos = _tile_lanes(cos_ref[...], wr)
                sin = _tile_lanes(sin_ref[...], wr)
                qr = rinv * (qr_raw * qgr_ref[...] * cos + qr_par * qgrp_ref[...] * sin)
            else:
                qr = rinv * (qr_raw * qgr_ref[...])
            oqr_ref[...] = qr.astype(BF16)

    tile = lambda w: pl.BlockSpec((t, w), lambda i: (i, 0))
    in_specs = [tile(d), _full((1, d)), _mod_spec(st, SC1, d), _mod_spec(st, SH1, d),
                _resident(w_in.shape), _full(q_norm.shape), _full(kv_norm.shape),
                _resident(w_uq.shape), _resident(w_ukv.shape),
                _full(qg_n.shape), _full(qg_r.shape), _full(qg_rp.shape),
                _full(kg_n.shape), _full(kg_r.shape), _full(kg_rp.shape)]
    args = [x, gain, mods, mods, w_in, q_norm, kv_norm, w_uq, w_ukv, qg_n, qg_r, qg_rp, kg_n, kg_r, kg_rp]
    if use_rope:
        tps = st.tiles_per_seg
        in_specs += [pl.BlockSpec((t, LANES), lambda i: (i % tps, 0))] * 2
        args += list(rope)
    out_specs, out_shape = [], []
    if need_q:
        out_specs += [tile(wn), tile(wr)]
        out_shape += [SDS((n, wn), BF16), SDS((n, wr), BF16)]
    out_specs += [tile(wn), tile(MLA_ROPE), tile(h * MLA_V)]
    out_shape += [SDS((n, wn), BF16), SDS((n, MLA_ROPE), BF16), SDS((n, h * MLA_V), BF16)]
    outs = pl.pallas_call(
        kern, grid=(st.tiles,), in_specs=in_specs, out_specs=out_specs, out_shape=out_shape,
        name="mla_proj", compiler_params=_params(("parallel",), 48))(*args)
    if need_q:
        return tuple(outs)
    return (None, None) + tuple(outs)


def _mla_attn_fast_call(batch, lq, tq, q, kv_sets, shift):
    qn, qr = q
    lk_total = sum(s[3] for s in kv_sets)
    hp = 2 if lk_total > 1024 else 8
    n_hp = MLA_HEADS // hp
    nq = lq // tq
    nsets = len(kv_sets)
    lengths = [s[3] for s in kv_sets]
    lk = sum(lengths)
    kw = 2 * LANES
    pad = kw - MLA_NOPE - MLA_ROPE

    def kern(*refs):
        sh_ref, qn_ref, qr_ref = refs[:3]
        sets = [refs[3 + 3 * s:6 + 3 * s] for s in range(nsets)]
        o_ref, kq_ref, vq_ref = refs[-3:]

        @pl.when(pl.program_id(2) == 0)
        def _():
            lane_k = lax.broadcasted_iota(jnp.int32, (lk, pad), 1)
            lane_v = lax.broadcasted_iota(jnp.int32, (lk, kw - MLA_V), 1)
            for hh in range(hp):
                off = 0
                for (kn_ref, kr_ref, v_ref), length in zip(sets, lengths):
                    kq_ref[hh, off:off + length, 0:MLA_NOPE] = kn_ref[:, hh * MLA_NOPE:(hh + 1) * MLA_NOPE]
                    kq_ref[hh, off:off + length, MLA_NOPE:MLA_NOPE + MLA_ROPE] = kr_ref[...]
                    vq_ref[hh, off:off + length, 0:MLA_V] = v_ref[:, hh * MLA_V:(hh + 1) * MLA_V]
                    off += length
                kq_ref[hh, :, MLA_NOPE + MLA_ROPE:] = jnp.where(lane_k == 0, -sh_ref[:, 0:pad], 0.0).astype(BF16)
                vq_ref[hh, :, MLA_V:] = jnp.where(lane_v == 0, 1.0, 0.0).astype(BF16)

        lane_q = lax.broadcasted_iota(jnp.int32, (tq, pad), 1)
        one = jnp.where(lane_q == 0, 1.0, 0.0).astype(BF16)
        for hh in range(hp):
            qc = jnp.concatenate([qn_ref[:, hh * MLA_NOPE:(hh + 1) * MLA_NOPE],
                                  qr_ref[:, hh * MLA_ROPE:(hh + 1) * MLA_ROPE], one], axis=1)
            p = jnp.exp2(_nt(qc, kq_ref[hh]).astype(BF16))
            acc = _dot(p, vq_ref[hh])
            o_ref[:, hh * MLA_V:(hh + 1) * MLA_V] = (acc[:, 0:MLA_V] / acc[:, MLA_V:MLA_V + 1]).astype(BF16)

    in_specs = [_full((1, LANES)),
                pl.BlockSpec((tq, hp * MLA_NOPE), lambda b, h, i: (b * nq + i, h)),
                pl.BlockSpec((tq, hp * MLA_ROPE), lambda b, h, i: (b * nq + i, h))]
    args = [shift, qn, qr]
    for kn, kr, v, length in kv_sets:
        in_specs += [pl.BlockSpec((length, hp * MLA_NOPE), lambda b, h, i: (b, h)),
                     pl.BlockSpec((length, MLA_ROPE), lambda b, h, i: (b, 0)),
                     pl.BlockSpec((length, hp * MLA_V), lambda b, h, i: (b, h))]
        args += [kn, kr, v]
    return pl.pallas_call(
        kern, grid=(batch, n_hp, nq), in_specs=in_specs,
        out_specs=pl.BlockSpec((tq, hp * MLA_V), lambda b, h, i: (b * nq + i, h)),
        out_shape=SDS((batch * lq, MLA_HEADS * MLA_V), BF16),
        scratch_shapes=[pltpu.VMEM((hp, lk, kw), BF16), pltpu.VMEM((hp, lk, kw), BF16)],
        name="mla_attn_fast",
        compiler_params=_params(("parallel", "parallel", "arbitrary"), 56))(*args)


def _mla_attn_call(batch, lq, tq, q, kv_sets):
    qn, qr = q
    hp = 2
    n_hp = MLA_HEADS // hp
    nq = lq // tq
    nsets = len(kv_sets)

    def kern(*refs):
        qn_ref, qr_ref = refs[:2]
        o_ref = refs[-1]
        sets = [refs[2 + 3 * s:5 + 3 * s] for s in range(nsets)]
        for hh in range(hp):
            qn_h = qn_ref[:, hh * MLA_NOPE:(hh + 1) * MLA_NOPE]
            qr_h = qr_ref[:, hh * MLA_ROPE:(hh + 1) * MLA_ROPE]
            scores = [_nt(qn_h, kn_ref[:, hh * MLA_NOPE:(hh + 1) * MLA_NOPE]) + _nt(qr_h, kr_ref[...])
                      for kn_ref, kr_ref, _ in sets]
            m = functools.reduce(jnp.maximum, [jnp.max(s, axis=-1, keepdims=True) for s in scores])
            ps = [jnp.exp2(s - m) for s in scores]
            den = functools.reduce(lambda a, b: a + b, [jnp.sum(p, axis=-1, keepdims=True) for p in ps])
            acc = functools.reduce(lambda a, b: a + b, [
                _dot(p.astype(BF16), v_ref[:, hh * MLA_V:(hh + 1) * MLA_V])
                for p, (_, _, v_ref) in zip(ps, sets)])
            o_ref[:, hh * MLA_V:(hh + 1) * MLA_V] = (acc / den).astype(BF16)

    in_specs = [pl.BlockSpec((tq, hp * MLA_NOPE), lambda b, h, i: (b * nq + i, h)),
                pl.BlockSpec((tq, hp * MLA_ROPE), lambda b, h, i: (b * nq + i, h))]
    args = [qn, qr]
    for kn, kr, v, length in kv_sets:
        in_specs += [pl.BlockSpec((length, hp * MLA_NOPE), lambda b, h, i: (b, h)),
                     pl.BlockSpec((length, MLA_ROPE), lambda b, h, i: (b, 0)),
                     pl.BlockSpec((length, hp * MLA_V), lambda b, h, i: (b, h))]
        args += [kn, kr, v]
    return pl.pallas_call(
        kern, grid=(batch, n_hp, nq), in_specs=in_specs,
        out_specs=pl.BlockSpec((tq, hp * MLA_V), lambda b, h, i: (b * nq + i, h)),
        out_shape=SDS((batch * lq, MLA_HEADS * MLA_V), BF16), name="mla_attn",
        compiler_params=_params(("parallel", "parallel", "arbitrary"), 56))(*args)


def _mla_layer(streams, xs, mods, gain, p, rope, ctx_next):
    st_c, st_x = streams
    xc, xx = xs
    h = MLA_HEADS
    w_in, q_norm, kv_norm, w_uq, w_ukv, q_gain, k_gain, w_out = p
    par = _rope_partner_perm(1)
    kr_cols = w_in[:, MLA_Q_RANK + MLA_KV_RANK:]
    w_in_p = jnp.concatenate([w_in, kr_cols[:, par]], axis=1).astype(BF16)
    wq = w_uq.reshape(MLA_Q_RANK, h, MLA_NOPE + MLA_ROPE)
    wq_n = wq[:, :, :MLA_NOPE].reshape(MLA_Q_RANK, h * MLA_NOPE)
    wq_r = wq[:, :, MLA_NOPE:]
    w_uq_p = jnp.concatenate([wq_n, wq_r.reshape(MLA_Q_RANK, h * MLA_ROPE),
                              wq_r[:, :, par].reshape(MLA_Q_RANK, h * MLA_ROPE)], axis=1).astype(BF16)
    wkv = w_ukv.reshape(MLA_KV_RANK, h, MLA_NOPE + MLA_V)
    w_ukv_p = jnp.concatenate([wkv[:, :, :MLA_NOPE].reshape(MLA_KV_RANK, h * MLA_NOPE),
                               wkv[:, :, MLA_NOPE:].reshape(MLA_KV_RANK, h * MLA_V)], axis=1).astype(BF16)
    qg_r = q_gain[MLA_NOPE:]
    kg_r = k_gain[MLA_NOPE:]
    wts = (w_in_p, q_norm.reshape(1, -1), kv_norm.reshape(1, -1), w_uq_p, w_ukv_p,
           jnp.tile(q_gain[:MLA_NOPE], h).reshape(1, -1), jnp.tile(qg_r, h).reshape(1, -1),
           jnp.tile(qg_r[par], h).reshape(1, -1), jnp.tile(k_gain[:MLA_NOPE], h).reshape(1, -1),
           kg_r.reshape(1, -1), kg_r[par].reshape(1, -1))
    w_out_b = w_out.astype(BF16)
    qn_c, qr_c, kn_c, kr_c, v_c = _mla_proj_call(st_c.with_tile(PROJ_TILE), xc, mods, gain, wts, None, ctx_next)
    qn_x, qr_x, kn_x, kr_x, v_x = _mla_proj_call(st_x.with_tile(PROJ_TILE), xx, mods, gain, wts, rope, True)
    b = st_x.batch
    amax = lambda g: jnp.max(jnp.abs(g))
    bound = LOG2E * MLA_SCALE * (MLA_NOPE * amax(q_gain[:MLA_NOPE]) * amax(k_gain[:MLA_NOPE])
                                 + MLA_ROPE * amax(qg_r) * amax(kg_r))
    shift = bound * SHIFT_MARGIN + 1.0
    shift_row = jnp.full((1, LANES), shift, F32)

    def attend(lq, tq, q, kv_sets):
        return lax.cond(shift <= MAX_FIXED_SHIFT,
                        lambda: _mla_attn_fast_call(b, lq, tq, q, kv_sets, shift_row),
                        lambda: _mla_attn_call(b, lq, min(tq, 512), q, kv_sets))

    a_x = attend(st_x.seg, min(st_x.seg, 2048), (qn_x, qr_x), [(kn_c, kr_c, v_c, st_c.seg), (kn_x, kr_x, v_x, st_x.seg)])
    out_x = _outproj_call(st_x, a_x, mods, w_out_b, "mla_out")
    out_c = None
    if ctx_next:
        a_c = attend(st_c.seg, st_c.seg, (qn_c, qr_c), [(kn_c, kr_c, v_c, st_c.seg)])
        out_c = _outproj_call(st_c, a_c, mods, w_out_b, "mla_out")
    return [out_c, out_x]


def _gqa_proj_call(st, x, mods, gain, wts, rope, need_q):
    n, d = x.shape
    t = st.tile
    wq, wk = GQA_Q_HEADS * GQA_HEAD_DIM, GQA_KV_HEADS * GQA_HEAD_DIM
    w_in, qg, qgp, kg, kgp = wts
    use_rope = rope is not None

    def kern(*refs):
        refs = list(refs)
        x_ref, g_ref, sc_ref, sh_ref, w_ref, qg_ref, qgp_ref, kg_ref, kgp_ref = refs[:9]
        rest = refs[9:]
        if use_rope:
            cos_ref, sin_ref = rest[:2]
            rest = rest[2:]
        if need_q:
            oq_ref = rest[0]
            rest = rest[1:]
        ok_ref, ov_ref = rest
        hm = _modnorm(x_ref[...], g_ref[...], sc_ref[...], sh_ref[...]).astype(BF16)
        kvp = _dot(hm, w_ref[:, 0:3 * wk])
        k_raw = kvp[:, 0:wk]
        k_rinv = _seg_rinv(k_raw, GQA_HEAD_DIM)
        ov_ref[...] = kvp[:, wk:2 * wk].astype(BF16)
        if use_rope:
            cos_k, sin_k = _tile_lanes(cos_ref[...], wk), _tile_lanes(sin_ref[...], wk)
            k = k_rinv * (k_raw * kg_ref[...] * cos_k + kvp[:, 2 * wk:] * kgp_ref[...] * sin_k)
        else:
            k = k_rinv * (k_raw * kg_ref[...])
        ok_ref[...] = k.astype(BF16)
        if need_q:
            qp = _dot(hm, w_ref[:, 3 * wk:])
            q_raw = qp[:, 0:wq]
            rinv = _seg_rinv(q_raw, GQA_HEAD_DIM) * (GQA_SCALE * LOG2E)
            if use_rope:
                cos_q, sin_q = _tile_lanes(cos_ref[...], wq), _tile_lanes(sin_ref[...], wq)
                q = rinv * (q_raw * qg_ref[...] * cos_q + qp[:, wq:] * qgp_ref[...] * sin_q)
            else:
                q = rinv * (q_raw * qg_ref[...])
            oq_ref[...] = q.astype(BF16)

    tile = lambda w: pl.BlockSpec((t, w), lambda i: (i, 0))
    in_specs = [tile(d), _full((1, d)), _mod_spec(st, SC1, d), _mod_spec(st, SH1, d),
                _resident(w_in.shape), _full(qg.shape), _full(qgp.shape), _full(kg.shape), _full(kgp.shape)]
    args = [x, gain, mods, mods, w_in, qg, qgp, kg, kgp]
    if use_rope:
        tps = st.tiles_per_seg
        in_specs += [pl.BlockSpec((t, LANES), lambda i: (i % tps, 0))] * 2
        args += list(rope)
    out_specs, out_shape = [], []
    if need_q:
        out_specs.append(tile(wq))
        out_shape.append(SDS((n, wq), BF16))
    out_specs += [tile(wk), tile(wk)]
    out_shape += [SDS((n, wk), BF16)] * 2
    outs = pl.pallas_call(
        kern, grid=(st.tiles,), in_specs=in_specs, out_specs=out_specs, out_shape=out_shape,
        name="gqa_proj", compiler_params=_params(("parallel",), 48))(*args)
    return tuple(outs) if need_q else (None,) + tuple(outs)


def _gqa_window_call(batch, length, lc, q, k, v, kc, vc, sink):
    nb = length // BLOCK
    hd = GQA_HEAD_DIM
    wq, wk = GQA_Q_HEADS * hd, GQA_KV_HEADS * hd
    rows = GQA_GROUP * BLOCK

    def kern(sink_ref, q_ref, kp_ref, k0_ref, kn_ref, vp_ref, v0_ref, vn_ref, kc_ref, vc_ref, o_ref):
        nblk = pl.program_id(1)
        r = lax.broadcasted_iota(jnp.int32, (rows, 3 * BLOCK), 0) & (BLOCK - 1)
        c = lax.broadcasted_iota(jnp.int32, (rows, 3 * BLOCK), 1)
        valid = (c >= r + BLOCK - WINDOW) & (c <= r + BLOCK + WINDOW)
        valid = valid & ((c >= BLOCK) | (nblk > 0)) & ((c < 2 * BLOCK) | (nblk < nb - 1))
        hrow = lax.broadcasted_iota(jnp.int32, (rows, 1), 0) // BLOCK
        for g in range(GQA_KV_HEADS):
            sl = slice(g * hd, (g + 1) * hd)
            qg = jnp.concatenate([q_ref[:, (g * GQA_GROUP + j) * hd:(g * GQA_GROUP + j + 1) * hd]
                                  for j in range(GQA_GROUP)], axis=0)
            kband = jnp.concatenate([kp_ref[:, sl], k0_ref[:, sl], kn_ref[:, sl]], axis=0)
            vband = jnp.concatenate([vp_ref[:, sl], v0_ref[:, sl], vn_ref[:, sl]], axis=0)
            s_c = _nt(qg, kc_ref[:, sl])
            s_b = jnp.where(valid, _nt(qg, kband), -1e30)
            snk = jnp.zeros((rows, 1), F32)
            for j in range(GQA_GROUP):
                snk = jnp.where(hrow == j, sink_ref[g * GQA_GROUP + j], snk)
            m = jnp.maximum(jnp.maximum(jnp.max(s_c, axis=-1, keepdims=True),
                                        jnp.max(s_b, axis=-1, keepdims=True)), snk)
            p_c = jnp.exp2(s_c - m)
            p_b = jnp.exp2(s_b - m)
            den = (jnp.sum(p_c, axis=-1, keepdims=True) + jnp.sum(p_b, axis=-1, keepdims=True)
                   + jnp.exp2(snk - m))
            o = (_dot(p_c.astype(BF16), vc_ref[:, sl]) + _dot(p_b.astype(BF16), vband)) / den
            o_ref[:, g * GQA_GROUP * hd:(g + 1) * GQA_GROUP * hd] = jnp.concatenate(
                [o[j * BLOCK:(j + 1) * BLOCK, :] for j in range(GQA_GROUP)], axis=-1).astype(BF16)

    blk = lambda f: pl.BlockSpec((BLOCK, wk), f)
    prev_ = lambda b, i: (b * nb + jnp.maximum(i - 1, 0), 0)
    cur_ = lambda b, i: (b * nb + i, 0)
    next_ = lambda b, i: (b * nb + jnp.minimum(i + 1, nb - 1), 0)
    ctx_spec = pl.BlockSpec((lc, wk), lambda b, i: (b, 0))
    return pl.pallas_call(
        kern, grid=(batch, nb),
        in_specs=[pl.BlockSpec(memory_space=pltpu.SMEM),
                  pl.BlockSpec((BLOCK, wq), cur_), blk(prev_), blk(cur_), blk(next_),
                  blk(prev_), blk(cur_), blk(next_), ctx_spec, ctx_spec],
        out_specs=pl.BlockSpec((BLOCK, wq), cur_),
        out_shape=SDS((batch * length, wq), BF16), name="gqa_window",
        compiler_params=_params(("parallel", "arbitrary"), 40))(sink, q, k, k, k, v, v, v, kc, vc)


def _gqa_window_fast_call(batch, length, lc, q, k, v, kc, vc, sink2, shift):
    nb = length // BLOCK
    hd = GQA_HEAD_DIM
    wq, wk = GQA_Q_HEADS * hd, GQA_KV_HEADS * hd
    rows = GQA_GROUP * BLOCK

    def kern(sink_ref, shift_ref, q_ref, kp_ref, k0_ref, kn_ref, vp_ref, v0_ref, vn_ref, kc_ref, vc_ref, o_ref):
        nblk = pl.program_id(1)
        r = lax.broadcasted_iota(jnp.int32, (rows, 3 * BLOCK), 0) & (BLOCK - 1)
        c = lax.broadcasted_iota(jnp.int32, (rows, 3 * BLOCK), 1)
        valid = (c >= r + BLOCK - WINDOW) & (c <= r + BLOCK + WINDOW)
        valid = valid & ((c >= BLOCK) | (nblk > 0)) & ((c < 2 * BLOCK) | (nblk < nb - 1))
        hrow = lax.broadcasted_iota(jnp.int32, (rows, 1), 0) // BLOCK
        lane0_q = lax.broadcasted_iota(jnp.int32, (BLOCK, hd), 1) == 0

        def with_one(x):
            lane0 = lax.broadcasted_iota(jnp.int32, (x.shape[0], hd), 1) == 0
            return jnp.concatenate([x, jnp.where(lane0, 1.0, 0.0).astype(BF16)], axis=1)

        for g in range(GQA_KV_HEADS):
            sl = slice(g * hd, (g + 1) * hd)
            heads = [g * GQA_GROUP + j for j in range(GQA_GROUP)]
            qg = jnp.concatenate(
                [jnp.concatenate([q_ref[:, h * hd:(h + 1) * hd],
                                  jnp.where(lane0_q, -shift_ref[h], 0.0).astype(BF16)], axis=1)
                 for h in heads], axis=0)
            kband = with_one(jnp.concatenate([kp_ref[:, sl], k0_ref[:, sl], kn_ref[:, sl]], axis=0))
            vband = with_one(jnp.concatenate([vp_ref[:, sl], v0_ref[:, sl], vn_ref[:, sl]], axis=0))
            p_c = jnp.exp2(_nt(qg, with_one(kc_ref[:, sl])).astype(BF16))
            p_b = jnp.exp2(jnp.where(valid, _nt(qg, kband), -1e30).astype(BF16))
            acc = _dot(p_c, with_one(vc_ref[:, sl])) + _dot(p_b, vband)
            snk = jnp.zeros((rows, 1), F32)
            for j, h in enumerate(heads):
                snk = jnp.where(hrow == j, sink_ref[h] - shift_ref[h], snk)
            o = acc[:, 0:hd] / (acc[:, hd:hd + 1] + jnp.exp2(snk))
            o_ref[:, g * GQA_GROUP * hd:(g + 1) * GQA_GROUP * hd] = jnp.concatenate(
                [o[j * BLOCK:(j + 1) * BLOCK, :] for j in range(GQA_GROUP)], axis=-1).astype(BF16)

    blk = lambda f: pl.BlockSpec((BLOCK, wk), f)
    prev_ = lambda b, i: (b * nb + jnp.maximum(i - 1, 0), 0)
    cur_ = lambda b, i: (b * nb + i, 0)
    next_ = lambda b, i: (b * nb + jnp.minimum(i + 1, nb - 1), 0)
    ctx_spec = pl.BlockSpec((lc, wk), lambda b, i: (b, 0))
    smem = pl.BlockSpec(memory_space=pltpu.SMEM)
    return pl.pallas_call(
        kern, grid=(batch, nb),
        in_specs=[smem, smem, pl.BlockSpec((BLOCK, wq), cur_), blk(prev_), blk(cur_), blk(next_),
                  blk(prev_), blk(cur_), blk(next_), ctx_spec, ctx_spec],
        out_specs=pl.BlockSpec((BLOCK, wq), cur_),
        out_shape=SDS((batch * length, wq), BF16), name="gqa_window_fast",
        compiler_params=_params(("parallel", "arbitrary"), 40))(sink2, shift, q, k, k, k, v, v, v, kc, vc)


def _gqa_layer(streams, xs, mods, gain, p, rope, ctx_next):
    assert not ctx_next, "the windowed-GQA mixer is only implemented as the last layer"
    st_c, st_x = streams
    xc, xx = xs
    w_in, q_gain, k_gain, sink, w_out = p
    wq, wk = GQA_Q_HEADS * GQA_HEAD_DIM, GQA_KV_HEADS * GQA_HEAD_DIM
    w_q, w_k, w_v = w_in[:, :wq], w_in[:, wq:wq + wk], w_in[:, wq + wk:]
    w_in_p = jnp.concatenate([w_k, w_v, w_k[:, _rope_partner_perm(GQA_KV_HEADS)],
                              w_q, w_q[:, _rope_partner_perm(GQA_Q_HEADS)]], axis=1).astype(BF16)
    par = _rope_partner_perm(1)
    wts = (w_in_p, jnp.tile(q_gain, GQA_Q_HEADS).reshape(1, -1), jnp.tile(q_gain[par], GQA_Q_HEADS).reshape(1, -1),
           jnp.tile(k_gain, GQA_KV_HEADS).reshape(1, -1), jnp.tile(k_gain[par], GQA_KV_HEADS).reshape(1, -1))
    _, kc, vc = _gqa_proj_call(st_c.with_tile(PROJ_TILE), xc, mods, gain, wts, None, False)
    q, k, v = _gqa_proj_call(st_x.with_tile(PROJ_TILE), xx, mods, gain, wts, rope, True)
    sink2 = sink * LOG2E
    bound = LOG2E * GQA_SCALE * GQA_HEAD_DIM * jnp.max(jnp.abs(q_gain)) * jnp.max(jnp.abs(k_gain))
    shift = jnp.maximum(bound * SHIFT_MARGIN + 1.0, sink2).astype(BF16).astype(F32)
    args = (st_x.batch, st_x.seg, st_c.seg, q, k, v, kc, vc, sink2)
    a = lax.cond(jnp.max(shift) <= MAX_FIXED_SHIFT,
                 lambda: _gqa_window_fast_call(*args, shift),
                 lambda: _gqa_window_call(*args))
    return [None, _outproj_call(st_x, a, mods, w_out.astype(BF16), "gqa_out")]


def _cast_experts_call(w, layer):
    _, ne, a, b = w.shape
    eb = 4

    def kern(w_ref, o_ref):
        o_ref[...] = w_ref[...].astype(BF16)

    return pl.pallas_call(
        kern, grid=(ne // eb,),
        in_specs=[pl.BlockSpec((None, eb, a, b), lambda i: (layer, i, 0, 0))],
        out_specs=pl.BlockSpec((eb, a, b), lambda i: (i, 0, 0)),
        out_shape=SDS((ne, a, b), BF16), name="cast_experts",
        compiler_params=_params(("parallel",), 32))(w)


def _route(logits_t, bias_col):
    scores = _sigmoid(logits_t)
    biased = scores + bias_col
    rows = [biased[e:e + 1, :] for e in range(N_EXPERTS)]
    srow = [scores[e:e + 1, :] for e in range(N_EXPERTS)]
    epg = EXPERTS_PER_GROUP
    gscore = []
    for g in range(N_EXPERT_GROUPS):
        v = rows[g * epg:(g + 1) * epg]
        pair = [v[a] + v[b] for a in range(epg) for b in range(a + 1, epg)]
        gscore.append(functools.reduce(jnp.maximum, pair))
    ind, wloc = [], [None] * epg
    for g in range(N_EXPERT_GROUPS):
        best = None
        for g2 in range(N_EXPERT_GROUPS):
            if g2 == g:
                continue
            cnd = gscore[g] > gscore[g2] if g2 < g else gscore[g] >= gscore[g2]
            best = cnd if best is None else best & cnd
        ind.append(jnp.where(best, 1.0, 0.0))
        for j in range(epg):
            e = g * epg + j
            rank = None
            for e2 in range(g * epg, (g + 1) * epg):
                if e2 == e:
                    continue
                ahead = rows[e2] >= rows[e] if e2 < e else rows[e2] > rows[e]
                one = jnp.where(ahead, 1.0, 0.0)
                rank = one if rank is None else rank + one
            w = jnp.where(best & (rank < 2.0), srow[e], 0.0)
            wloc[j] = w if wloc[j] is None else wloc[j] + w
    den = functools.reduce(lambda a, b: a + b, wloc)
    return ind, [w / den for w in wloc]


MOE_TILE = 1024
MOE_CHUNK = 128


def _moe_call(st, x_in, y_in, mods, gain, router_wt, router_bias, w_gate, w_up, w_down):
    n, d = x_in.shape
    t = min(MOE_TILE, n)
    r = MOE_CHUNK
    ne, _, ff = w_gate.shape
    epg, ng = EXPERTS_PER_GROUP, N_EXPERT_GROUPS
    nch = t // r + ng - 1
    tiles = n // t
    wd2 = w_down.reshape(ne * ff, d)

    def kern(x_ref, y_ref, g_ref, sc_ref, sh_ref, g2_ref, rw_ref, rb_ref, wg_ref, wu_ref, wd_ref, o_ref,
             hb_ref, oh_ref, xg_ref):
        hb_ref[...] = _modnorm(x_ref[...] + y_ref[...].astype(F32), g_ref[...], sc_ref[...], sh_ref[...]).astype(BF16)
        ind, wloc = _route(_nt(rw_ref[...], hb_ref[...]), rb_ref[...])
        ind8 = jnp.concatenate(ind + [jnp.zeros((8 - ng, t), F32)], axis=0)
        upper = jnp.where(lax.broadcasted_iota(jnp.int32, (LANES, LANES), 0)
                          < lax.broadcasted_iota(jnp.int32, (LANES, LANES), 1), 1.0, 0.0).astype(BF16)
        blocks, before = [], jnp.zeros((8, 1), F32)
        for k in range(t // LANES):
            blk = ind8[:, k * LANES:(k + 1) * LANES]
            blocks.append(_dot(blk.astype(BF16), upper) + before)
            before = before + jnp.sum(blk, axis=1, keepdims=True)
        pos = jnp.concatenate(blocks, axis=1)
        count = [jnp.sum(ind[g]).astype(jnp.int32) for g in range(ng)]
        first = [jnp.int32(0)]
        for g in range(ng):
            first.append(first[g] + (count[g] + (r - 1)) // r)
        slot = functools.reduce(lambda a, b: a + b, [
            ind[g] * (pos[g:g + 1, :] + (first[g] * r).astype(F32)) for g in range(ng)])
        w_hi = [w.astype(BF16) for w in wloc]
        w_lo = [(w - h.astype(F32)).astype(BF16) for w, h in zip(wloc, w_hi)]
        wst = jnp.concatenate(w_hi + w_lo, axis=0)
        row_id = lax.broadcasted_iota(jnp.int32, (r, t), 0).astype(F32)
        onehot = lambda c: jnp.where(slot == row_id + float(c * r), 1.0, 0.0).astype(BF16)
        last = nch - 1
        for c in range(last):
            oh_ref[c * r:(c + 1) * r, :] = onehot(c)
        xg_ref[...] = _dot(oh_ref[...], hb_ref[...]).astype(BF16)
        wr_all = _nt(wst, oh_ref[...]).T

        def chunk_ffn(c, xg, wr):
            grp = functools.reduce(lambda a, b: a + b,
                                   [(c >= first[g]).astype(jnp.int32) for g in range(1, ng)])
            acts = []
            for j in range(epg):
                e = grp * epg + j
                gt = _dot(xg, wg_ref[e])
                up = _dot(xg, wu_ref[e])
                acts.append((gt * _sigmoid(gt) * up * (wr[:, j:j + 1] + wr[:, epg + j:epg + j + 1])).astype(BF16))
            wd_g = wd_ref[pl.ds(pl.multiple_of(grp * (epg * ff), epg * ff), epg * ff), :]
            return _dot(jnp.concatenate(acts, axis=1), wd_g).astype(BF16)

        for c in range(last):
            rows = slice(c * r, (c + 1) * r)

            def run(c=c, rows=rows):
                xg_ref[rows, :] = chunk_ffn(c, xg_ref[rows, :], wr_all[c * r:(c + 1) * r, :])

            if c < t // r:
                run()
            else:
                pl.when(c < first[ng])(run)

        tdot = lambda a, b: lax.dot_general(a, b, (((0,), (0,)), ((), ())), preferred_element_type=F32)
        o_ref[...] = x_ref[...] + y_ref[...].astype(F32) + g2_ref[...] * tdot(oh_ref[...], xg_ref[...])

        @pl.when(last < first[ng])
        def _():
            oh = onehot(last)
            out = chunk_ffn(last, _dot(oh, hb_ref[...]).astype(BF16), _nt(wst, oh).T)
            o_ref[...] += g2_ref[...] * tdot(oh, out)

    row = (lambda i: st.batch) if st.is_ctx else (lambda i: i // (st.seg // t))
    mod = lambda chunk: pl.BlockSpec((None, 1, d), lambda i: (row(i), 0, chunk))
    return pl.pallas_call(
        kern, grid=(tiles,),
        in_specs=[pl.BlockSpec((t, d), lambda i: (i, 0)),
                  pl.BlockSpec((t, d), lambda i: (i, 0)),
                  _full((1, d)), mod(SC2), mod(SH2), mod(G2), _full((ne, d)), _full((ne, 1)),
                  _resident((ne, d, ff)), _resident((ne, d, ff)), _resident((ne * ff, d))],
        out_specs=pl.BlockSpec((t, d), lambda i: (i, 0)),
        out_shape=SDS((n, d), F32),
        scratch_shapes=[pltpu.VMEM((t, d), BF16), pltpu.VMEM(((nch - 1) * r, t), BF16),
                        pltpu.VMEM(((nch - 1) * r, d), BF16)],
        name="moe", compiler_params=_params(("parallel",), 58))(
            x_in, y_in, gain, mods, mods, mods, router_wt, router_bias, w_gate, w_up, wd2)


def kernel(x, c, ctx, c_ctx, w_ada, b_ada, norm_mix, norm_ffn, fourier_w_out, conv_w_in, conv_w, conv_w_out,
           mla_w_in, mla_q_norm, mla_kv_norm, mla_w_uq, mla_w_ukv, mla_q_gain, mla_k_gain, mla_w_out,
           gqa_w_in, gqa_q_gain, gqa_k_gain, gqa_sink, gqa_w_out, router_w, router_bias,
           moe_w_gate, moe_w_up, moe_w_down):
    b, l, d = x.shape
    lc = ctx.shape[1]
    depth = w_ada.shape[0]
    st_c = Stream(b * lc, lc, min(b * lc, TOKEN_TILE), b, True)
    st_x = Stream(b * l, l, min(l, TOKEN_TILE), b, False)
    streams = [st_c, st_x]
    r8 = -(-(b + 1) // 8) * 8
    cvec = jnp.concatenate([c, c_ctx[None, :], jnp.zeros((r8 - b - 1, d), F32)], axis=0)
    mods_all = _ada_call(cvec, w_ada, b_ada).reshape(depth, r8, 1, 6 * d)
    rope = _rope_tables(l)
    router_wt = router_w.T.astype(BF16)
    router_b = router_bias.reshape(-1, 1)
    xs = [ctx.reshape(b * lc, d), x.reshape(b * l, d)]
    for i in range(depth):
        kind, j = i % 4, i // 4
        ctx_next = i < depth - 1
        mods = mods_all[i]
        gain = norm_mix[i].reshape(1, d)
        if not (ctx_next or kind >= 2):
            xs[0] = None
        if kind == 0:
            ys = _fourier_layer(streams, xs, mods, gain, fourier_w_out[j])
        elif kind == 1:
            ys = _conv_layer(streams, xs, mods, gain, conv_w_in[j], conv_w[j], conv_w_out[j])
        elif kind == 2:
            ys = _mla_layer(streams, xs, mods, gain,
                            (mla_w_in[j], mla_q_norm[j], mla_kv_norm[j], mla_w_uq[j], mla_w_ukv[j],
                             mla_q_gain[j], mla_k_gain[j], mla_w_out[j]), rope, ctx_next)
        else:
            ys = _gqa_layer(streams, xs, mods, gain,
                            (gqa_w_in[j], gqa_q_gain[j], gqa_k_gain[j], gqa_sink[j], gqa_w_out[j]),
                            rope, ctx_next)
        if not ctx_next:
            ys[0] = None
        gain2 = norm_ffn[i].reshape(1, d)
        wg, wu, wd = (_cast_experts_call(w, i) for w in (moe_w_gate, moe_w_up, moe_w_down))
        xs = [None if y is None else _moe_call(st, x, y, mods, gain2, router_wt, router_b, wg, wu, wd)
              for st, x, y in zip(streams, xs, ys)]
    return xs[1].reshape(b, l, d)
```

```python
import functools

import numpy as np
import jax
import jax.numpy as jnp
from jax import lax
from jax.experimental import pallas as pl
from jax.experimental.pallas import tpu as pltpu

F32, BF16 = jnp.float32, jnp.bfloat16
SDS = jax.ShapeDtypeStruct

EPS = 1e-6
GRID_W = 64
ROPE_THETA = 10000.0
N_FOURIER_GROUPS = 4
MLA_HEADS, MLA_Q_RANK, MLA_KV_RANK = 16, 256, 128
MLA_NOPE, MLA_ROPE, MLA_V = 128, 64, 128
MLA_SCALE = (MLA_NOPE + MLA_ROPE) ** -0.5
GQA_Q_HEADS, GQA_KV_HEADS, GQA_HEAD_DIM = 16, 4, 64
GQA_GROUP = GQA_Q_HEADS // GQA_KV_HEADS
GQA_SCALE = GQA_HEAD_DIM ** -0.5
WINDOW = 128
BLOCK = 128
N_EXPERTS, N_EXPERT_GROUPS, EXPERT_FF = 16, 4, 256
EXPERTS_PER_GROUP = N_EXPERTS // N_EXPERT_GROUPS
ROPE_DIM = 64
LOG2E = 1.4426950408889634
MAX_FIXED_SHIFT = 50.0
SHIFT_MARGIN = 1.02
TOKEN_TILE = 1024
PROJ_TILE = 512

V7X_VMEM_BYTES = 64 * 1024 * 1024
LANES = 128
SH1, SC1, G1, SH2, SC2, G2 = range(6)


def _params(sem, vmem_mb):
    return pltpu.CompilerParams(dimension_semantics=sem, vmem_limit_bytes=vmem_mb * 1024 * 1024)


def _sigmoid(v):
    return 1.0 / (1.0 + jnp.exp(-v))


def _modnorm(x, gain, sc, sh):
    ms = jnp.mean(x * x, axis=-1, keepdims=True)
    return x * lax.rsqrt(ms + EPS) * (gain * (1.0 + sc)) + sh


def _nt(a, b):
    return lax.dot_general(a, b, (((1,), (1,)), ((), ())), preferred_element_type=F32)


def _dot(a, b):
    return jnp.dot(a, b, preferred_element_type=F32)


class Stream:
    def __init__(self, n, seg, tile, batch, is_ctx):
        self.n, self.seg, self.tile, self.batch, self.is_ctx = n, seg, tile, batch, is_ctx
        self.tiles = n // tile
        self.tiles_per_seg = seg // tile

    def mod_row(self, t):
        return self.batch if self.is_ctx else t // self.tiles_per_seg

    def seg_row(self, b):
        return self.batch if self.is_ctx else b

    def with_tile(self, tile):
        return Stream(self.n, self.seg, min(tile, self.tile), self.batch, self.is_ctx)


def _mod_spec(st, chunk, d):
    return pl.BlockSpec((None, 1, d), lambda t: (st.mod_row(t), 0, chunk))


def _full(shape):
    nd = len(shape)
    return pl.BlockSpec(shape, lambda *_: (0,) * nd)


def _resident(shape):
    nd = len(shape)
    return pl.BlockSpec(shape, lambda *_: (0,) * nd, pipeline_mode=pl.Buffered(1))


def _ada_call(cvec, w_ada, b_ada):
    depth, d, d6 = w_ada.shape
    r8 = cvec.shape[0]
    tn = d6 // 4

    def kern(c_ref, w_ref, b_ref, o_ref):
        c = c_ref[...]
        s = (c * _sigmoid(c)).astype(BF16)
        o_ref[...] = _dot(s, w_ref[...].astype(BF16)) + b_ref[...]

    return pl.pallas_call(
        kern, grid=(depth, d6 // tn),
        in_specs=[pl.BlockSpec((r8, d), lambda i, j: (0, 0)),
                  pl.BlockSpec((None, d, tn), lambda i, j: (i, 0, j)),
                  pl.BlockSpec((None, 1, tn), lambda i, j: (i, 0, j))],
        out_specs=pl.BlockSpec((None, r8, tn), lambda i, j: (i, 0, j)),
        out_shape=SDS((depth, r8, d6), F32), name="ada",
        compiler_params=_params(("arbitrary", "arbitrary"), 40))(cvec, w_ada, b_ada.reshape(depth, 1, d6))


def _outproj_call(st, a, mods, w, name):
    n, k = a.shape
    d = w.shape[1]
    t = st.tile

    def kern(a_ref, g1_ref, w_ref, o_ref):
        o_ref[...] = (g1_ref[...] * _dot(a_ref[...], w_ref[...])).astype(BF16)

    return pl.pallas_call(
        kern, grid=(st.tiles,),
        in_specs=[pl.BlockSpec((t, k), lambda i: (i, 0)), _mod_spec(st, G1, d), _resident((k, d))],
        out_specs=pl.BlockSpec((t, d), lambda i: (i, 0)),
        out_shape=SDS((n, d), BF16), name=name,
        compiler_params=_params(("parallel",), 40))(a, mods, w)


def _dft_tables(length, radix):
    lr = length // radix
    m = np.arange(lr)[None, :, None]
    j = np.arange(radix)[:, None, None]
    nn = np.arange(lr)[None, None, :]
    ang = 2.0 * np.pi * (((radix * m + j) * nn) % length) / length
    e = np.concatenate([np.cos(ang), np.sin(ang)], axis=-1) / np.sqrt(length)
    return jnp.asarray(e, dtype=F32).astype(BF16)


def _group_dft_tables(group):
    k = np.arange(group)
    ang = 2.0 * np.pi * ((k[:, None] * k[None, :]) % group) / group
    return (jnp.asarray(np.cos(ang) / np.sqrt(group), dtype=F32),
            jnp.asarray(np.sin(ang) / np.sqrt(group), dtype=F32))


def _fourier_weight_call(w_out):
    d = w_out.shape[0]
    grp = d // N_FOURIER_GROUPS
    cg, sg = _group_dft_tables(grp)

    def kern(cg_ref, sg_ref, w_ref, o_ref):
        w = w_ref[...]
        o_ref[:, :d] = jnp.dot(cg_ref[...], w, preferred_element_type=F32,
                               precision=lax.Precision.HIGHEST).astype(BF16)
        o_ref[:, d:] = jnp.dot(sg_ref[...], w, preferred_element_type=F32,
                               precision=lax.Precision.HIGHEST).astype(BF16)

    return pl.pallas_call(
        kern, grid=(N_FOURIER_GROUPS,),
        in_specs=[_full((grp, grp)), _full((grp, grp)), pl.BlockSpec((grp, d), lambda g: (g, 0))],
        out_specs=pl.BlockSpec((grp, 2 * d), lambda g: (g, 0)),
        out_shape=SDS((d, 2 * d), BF16), name="fourier_w",
        compiler_params=_params(("arbitrary",), 32))(cg, sg, w_out)


def _radix_terms(radix, j):
    real, imag = [], []
    for q in range(radix):
        k = (j * q * (4 // radix)) % 4 if radix > 1 else 0
        if k == 0:
            real.append((1, 0, q)); imag.append((-1, 1, q))
        elif k == 1:
            real.append((-1, 1, q)); imag.append((-1, 0, q))
        elif k == 2:
            real.append((-1, 0, q)); imag.append((1, 1, q))
        else:
            real.append((1, 1, q)); imag.append((1, 0, q))
    return real, imag


def _fourier_mix_call(st, x, mods, gain, wcs, radix):
    n, d = x.shape
    seg = st.seg
    lr = seg // radix
    e = _dft_tables(seg, radix)
    nb = n // seg
    rt = min(seg, 512)
    cblk = 2 * LANES

    def kern(x_ref, g_ref, sc_ref, sh_ref, g1_ref, w_ref, e_ref, o_ref, p_ref, v_ref, *z_refs):
        for i in range(seg // rt):
            rows = slice(i * rt, (i + 1) * rt)
            h = _modnorm(x_ref[rows, :], g_ref[...], sc_ref[...], sh_ref[...]).astype(BF16)
            p_ref[rows, :] = _dot(h, w_ref[...]).astype(BF16)
        for c in range(d // cblk):
            cols = (slice(c * cblk, (c + 1) * cblk), slice(d + c * cblk, d + (c + 1) * cblk))
            for j in range(radix):
                real, imag = _radix_terms(radix, j)

                def comb(terms):
                    acc = None
                    for sgn, part, q in terms:
                        v = p_ref[q * lr:(q + 1) * lr, cols[part]]
                        if acc is None:
                            acc = v if sgn > 0 else -v
                        else:
                            acc = acc + v if sgn > 0 else acc - v
                    return acc

                v_ref[0:lr, :] = comb(real).astype(BF16)
                v_ref[lr:2 * lr, :] = comb(imag).astype(BF16)
                z = _dot(e_ref[j], v_ref[...])
                for k, z_ref in enumerate(z_refs):
                    z_ref[pl.ds(j, lr, stride=radix), :] = z[:, k * LANES:(k + 1) * LANES]
            for k, z_ref in enumerate(z_refs):
                sl = slice(c * cblk + k * LANES, c * cblk + (k + 1) * LANES)
                o_ref[:, sl] = (g1_ref[:, sl] * z_ref[...]).astype(BF16)

    mod = lambda chunk: pl.BlockSpec((None, 1, d), lambda b: (st.seg_row(b), 0, chunk))
    return pl.pallas_call(
        kern, grid=(nb,),
        in_specs=[pl.BlockSpec((seg, d), lambda b: (b, 0)), _full((1, d)), mod(SC1), mod(SH1), mod(G1),
                  _resident((d, 2 * d)), _resident((radix, lr, 2 * lr))],
        out_specs=pl.BlockSpec((seg, d), lambda b: (b, 0)),
        out_shape=SDS((n, d), BF16),
        scratch_shapes=[pltpu.VMEM((seg, 2 * d), BF16), pltpu.VMEM((2 * lr, cblk), BF16)]
        + [pltpu.VMEM((seg, LANES), F32)] * (cblk // LANES),
        name=f"fourier_mix_r{radix}",
        compiler_params=_params(("parallel",), 56))(x, gain, mods, mods, mods, wcs, e)


def _fourier_layer(streams, xs, mods, gain, w_out):
    wcs = _fourier_weight_call(w_out)
    return [None if x is None else _fourier_mix_call(st, x, mods, gain, wcs, 4 if st.seg >= 1024 else 1)
            for st, x in zip(streams, xs)]


def _conv_mix_call(st, x, mods, gain, w_in, conv_w, w_out):
    n, d = x.shape
    seg = st.seg
    nb = n // seg
    rt = min(seg, 512)
    nrt = seg // rt
    halo = 16

    def kern(x_ref, g_ref, sc_ref, sh_ref, g1_ref, win_ref, cw_ref, wout_ref, o_ref, bg_ref, u_ref):
        for i in range(nrt):
            rows = slice(i * rt, (i + 1) * rt)
            h = _modnorm(x_ref[rows, :], g_ref[...], sc_ref[...], sh_ref[...]).astype(BF16)
            bg_ref[rows, :] = _dot(h, win_ref[:, 0:d]).astype(BF16)
            u_ref[rows, :] = (_dot(h, win_ref[:, d:2 * d]) * _dot(h, win_ref[:, 2 * d:3 * d])).astype(BF16)
        row = lax.broadcasted_iota(jnp.int32, (rt, 1), 0)
        zero_row = jnp.zeros((1, d), F32)
        for i in range(nrt):
            rows = slice(i * rt, (i + 1) * rt)
            u = u_ref[rows, :].astype(F32)
            prev_row = zero_row if i == 0 else u_ref[i * rt - halo:i * rt, :].astype(F32)[halo - 1:halo, :]
            next_row = (zero_row if i == nrt - 1
                        else u_ref[(i + 1) * rt:(i + 1) * rt + halo, :].astype(F32)[0:1, :])
            um = jnp.where(row == 0, prev_row, pltpu.roll(u, 1, axis=0))
            up = jnp.where(row == rt - 1, next_row, pltpu.roll(u, rt - 1, axis=0))
            z = cw_ref[0:1, :] * um + cw_ref[1:2, :] * u + cw_ref[2:3, :] * up
            a = (bg_ref[rows, :].astype(F32) * z).astype(BF16)
            o_ref[rows, :] = (g1_ref[...] * _dot(a, wout_ref[...])).astype(BF16)

    mod = lambda chunk: pl.BlockSpec((None, 1, d), lambda b: (st.seg_row(b), 0, chunk))
    seg_spec = pl.BlockSpec((seg, d), lambda b: (b, 0))
    return pl.pallas_call(
        kern, grid=(nb,),
        in_specs=[seg_spec, _full((1, d)), mod(SC1), mod(SH1), mod(G1),
                  _resident((d, 3 * d)), _full((3, d)), _resident((d, d))],
        out_specs=seg_spec, out_shape=SDS((n, d), BF16),
        scratch_shapes=[pltpu.VMEM((seg, d), BF16), pltpu.VMEM((seg, d), BF16)],
        name="conv_mix", compiler_params=_params(("parallel",), 56))(
            x, gain, mods, mods, mods, w_in, conv_w, w_out)


def _conv_layer(streams, xs, mods, gain, w_in, conv_w, w_out):
    w_in_b, w_out_b = w_in.astype(BF16), w_out.astype(BF16)
    return [None if x is None else _conv_mix_call(st, x, mods, gain, w_in_b, conv_w, w_out_b)
            for st, x in zip(streams, xs)]


def _rope_tables(length):
    q = ROPE_DIM // 4
    pos = np.arange(length)
    rc = np.stack([pos // GRID_W, pos % GRID_W], axis=1).astype(np.float32)
    inv = (ROPE_THETA ** (-np.arange(q, dtype=np.float32) / q)).astype(np.float32)
    lane = np.arange(ROPE_DIM)
    ang = rc[:, lane >> 5] * inv[lane & (q - 1)][None, :]
    sign = np.where((lane & q) == 0, -1.0, 1.0)[None, :]
    cos = np.cos(ang.astype(np.float32)).astype(np.float32)
    sin = (np.sin(ang.astype(np.float32)) * sign).astype(np.float32)
    reps = LANES // ROPE_DIM
    return jnp.asarray(np.tile(cos, (1, reps))), jnp.asarray(np.tile(sin, (1, reps)))


def _rope_partner_perm(n_heads):
    lane = np.arange(n_heads * ROPE_DIM)
    return lane ^ (ROPE_DIM // 4)


def _seg_rinv(x, seg):
    t, w = x.shape
    cols = []
    for c in range(w // LANES):
        blk = x[:, c * LANES:(c + 1) * LANES]
        sq = blk * blk
        if seg == LANES:
            cols.append(jnp.broadcast_to(
                lax.rsqrt(jnp.mean(sq, axis=-1, keepdims=True) + EPS), (t, LANES)))
        else:
            lane = lax.broadcasted_iota(jnp.int32, (t, LANES), 1)
            low = lane < seg
            lo = jnp.sum(jnp.where(low, sq, 0.0), axis=-1, keepdims=True) * (1.0 / seg)
            hi = jnp.sum(jnp.where(low, 0.0, sq), axis=-1, keepdims=True) * (1.0 / seg)
            cols.append(jnp.where(low, lax.rsqrt(lo + EPS), lax.rsqrt(hi + EPS)))
    return cols[0] if len(cols) == 1 else jnp.concatenate(cols, axis=-1)


def _tile_lanes(v, w):
    reps = w // LANES
    return v if reps == 1 else jnp.concatenate([v] * reps, axis=-1)


def _mla_proj_call(st, x, mods, gain, wts, rope, need_q):
    n, d = x.shape
    t = st.tile
    h = MLA_HEADS
    wn, wr = h * MLA_NOPE, h * MLA_ROPE
    (w_in, q_norm, kv_norm, w_uq, w_ukv, qg_n, qg_r, qg_rp, kg_n, kg_r, kg_rp) = wts
    use_rope = rope is not None

    def kern(*refs):
        refs = list(refs)
        x_ref, g_ref, sc_ref, sh_ref, win_ref, qn_ref_, kvn_ref_, wuq_ref, wukv_ref = refs[:9]
        qgn_ref, qgr_ref, qgrp_ref, kgn_ref, kgr_ref, kgrp_ref = refs[9:15]
        rest = refs[15:]
        if use_rope:
            cos_ref, sin_ref = rest[:2]
            rest = rest[2:]
        if need_q:
            oqn_ref, oqr_ref = rest[:2]
            rest = rest[2:]
        okn_ref, okr_ref, ov_ref = rest
        hm = _modnorm(x_ref[...], g_ref[...], sc_ref[...], sh_ref[...]).astype(BF16)
        ck = _dot(hm, win_ref[...])
        kr2 = ck[:, MLA_Q_RANK + MLA_KV_RANK:]
        kr_rinv = _seg_rinv(kr2, MLA_ROPE)[:, 0:MLA_ROPE]
        kr_raw = kr2[:, 0:MLA_ROPE] * kgr_ref[...]
        if use_rope:
            kr_par = kr2[:, MLA_ROPE:] * kgrp_ref[...]
            kr = kr_rinv * (kr_raw * cos_ref[:, 0:MLA_ROPE] + kr_par * sin_ref[:, 0:MLA_ROPE])
        else:
            kr = kr_rinv * kr_raw
        okr_ref[...] = kr.astype(BF16)
        ckv = ck[:, MLA_Q_RANK:MLA_Q_RANK + MLA_KV_RANK]
        ckv = (ckv * _seg_rinv(ckv, LANES) * kvn_ref_[...]).astype(BF16)
        kv = _dot(ckv, wukv_ref[...])
        kn = kv[:, 0:wn]
        okn_ref[...] = (kn * _seg_rinv(kn, MLA_NOPE) * kgn_ref[...]).astype(BF16)
        ov_ref[...] = kv[:, wn:].astype(BF16)
        if need_q:
            cq = ck[:, 0:MLA_Q_RANK]
            rq = lax.rsqrt(jnp.mean(cq * cq, axis=-1, keepdims=True) + EPS)
            cq = (cq * rq * qn_ref_[...]).astype(BF16)
            q = _dot(cq, wuq_ref[...])
            qn = q[:, 0:wn]
            oqn_ref[...] = (qn * _seg_rinv(qn, MLA_NOPE) * (qgn_ref[...] * (MLA_SCALE * LOG2E))).astype(BF16)
            qr_raw = q[:, wn:wn + wr]
            rinv = _seg_rinv(qr_raw, MLA_ROPE) * (MLA_SCALE * LOG2E)
            if use_rope:
                qr_par = q[:, wn + wr:]
                cos = _tile_lanes(cos_ref[...], wr)
                sin = _tile_lanes(sin_ref[...], wr)
                qr = rinv * (qr_raw * qgr_ref[...] * cos + qr_par * qgrp_ref[...] * sin)
            else:
                qr = rinv * (qr_raw * qgr_ref[...])
            oqr_ref[...] = qr.astype(BF16)

    tile = lambda w: pl.BlockSpec((t, w), lambda i: (i, 0))
    in_specs = [tile(d), _full((1, d)), _mod_spec(st, SC1, d), _mod_spec(st, SH1, d),
                _resident(w_in.shape), _full(q_norm.shape), _full(kv_norm.shape),
                _resident(w_uq.shape), _resident(w_ukv.shape),
                _full(qg_n.shape), _full(qg_r.shape), _full(qg_rp.shape),
                _full(kg_n.shape), _full(kg_r.shape), _full(kg_rp.shape)]
    args = [x, gain, mods, mods, w_in, q_norm, kv_norm, w_uq, w_ukv, qg_n, qg_r, qg_rp, kg_n, kg_r, kg_rp]
    if use_rope:
        tps = st.tiles_per_seg
        in_specs += [pl.BlockSpec((t, LANES), lambda i: (i % tps, 0))] * 2
        args += list(rope)
    out_specs, out_shape = [], []
    if need_q:
        out_specs += [tile(wn), tile(wr)]
        out_shape += [SDS((n, wn), BF16), SDS((n, wr), BF16)]
    out_specs += [tile(wn), tile(MLA_ROPE), tile(h * MLA_V)]
    out_shape += [SDS((n, wn), BF16), SDS((n, MLA_ROPE), BF16), SDS((n, h * MLA_V), BF16)]
    outs = pl.pallas_call(
        kern, grid=(st.tiles,), in_specs=in_specs, out_specs=out_specs, out_shape=out_shape,
        name="mla_proj", compiler_params=_params(("parallel",), 48))(*args)
    if need_q:
        return tuple(outs)
    return (None, None) + tuple(outs)


def _mla_attn_fast_call(batch, lq, tq, q, kv_sets, shift):
    qn, qr = q
    lk_total = sum(s[3] for s in kv_sets)
    hp = 2 if lk_total > 1024 else 8
    n_hp = MLA_HEADS // hp
    nq = lq // tq
    nsets = len(kv_sets)
    lengths = [s[3] for s in kv_sets]
    lk = sum(lengths)
    kw = 2 * LANES
    pad = kw - MLA_NOPE - MLA_ROPE

    def kern(*refs):
        sh_ref, qn_ref, qr_ref = refs[:3]
        sets = [refs[3 + 3 * s:6 + 3 * s] for s in range(nsets)]
        o_ref, kq_ref, vq_ref = refs[-3:]

        def build_kv():
            lane_k = lax.broadcasted_iota(jnp.int32, (lk, pad), 1)
            lane_v = lax.broadcasted_iota(jnp.int32, (lk, kw - MLA_V), 1)
            for hh in range(hp):
                off = 0
                for (kn_ref, kr_ref, v_ref), length in zip(sets, lengths):
                    kq_ref[hh, off:off + length, 0:MLA_NOPE] = kn_ref[:, hh * MLA_NOPE:(hh + 1) * MLA_NOPE]
                    kq_ref[hh, off:off + length, MLA_NOPE:MLA_NOPE + MLA_ROPE] = kr_ref[...]
                    vq_ref[hh, off:off + length, 0:MLA_V] = v_ref[:, hh * MLA_V:(hh + 1) * MLA_V]
                    off += length
                kq_ref[hh, :, MLA_NOPE + MLA_ROPE:] = jnp.where(lane_k == 0, -sh_ref[:, 0:pad], 0.0).astype(BF16)
                vq_ref[hh, :, MLA_V:] = jnp.where(lane_v == 0, 1.0, 0.0).astype(BF16)

        if nq == 1:
            build_kv()
        else:
            pl.when(pl.program_id(2) == 0)(build_kv)

        lane_q = lax.broadcasted_iota(jnp.int32, (tq, pad), 1)
        one = jnp.where(lane_q == 0, 1.0, 0.0).astype(BF16)
        for hh in range(hp):
            qc = jnp.concatenate([qn_ref[:, hh * MLA_NOPE:(hh + 1) * MLA_NOPE],
                                  qr_ref[:, hh * MLA_ROPE:(hh + 1) * MLA_ROPE], one], axis=1)
            p = jnp.exp2(_nt(qc, kq_ref[hh]).astype(BF16))
            acc = _dot(p, vq_ref[hh])
            o_ref[:, hh * MLA_V:(hh + 1) * MLA_V] = (acc[:, 0:MLA_V] / acc[:, MLA_V:MLA_V + 1]).astype(BF16)

    in_specs = [_full((1, LANES)),
                pl.BlockSpec((tq, hp * MLA_NOPE), lambda b, h, i: (b * nq + i, h)),
                pl.BlockSpec((tq, hp * MLA_ROPE), lambda b, h, i: (b * nq + i, h))]
    args = [shift, qn, qr]
    for kn, kr, v, length in kv_sets:
        in_specs += [pl.BlockSpec((length, hp * MLA_NOPE), lambda b, h, i: (b, h)),
                     pl.BlockSpec((length, MLA_ROPE), lambda b, h, i: (b, 0)),
                     pl.BlockSpec((length, hp * MLA_V), lambda b, h, i: (b, h))]
        args += [kn, kr, v]
    return pl.pallas_call(
        kern, grid=(batch, n_hp, nq), in_specs=in_specs,
        out_specs=pl.BlockSpec((tq, hp * MLA_V), lambda b, h, i: (b * nq + i, h)),
        out_shape=SDS((batch * lq, MLA_HEADS * MLA_V), BF16),
        scratch_shapes=[pltpu.VMEM((hp, lk, kw), BF16), pltpu.VMEM((hp, lk, kw), BF16)],
        name="mla_attn_fast",
        compiler_params=_params(("parallel", "parallel", "arbitrary"), 56))(*args)


def _mla_attn_call(batch, lq, tq, q, kv_sets):
    qn, qr = q
    hp = 2
    n_hp = MLA_HEADS // hp
    nq = lq // tq
    nsets = len(kv_sets)

    def kern(*refs):
        qn_ref, qr_ref = refs[:2]
        o_ref = refs[-1]
        sets = [refs[2 + 3 * s:5 + 3 * s] for s in range(nsets)]
        for hh in range(hp):
            qn_h = qn_ref[:, hh * MLA_NOPE:(hh + 1) * MLA_NOPE]
            qr_h = qr_ref[:, hh * MLA_ROPE:(hh + 1) * MLA_ROPE]
            scores = [_nt(qn_h, kn_ref[:, hh * MLA_NOPE:(hh + 1) * MLA_NOPE]) + _nt(qr_h, kr_ref[...])
                      for kn_ref, kr_ref, _ in sets]
            m = functools.reduce(jnp.maximum, [jnp.max(s, axis=-1, keepdims=True) for s in scores])
            ps = [jnp.exp2(s - m) for s in scores]
            den = functools.reduce(lambda a, b: a + b, [jnp.sum(p, axis=-1, keepdims=True) for p in ps])
            acc = functools.reduce(lambda a, b: a + b, [
                _dot(p.astype(BF16), v_ref[:, hh * MLA_V:(hh + 1) * MLA_V])
                for p, (_, _, v_ref) in zip(ps, sets)])
            o_ref[:, hh * MLA_V:(hh + 1) * MLA_V] = (acc / den).astype(BF16)

    in_specs = [pl.BlockSpec((tq, hp * MLA_NOPE), lambda b, h, i: (b * nq + i, h)),
                pl.BlockSpec((tq, hp * MLA_ROPE), lambda b, h, i: (b * nq + i, h))]
    args = [qn, qr]
    for kn, kr, v, length in kv_sets:
        in_specs += [pl.BlockSpec((length, hp * MLA_NOPE), lambda b, h, i: (b, h)),
                     pl.BlockSpec((length, MLA_ROPE), lambda b, h, i: (b, 0)),
                     pl.BlockSpec((length, hp * MLA_V), lambda b, h, i: (b, h))]
        args += [kn, kr, v]
    return pl.pallas_call(
        kern, grid=(batch, n_hp, nq), in_specs=in_specs,
        out_specs=pl.BlockSpec((tq, hp * MLA_V), lambda b, h, i: (b * nq + i, h)),
        out_shape=SDS((batch * lq, MLA_HEADS * MLA_V), BF16), name="mla_attn",
        compiler_params=_params(("parallel", "parallel", "arbitrary"), 56))(*args)


def _mla_layer(streams, xs, mods, gain, p, rope, ctx_next):
    st_c, st_x = streams
    xc, xx = xs
    h = MLA_HEADS
    w_in, q_norm, kv_norm, w_uq, w_ukv, q_gain, k_gain, w_out = p
    par = _rope_partner_perm(1)
    kr_cols = w_in[:, MLA_Q_RANK + MLA_KV_RANK:]
    w_in_p = jnp.concatenate([w_in, kr_cols[:, par]], axis=1).astype(BF16)
    wq = w_uq.reshape(MLA_Q_RANK, h, MLA_NOPE + MLA_ROPE)
    wq_n = wq[:, :, :MLA_NOPE].reshape(MLA_Q_RANK, h * MLA_NOPE)
    wq_r = wq[:, :, MLA_NOPE:]
    w_uq_p = jnp.concatenate([wq_n, wq_r.reshape(MLA_Q_RANK, h * MLA_ROPE),
                              wq_r[:, :, par].reshape(MLA_Q_RANK, h * MLA_ROPE)], axis=1).astype(BF16)
    wkv = w_ukv.reshape(MLA_KV_RANK, h, MLA_NOPE + MLA_V)
    w_ukv_p = jnp.concatenate([wkv[:, :, :MLA_NOPE].reshape(MLA_KV_RANK, h * MLA_NOPE),
                               wkv[:, :, MLA_NOPE:].reshape(MLA_KV_RANK, h * MLA_V)], axis=1).astype(BF16)
    qg_r = q_gain[MLA_NOPE:]
    kg_r = k_gain[MLA_NOPE:]
    wts = (w_in_p, q_norm.reshape(1, -1), kv_norm.reshape(1, -1), w_uq_p, w_ukv_p,
           jnp.tile(q_gain[:MLA_NOPE], h).reshape(1, -1), jnp.tile(qg_r, h).reshape(1, -1),
           jnp.tile(qg_r[par], h).reshape(1, -1), jnp.tile(k_gain[:MLA_NOPE], h).reshape(1, -1),
           kg_r.reshape(1, -1), kg_r[par].reshape(1, -1))
    w_out_b = w_out.astype(BF16)
    qn_c, qr_c, kn_c, kr_c, v_c = _mla_proj_call(st_c.with_tile(PROJ_TILE), xc, mods, gain, wts, None, ctx_next)
    qn_x, qr_x, kn_x, kr_x, v_x = _mla_proj_call(st_x.with_tile(PROJ_TILE), xx, mods, gain, wts, rope, True)
    b = st_x.batch
    amax = lambda g: jnp.max(jnp.abs(g))
    bound = LOG2E * MLA_SCALE * (MLA_NOPE * amax(q_gain[:MLA_NOPE]) * amax(k_gain[:MLA_NOPE])
                                 + MLA_ROPE * amax(qg_r) * amax(kg_r))
    shift = bound * SHIFT_MARGIN + 1.0
    shift_row = jnp.full((1, LANES), shift, F32)

    def attend(lq, tq, q, kv_sets):
        return lax.cond(shift <= MAX_FIXED_SHIFT,
                        lambda: _mla_attn_fast_call(b, lq, tq, q, kv_sets, shift_row),
                        lambda: _mla_attn_call(b, lq, min(tq, 512), q, kv_sets))

    a_x = attend(st_x.seg, min(st_x.seg, 2048), (qn_x, qr_x), [(kn_c, kr_c, v_c, st_c.seg), (kn_x, kr_x, v_x, st_x.seg)])
    out_x = _outproj_call(st_x, a_x, mods, w_out_b, "mla_out")
    out_c = None
    if ctx_next:
        a_c = attend(st_c.seg, st_c.seg, (qn_c, qr_c), [(kn_c, kr_c, v_c, st_c.seg)])
        out_c = _outproj_call(st_c, a_c, mods, w_out_b, "mla_out")
    return [out_c, out_x]


def _gqa_proj_call(st, x, mods, gain, wts, rope, need_q):
    n, d = x.shape
    t = st.tile
    wq, wk = GQA_Q_HEADS * GQA_HEAD_DIM, GQA_KV_HEADS * GQA_HEAD_DIM
    w_in, qg, qgp, kg, kgp = wts
    use_rope = rope is not None

    def kern(*refs):
        refs = list(refs)
        x_ref, g_ref, sc_ref, sh_ref, w_ref, qg_ref, qgp_ref, kg_ref, kgp_ref = refs[:9]
        rest = refs[9:]
        if use_rope:
            cos_ref, sin_ref = rest[:2]
            rest = rest[2:]
        if need_q:
            oq_ref = rest[0]
            rest = rest[1:]
        ok_ref, ov_ref = rest
        hm = _modnorm(x_ref[...], g_ref[...], sc_ref[...], sh_ref[...]).astype(BF16)
        kvp = _dot(hm, w_ref[:, 0:3 * wk])
        k_raw = kvp[:, 0:wk]
        k_rinv = _seg_rinv(k_raw, GQA_HEAD_DIM)
        ov_ref[...] = kvp[:, wk:2 * wk].astype(BF16)
        if use_rope:
            cos_k, sin_k = _tile_lanes(cos_ref[...], wk), _tile_lanes(sin_ref[...], wk)
            k = k_rinv * (k_raw * kg_ref[...] * cos_k + kvp[:, 2 * wk:] * kgp_ref[...] * sin_k)
        else:
            k = k_rinv * (k_raw * kg_ref[...])
        ok_ref[...] = k.astype(BF16)
        if need_q:
            qp = _dot(hm, w_ref[:, 3 * wk:])
            q_raw = qp[:, 0:wq]
            rinv = _seg_rinv(q_raw, GQA_HEAD_DIM) * (GQA_SCALE * LOG2E)
            if use_rope:
                cos_q, sin_q = _tile_lanes(cos_ref[...], wq), _tile_lanes(sin_ref[...], wq)
                q = rinv * (q_raw * qg_ref[...] * cos_q + qp[:, wq:] * qgp_ref[...] * sin_q)
            else:
                q = rinv * (q_raw * qg_ref[...])
            oq_ref[...] = q.astype(BF16)

    tile = lambda w: pl.BlockSpec((t, w), lambda i: (i, 0))
    in_specs = [tile(d), _full((1, d)), _mod_spec(st, SC1, d), _mod_spec(st, SH1, d),
                _resident(w_in.shape), _full(qg.shape), _full(qgp.shape), _full(kg.shape), _full(kgp.shape)]
    args = [x, gain, mods, mods, w_in, qg, qgp, kg, kgp]
    if use_rope:
        tps = st.tiles_per_seg
        in_specs += [pl.BlockSpec((t, LANES), lambda i: (i % tps, 0))] * 2
        args += list(rope)
    out_specs, out_shape = [], []
    if need_q:
        out_specs.append(tile(wq))
        out_shape.append(SDS((n, wq), BF16))
    out_specs += [tile(wk), tile(wk)]
    out_shape += [SDS((n, wk), BF16)] * 2
    outs = pl.pallas_call(
        kern, grid=(st.tiles,), in_specs=in_specs, out_specs=out_specs, out_shape=out_shape,
        name="gqa_proj", compiler_params=_params(("parallel",), 48))(*args)
    return tuple(outs) if need_q else (None,) + tuple(outs)


def _gqa_window_call(batch, length, lc, q, k, v, kc, vc, sink):
    nb = length // BLOCK
    hd = GQA_HEAD_DIM
    wq, wk = GQA_Q_HEADS * hd, GQA_KV_HEADS * hd
    rows = GQA_GROUP * BLOCK

    def kern(sink_ref, q_ref, kp_ref, k0_ref, kn_ref, vp_ref, v0_ref, vn_ref, kc_ref, vc_ref, o_ref):
        nblk = pl.program_id(1)
        r = lax.broadcasted_iota(jnp.int32, (rows, 3 * BLOCK), 0) & (BLOCK - 1)
        c = lax.broadcasted_iota(jnp.int32, (rows, 3 * BLOCK), 1)
        valid = (c >= r + BLOCK - WINDOW) & (c <= r + BLOCK + WINDOW)
        valid = valid & ((c >= BLOCK) | (nblk > 0)) & ((c < 2 * BLOCK) | (nblk < nb - 1))
        hrow = lax.broadcasted_iota(jnp.int32, (rows, 1), 0) // BLOCK
        for g in range(GQA_KV_HEADS):
            sl = slice(g * hd, (g + 1) * hd)
            qg = jnp.concatenate([q_ref[:, (g * GQA_GROUP + j) * hd:(g * GQA_GROUP + j + 1) * hd]
                                  for j in range(GQA_GROUP)], axis=0)
            kband = jnp.concatenate([kp_ref[:, sl], k0_ref[:, sl], kn_ref[:, sl]], axis=0)
            vband = jnp.concatenate([vp_ref[:, sl], v0_ref[:, sl], vn_ref[:, sl]], axis=0)
            s_c = _nt(qg, kc_ref[:, sl])
            s_b = jnp.where(valid, _nt(qg, kband), -1e30)
            snk = jnp.zeros((rows, 1), F32)
            for j in range(GQA_GROUP):
                snk = jnp.where(hrow == j, sink_ref[g * GQA_GROUP + j], snk)
            m = jnp.maximum(jnp.maximum(jnp.max(s_c, axis=-1, keepdims=True),
                                        jnp.max(s_b, axis=-1, keepdims=True)), snk)
            p_c = jnp.exp2(s_c - m)
            p_b = jnp.exp2(s_b - m)
            den = (jnp.sum(p_c, axis=-1, keepdims=True) + jnp.sum(p_b, axis=-1, keepdims=True)
                   + jnp.exp2(snk - m))
            o = (_dot(p_c.astype(BF16), vc_ref[:, sl]) + _dot(p_b.astype(BF16), vband)) / den
            o_ref[:, g * GQA_GROUP * hd:(g + 1) * GQA_GROUP * hd] = jnp.concatenate(
                [o[j * BLOCK:(j + 1) * BLOCK, :] for j in range(GQA_GROUP)], axis=-1).astype(BF16)

    blk = lambda f: pl.BlockSpec((BLOCK, wk), f)
    prev_ = lambda b, i: (b * nb + jnp.maximum(i - 1, 0), 0)
    cur_ = lambda b, i: (b * nb + i, 0)
    next_ = lambda b, i: (b * nb + jnp.minimum(i + 1, nb - 1), 0)
    ctx_spec = pl.BlockSpec((lc, wk), lambda b, i: (b, 0))
    return pl.pallas_call(
        kern, grid=(batch, nb),
        in_specs=[pl.BlockSpec(memory_space=pltpu.SMEM),
                  pl.BlockSpec((BLOCK, wq), cur_), blk(prev_), blk(cur_), blk(next_),
                  blk(prev_), blk(cur_), blk(next_), ctx_spec, ctx_spec],
        out_specs=pl.BlockSpec((BLOCK, wq), cur_),
        out_shape=SDS((batch * length, wq), BF16), name="gqa_window",
        compiler_params=_params(("parallel", "arbitrary"), 40))(sink, q, k, k, k, v, v, v, kc, vc)


def _gqa_window_fast_call(batch, length, lc, q, k, v, kc, vc, sink2, shift):
    nb = length // BLOCK
    hd = GQA_HEAD_DIM
    wq, wk = GQA_Q_HEADS * hd, GQA_KV_HEADS * hd
    rows = GQA_GROUP * BLOCK

    def kern(sink_ref, shift_ref, q_ref, kp_ref, k0_ref, kn_ref, vp_ref, v0_ref, vn_ref, kc_ref, vc_ref, o_ref):
        nblk = pl.program_id(1)
        r = lax.broadcasted_iota(jnp.int32, (rows, 3 * BLOCK), 0) & (BLOCK - 1)
        c = lax.broadcasted_iota(jnp.int32, (rows, 3 * BLOCK), 1)
        valid = (c >= r + BLOCK - WINDOW) & (c <= r + BLOCK + WINDOW)
        valid = valid & ((c >= BLOCK) | (nblk > 0)) & ((c < 2 * BLOCK) | (nblk < nb - 1))
        hrow = lax.broadcasted_iota(jnp.int32, (rows, 1), 0) // BLOCK
        lane0_q = lax.broadcasted_iota(jnp.int32, (BLOCK, hd), 1) == 0

        def with_one(x):
            lane0 = lax.broadcasted_iota(jnp.int32, (x.shape[0], hd), 1) == 0
            return jnp.concatenate([x, jnp.where(lane0, 1.0, 0.0).astype(BF16)], axis=1)

        for g in range(GQA_KV_HEADS):
            sl = slice(g * hd, (g + 1) * hd)
            heads = [g * GQA_GROUP + j for j in range(GQA_GROUP)]
            qg = jnp.concatenate(
                [jnp.concatenate([q_ref[:, h * hd:(h + 1) * hd],
                                  jnp.where(lane0_q, -shift_ref[h], 0.0).astype(BF16)], axis=1)
                 for h in heads], axis=0)
            kband = with_one(jnp.concatenate([kp_ref[:, sl], k0_ref[:, sl], kn_ref[:, sl]], axis=0))
            vband = with_one(jnp.concatenate([vp_ref[:, sl], v0_ref[:, sl], vn_ref[:, sl]], axis=0))
            p_c = jnp.exp2(_nt(qg, with_one(kc_ref[:, sl])).astype(BF16))
            p_b = jnp.exp2(jnp.where(valid, _nt(qg, kband), -1e30).astype(BF16))
            acc = _dot(p_c, with_one(vc_ref[:, sl])) + _dot(p_b, vband)
            snk = jnp.zeros((rows, 1), F32)
            for j, h in enumerate(heads):
                snk = jnp.where(hrow == j, sink_ref[h] - shift_ref[h], snk)
            o = acc[:, 0:hd] / (acc[:, hd:hd + 1] + jnp.exp2(snk))
            o_ref[:, g * GQA_GROUP * hd:(g + 1) * GQA_GROUP * hd] = jnp.concatenate(
                [o[j * BLOCK:(j + 1) * BLOCK, :] for j in range(GQA_GROUP)], axis=-1).astype(BF16)

    blk = lambda f: pl.BlockSpec((BLOCK, wk), f)
    prev_ = lambda b, i: (b * nb + jnp.maximum(i - 1, 0), 0)
    cur_ = lambda b, i: (b * nb + i, 0)
    next_ = lambda b, i: (b * nb + jnp.minimum(i + 1, nb - 1), 0)
    ctx_spec = pl.BlockSpec((lc, wk), lambda b, i: (b, 0))
    smem = pl.BlockSpec(memory_space=pltpu.SMEM)
    return pl.pallas_call(
        kern, grid=(batch, nb),
        in_specs=[smem, smem, pl.BlockSpec((BLOCK, wq), cur_), blk(prev_), blk(cur_), blk(next_),
                  blk(prev_), blk(cur_), blk(next_), ctx_spec, ctx_spec],
        out_specs=pl.BlockSpec((BLOCK, wq), cur_),
        out_shape=SDS((batch * length, wq), BF16), name="gqa_window_fast",
        compiler_params=_params(("parallel", "arbitrary"), 40))(sink2, shift, q, k, k, k, v, v, v, kc, vc)


def _gqa_layer(streams, xs, mods, gain, p, rope, ctx_next):
    assert not ctx_next, "the windowed-GQA mixer is only implemented as the last layer"
    st_c, st_x = streams
    xc, xx = xs
    w_in, q_gain, k_gain, sink, w_out = p
    wq, wk = GQA_Q_HEADS * GQA_HEAD_DIM, GQA_KV_HEADS * GQA_HEAD_DIM
    w_q, w_k, w_v = w_in[:, :wq], w_in[:, wq:wq + wk], w_in[:, wq + wk:]
    w_in_p = jnp.concatenate([w_k, w_v, w_k[:, _rope_partner_perm(GQA_KV_HEADS)],
                              w_q, w_q[:, _rope_partner_perm(GQA_Q_HEADS)]], axis=1).astype(BF16)
    par = _rope_partner_perm(1)
    wts = (w_in_p, jnp.tile(q_gain, GQA_Q_HEADS).reshape(1, -1), jnp.tile(q_gain[par], GQA_Q_HEADS).reshape(1, -1),
           jnp.tile(k_gain, GQA_KV_HEADS).reshape(1, -1), jnp.tile(k_gain[par], GQA_KV_HEADS).reshape(1, -1))
    _, kc, vc = _gqa_proj_call(st_c.with_tile(PROJ_TILE), xc, mods, gain, wts, None, False)
    q, k, v = _gqa_proj_call(st_x.with_tile(PROJ_TILE), xx, mods, gain, wts, rope, True)
    sink2 = sink * LOG2E
    bound = LOG2E * GQA_SCALE * GQA_HEAD_DIM * jnp.max(jnp.abs(q_gain)) * jnp.max(jnp.abs(k_gain))
    shift = jnp.maximum(bound * SHIFT_MARGIN + 1.0, sink2).astype(BF16).astype(F32)
    args = (st_x.batch, st_x.seg, st_c.seg, q, k, v, kc, vc, sink2)
    a = lax.cond(jnp.max(shift) <= MAX_FIXED_SHIFT,
                 lambda: _gqa_window_fast_call(*args, shift),
                 lambda: _gqa_window_call(*args))
    return [None, _outproj_call(st_x, a, mods, w_out.astype(BF16), "gqa_out")]


def _cast_experts_call(w, layer):
    _, ne, a, b = w.shape
    eb = 4

    def kern(w_ref, o_ref):
        o_ref[...] = w_ref[...].astype(BF16)

    return pl.pallas_call(
        kern, grid=(ne // eb,),
        in_specs=[pl.BlockSpec((None, eb, a, b), lambda i: (layer, i, 0, 0))],
        out_specs=pl.BlockSpec((eb, a, b), lambda i: (i, 0, 0)),
        out_shape=SDS((ne, a, b), BF16), name="cast_experts",
        compiler_params=_params(("parallel",), 32))(w)


def _route(logits_t, bias_col):
    scores = _sigmoid(logits_t)
    biased = scores + bias_col
    rows = [biased[e:e + 1, :] for e in range(N_EXPERTS)]
    srow = [scores[e:e + 1, :] for e in range(N_EXPERTS)]
    epg = EXPERTS_PER_GROUP
    gscore = []
    for g in range(N_EXPERT_GROUPS):
        v = rows[g * epg:(g + 1) * epg]
        pair = [v[a] + v[b] for a in range(epg) for b in range(a + 1, epg)]
        gscore.append(functools.reduce(jnp.maximum, pair))
    ind, wloc = [], [None] * epg
    for g in range(N_EXPERT_GROUPS):
        best = None
        for g2 in range(N_EXPERT_GROUPS):
            if g2 == g:
                continue
            cnd = gscore[g] > gscore[g2] if g2 < g else gscore[g] >= gscore[g2]
            best = cnd if best is None else best & cnd
        ind.append(jnp.where(best, 1.0, 0.0))
        for j in range(epg):
            e = g * epg + j
            rank = None
            for e2 in range(g * epg, (g + 1) * epg):
                if e2 == e:
                    continue
                ahead = rows[e2] >= rows[e] if e2 < e else rows[e2] > rows[e]
                one = jnp.where(ahead, 1.0, 0.0)
                rank = one if rank is None else rank + one
            w = jnp.where(best & (rank < 2.0), srow[e], 0.0)
            wloc[j] = w if wloc[j] is None else wloc[j] + w
    den = functools.reduce(lambda a, b: a + b, wloc)
    return ind, [w / den for w in wloc]


MOE_TILE = 1024
MOE_CHUNK = 128


def _moe_call(st, x_in, y_in, mods, gain, router_wt, router_bias, w_gate, w_up, w_down):
    n, d = x_in.shape
    t = min(MOE_TILE, n)
    r = MOE_CHUNK
    ne, _, ff = w_gate.shape
    epg, ng = EXPERTS_PER_GROUP, N_EXPERT_GROUPS
    nch = t // r + ng - 1
    tiles = n // t
    wd2 = w_down.reshape(ne * ff, d)

    def kern(x_ref, y_ref, g_ref, sc_ref, sh_ref, g2_ref, rw_ref, rb_ref, wg_ref, wu_ref, wd_ref, o_ref,
             hb_ref, oh_ref, xg_ref):
        hb_ref[...] = _modnorm(x_ref[...] + y_ref[...].astype(F32), g_ref[...], sc_ref[...], sh_ref[...]).astype(BF16)
        ind, wloc = _route(_nt(rw_ref[...], hb_ref[...]), rb_ref[...])
        ind8 = jnp.concatenate(ind + [jnp.zeros((8 - ng, t), F32)], axis=0)
        upper = jnp.where(lax.broadcasted_iota(jnp.int32, (LANES, LANES), 0)
                          < lax.broadcasted_iota(jnp.int32, (LANES, LANES), 1), 1.0, 0.0).astype(BF16)
        blocks, before = [], jnp.zeros((8, 1), F32)
        for k in range(t // LANES):
            blk = ind8[:, k * LANES:(k + 1) * LANES]
            blocks.append(_dot(blk.astype(BF16), upper) + before)
            before = before + jnp.sum(blk, axis=1, keepdims=True)
        pos = jnp.concatenate(blocks, axis=1)
        count = [jnp.sum(ind[g]).astype(jnp.int32) for g in range(ng)]
        first = [jnp.int32(0)]
        for g in range(ng):
            first.append(first[g] + (count[g] + (r - 1)) // r)
        slot = functools.reduce(lambda a, b: a + b, [
            ind[g] * (pos[g:g + 1, :] + (first[g] * r).astype(F32)) for g in range(ng)])
        w_hi = [w.astype(BF16) for w in wloc]
        w_lo = [(w - h.astype(F32)).astype(BF16) for w, h in zip(wloc, w_hi)]
        wst = jnp.concatenate(w_hi + w_lo, axis=0)
        row_id = lax.broadcasted_iota(jnp.int32, (r, t), 0).astype(F32)
        onehot = lambda c: jnp.where(slot == row_id + float(c * r), 1.0, 0.0).astype(BF16)
        last = nch - 1
        for c in range(last):
            oh_ref[c * r:(c + 1) * r, :] = onehot(c)
        xg_ref[...] = _dot(oh_ref[...], hb_ref[...]).astype(BF16)
        wr_all = _nt(wst, oh_ref[...]).T

        def chunk_ffn(c, xg, wr):
            grp = functools.reduce(lambda a, b: a + b,
                                   [(c >= first[g]).astype(jnp.int32) for g in range(1, ng)])
            acts = []
            for j in range(epg):
                e = grp * epg + j
                gt = _dot(xg, wg_ref[e])
                up = _dot(xg, wu_ref[e])
                acts.append((gt * _sigmoid(gt) * up * (wr[:, j:j + 1] + wr[:, epg + j:epg + j + 1])).astype(BF16))
            wd_g = wd_ref[pl.ds(pl.multiple_of(grp * (epg * ff), epg * ff), epg * ff), :]
            return _dot(jnp.concatenate(acts, axis=1), wd_g).astype(BF16)

        for c in range(last):
            rows = slice(c * r, (c + 1) * r)

            def run(c=c, rows=rows):
                xg_ref[rows, :] = chunk_ffn(c, xg_ref[rows, :], wr_all[c * r:(c + 1) * r, :])

            if c < t // r:
                run()
            else:
                pl.when(c < first[ng])(run)

        tdot = lambda a, b: lax.dot_general(a, b, (((0,), (0,)), ((), ())), preferred_element_type=F32)
        o_ref[...] = x_ref[...] + y_ref[...].astype(F32) + g2_ref[...] * tdot(oh_ref[...], xg_ref[...])

        @pl.when(last < first[ng])
        def _():
            oh = onehot(last)
            out = chunk_ffn(last, _dot(oh, hb_ref[...]).astype(BF16), _nt(wst, oh).T)
            o_ref[...] += g2_ref[...] * tdot(oh, out)

    row = (lambda i: st.batch) if st.is_ctx else (lambda i: i // (st.seg // t))
    mod = lambda chunk: pl.BlockSpec((None, 1, d), lambda i: (row(i), 0, chunk))
    return pl.pallas_call(
        kern, grid=(tiles,),
        in_specs=[pl.BlockSpec((t, d), lambda i: (i, 0)),
                  pl.BlockSpec((t, d), lambda i: (i, 0)),
                  _full((1, d)), mod(SC2), mod(SH2), mod(G2), _full((ne, d)), _full((ne, 1)),
                  _resident((ne, d, ff)), _resident((ne, d, ff)), _resident((ne * ff, d))],
        out_specs=pl.BlockSpec((t, d), lambda i: (i, 0)),
        out_shape=SDS((n, d), F32),
        scratch_shapes=[pltpu.VMEM((t, d), BF16), pltpu.VMEM(((nch - 1) * r, t), BF16),
                        pltpu.VMEM(((nch - 1) * r, d), BF16)],
        name="moe", compiler_params=_params(("parallel",), 58))(
            x_in, y_in, gain, mods, mods, mods, router_wt, router_bias, w_gate, w_up, wd2)


def kernel(x, c, ctx, c_ctx, w_ada, b_ada, norm_mix, norm_ffn, fourier_w_out, conv_w_in, conv_w, conv_w_out,
           mla_w_in, mla_q_norm, mla_kv_norm, mla_w_uq, mla_w_ukv, mla_q_gain, mla_k_gain, mla_w_out,
           gqa_w_in, gqa_q_gain, gqa_k_gain, gqa_sink, gqa_w_out, router_w, router_bias,
           moe_w_gate, moe_w_up, moe_w_down):
    b, l, d = x.shape
    lc = ctx.shape[1]
    depth = w_ada.shape[0]
    st_c = Stream(b * lc, lc, min(b * lc, TOKEN_TILE), b, True)
    st_x = Stream(b * l, l, min(l, TOKEN_TILE), b, False)
    streams = [st_c, st_x]
    r8 = -(-(b + 1) // 8) * 8
    cvec = jnp.concatenate([c, c_ctx[None, :], jnp.zeros((r8 - b - 1, d), F32)], axis=0)
    mods_all = _ada_call(cvec, w_ada, b_ada).reshape(depth, r8, 1, 6 * d)
    rope = _rope_tables(l)
    router_wt = router_w.T.astype(BF16)
    router_b = router_bias.reshape(-1, 1)
    xs = [ctx.reshape(b * lc, d), x.reshape(b * l, d)]
    for i in range(depth):
        kind, j = i % 4, i // 4
        ctx_next = i < depth - 1
        mods = mods_all[i]
        gain = norm_mix[i].reshape(1, d)
        if not (ctx_next or kind >= 2):
            xs[0] = None
        if kind == 0:
            ys = _fourier_layer(streams, xs, mods, gain, fourier_w_out[j])
        elif kind == 1:
            ys = _conv_layer(streams, xs, mods, gain, conv_w_in[j], conv_w[j], conv_w_out[j])
        elif kind == 2:
            ys = _mla_layer(streams, xs, mods, gain,
                            (mla_w_in[j], mla_q_norm[j], mla_kv_norm[j], mla_w_uq[j], mla_w_ukv[j],
                             mla_q_gain[j], mla_k_gain[j], mla_w_out[j]), rope, ctx_next)
        else:
            ys = _gqa_layer(streams, xs, mods, gain,
                            (gqa_w_in[j], gqa_q_gain[j], gqa_k_gain[j], gqa_sink[j], gqa_w_out[j]),
                            rope, ctx_next)
        if not ctx_next:
            ys[0] = None
        gain2 = norm_ffn[i].reshape(1, d)
        wg, wu, wd = (_cast_experts_call(w, i) for w in (moe_w_gate, moe_w_up, moe_w_down))
        xs = [None if y is None else _moe_call(st, x, y, mods, gain2, router_wt, router_b, wg, wu, wd)
              for st, x, y in zip(streams, xs, ys)]
    return xs[1].reshape(b, l, d)
```

```python
import functools

import numpy as np
import jax
import jax.numpy as jnp
from jax import lax
from jax.experimental import pallas as pl
from jax.experimental.pallas import tpu as pltpu

F32, BF16 = jnp.float32, jnp.bfloat16
SDS = jax.ShapeDtypeStruct

EPS = 1e-6
GRID_W = 64
ROPE_THETA = 10000.0
N_FOURIER_GROUPS = 4
MLA_HEADS, MLA_Q_RANK, MLA_KV_RANK = 16, 256, 128
MLA_NOPE, MLA_ROPE, MLA_V = 128, 64, 128
MLA_SCALE = (MLA_NOPE + MLA_ROPE) ** -0.5
GQA_Q_HEADS, GQA_KV_HEADS, GQA_HEAD_DIM = 16, 4, 64
GQA_GROUP = GQA_Q_HEADS // GQA_KV_HEADS
GQA_SCALE = GQA_HEAD_DIM ** -0.5
WINDOW = 128
BLOCK = 128
N_EXPERTS, N_EXPERT_GROUPS, EXPERT_FF = 16, 4, 256
EXPERTS_PER_GROUP = N_EXPERTS // N_EXPERT_GROUPS
ROPE_DIM = 64
LOG2E = 1.4426950408889634
MAX_FIXED_SHIFT = 50.0
SHIFT_MARGIN = 1.02
TOKEN_TILE = 1024
PROJ_TILE = 512

V7X_VMEM_BYTES = 64 * 1024 * 1024
LANES = 128
SH1, SC1, G1, SH2, SC2, G2 = range(6)


def _params(sem, vmem_mb):
    return pltpu.CompilerParams(dimension_semantics=sem, vmem_limit_bytes=vmem_mb * 1024 * 1024)


def _sigmoid(v):
    return 1.0 / (1.0 + jnp.exp(-v))


def _modnorm(x, gain, sc, sh):
    ms = jnp.mean(x * x, axis=-1, keepdims=True)
    return x * lax.rsqrt(ms + EPS) * (gain * (1.0 + sc)) + sh


def _nt(a, b):
    return lax.dot_general(a, b, (((1,), (1,)), ((), ())), preferred_element_type=F32)


def _dot(a, b):
    return jnp.dot(a, b, preferred_element_type=F32)


class Stream:
    def __init__(self, n, seg, tile, batch, is_ctx):
        self.n, self.seg, self.tile, self.batch, self.is_ctx = n, seg, tile, batch, is_ctx
        self.tiles = n // tile
        self.tiles_per_seg = seg // tile

    def mod_row(self, t):
        return self.batch if self.is_ctx else t // self.tiles_per_seg

    def seg_row(self, b):
        return self.batch if self.is_ctx else b

    def with_tile(self, tile):
        return Stream(self.n, self.seg, min(tile, self.tile), self.batch, self.is_ctx)


def _mod_spec(st, chunk, d):
    return pl.BlockSpec((None, 1, d), lambda t: (st.mod_row(t), 0, chunk))


def _full(shape):
    nd = len(shape)
    return pl.BlockSpec(shape, lambda *_: (0,) * nd)


def _resident(shape):
    nd = len(shape)
    return pl.BlockSpec(shape, lambda *_: (0,) * nd, pipeline_mode=pl.Buffered(1))


def _ada_call(cvec, w_ada, b_ada):
    depth, d, d6 = w_ada.shape
    r8 = cvec.shape[0]
    tn = d6 // 4

    def kern(c_ref, w_ref, b_ref, o_ref):
        c = c_ref[...]
        s = (c * _sigmoid(c)).astype(BF16)
        o_ref[...] = _dot(s, w_ref[...].astype(BF16)) + b_ref[...]

    return pl.pallas_call(
        kern, grid=(depth, d6 // tn),
        in_specs=[pl.BlockSpec((r8, d), lambda i, j: (0, 0)),
                  pl.BlockSpec((None, d, tn), lambda i, j: (i, 0, j)),
                  pl.BlockSpec((None, 1, tn), lambda i, j: (i, 0, j))],
        out_specs=pl.BlockSpec((None, r8, tn), lambda i, j: (i, 0, j)),
        out_shape=SDS((depth, r8, d6), F32), name="ada",
        compiler_params=_params(("arbitrary", "arbitrary"), 40))(cvec, w_ada, b_ada.reshape(depth, 1, d6))


def _outproj_call(st, a, mods, w, name):
    n, k = a.shape
    d = w.shape[1]
    t = st.tile

    def kern(a_ref, g1_ref, w_ref, o_ref):
        o_ref[...] = (g1_ref[...] * _dot(a_ref[...], w_ref[...])).astype(BF16)

    return pl.pallas_call(
        kern, grid=(st.tiles,),
        in_specs=[pl.BlockSpec((t, k), lambda i: (i, 0)), _mod_spec(st, G1, d), _resident((k, d))],
        out_specs=pl.BlockSpec((t, d), lambda i: (i, 0)),
        out_shape=SDS((n, d), BF16), name=name,
        compiler_params=_params(("parallel",), 40))(a, mods, w)


def _dft_tables(length, radix):
    lr = length // radix
    m = np.arange(lr)[None, :, None]
    j = np.arange(radix)[:, None, None]
    nn = np.arange(lr)[None, None, :]
    ang = 2.0 * np.pi * (((radix * m + j) * nn) % length) / length
    e = np.concatenate([np.cos(ang), np.sin(ang)], axis=-1) / np.sqrt(length)
    return jnp.asarray(e, dtype=F32).astype(BF16)


def _group_dft_tables(group):
    k = np.arange(group)
    ang = 2.0 * np.pi * ((k[:, None] * k[None, :]) % group) / group
    return (jnp.asarray(np.cos(ang) / np.sqrt(group), dtype=F32),
            jnp.asarray(np.sin(ang) / np.sqrt(group), dtype=F32))


def _fourier_weight_call(w_out):
    d = w_out.shape[0]
    grp = d // N_FOURIER_GROUPS
    cg, sg = _group_dft_tables(grp)

    def kern(cg_ref, sg_ref, w_ref, o_ref):
        w = w_ref[...]
        o_ref[:, :d] = jnp.dot(cg_ref[...], w, preferred_element_type=F32,
                               precision=lax.Precision.HIGHEST).astype(BF16)
        o_ref[:, d:] = jnp.dot(sg_ref[...], w, preferred_element_type=F32,
                               precision=lax.Precision.HIGHEST).astype(BF16)

    return pl.pallas_call(
        kern, grid=(N_FOURIER_GROUPS,),
        in_specs=[_full((grp, grp)), _full((grp, grp)), pl.BlockSpec((grp, d), lambda g: (g, 0))],
        out_specs=pl.BlockSpec((grp, 2 * d), lambda g: (g, 0)),
        out_shape=SDS((d, 2 * d), BF16), name="fourier_w",
        compiler_params=_params(("arbitrary",), 32))(cg, sg, w_out)


def _radix_terms(radix, j):
    real, imag = [], []
    for q in range(radix):
        k = (j * q * (4 // radix)) % 4 if radix > 1 else 0
        if k == 0:
            real.append((1, 0, q)); imag.append((-1, 1, q))
        elif k == 1:
            real.append((-1, 1, q)); imag.append((-1, 0, q))
        elif k == 2:
            real.append((-1, 0, q)); imag.append((1, 1, q))
        else:
            real.append((1, 1, q)); imag.append((1, 0, q))
    return real, imag


def _fourier_mix_call(st, x, mods, gain, wcs, radix):
    n, d = x.shape
    seg = st.seg
    lr = seg // radix
    e = _dft_tables(seg, radix)
    nb = n // seg
    rt = min(seg, 512)
    cblk = 2 * LANES

    def kern(x_ref, g_ref, sc_ref, sh_ref, g1_ref, w_ref, e_ref, o_ref, p_ref, v_ref, *z_refs):
        for i in range(seg // rt):
            rows = slice(i * rt, (i + 1) * rt)
            h = _modnorm(x_ref[rows, :], g_ref[...], sc_ref[...], sh_ref[...]).astype(BF16)
            p_ref[rows, :] = _dot(h, w_ref[...]).astype(BF16)
        for c in range(d // cblk):
            cols = (slice(c * cblk, (c + 1) * cblk), slice(d + c * cblk, d + (c + 1) * cblk))
            for j in range(radix):
                real, imag = _radix_terms(radix, j)

                def comb(terms):
                    acc = None
                    for sgn, part, q in terms:
                        v = p_ref[q * lr:(q + 1) * lr, cols[part]]
                        if acc is None:
                            acc = v if sgn > 0 else -v
                        else:
                            acc = acc + v if sgn > 0 else acc - v
                    return acc

                v_ref[0:lr, :] = comb(real).astype(BF16)
                v_ref[lr:2 * lr, :] = comb(imag).astype(BF16)
                z = _dot(e_ref[j], v_ref[...])
                for k, z_ref in enumerate(z_refs):
                    z_ref[pl.ds(j, lr, stride=radix), :] = z[:, k * LANES:(k + 1) * LANES]
            for k, z_ref in enumerate(z_refs):
                sl = slice(c * cblk + k * LANES, c * cblk + (k + 1) * LANES)
                o_ref[:, sl] = (g1_ref[:, sl] * z_ref[...]).astype(BF16)

    mod = lambda chunk: pl.BlockSpec((None, 1, d), lambda b: (st.seg_row(b), 0, chunk))
    return pl.pallas_call(
        kern, grid=(nb,),
        in_specs=[pl.BlockSpec((seg, d), lambda b: (b, 0)), _full((1, d)), mod(SC1), mod(SH1), mod(G1),
                  _resident((d, 2 * d)), _resident((radix, lr, 2 * lr))],
        out_specs=pl.BlockSpec((seg, d), lambda b: (b, 0)),
        out_shape=SDS((n, d), BF16),
        scratch_shapes=[pltpu.VMEM((seg, 2 * d), BF16), pltpu.VMEM((2 * lr, cblk), BF16)]
        + [pltpu.VMEM((seg, LANES), F32)] * (cblk // LANES),
        name=f"fourier_mix_r{radix}",
        compiler_params=_params(("parallel",), 56))(x, gain, mods, mods, mods, wcs, e)


def _fourier_layer(streams, xs, mods, gain, w_out):
    wcs = _fourier_weight_call(w_out)
    return [None if x is None else _fourier_mix_call(st, x, mods, gain, wcs, 4 if st.seg >= 1024 else 1)
            for st, x in zip(streams, xs)]


def _conv_mix_call(st, x, mods, gain, w_in, conv_w, w_out):
    n, d = x.shape
    seg = st.seg
    nb = n // seg
    rt = min(seg, 512)
    nrt = seg // rt
    halo = 16

    def kern(x_ref, g_ref, sc_ref, sh_ref, g1_ref, win_ref, cw_ref, wout_ref, o_ref, bg_ref, u_ref):
        for i in range(nrt):
            rows = slice(i * rt, (i + 1) * rt)
            h = _modnorm(x_ref[rows, :], g_ref[...], sc_ref[...], sh_ref[...]).astype(BF16)
            bg_ref[rows, :] = _dot(h, win_ref[:, 0:d]).astype(BF16)
            u_ref[rows, :] = (_dot(h, win_ref[:, d:2 * d]) * _dot(h, win_ref[:, 2 * d:3 * d])).astype(BF16)
        row = lax.broadcasted_iota(jnp.int32, (rt, 1), 0)
        zero_row = jnp.zeros((1, d), F32)
        for i in range(nrt):
            rows = slice(i * rt, (i + 1) * rt)
            u = u_ref[rows, :].astype(F32)
            prev_row = zero_row if i == 0 else u_ref[i * rt - halo:i * rt, :].astype(F32)[halo - 1:halo, :]
            next_row = (zero_row if i == nrt - 1
                        else u_ref[(i + 1) * rt:(i + 1) * rt + halo, :].astype(F32)[0:1, :])
            um = jnp.where(row == 0, prev_row, pltpu.roll(u, 1, axis=0))
            up = jnp.where(row == rt - 1, next_row, pltpu.roll(u, rt - 1, axis=0))
            z = cw_ref[0:1, :] * um + cw_ref[1:2, :] * u + cw_ref[2:3, :] * up
            a = (bg_ref[rows, :].astype(F32) * z).astype(BF16)
            o_ref[rows, :] = (g1_ref[...] * _dot(a, wout_ref[...])).astype(BF16)

    mod = lambda chunk: pl.BlockSpec((None, 1, d), lambda b: (st.seg_row(b), 0, chunk))
    seg_spec = pl.BlockSpec((seg, d), lambda b: (b, 0))
    return pl.pallas_call(
        kern, grid=(nb,),
        in_specs=[seg_spec, _full((1, d)), mod(SC1), mod(SH1), mod(G1),
                  _resident((d, 3 * d)), _full((3, d)), _resident((d, d))],
        out_specs=seg_spec, out_shape=SDS((n, d), BF16),
        scratch_shapes=[pltpu.VMEM((seg, d), BF16), pltpu.VMEM((seg, d), BF16)],
        name="conv_mix", compiler_params=_params(("parallel",), 56))(
            x, gain, mods, mods, mods, w_in, conv_w, w_out)


def _conv_layer(streams, xs, mods, gain, w_in, conv_w, w_out):
    w_in_b, w_out_b = w_in.astype(BF16), w_out.astype(BF16)
    return [None if x is None else _conv_mix_call(st, x, mods, gain, w_in_b, conv_w, w_out_b)
            for st, x in zip(streams, xs)]


def _rope_tables(length):
    q = ROPE_DIM // 4
    pos = np.arange(length)
    rc = np.stack([pos // GRID_W, pos % GRID_W], axis=1).astype(np.float32)
    inv = (ROPE_THETA ** (-np.arange(q, dtype=np.float32) / q)).astype(np.float32)
    lane = np.arange(ROPE_DIM)
    ang = rc[:, lane >> 5] * inv[lane & (q - 1)][None, :]
    sign = np.where((lane & q) == 0, -1.0, 1.0)[None, :]
    cos = np.cos(ang.astype(np.float32)).astype(np.float32)
    sin = (np.sin(ang.astype(np.float32)) * sign).astype(np.float32)
    reps = LANES // ROPE_DIM
    return jnp.asarray(np.tile(cos, (1, reps))), jnp.asarray(np.tile(sin, (1, reps)))


def _rope_partner_perm(n_heads):
    lane = np.arange(n_heads * ROPE_DIM)
    return lane ^ (ROPE_DIM // 4)


def _seg_rinv(x, seg):
    t, w = x.shape
    cols = []
    for c in range(w // LANES):
        blk = x[:, c * LANES:(c + 1) * LANES]
        sq = blk * blk
        if seg == LANES:
            cols.append(jnp.broadcast_to(
                lax.rsqrt(jnp.mean(sq, axis=-1, keepdims=True) + EPS), (t, LANES)))
        else:
            lane = lax.broadcasted_iota(jnp.int32, (t, LANES), 1)
            low = lane < seg
            lo = jnp.sum(jnp.where(low, sq, 0.0), axis=-1, keepdims=True) * (1.0 / seg)
            hi = jnp.sum(jnp.where(low, 0.0, sq), axis=-1, keepdims=True) * (1.0 / seg)
            cols.append(jnp.where(low, lax.rsqrt(lo + EPS), lax.rsqrt(hi + EPS)))
    return cols[0] if len(cols) == 1 else jnp.concatenate(cols, axis=-1)


def _tile_lanes(v, w):
    reps = w // LANES
    return v if reps == 1 else jnp.concatenate([v] * reps, axis=-1)


def _mla_proj_call(st, x, mods, gain, wts, rope, need_q):
    n, d = x.shape
    t = st.tile
    h = MLA_HEADS
    wn, wr = h * MLA_NOPE, h * MLA_ROPE
    (w_in, q_norm, kv_norm, w_uq, w_ukv, qg_n, qg_r, qg_rp, kg_n, kg_r, kg_rp) = wts
    use_rope = rope is not None

    def kern(*refs):
        refs = list(refs)
        x_ref, g_ref, sc_ref, sh_ref, win_ref, qn_ref_, kvn_ref_, wuq_ref, wukv_ref = refs[:9]
        qgn_ref, qgr_ref, qgrp_ref, kgn_ref, kgr_ref, kgrp_ref = refs[9:15]
        rest = refs[15:]
        if use_rope:
            cos_ref, sin_ref = rest[:2]
            rest = rest[2:]
        if need_q:
            oqn_ref, oqr_ref = rest[:2]
            rest = rest[2:]
        okn_ref, okr_ref, ov_ref = rest
        hm = _modnorm(x_ref[...], g_ref[...], sc_ref[...], sh_ref[...]).astype(BF16)
        ck = _dot(hm, win_ref[...])
        kr2 = ck[:, MLA_Q_RANK + MLA_KV_RANK:]
        kr_rinv = _seg_rinv(kr2, MLA_ROPE)[:, 0:MLA_ROPE]
        kr_raw = kr2[:, 0:MLA_ROPE] * kgr_ref[...]
        if use_rope:
            kr_par = kr2[:, MLA_ROPE:] * kgrp_ref[...]
            kr = kr_rinv * (kr_raw * cos_ref[:, 0:MLA_ROPE] + kr_par * sin_ref[:, 0:MLA_ROPE])
        else:
            kr = kr_rinv * kr_raw
        okr_ref[...] = kr.astype(BF16)
        ckv = ck[:, MLA_Q_RANK:MLA_Q_RANK + MLA_KV_RANK]
        ckv = (ckv * _seg_rinv(ckv, LANES) * kvn_ref_[...]).astype(BF16)
        kv = _dot(ckv, wukv_ref[...])
        kn = kv[:, 0:wn]
        okn_ref[...] = (kn * _seg_rinv(kn, MLA_NOPE) * kgn_ref[...]).astype(BF16)
        ov_ref[...] = kv[:, wn:].astype(BF16)
        if need_q:
            cq = ck[:, 0:MLA_Q_RANK]
            rq = lax.rsqrt(jnp.mean(cq * cq, axis=-1, keepdims=True) + EPS)
            cq = (cq * rq * qn_ref_[...]).astype(BF16)
            q = _dot(cq, wuq_ref[...])
            qn = q[:, 0:wn]
            oqn_ref[...] = (qn * _seg_rinv(qn, MLA_NOPE) * (qgn_ref[...] * (MLA_SCALE * LOG2E))).astype(BF16)
            qr_raw = q[:, wn:wn + wr]
            rinv = _seg_rinv(qr_raw, MLA_ROPE) * (MLA_SCALE * LOG2E)
            if use_rope:
                qr_par = q[:, wn + wr:]
                cos = _tile_lanes(cos_ref[...], wr)
                sin = _tile_lanes(sin_ref[...], wr)
                qr = rinv * (qr_raw * qgr_ref[...] * cos + qr_par * qgrp_ref[...] * sin)
            else:
                qr = rinv * (qr_raw * qgr_ref[...])
            oqr_ref[...] = qr.astype(BF16)

    tile = lambda w: pl.BlockSpec((t, w), lambda i: (i, 0))
    in_specs = [tile(d), _full((1, d)), _mod_spec(st, SC1, d), _mod_spec(st, SH1, d),
                _resident(w_in.shape), _full(q_norm.shape), _full(kv_norm.shape),
                _resident(w_uq.shape), _resident(w_ukv.shape),
                _full(qg_n.shape), _full(qg_r.shape), _full(qg_rp.shape),
                _full(kg_n.shape), _full(kg_r.shape), _full(kg_rp.shape)]
    args = [x, gain, mods, mods, w_in, q_norm, kv_norm, w_uq, w_ukv, qg_n, qg_r, qg_rp, kg_n, kg_r, kg_rp]
    if use_rope:
        tps = st.tiles_per_seg
        in_specs += [pl.BlockSpec((t, LANES), lambda i: (i % tps, 0))] * 2
        args += list(rope)
    out_specs, out_shape = [], []
    if need_q:
        out_specs += [tile(wn), tile(wr)]
        out_shape += [SDS((n, wn), BF16), SDS((n, wr), BF16)]
    out_specs += [tile(wn), tile(MLA_ROPE), tile(h * MLA_V)]
    out_shape += [SDS((n, wn), BF16), SDS((n, MLA_ROPE), BF16), SDS((n, h * MLA_V), BF16)]
    outs = pl.pallas_call(
        kern, grid=(st.tiles,), in_specs=in_specs, out_specs=out_specs, out_shape=out_shape,
        name="mla_proj", compiler_params=_params(("parallel",), 48))(*args)
    if need_q:
        return tuple(outs)
    return (None, None) + tuple(outs)


def _mla_attn_fast_call(batch, lq, tq, q, kv_sets, shift):
    qn, qr = q
    lk_total = sum(s[3] for s in kv_sets)
    hp = 4 if lk_total > 1024 else 8
    n_hp = MLA_HEADS // hp
    nq = lq // tq
    nsets = len(kv_sets)
    lengths = [s[3] for s in kv_sets]
    lk = sum(lengths)
    kw = 2 * LANES
    pad = kw - MLA_NOPE - MLA_ROPE

    def kern(*refs):
        sh_ref, qn_ref, qr_ref = refs[:3]
        sets = [refs[3 + 3 * s:6 + 3 * s] for s in range(nsets)]
        o_ref, kq_ref, vq_ref = refs[-3:]

        def build_kv():
            lane_k = lax.broadcasted_iota(jnp.int32, (lk, pad), 1)
            lane_v = lax.broadcasted_iota(jnp.int32, (lk, kw - MLA_V), 1)
            for hh in range(hp):
                off = 0
                for (kn_ref, kr_ref, v_ref), length in zip(sets, lengths):
                    kq_ref[hh, off:off + length, 0:MLA_NOPE] = kn_ref[:, hh * MLA_NOPE:(hh + 1) * MLA_NOPE]
                    kq_ref[hh, off:off + length, MLA_NOPE:MLA_NOPE + MLA_ROPE] = kr_ref[...]
                    vq_ref[hh, off:off + length, 0:MLA_V] = v_ref[:, hh * MLA_V:(hh + 1) * MLA_V]
                    off += length
                kq_ref[hh, :, MLA_NOPE + MLA_ROPE:] = jnp.where(lane_k == 0, -sh_ref[:, 0:pad], 0.0).astype(BF16)
                vq_ref[hh, :, MLA_V:] = jnp.where(lane_v == 0, 1.0, 0.0).astype(BF16)

        if nq == 1:
            build_kv()
        else:
            pl.when(pl.program_id(2) == 0)(build_kv)

        lane_q = lax.broadcasted_iota(jnp.int32, (tq, pad), 1)
        one = jnp.where(lane_q == 0, 1.0, 0.0).astype(BF16)
        for hh in range(hp):
            qc = jnp.concatenate([qn_ref[:, hh * MLA_NOPE:(hh + 1) * MLA_NOPE],
                                  qr_ref[:, hh * MLA_ROPE:(hh + 1) * MLA_ROPE], one], axis=1)
            p = jnp.exp2(_nt(qc, kq_ref[hh]).astype(BF16))
            acc = _dot(p, vq_ref[hh])
            o_ref[:, hh * MLA_V:(hh + 1) * MLA_V] = (acc[:, 0:MLA_V] / acc[:, MLA_V:MLA_V + 1]).astype(BF16)

    in_specs = [_full((1, LANES)),
                pl.BlockSpec((tq, hp * MLA_NOPE), lambda b, h, i: (b * nq + i, h)),
                pl.BlockSpec((tq, hp * MLA_ROPE), lambda b, h, i: (b * nq + i, h))]
    args = [shift, qn, qr]
    for kn, kr, v, length in kv_sets:
        in_specs += [pl.BlockSpec((length, hp * MLA_NOPE), lambda b, h, i: (b, h)),
                     pl.BlockSpec((length, MLA_ROPE), lambda b, h, i: (b, 0)),
                     pl.BlockSpec((length, hp * MLA_V), lambda b, h, i: (b, h))]
        args += [kn, kr, v]
    return pl.pallas_call(
        kern, grid=(batch, n_hp, nq), in_specs=in_specs,
        out_specs=pl.BlockSpec((tq, hp * MLA_V), lambda b, h, i: (b * nq + i, h)),
        out_shape=SDS((batch * lq, MLA_HEADS * MLA_V), BF16),
        scratch_shapes=[pltpu.VMEM((hp, lk, kw), BF16), pltpu.VMEM((hp, lk, kw), BF16)],
        name="mla_attn_fast",
        compiler_params=_params(("parallel", "parallel", "arbitrary"), 56))(*args)


def _mla_attn_call(batch, lq, tq, q, kv_sets):
    qn, qr = q
    hp = 2
    n_hp = MLA_HEADS // hp
    nq = lq // tq
    nsets = len(kv_sets)

    def kern(*refs):
        qn_ref, qr_ref = refs[:2]
        o_ref = refs[-1]
        sets = [refs[2 + 3 * s:5 + 3 * s] for s in range(nsets)]
        for hh in range(hp):
            qn_h = qn_ref[:, hh * MLA_NOPE:(hh + 1) * MLA_NOPE]
            qr_h = qr_ref[:, hh * MLA_ROPE:(hh + 1) * MLA_ROPE]
            scores = [_nt(qn_h, kn_ref[:, hh * MLA_NOPE:(hh + 1) * MLA_NOPE]) + _nt(qr_h, kr_ref[...])
                      for kn_ref, kr_ref, _ in sets]
            m = functools.reduce(jnp.maximum, [jnp.max(s, axis=-1, keepdims=True) for s in scores])
            ps = [jnp.exp2(s - m) for s in scores]
            den = functools.reduce(lambda a, b: a + b, [jnp.sum(p, axis=-1, keepdims=True) for p in ps])
            acc = functools.reduce(lambda a, b: a + b, [
                _dot(p.astype(BF16), v_ref[:, hh * MLA_V:(hh + 1) * MLA_V])
                for p, (_, _, v_ref) in zip(ps, sets)])
            o_ref[:, hh * MLA_V:(hh + 1) * MLA_V] = (acc / den).astype(BF16)

    in_specs = [pl.BlockSpec((tq, hp * MLA_NOPE), lambda b, h, i: (b * nq + i, h)),
                pl.BlockSpec((tq, hp * MLA_ROPE), lambda b, h, i: (b * nq + i, h))]
    args = [qn, qr]
    for kn, kr, v, length in kv_sets:
        in_specs += [pl.BlockSpec((length, hp * MLA_NOPE), lambda b, h, i: (b, h)),
                     pl.BlockSpec((length, MLA_ROPE), lambda b, h, i: (b, 0)),
                     pl.BlockSpec((length, hp * MLA_V), lambda b, h, i: (b, h))]
        args += [kn, kr, v]
    return pl.pallas_call(
        kern, grid=(batch, n_hp, nq), in_specs=in_specs,
        out_specs=pl.BlockSpec((tq, hp * MLA_V), lambda b, h, i: (b * nq + i, h)),
        out_shape=SDS((batch * lq, MLA_HEADS * MLA_V), BF16), name="mla_attn",
        compiler_params=_params(("parallel", "parallel", "arbitrary"), 56))(*args)


def _mla_layer(streams, xs, mods, gain, p, rope, ctx_next):
    st_c, st_x = streams
    xc, xx = xs
    h = MLA_HEADS
    w_in, q_norm, kv_norm, w_uq, w_ukv, q_gain, k_gain, w_out = p
    par = _rope_partner_perm(1)
    kr_cols = w_in[:, MLA_Q_RANK + MLA_KV_RANK:]
    w_in_p = jnp.concatenate([w_in, kr_cols[:, par]], axis=1).astype(BF16)
    wq = w_uq.reshape(MLA_Q_RANK, h, MLA_NOPE + MLA_ROPE)
    wq_n = wq[:, :, :MLA_NOPE].reshape(MLA_Q_RANK, h * MLA_NOPE)
    wq_r = wq[:, :, MLA_NOPE:]
    w_uq_p = jnp.concatenate([wq_n, wq_r.reshape(MLA_Q_RANK, h * MLA_ROPE),
                              wq_r[:, :, par].reshape(MLA_Q_RANK, h * MLA_ROPE)], axis=1).astype(BF16)
    wkv = w_ukv.reshape(MLA_KV_RANK, h, MLA_NOPE + MLA_V)
    w_ukv_p = jnp.concatenate([wkv[:, :, :MLA_NOPE].reshape(MLA_KV_RANK, h * MLA_NOPE),
                               wkv[:, :, MLA_NOPE:].reshape(MLA_KV_RANK, h * MLA_V)], axis=1).astype(BF16)
    qg_r = q_gain[MLA_NOPE:]
    kg_r = k_gain[MLA_NOPE:]
    wts = (w_in_p, q_norm.reshape(1, -1), kv_norm.reshape(1, -1), w_uq_p, w_ukv_p,
           jnp.tile(q_gain[:MLA_NOPE], h).reshape(1, -1), jnp.tile(qg_r, h).reshape(1, -1),
           jnp.tile(qg_r[par], h).reshape(1, -1), jnp.tile(k_gain[:MLA_NOPE], h).reshape(1, -1),
           kg_r.reshape(1, -1), kg_r[par].reshape(1, -1))
    w_out_b = w_out.astype(BF16)
    qn_c, qr_c, kn_c, kr_c, v_c = _mla_proj_call(st_c.with_tile(PROJ_TILE), xc, mods, gain, wts, None, ctx_next)
    qn_x, qr_x, kn_x, kr_x, v_x = _mla_proj_call(st_x.with_tile(PROJ_TILE), xx, mods, gain, wts, rope, True)
    b = st_x.batch
    amax = lambda g: jnp.max(jnp.abs(g))
    bound = LOG2E * MLA_SCALE * (MLA_NOPE * amax(q_gain[:MLA_NOPE]) * amax(k_gain[:MLA_NOPE])
                                 + MLA_ROPE * amax(qg_r) * amax(kg_r))
    shift = bound * SHIFT_MARGIN + 1.0
    shift_row = jnp.full((1, LANES), shift, F32)

    def attend(lq, tq, q, kv_sets):
        return lax.cond(shift <= MAX_FIXED_SHIFT,
                        lambda: _mla_attn_fast_call(b, lq, tq, q, kv_sets, shift_row),
                        lambda: _mla_attn_call(b, lq, min(tq, 512), q, kv_sets))

    a_x = attend(st_x.seg, min(st_x.seg, 2048), (qn_x, qr_x), [(kn_c, kr_c, v_c, st_c.seg), (kn_x, kr_x, v_x, st_x.seg)])
    out_x = _outproj_call(st_x, a_x, mods, w_out_b, "mla_out")
    out_c = None
    if ctx_next:
        a_c = attend(st_c.seg, st_c.seg, (qn_c, qr_c), [(kn_c, kr_c, v_c, st_c.seg)])
        out_c = _outproj_call(st_c, a_c, mods, w_out_b, "mla_out")
    return [out_c, out_x]


def _gqa_proj_call(st, x, mods, gain, wts, rope, need_q):
    n, d = x.shape
    t = st.tile
    wq, wk = GQA_Q_HEADS * GQA_HEAD_DIM, GQA_KV_HEADS * GQA_HEAD_DIM
    w_in, qg, qgp, kg, kgp = wts
    use_rope = rope is not None

    def kern(*refs):
        refs = list(refs)
        x_ref, g_ref, sc_ref, sh_ref, w_ref, qg_ref, qgp_ref, kg_ref, kgp_ref = refs[:9]
        rest = refs[9:]
        if use_rope:
            cos_ref, sin_ref = rest[:2]
            rest = rest[2:]
        if need_q:
            oq_ref = rest[0]
            rest = rest[1:]
        ok_ref, ov_ref = rest
        hm = _modnorm(x_ref[...], g_ref[...], sc_ref[...], sh_ref[...]).astype(BF16)
        kvp = _dot(hm, w_ref[:, 0:3 * wk])
        k_raw = kvp[:, 0:wk]
        k_rinv = _seg_rinv(k_raw, GQA_HEAD_DIM)
        ov_ref[...] = kvp[:, wk:2 * wk].astype(BF16)
        if use_rope:
            cos_k, sin_k = _tile_lanes(cos_ref[...], wk), _tile_lanes(sin_ref[...], wk)
            k = k_rinv * (k_raw * kg_ref[...] * cos_k + kvp[:, 2 * wk:] * kgp_ref[...] * sin_k)
        else:
            k = k_rinv * (k_raw * kg_ref[...])
        ok_ref[...] = k.astype(BF16)
        if need_q:
            qp = _dot(hm, w_ref[:, 3 * wk:])
            q_raw = qp[:, 0:wq]
            rinv = _seg_rinv(q_raw, GQA_HEAD_DIM) * (GQA_SCALE * LOG2E)
            if use_rope:
                cos_q, sin_q = _tile_lanes(cos_ref[...], wq), _tile_lanes(sin_ref[...], wq)
                q = rinv * (q_raw * qg_ref[...] * cos_q + qp[:, wq:] * qgp_ref[...] * sin_q)
            else:
                q = rinv * (q_raw * qg_ref[...])
            oq_ref[...] = q.astype(BF16)

    tile = lambda w: pl.BlockSpec((t, w), lambda i: (i, 0))
    in_specs = [tile(d), _full((1, d)), _mod_spec(st, SC1, d), _mod_spec(st, SH1, d),
                _resident(w_in.shape), _full(qg.shape), _full(qgp.shape), _full(kg.shape), _full(kgp.shape)]
    args = [x, gain, mods, mods, w_in, qg, qgp, kg, kgp]
    if use_rope:
        tps = st.tiles_per_seg
        in_specs += [pl.BlockSpec((t, LANES), lambda i: (i % tps, 0))] * 2
        args += list(rope)
    out_specs, out_shape = [], []
    if need_q:
        out_specs.append(tile(wq))
        out_shape.append(SDS((n, wq), BF16))
    out_specs += [tile(wk), tile(wk)]
    out_shape += [SDS((n, wk), BF16)] * 2
    outs = pl.pallas_call(
        kern, grid=(st.tiles,), in_specs=in_specs, out_specs=out_specs, out_shape=out_shape,
        name="gqa_proj", compiler_params=_params(("parallel",), 48))(*args)
    return tuple(outs) if need_q else (None,) + tuple(outs)


def _gqa_window_call(batch, length, lc, q, k, v, kc, vc, sink):
    nb = length // BLOCK
    hd = GQA_HEAD_DIM
    wq, wk = GQA_Q_HEADS * hd, GQA_KV_HEADS * hd
    rows = GQA_GROUP * BLOCK

    def kern(sink_ref, q_ref, kp_ref, k0_ref, kn_ref, vp_ref, v0_ref, vn_ref, kc_ref, vc_ref, o_ref):
        nblk = pl.program_id(1)
        r = lax.broadcasted_iota(jnp.int32, (rows, 3 * BLOCK), 0) & (BLOCK - 1)
        c = lax.broadcasted_iota(jnp.int32, (rows, 3 * BLOCK), 1)
        valid = (c >= r + BLOCK - WINDOW) & (c <= r + BLOCK + WINDOW)
        valid = valid & ((c >= BLOCK) | (nblk > 0)) & ((c < 2 * BLOCK) | (nblk < nb - 1))
        hrow = lax.broadcasted_iota(jnp.int32, (rows, 1), 0) // BLOCK
        for g in range(GQA_KV_HEADS):
            sl = slice(g * hd, (g + 1) * hd)
            qg = jnp.concatenate([q_ref[:, (g * GQA_GROUP + j) * hd:(g * GQA_GROUP + j + 1) * hd]
                                  for j in range(GQA_GROUP)], axis=0)
            kband = jnp.concatenate([kp_ref[:, sl], k0_ref[:, sl], kn_ref[:, sl]], axis=0)
            vband = jnp.concatenate([vp_ref[:, sl], v0_ref[:, sl], vn_ref[:, sl]], axis=0)
            s_c = _nt(qg, kc_ref[:, sl])
            s_b = jnp.where(valid, _nt(qg, kband), -1e30)
            snk = jnp.zeros((rows, 1), F32)
            for j in range(GQA_GROUP):
                snk = jnp.where(hrow == j, sink_ref[g * GQA_GROUP + j], snk)
            m = jnp.maximum(jnp.maximum(jnp.max(s_c, axis=-1, keepdims=True),
                                        jnp.max(s_b, axis=-1, keepdims=True)), snk)
            p_c = jnp.exp2(s_c - m)
            p_b = jnp.exp2(s_b - m)
            den = (jnp.sum(p_c, axis=-1, keepdims=True) + jnp.sum(p_b, axis=-1, keepdims=True)
                   + jnp.exp2(snk - m))
            o = (_dot(p_c.astype(BF16), vc_ref[:, sl]) + _dot(p_b.astype(BF16), vband)) / den
            o_ref[:, g * GQA_GROUP * hd:(g + 1) * GQA_GROUP * hd] = jnp.concatenate(
                [o[j * BLOCK:(j + 1) * BLOCK, :] for j in range(GQA_GROUP)], axis=-1).astype(BF16)

    blk = lambda f: pl.BlockSpec((BLOCK, wk), f)
    prev_ = lambda b, i: (b * nb + jnp.maximum(i - 1, 0), 0)
    cur_ = lambda b, i: (b * nb + i, 0)
    next_ = lambda b, i: (b * nb + jnp.minimum(i + 1, nb - 1), 0)
    ctx_spec = pl.BlockSpec((lc, wk), lambda b, i: (b, 0))
    return pl.pallas_call(
        kern, grid=(batch, nb),
        in_specs=[pl.BlockSpec(memory_space=pltpu.SMEM),
                  pl.BlockSpec((BLOCK, wq), cur_), blk(prev_), blk(cur_), blk(next_),
                  blk(prev_), blk(cur_), blk(next_), ctx_spec, ctx_spec],
        out_specs=pl.BlockSpec((BLOCK, wq), cur_),
        out_shape=SDS((batch * length, wq), BF16), name="gqa_window",
        compiler_params=_params(("parallel", "arbitrary"), 40))(sink, q, k, k, k, v, v, v, kc, vc)


def _gqa_window_fast_call(batch, length, lc, q, k, v, kc, vc, sink2, shift):
    nb = length // BLOCK
    hd = GQA_HEAD_DIM
    wq, wk = GQA_Q_HEADS * hd, GQA_KV_HEADS * hd
    rows = GQA_GROUP * BLOCK

    def kern(sink_ref, shift_ref, q_ref, kp_ref, k0_ref, kn_ref, vp_ref, v0_ref, vn_ref, kc_ref, vc_ref, o_ref):
        nblk = pl.program_id(1)
        r = lax.broadcasted_iota(jnp.int32, (rows, 3 * BLOCK), 0) & (BLOCK - 1)
        c = lax.broadcasted_iota(jnp.int32, (rows, 3 * BLOCK), 1)
        valid = (c >= r + BLOCK - WINDOW) & (c <= r + BLOCK + WINDOW)
        valid = valid & ((c >= BLOCK) | (nblk > 0)) & ((c < 2 * BLOCK) | (nblk < nb - 1))
        hrow = lax.broadcasted_iota(jnp.int32, (rows, 1), 0) // BLOCK
        lane0_q = lax.broadcasted_iota(jnp.int32, (BLOCK, hd), 1) == 0

        def with_one(x):
            lane0 = lax.broadcasted_iota(jnp.int32, (x.shape[0], hd), 1) == 0
            return jnp.concatenate([x, jnp.where(lane0, 1.0, 0.0).astype(BF16)], axis=1)

        probs, values = [], []
        for g in range(GQA_KV_HEADS):
            sl = slice(g * hd, (g + 1) * hd)
            qg = jnp.concatenate(
                [jnp.concatenate([q_ref[:, h * hd:(h + 1) * hd],
                                  jnp.where(lane0_q, -shift_ref[h], 0.0).astype(BF16)], axis=1)
                 for h in range(g * GQA_GROUP, (g + 1) * GQA_GROUP)], axis=0)
            keys = with_one(jnp.concatenate([kc_ref[:, sl], kp_ref[:, sl], k0_ref[:, sl], kn_ref[:, sl]], axis=0))
            values.append(with_one(jnp.concatenate(
                [vc_ref[:, sl], vp_ref[:, sl], v0_ref[:, sl], vn_ref[:, sl]], axis=0)))
            s = _nt(qg, keys)
            s = jnp.concatenate([s[:, 0:lc], jnp.where(valid, s[:, lc:], -1e30)], axis=1)
            probs.append(jnp.exp2(s.astype(BF16)))
        for g in range(GQA_KV_HEADS):
            heads = range(g * GQA_GROUP, (g + 1) * GQA_GROUP)
            acc = _dot(probs[g], values[g])
            snk = jnp.zeros((rows, 1), F32)
            for j, h in enumerate(heads):
                snk = jnp.where(hrow == j, sink_ref[h] - shift_ref[h], snk)
            o = acc[:, 0:hd] / (acc[:, hd:hd + 1] + jnp.exp2(snk))
            o_ref[:, g * GQA_GROUP * hd:(g + 1) * GQA_GROUP * hd] = jnp.concatenate(
                [o[j * BLOCK:(j + 1) * BLOCK, :] for j in range(GQA_GROUP)], axis=-1).astype(BF16)

    blk = lambda f: pl.BlockSpec((BLOCK, wk), f)
    prev_ = lambda b, i: (b * nb + jnp.maximum(i - 1, 0), 0)
    cur_ = lambda b, i: (b * nb + i, 0)
    next_ = lambda b, i: (b * nb + jnp.minimum(i + 1, nb - 1), 0)
    ctx_spec = pl.BlockSpec((lc, wk), lambda b, i: (b, 0))
    smem = pl.BlockSpec(memory_space=pltpu.SMEM)
    return pl.pallas_call(
        kern, grid=(batch, nb),
        in_specs=[smem, smem, pl.BlockSpec((BLOCK, wq), cur_), blk(prev_), blk(cur_), blk(next_),
                  blk(prev_), blk(cur_), blk(next_), ctx_spec, ctx_spec],
        out_specs=pl.BlockSpec((BLOCK, wq), cur_),
        out_shape=SDS((batch * length, wq), BF16), name="gqa_window_fast",
        compiler_params=_params(("parallel", "arbitrary"), 40))(sink2, shift, q, k, k, k, v, v, v, kc, vc)


def _gqa_layer(streams, xs, mods, gain, p, rope, ctx_next):
    assert not ctx_next, "the windowed-GQA mixer is only implemented as the last layer"
    st_c, st_x = streams
    xc, xx = xs
    w_in, q_gain, k_gain, sink, w_out = p
    wq, wk = GQA_Q_HEADS * GQA_HEAD_DIM, GQA_KV_HEADS * GQA_HEAD_DIM
    w_q, w_k, w_v = w_in[:, :wq], w_in[:, wq:wq + wk], w_in[:, wq + wk:]
    w_in_p = jnp.concatenate([w_k, w_v, w_k[:, _rope_partner_perm(GQA_KV_HEADS)],
                              w_q, w_q[:, _rope_partner_perm(GQA_Q_HEADS)]], axis=1).astype(BF16)
    par = _rope_partner_perm(1)
    wts = (w_in_p, jnp.tile(q_gain, GQA_Q_HEADS).reshape(1, -1), jnp.tile(q_gain[par], GQA_Q_HEADS).reshape(1, -1),
           jnp.tile(k_gain, GQA_KV_HEADS).reshape(1, -1), jnp.tile(k_gain[par], GQA_KV_HEADS).reshape(1, -1))
    _, kc, vc = _gqa_proj_call(st_c.with_tile(PROJ_TILE), xc, mods, gain, wts, None, False)
    q, k, v = _gqa_proj_call(st_x.with_tile(PROJ_TILE), xx, mods, gain, wts, rope, True)
    sink2 = sink * LOG2E
    bound = LOG2E * GQA_SCALE * GQA_HEAD_DIM * jnp.max(jnp.abs(q_gain)) * jnp.max(jnp.abs(k_gain))
    shift = jnp.maximum(bound * SHIFT_MARGIN + 1.0, sink2).astype(BF16).astype(F32)
    args = (st_x.batch, st_x.seg, st_c.seg, q, k, v, kc, vc, sink2)
    a = lax.cond(jnp.max(shift) <= MAX_FIXED_SHIFT,
                 lambda: _gqa_window_fast_call(*args, shift),
                 lambda: _gqa_window_call(*args))
    return [None, _outproj_call(st_x, a, mods, w_out.astype(BF16), "gqa_out")]


def _cast_experts_call(w, layer):
    _, ne, a, b = w.shape
    eb = 4

    def kern(w_ref, o_ref):
        o_ref[...] = w_ref[...].astype(BF16)

    return pl.pallas_call(
        kern, grid=(ne // eb,),
        in_specs=[pl.BlockSpec((None, eb, a, b), lambda i: (layer, i, 0, 0))],
        out_specs=pl.BlockSpec((eb, a, b), lambda i: (i, 0, 0)),
        out_shape=SDS((ne, a, b), BF16), name="cast_experts",
        compiler_params=_params(("parallel",), 32))(w)


def _route(logits_t, bias_col):
    scores = _sigmoid(logits_t)
    biased = scores + bias_col
    rows = [biased[e:e + 1, :] for e in range(N_EXPERTS)]
    srow = [scores[e:e + 1, :] for e in range(N_EXPERTS)]
    epg = EXPERTS_PER_GROUP
    gscore = []
    for g in range(N_EXPERT_GROUPS):
        v = rows[g * epg:(g + 1) * epg]
        pair = [v[a] + v[b] for a in range(epg) for b in range(a + 1, epg)]
        gscore.append(functools.reduce(jnp.maximum, pair))
    ind, wloc = [], [None] * epg
    for g in range(N_EXPERT_GROUPS):
        best = None
        for g2 in range(N_EXPERT_GROUPS):
            if g2 == g:
                continue
            cnd = gscore[g] > gscore[g2] if g2 < g else gscore[g] >= gscore[g2]
            best = cnd if best is None else best & cnd
        ind.append(jnp.where(best, 1.0, 0.0))
        for j in range(epg):
            e = g * epg + j
            rank = None
            for e2 in range(g * epg, (g + 1) * epg):
                if e2 == e:
                    continue
                ahead = rows[e2] >= rows[e] if e2 < e else rows[e2] > rows[e]
                one = jnp.where(ahead, 1.0, 0.0)
                rank = one if rank is None else rank + one
            w = jnp.where(best & (rank < 2.0), srow[e], 0.0)
            wloc[j] = w if wloc[j] is None else wloc[j] + w
    den = functools.reduce(lambda a, b: a + b, wloc)
    return ind, [w / den for w in wloc]


MOE_TILE = 1024
MOE_CHUNK = 128


def _moe_call(st, x_in, y_in, mods, gain, router_wt, router_bias, w_gate, w_up, w_down):
    n, d = x_in.shape
    t = min(MOE_TILE, n)
    r = MOE_CHUNK
    ne, _, ff = w_gate.shape
    epg, ng = EXPERTS_PER_GROUP, N_EXPERT_GROUPS
    nch = t // r + ng - 1
    tiles = n // t
    wd2 = w_down.reshape(ne * ff, d)

    def kern(x_ref, y_ref, g_ref, sc_ref, sh_ref, g2_ref, rw_ref, rb_ref, wg_ref, wu_ref, wd_ref, o_ref,
             hb_ref, oh_ref, xg_ref):
        hb_ref[...] = _modnorm(x_ref[...] + y_ref[...].astype(F32), g_ref[...], sc_ref[...], sh_ref[...]).astype(BF16)
        ind, wloc = _route(_nt(rw_ref[...], hb_ref[...]), rb_ref[...])
        ind8 = jnp.concatenate(ind + [jnp.zeros((8 - ng, t), F32)], axis=0)
        upper = jnp.where(lax.broadcasted_iota(jnp.int32, (LANES, LANES), 0)
                          < lax.broadcasted_iota(jnp.int32, (LANES, LANES), 1), 1.0, 0.0).astype(BF16)
        blocks, before = [], jnp.zeros((8, 1), F32)
        for k in range(t // LANES):
            blk = ind8[:, k * LANES:(k + 1) * LANES]
            blocks.append(_dot(blk.astype(BF16), upper) + before)
            before = before + jnp.sum(blk, axis=1, keepdims=True)
        pos = jnp.concatenate(blocks, axis=1)
        count = [jnp.sum(ind[g]).astype(jnp.int32) for g in range(ng)]
        first = [jnp.int32(0)]
        for g in range(ng):
            first.append(first[g] + (count[g] + (r - 1)) // r)
        slot = functools.reduce(lambda a, b: a + b, [
            ind[g] * (pos[g:g + 1, :] + (first[g] * r).astype(F32)) for g in range(ng)])
        w_hi = [w.astype(BF16) for w in wloc]
        w_lo = [(w - h.astype(F32)).astype(BF16) for w, h in zip(wloc, w_hi)]
        wst = jnp.concatenate(w_hi + w_lo, axis=0)
        row_id = lax.broadcasted_iota(jnp.int32, (r, t), 0).astype(F32)
        onehot = lambda c: jnp.where(slot == row_id + float(c * r), 1.0, 0.0).astype(BF16)
        last = nch - 1
        for c in range(last):
            oh_ref[c * r:(c + 1) * r, :] = onehot(c)
        xg_ref[...] = _dot(oh_ref[...], hb_ref[...]).astype(BF16)
        wr_all = _nt(wst, oh_ref[...]).T

        def chunk_ffn(c, xg, wr):
            grp = functools.reduce(lambda a, b: a + b,
                                   [(c >= first[g]).astype(jnp.int32) for g in range(1, ng)])
            acts = []
            for j in range(epg):
                e = grp * epg + j
                gt = _dot(xg, wg_ref[e])
                up = _dot(xg, wu_ref[e])
                acts.append((gt * _sigmoid(gt) * up * (wr[:, j:j + 1] + wr[:, epg + j:epg + j + 1])).astype(BF16))
            wd_g = wd_ref[pl.ds(pl.multiple_of(grp * (epg * ff), epg * ff), epg * ff), :]
            return _dot(jnp.concatenate(acts, axis=1), wd_g).astype(BF16)

        for c in range(last):
            rows = slice(c * r, (c + 1) * r)

            def run(c=c, rows=rows):
                xg_ref[rows, :] = chunk_ffn(c, xg_ref[rows, :], wr_all[c * r:(c + 1) * r, :])

            if c < t // r:
                run()
            else:
                pl.when(c < first[ng])(run)

        tdot = lambda a, b: lax.dot_general(a, b, (((0,), (0,)), ((), ())), preferred_element_type=F32)
        o_ref[...] = x_ref[...] + y_ref[...].astype(F32) + g2_ref[...] * tdot(oh_ref[...], xg_ref[...])

        @pl.when(last < first[ng])
        def _():
            oh = onehot(last)
            out = chunk_ffn(last, _dot(oh, hb_ref[...]).astype(BF16), _nt(wst, oh).T)
            o_ref[...] += g2_ref[...] * tdot(oh, out)

    row = (lambda i: st.batch) if st.is_ctx else (lambda i: i // (st.seg // t))
    mod = lambda chunk: pl.BlockSpec((None, 1, d), lambda i: (row(i), 0, chunk))
    return pl.pallas_call(
        kern, grid=(tiles,),
        in_specs=[pl.BlockSpec((t, d), lambda i: (i, 0)),
                  pl.BlockSpec((t, d), lambda i: (i, 0)),
                  _full((1, d)), mod(SC2), mod(SH2), mod(G2), _full((ne, d)), _full((ne, 1)),
                  _resident((ne, d, ff)), _resident((ne, d, ff)), _resident((ne * ff, d))],
        out_specs=pl.BlockSpec((t, d), lambda i: (i, 0)),
        out_shape=SDS((n, d), F32),
        scratch_shapes=[pltpu.VMEM((t, d), BF16), pltpu.VMEM(((nch - 1) * r, t), BF16),
                        pltpu.VMEM(((nch - 1) * r, d), BF16)],
        name="moe", compiler_params=_params(("parallel",), 58))(
            x_in, y_in, gain, mods, mods, mods, router_wt, router_bias, w_gate, w_up, wd2)


def kernel(x, c, ctx, c_ctx, w_ada, b_ada, norm_mix, norm_ffn, fourier_w_out, conv_w_in, conv_w, conv_w_out,
           mla_w_in, mla_q_norm, mla_kv_norm, mla_w_uq, mla_w_ukv, mla_q_gain, mla_k_gain, mla_w_out,
           gqa_w_in, gqa_q_gain, gqa_k_gain, gqa_sink, gqa_w_out, router_w, router_bias,
           moe_w_gate, moe_w_up, moe_w_down):
    b, l, d = x.shape
    lc = ctx.shape[1]
    depth = w_ada.shape[0]
    st_c = Stream(b * lc, lc, min(b * lc, TOKEN_TILE), b, True)
    st_x = Stream(b * l, l, min(l, TOKEN_TILE), b, False)
    streams = [st_c, st_x]
    r8 = -(-(b + 1) // 8) * 8
    cvec = jnp.concatenate([c, c_ctx[None, :], jnp.zeros((r8 - b - 1, d), F32)], axis=0)
    mods_all = _ada_call(cvec, w_ada, b_ada).reshape(depth, r8, 1, 6 * d)
    rope = _rope_tables(l)
    router_wt = router_w.T.astype(BF16)
    router_b = router_bias.reshape(-1, 1)
    xs = [ctx.reshape(b * lc, d), x.reshape(b * l, d)]
    for i in range(depth):
        kind, j = i % 4, i // 4
        ctx_next = i < depth - 1
        mods = mods_all[i]
        gain = norm_mix[i].reshape(1, d)
        if not (ctx_next or kind >= 2):
            xs[0] = None
        if kind == 0:
            ys = _fourier_layer(streams, xs, mods, gain, fourier_w_out[j])
        elif kind == 1:
            ys = _conv_layer(streams, xs, mods, gain, conv_w_in[j], conv_w[j], conv_w_out[j])
        elif kind == 2:
            ys = _mla_layer(streams, xs, mods, gain,
                            (mla_w_in[j], mla_q_norm[j], mla_kv_norm[j], mla_w_uq[j], mla_w_ukv[j],
                             mla_q_gain[j], mla_k_gain[j], mla_w_out[j]), rope, ctx_next)
        else:
            ys = _gqa_layer(streams, xs, mods, gain,
                            (gqa_w_in[j], gqa_q_gain[j], gqa_k_gain[j], gqa_sink[j], gqa_w_out[j]),
                            rope, ctx_next)
        if not ctx_next:
            ys[0] = None
        gain2 = norm_ffn[i].reshape(1, d)
        wg, wu, wd = (_cast_experts_call(w, i) for w in (moe_w_gate, moe_w_up, moe_w_down))
        xs = [None if y is None else _moe_call(st, x, y, mods, gain2, router_wt, router_b, wg, wu, wd)
              for st, x, y in zip(streams, xs, ys)]
    return xs[1].reshape(b, l, d)
```

```python
import functools

import numpy as np
import jax
import jax.numpy as jnp
from jax import lax
from jax.experimental import pallas as pl
from jax.experimental.pallas import tpu as pltpu

F32, BF16 = jnp.float32, jnp.bfloat16
SDS = jax.ShapeDtypeStruct

EPS = 1e-6
GRID_W = 64
ROPE_THETA = 10000.0
N_FOURIER_GROUPS = 4
MLA_HEADS, MLA_Q_RANK, MLA_KV_RANK = 16, 256, 128
MLA_NOPE, MLA_ROPE, MLA_V = 128, 64, 128
MLA_SCALE = (MLA_NOPE + MLA_ROPE) ** -0.5
GQA_Q_HEADS, GQA_KV_HEADS, GQA_HEAD_DIM = 16, 4, 64
GQA_GROUP = GQA_Q_HEADS // GQA_KV_HEADS
GQA_SCALE = GQA_HEAD_DIM ** -0.5
WINDOW = 128
BLOCK = 128
N_EXPERTS, N_EXPERT_GROUPS, EXPERT_FF = 16, 4, 256
EXPERTS_PER_GROUP = N_EXPERTS // N_EXPERT_GROUPS
ROPE_DIM = 64
LOG2E = 1.4426950408889634
MAX_FIXED_SHIFT = 50.0
SHIFT_MARGIN = 1.02
TOKEN_TILE = 1024
PROJ_TILE = 512

V7X_VMEM_BYTES = 64 * 1024 * 1024
LANES = 128
SH1, SC1, G1, SH2, SC2, G2 = range(6)


def _params(sem, vmem_mb):
    return pltpu.CompilerParams(dimension_semantics=sem, vmem_limit_bytes=vmem_mb * 1024 * 1024)


def _sigmoid(v):
    return 1.0 / (1.0 + jnp.exp(-v))


def _modnorm(x, gain, sc, sh):
    ms = jnp.mean(x * x, axis=-1, keepdims=True)
    return x * lax.rsqrt(ms + EPS) * (gain * (1.0 + sc)) + sh


def _nt(a, b):
    return lax.dot_general(a, b, (((1,), (1,)), ((), ())), preferred_element_type=F32)


def _dot(a, b):
    return jnp.dot(a, b, preferred_element_type=F32)


class Stream:
    def __init__(self, n, seg, tile, batch, is_ctx):
        self.n, self.seg, self.tile, self.batch, self.is_ctx = n, seg, tile, batch, is_ctx
        self.tiles = n // tile
        self.tiles_per_seg = seg // tile

    def mod_row(self, t):
        return self.batch if self.is_ctx else t // self.tiles_per_seg

    def seg_row(self, b):
        return self.batch if self.is_ctx else b

    def with_tile(self, tile):
        return Stream(self.n, self.seg, min(tile, self.tile), self.batch, self.is_ctx)


def _mod_spec(st, chunk, d):
    return pl.BlockSpec((None, 1, d), lambda t: (st.mod_row(t), 0, chunk))


def _full(shape):
    nd = len(shape)
    return pl.BlockSpec(shape, lambda *_: (0,) * nd)


def _resident(shape):
    nd = len(shape)
    return pl.BlockSpec(shape, lambda *_: (0,) * nd, pipeline_mode=pl.Buffered(1))


def _ada_call(cvec, w_ada, b_ada):
    depth, d, d6 = w_ada.shape
    r8 = cvec.shape[0]
    tn = d6 // 4

    def kern(c_ref, w_ref, b_ref, o_ref):
        c = c_ref[...]
        s = (c * _sigmoid(c)).astype(BF16)
        o_ref[...] = _dot(s, w_ref[...].astype(BF16)) + b_ref[...]

    return pl.pallas_call(
        kern, grid=(depth, d6 // tn),
        in_specs=[pl.BlockSpec((r8, d), lambda i, j: (0, 0)),
                  pl.BlockSpec((None, d, tn), lambda i, j: (i, 0, j)),
                  pl.BlockSpec((None, 1, tn), lambda i, j: (i, 0, j))],
        out_specs=pl.BlockSpec((None, r8, tn), lambda i, j: (i, 0, j)),
        out_shape=SDS((depth, r8, d6), F32), name="ada",
        compiler_params=_params(("arbitrary", "arbitrary"), 40))(cvec, w_ada, b_ada.reshape(depth, 1, d6))


def _outproj_call(st, a, mods, w, name):
    n, k = a.shape
    d = w.shape[1]
    t = st.tile

    def kern(a_ref, g1_ref, w_ref, o_ref):
        o_ref[...] = (g1_ref[...] * _dot(a_ref[...], w_ref[...])).astype(BF16)

    return pl.pallas_call(
        kern, grid=(st.tiles,),
        in_specs=[pl.BlockSpec((t, k), lambda i: (i, 0)), _mod_spec(st, G1, d), _resident((k, d))],
        out_specs=pl.BlockSpec((t, d), lambda i: (i, 0)),
        out_shape=SDS((n, d), BF16), name=name,
        compiler_params=_params(("parallel",), 40))(a, mods, w)


def _dft_tables(length, radix):
    lr = length // radix
    m = np.arange(lr)[None, :, None]
    j = np.arange(radix)[:, None, None]
    nn = np.arange(lr)[None, None, :]
    ang = 2.0 * np.pi * (((radix * m + j) * nn) % length) / length
    e = np.concatenate([np.cos(ang), np.sin(ang)], axis=-1) / np.sqrt(length)
    return jnp.asarray(e, dtype=F32).astype(BF16)


def _group_dft_tables(group):
    k = np.arange(group)
    ang = 2.0 * np.pi * ((k[:, None] * k[None, :]) % group) / group
    return (jnp.asarray(np.cos(ang) / np.sqrt(group), dtype=F32),
            jnp.asarray(np.sin(ang) / np.sqrt(group), dtype=F32))


def _fourier_weight_call(w_out):
    d = w_out.shape[0]
    grp = d // N_FOURIER_GROUPS
    cg, sg = _group_dft_tables(grp)

    def kern(cg_ref, sg_ref, w_ref, o_ref):
        w = w_ref[...]
        o_ref[:, :d] = jnp.dot(cg_ref[...], w, preferred_element_type=F32,
                               precision=lax.Precision.HIGHEST).astype(BF16)
        o_ref[:, d:] = jnp.dot(sg_ref[...], w, preferred_element_type=F32,
                               precision=lax.Precision.HIGHEST).astype(BF16)

    return pl.pallas_call(
        kern, grid=(N_FOURIER_GROUPS,),
        in_specs=[_full((grp, grp)), _full((grp, grp)), pl.BlockSpec((grp, d), lambda g: (g, 0))],
        out_specs=pl.BlockSpec((grp, 2 * d), lambda g: (g, 0)),
        out_shape=SDS((d, 2 * d), BF16), name="fourier_w",
        compiler_params=_params(("arbitrary",), 32))(cg, sg, w_out)


def _radix_terms(radix, j):
    real, imag = [], []
    for q in range(radix):
        k = (j * q * (4 // radix)) % 4 if radix > 1 else 0
        if k == 0:
            real.append((1, 0, q)); imag.append((-1, 1, q))
        elif k == 1:
            real.append((-1, 1, q)); imag.append((-1, 0, q))
        elif k == 2:
            real.append((-1, 0, q)); imag.append((1, 1, q))
        else:
            real.append((1, 1, q)); imag.append((1, 0, q))
    return real, imag


def _fourier_mix_call(st, x, mods, gain, wcs, radix):
    n, d = x.shape
    seg = st.seg
    lr = seg // radix
    e = _dft_tables(seg, radix)
    nb = n // seg
    rt = min(seg, 512)
    cblk = 2 * LANES

    def kern(x_ref, g_ref, sc_ref, sh_ref, g1_ref, w_ref, e_ref, o_ref, p_ref, v_ref, *z_refs):
        for i in range(seg // rt):
            rows = slice(i * rt, (i + 1) * rt)
            h = _modnorm(x_ref[rows, :], g_ref[...], sc_ref[...], sh_ref[...]).astype(BF16)
            p_ref[rows, :] = _dot(h, w_ref[...]).astype(BF16)
        for c in range(d // cblk):
            cols = (slice(c * cblk, (c + 1) * cblk), slice(d + c * cblk, d + (c + 1) * cblk))
            for j in range(radix):
                real, imag = _radix_terms(radix, j)

                def comb(terms):
                    acc = None
                    for sgn, part, q in terms:
                        v = p_ref[q * lr:(q + 1) * lr, cols[part]]
                        if acc is None:
                            acc = v if sgn > 0 else -v
                        else:
                            acc = acc + v if sgn > 0 else acc - v
                    return acc

                v_ref[0:lr, :] = comb(real).astype(BF16)
                v_ref[lr:2 * lr, :] = comb(imag).astype(BF16)
                z = _dot(e_ref[j], v_ref[...])
                for k, z_ref in enumerate(z_refs):
                    z_ref[pl.ds(j, lr, stride=radix), :] = z[:, k * LANES:(k + 1) * LANES]
            for k, z_ref in enumerate(z_refs):
                sl = slice(c * cblk + k * LANES, c * cblk + (k + 1) * LANES)
                o_ref[:, sl] = (g1_ref[:, sl] * z_ref[...]).astype(BF16)

    mod = lambda chunk: pl.BlockSpec((None, 1, d), lambda b: (st.seg_row(b), 0, chunk))
    return pl.pallas_call(
        kern, grid=(nb,),
        in_specs=[pl.BlockSpec((seg, d), lambda b: (b, 0)), _full((1, d)), mod(SC1), mod(SH1), mod(G1),
                  _resident((d, 2 * d)), _resident((radix, lr, 2 * lr))],
        out_specs=pl.BlockSpec((seg, d), lambda b: (b, 0)),
        out_shape=SDS((n, d), BF16),
        scratch_shapes=[pltpu.VMEM((seg, 2 * d), BF16), pltpu.VMEM((2 * lr, cblk), BF16)]
        + [pltpu.VMEM((seg, LANES), F32)] * (cblk // LANES),
        name=f"fourier_mix_r{radix}",
        compiler_params=_params(("parallel",), 56))(x, gain, mods, mods, mods, wcs, e)


def _fourier_layer(streams, xs, mods, gain, w_out):
    wcs = _fourier_weight_call(w_out)
    return [None if x is None else _fourier_mix_call(st, x, mods, gain, wcs, 4 if st.seg >= 1024 else 1)
            for st, x in zip(streams, xs)]


def _conv_mix_call(st, x, mods, gain, w_in, conv_w, w_out):
    n, d = x.shape
    seg = st.seg
    nb = n // seg
    rt = min(seg, 512)
    nrt = seg // rt
    halo = 16

    def kern(x_ref, g_ref, sc_ref, sh_ref, g1_ref, win_ref, cw_ref, wout_ref, o_ref, bg_ref, u_ref):
        for i in range(nrt):
            rows = slice(i * rt, (i + 1) * rt)
            h = _modnorm(x_ref[rows, :], g_ref[...], sc_ref[...], sh_ref[...]).astype(BF16)
            bg_ref[rows, :] = _dot(h, win_ref[:, 0:d]).astype(BF16)
            u_ref[rows, :] = (_dot(h, win_ref[:, d:2 * d]) * _dot(h, win_ref[:, 2 * d:3 * d])).astype(BF16)
        row = lax.broadcasted_iota(jnp.int32, (rt, 1), 0)
        zero_row = jnp.zeros((1, d), F32)
        for i in range(nrt):
            rows = slice(i * rt, (i + 1) * rt)
            u = u_ref[rows, :].astype(F32)
            prev_row = zero_row if i == 0 else u_ref[i * rt - halo:i * rt, :].astype(F32)[halo - 1:halo, :]
            next_row = (zero_row if i == nrt - 1
                        else u_ref[(i + 1) * rt:(i + 1) * rt + halo, :].astype(F32)[0:1, :])
            um = jnp.where(row == 0, prev_row, pltpu.roll(u, 1, axis=0))
            up = jnp.where(row == rt - 1, next_row, pltpu.roll(u, rt - 1, axis=0))
            z = cw_ref[0:1, :] * um + cw_ref[1:2, :] * u + cw_ref[2:3, :] * up
            a = (bg_ref[rows, :].astype(F32) * z).astype(BF16)
            o_ref[rows, :] = (g1_ref[...] * _dot(a, wout_ref[...])).astype(BF16)

    mod = lambda chunk: pl.BlockSpec((None, 1, d), lambda b: (st.seg_row(b), 0, chunk))
    seg_spec = pl.BlockSpec((seg, d), lambda b: (b, 0))
    return pl.pallas_call(
        kern, grid=(nb,),
        in_specs=[seg_spec, _full((1, d)), mod(SC1), mod(SH1), mod(G1),
                  _resident((d, 3 * d)), _full((3, d)), _resident((d, d))],
        out_specs=seg_spec, out_shape=SDS((n, d), BF16),
        scratch_shapes=[pltpu.VMEM((seg, d), BF16), pltpu.VMEM((seg, d), BF16)],
        name="conv_mix", compiler_params=_params(("parallel",), 56))(
            x, gain, mods, mods, mods, w_in, conv_w, w_out)


def _conv_layer(streams, xs, mods, gain, w_in, conv_w, w_out):
    w_in_b, w_out_b = w_in.astype(BF16), w_out.astype(BF16)
    return [None if x is None else _conv_mix_call(st, x, mods, gain, w_in_b, conv_w, w_out_b)
            for st, x in zip(streams, xs)]


def _rope_tables(length):
    q = ROPE_DIM // 4
    pos = np.arange(length)
    rc = np.stack([pos // GRID_W, pos % GRID_W], axis=1).astype(np.float32)
    inv = (ROPE_THETA ** (-np.arange(q, dtype=np.float32) / q)).astype(np.float32)
    lane = np.arange(ROPE_DIM)
    ang = rc[:, lane >> 5] * inv[lane & (q - 1)][None, :]
    sign = np.where((lane & q) == 0, -1.0, 1.0)[None, :]
    cos = np.cos(ang.astype(np.float32)).astype(np.float32)
    sin = (np.sin(ang.astype(np.float32)) * sign).astype(np.float32)
    reps = LANES // ROPE_DIM
    return jnp.asarray(np.tile(cos, (1, reps))), jnp.asarray(np.tile(sin, (1, reps)))


def _rope_partner_perm(n_heads):
    lane = np.arange(n_heads * ROPE_DIM)
    return lane ^ (ROPE_DIM // 4)


def _seg_rinv(x, seg, on_mxu=False):
    t, w = x.shape
    cols = []
    if on_mxu:
        cw = 2 * LANES if w % (2 * LANES) == 0 else LANES
        same = (lax.broadcasted_iota(jnp.int32, (cw, cw), 0) // seg
                == lax.broadcasted_iota(jnp.int32, (cw, cw), 1) // seg)
        ones = jnp.where(same, 1.0, 0.0).astype(BF16)
        for c in range(w // cw):
            blk = x[:, c * cw:(c + 1) * cw]
            cols.append(lax.rsqrt(_dot((blk * blk).astype(BF16), ones) * (1.0 / seg) + EPS))
        return cols[0] if len(cols) == 1 else jnp.concatenate(cols, axis=-1)
    for c in range(w // LANES):
        blk = x[:, c * LANES:(c + 1) * LANES]
        sq = blk * blk
        if seg == LANES:
            cols.append(jnp.broadcast_to(
                lax.rsqrt(jnp.mean(sq, axis=-1, keepdims=True) + EPS), (t, LANES)))
        else:
            lane = lax.broadcasted_iota(jnp.int32, (t, LANES), 1)
            low = lane < seg
            lo = jnp.sum(jnp.where(low, sq, 0.0), axis=-1, keepdims=True) * (1.0 / seg)
            hi = jnp.sum(jnp.where(low, 0.0, sq), axis=-1, keepdims=True) * (1.0 / seg)
            cols.append(jnp.where(low, lax.rsqrt(lo + EPS), lax.rsqrt(hi + EPS)))
    return cols[0] if len(cols) == 1 else jnp.concatenate(cols, axis=-1)


def _tile_lanes(v, w):
    reps = w // LANES
    return v if reps == 1 else jnp.concatenate([v] * reps, axis=-1)


def _mla_proj_call(st, x, mods, gain, wts, rope, need_q):
    n, d = x.shape
    t = st.tile
    h = MLA_HEADS
    wn, wr = h * MLA_NOPE, h * MLA_ROPE
    (w_in, q_norm, kv_norm, w_uq, w_ukv, qg_n, qg_r, qg_rp, kg_n, kg_r, kg_rp) = wts
    use_rope = rope is not None

    def kern(*refs):
        refs = list(refs)
        x_ref, g_ref, sc_ref, sh_ref, win_ref, qn_ref_, kvn_ref_, wuq_ref, wukv_ref = refs[:9]
        qgn_ref, qgr_ref, qgrp_ref, kgn_ref, kgr_ref, kgrp_ref = refs[9:15]
        rest = refs[15:]
        if use_rope:
            cos_ref, sin_ref = rest[:2]
            rest = rest[2:]
        if need_q:
            oqn_ref, oqr_ref = rest[:2]
            rest = rest[2:]
        okn_ref, okr_ref, ov_ref = rest
        hm = _modnorm(x_ref[...], g_ref[...], sc_ref[...], sh_ref[...]).astype(BF16)
        ck = _dot(hm, win_ref[...])
        kr2 = ck[:, MLA_Q_RANK + MLA_KV_RANK:]
        kr_rinv = _seg_rinv(kr2, MLA_ROPE, on_mxu=True)[:, 0:MLA_ROPE]
        kr_raw = kr2[:, 0:MLA_ROPE] * kgr_ref[...]
        if use_rope:
            kr_par = kr2[:, MLA_ROPE:] * kgrp_ref[...]
            kr = kr_rinv * (kr_raw * cos_ref[:, 0:MLA_ROPE] + kr_par * sin_ref[:, 0:MLA_ROPE])
        else:
            kr = kr_rinv * kr_raw
        okr_ref[...] = kr.astype(BF16)
        ckv = ck[:, MLA_Q_RANK:MLA_Q_RANK + MLA_KV_RANK]
        ckv = (ckv * _seg_rinv(ckv, LANES) * kvn_ref_[...]).astype(BF16)
        kv = _dot(ckv, wukv_ref[...])
        kn = kv[:, 0:wn]
        okn_ref[...] = (kn * _seg_rinv(kn, MLA_NOPE) * kgn_ref[...]).astype(BF16)
        ov_ref[...] = kv[:, wn:].astype(BF16)
        if need_q:
            cq = ck[:, 0:MLA_Q_RANK]
            rq = lax.rsqrt(jnp.mean(cq * cq, axis=-1, keepdims=True) + EPS)
            cq = (cq * rq * qn_ref_[...]).astype(BF16)
            q = _dot(cq, wuq_ref[...])
            qn = q[:, 0:wn]
            oqn_ref[...] = (qn * _seg_rinv(qn, MLA_NOPE) * (qgn_ref[...] * (MLA_SCALE * LOG2E))).astype(BF16)
            qr_raw = q[:, wn:wn + wr]
            rinv = _seg_rinv(qr_raw, MLA_ROPE, on_mxu=True) * (MLA_SCALE * LOG2E)
            if use_rope:
                qr_par = q[:, wn + wr:]
                cos = _tile_lanes(cos_ref[...], wr)
                sin = _tile_lanes(sin_ref[...], wr)
                qr = rinv * (qr_raw * qgr_ref[...] * cos + qr_par * qgrp_ref[...] * sin)
            else:
                qr = rinv * (qr_raw * qgr_ref[...])
            oqr_ref[...] = qr.astype(BF16)

    tile = lambda w: pl.BlockSpec((t, w), lambda i: (i, 0))
    in_specs = [tile(d), _full((1, d)), _mod_spec(st, SC1, d), _mod_spec(st, SH1, d),
                _resident(w_in.shape), _full(q_norm.shape), _full(kv_norm.shape),
                _resident(w_uq.shape), _resident(w_ukv.shape),
                _full(qg_n.shape), _full(qg_r.shape), _full(qg_rp.shape),
                _full(kg_n.shape), _full(kg_r.shape), _full(kg_rp.shape)]
    args = [x, gain, mods, mods, w_in, q_norm, kv_norm, w_uq, w_ukv, qg_n, qg_r, qg_rp, kg_n, kg_r, kg_rp]
    if use_rope:
        tps = st.tiles_per_seg
        in_specs += [pl.BlockSpec((t, LANES), lambda i: (i % tps, 0))] * 2
        args += list(rope)
    out_specs, out_shape = [], []
    if need_q:
        out_specs += [tile(wn), tile(wr)]
        out_shape += [SDS((n, wn), BF16), SDS((n, wr), BF16)]
    out_specs += [tile(wn), tile(MLA_ROPE), tile(h * MLA_V)]
    out_shape += [SDS((n, wn), BF16), SDS((n, MLA_ROPE), BF16), SDS((n, h * MLA_V), BF16)]
    outs = pl.pallas_call(
        kern, grid=(st.tiles,), in_specs=in_specs, out_specs=out_specs, out_shape=out_shape,
        name="mla_proj", compiler_params=_params(("parallel",), 48))(*args)
    if need_q:
        return tuple(outs)
    return (None, None) + tuple(outs)


def _mla_attn_fast_call(batch, lq, tq, q, kv_sets, shift):
    qn, qr = q
    lk_total = sum(s[3] for s in kv_sets)
    hp = 4 if lk_total > 1024 else 8
    n_hp = MLA_HEADS // hp
    nq = lq // tq
    nsets = len(kv_sets)
    lengths = [s[3] for s in kv_sets]
    lk = sum(lengths)
    kw = 2 * LANES
    pad = kw - MLA_NOPE - MLA_ROPE

    def kern(*refs):
        sh_ref, qn_ref, qr_ref = refs[:3]
        sets = [refs[3 + 3 * s:6 + 3 * s] for s in range(nsets)]
        o_ref, kq_ref, vq_ref = refs[-3:]

        def build_kv():
            lane_k = lax.broadcasted_iota(jnp.int32, (lk, pad), 1)
            lane_v = lax.broadcasted_iota(jnp.int32, (lk, kw - MLA_V), 1)
            for hh in range(hp):
                off = 0
                for (kn_ref, kr_ref, v_ref), length in zip(sets, lengths):
                    kq_ref[hh, off:off + length, 0:MLA_NOPE] = kn_ref[:, hh * MLA_NOPE:(hh + 1) * MLA_NOPE]
                    kq_ref[hh, off:off + length, MLA_NOPE:MLA_NOPE + MLA_ROPE] = kr_ref[...]
                    vq_ref[hh, off:off + length, 0:MLA_V] = v_ref[:, hh * MLA_V:(hh + 1) * MLA_V]
                    off += length
                kq_ref[hh, :, MLA_NOPE + MLA_ROPE:] = jnp.where(lane_k == 0, -sh_ref[:, 0:pad], 0.0).astype(BF16)
                vq_ref[hh, :, MLA_V:] = jnp.where(lane_v == 0, 1.0, 0.0).astype(BF16)

        if nq == 1:
            build_kv()
        else:
            pl.when(pl.program_id(2) == 0)(build_kv)

        lane_q = lax.broadcasted_iota(jnp.int32, (tq, pad), 1)
        one = jnp.where(lane_q == 0, 1.0, 0.0).astype(BF16)
        for hh in range(hp):
            qc = jnp.concatenate([qn_ref[:, hh * MLA_NOPE:(hh + 1) * MLA_NOPE],
                                  qr_ref[:, hh * MLA_ROPE:(hh + 1) * MLA_ROPE], one], axis=1)
            p = jnp.exp2(_nt(qc, kq_ref[hh]).astype(BF16))
            acc = _dot(p, vq_ref[hh])
            o_ref[:, hh * MLA_V:(hh + 1) * MLA_V] = (acc[:, 0:MLA_V] / acc[:, MLA_V:MLA_V + 1]).astype(BF16)

    in_specs = [_full((1, LANES)),
                pl.BlockSpec((tq, hp * MLA_NOPE), lambda b, h, i: (b * nq + i, h)),
                pl.BlockSpec((tq, hp * MLA_ROPE), lambda b, h, i: (b * nq + i, h))]
    args = [shift, qn, qr]
    for kn, kr, v, length in kv_sets:
        in_specs += [pl.BlockSpec((length, hp * MLA_NOPE), lambda b, h, i: (b, h)),
                     pl.BlockSpec((length, MLA_ROPE), lambda b, h, i: (b, 0)),
                     pl.BlockSpec((length, hp * MLA_V), lambda b, h, i: (b, h))]
        args += [kn, kr, v]
    return pl.pallas_call(
        kern, grid=(batch, n_hp, nq), in_specs=in_specs,
        out_specs=pl.BlockSpec((tq, hp * MLA_V), lambda b, h, i: (b * nq + i, h)),
        out_shape=SDS((batch * lq, MLA_HEADS * MLA_V), BF16),
        scratch_shapes=[pltpu.VMEM((hp, lk, kw), BF16), pltpu.VMEM((hp, lk, kw), BF16)],
        name="mla_attn_fast",
        compiler_params=_params(("parallel", "parallel", "arbitrary"), 56))(*args)


def _mla_attn_call(batch, lq, tq, q, kv_sets):
    qn, qr = q
    hp = 2
    n_hp = MLA_HEADS // hp
    nq = lq // tq
    nsets = len(kv_sets)

    def kern(*refs):
        qn_ref, qr_ref = refs[:2]
        o_ref = refs[-1]
        sets = [refs[2 + 3 * s:5 + 3 * s] for s in range(nsets)]
        for hh in range(hp):
            qn_h = qn_ref[:, hh * MLA_NOPE:(hh + 1) * MLA_NOPE]
            qr_h = qr_ref[:, hh * MLA_ROPE:(hh + 1) * MLA_ROPE]
            scores = [_nt(qn_h, kn_ref[:, hh * MLA_NOPE:(hh + 1) * MLA_NOPE]) + _nt(qr_h, kr_ref[...])
                      for kn_ref, kr_ref, _ in sets]
            m = functools.reduce(jnp.maximum, [jnp.max(s, axis=-1, keepdims=True) for s in scores])
            ps = [jnp.exp2(s - m) for s in scores]
            den = functools.reduce(lambda a, b: a + b, [jnp.sum(p, axis=-1, keepdims=True) for p in ps])
            acc = functools.reduce(lambda a, b: a + b, [
                _dot(p.astype(BF16), v_ref[:, hh * MLA_V:(hh + 1) * MLA_V])
                for p, (_, _, v_ref) in zip(ps, sets)])
            o_ref[:, hh * MLA_V:(hh + 1) * MLA_V] = (acc / den).astype(BF16)

    in_specs = [pl.BlockSpec((tq, hp * MLA_NOPE), lambda b, h, i: (b * nq + i, h)),
                pl.BlockSpec((tq, hp * MLA_ROPE), lambda b, h, i: (b * nq + i, h))]
    args = [qn, qr]
    for kn, kr, v, length in kv_sets:
        in_specs += [pl.BlockSpec((length, hp * MLA_NOPE), lambda b, h, i: (b, h)),
                     pl.BlockSpec((length, MLA_ROPE), lambda b, h, i: (b, 0)),
                     pl.BlockSpec((length, hp * MLA_V), lambda b, h, i: (b, h))]
        args += [kn, kr, v]
    return pl.pallas_call(
        kern, grid=(batch, n_hp, nq), in_specs=in_specs,
        out_specs=pl.BlockSpec((tq, hp * MLA_V), lambda b, h, i: (b * nq + i, h)),
        out_shape=SDS((batch * lq, MLA_HEADS * MLA_V), BF16), name="mla_attn",
        compiler_params=_params(("parallel", "parallel", "arbitrary"), 56))(*args)


def _mla_layer(streams, xs, mods, gain, p, rope, ctx_next):
    st_c, st_x = streams
    xc, xx = xs
    h = MLA_HEADS
    w_in, q_norm, kv_norm, w_uq, w_ukv, q_gain, k_gain, w_out = p
    par = _rope_partner_perm(1)
    kr_cols = w_in[:, MLA_Q_RANK + MLA_KV_RANK:]
    w_in_p = jnp.concatenate([w_in, kr_cols[:, par]], axis=1).astype(BF16)
    wq = w_uq.reshape(MLA_Q_RANK, h, MLA_NOPE + MLA_ROPE)
    wq_n = wq[:, :, :MLA_NOPE].reshape(MLA_Q_RANK, h * MLA_NOPE)
    wq_r = wq[:, :, MLA_NOPE:]
    w_uq_p = jnp.concatenate([wq_n, wq_r.reshape(MLA_Q_RANK, h * MLA_ROPE),
                              wq_r[:, :, par].reshape(MLA_Q_RANK, h * MLA_ROPE)], axis=1).astype(BF16)
    wkv = w_ukv.reshape(MLA_KV_RANK, h, MLA_NOPE + MLA_V)
    w_ukv_p = jnp.concatenate([wkv[:, :, :MLA_NOPE].reshape(MLA_KV_RANK, h * MLA_NOPE),
                               wkv[:, :, MLA_NOPE:].reshape(MLA_KV_RANK, h * MLA_V)], axis=1).astype(BF16)
    qg_r = q_gain[MLA_NOPE:]
    kg_r = k_gain[MLA_NOPE:]
    wts = (w_in_p, q_norm.reshape(1, -1), kv_norm.reshape(1, -1), w_uq_p, w_ukv_p,
           jnp.tile(q_gain[:MLA_NOPE], h).reshape(1, -1), jnp.tile(qg_r, h).reshape(1, -1),
           jnp.tile(qg_r[par], h).reshape(1, -1), jnp.tile(k_gain[:MLA_NOPE], h).reshape(1, -1),
           kg_r.reshape(1, -1), kg_r[par].reshape(1, -1))
    w_out_b = w_out.astype(BF16)
    qn_c, qr_c, kn_c, kr_c, v_c = _mla_proj_call(st_c.with_tile(PROJ_TILE), xc, mods, gain, wts, None, ctx_next)
    qn_x, qr_x, kn_x, kr_x, v_x = _mla_proj_call(st_x.with_tile(PROJ_TILE), xx, mods, gain, wts, rope, True)
    b = st_x.batch
    amax = lambda g: jnp.max(jnp.abs(g))
    bound = LOG2E * MLA_SCALE * (MLA_NOPE * amax(q_gain[:MLA_NOPE]) * amax(k_gain[:MLA_NOPE])
                                 + MLA_ROPE * amax(qg_r) * amax(kg_r))
    shift = bound * SHIFT_MARGIN + 1.0
    shift_row = jnp.full((1, LANES), shift, F32)

    def attend(lq, tq, q, kv_sets):
        return lax.cond(shift <= MAX_FIXED_SHIFT,
                        lambda: _mla_attn_fast_call(b, lq, tq, q, kv_sets, shift_row),
                        lambda: _mla_attn_call(b, lq, min(tq, 512), q, kv_sets))

    a_x = attend(st_x.seg, min(st_x.seg, 2048), (qn_x, qr_x), [(kn_c, kr_c, v_c, st_c.seg), (kn_x, kr_x, v_x, st_x.seg)])
    out_x = _outproj_call(st_x, a_x, mods, w_out_b, "mla_out")
    out_c = None
    if ctx_next:
        a_c = attend(st_c.seg, st_c.seg, (qn_c, qr_c), [(kn_c, kr_c, v_c, st_c.seg)])
        out_c = _outproj_call(st_c, a_c, mods, w_out_b, "mla_out")
    return [out_c, out_x]


def _gqa_proj_call(st, x, mods, gain, wts, rope, need_q):
    n, d = x.shape
    t = st.tile
    wq, wk = GQA_Q_HEADS * GQA_HEAD_DIM, GQA_KV_HEADS * GQA_HEAD_DIM
    w_in, qg, qgp, kg, kgp = wts
    use_rope = rope is not None

    def kern(*refs):
        refs = list(refs)
        x_ref, g_ref, sc_ref, sh_ref, w_ref, qg_ref, qgp_ref, kg_ref, kgp_ref = refs[:9]
        rest = refs[9:]
        if use_rope:
            cos_ref, sin_ref = rest[:2]
            rest = rest[2:]
        if need_q:
            oq_ref = rest[0]
            rest = rest[1:]
        ok_ref, ov_ref = rest
        hm = _modnorm(x_ref[...], g_ref[...], sc_ref[...], sh_ref[...]).astype(BF16)
        kvp = _dot(hm, w_ref[:, 0:3 * wk])
        k_raw = kvp[:, 0:wk]
        k_rinv = _seg_rinv(k_raw, GQA_HEAD_DIM)
        ov_ref[...] = kvp[:, wk:2 * wk].astype(BF16)
        if use_rope:
            cos_k, sin_k = _tile_lanes(cos_ref[...], wk), _tile_lanes(sin_ref[...], wk)
            k = k_rinv * (k_raw * kg_ref[...] * cos_k + kvp[:, 2 * wk:] * kgp_ref[...] * sin_k)
        else:
            k = k_rinv * (k_raw * kg_ref[...])
        ok_ref[...] = k.astype(BF16)
        if need_q:
            qp = _dot(hm, w_ref[:, 3 * wk:])
            q_raw = qp[:, 0:wq]
            rinv = _seg_rinv(q_raw, GQA_HEAD_DIM) * (GQA_SCALE * LOG2E)
            if use_rope:
                cos_q, sin_q = _tile_lanes(cos_ref[...], wq), _tile_lanes(sin_ref[...], wq)
                q = rinv * (q_raw * qg_ref[...] * cos_q + qp[:, wq:] * qgp_ref[...] * sin_q)
            else:
                q = rinv * (q_raw * qg_ref[...])
            oq_ref[...] = q.astype(BF16)

    tile = lambda w: pl.BlockSpec((t, w), lambda i: (i, 0))
    in_specs = [tile(d), _full((1, d)), _mod_spec(st, SC1, d), _mod_spec(st, SH1, d),
                _resident(w_in.shape), _full(qg.shape), _full(qgp.shape), _full(kg.shape), _full(kgp.shape)]
    args = [x, gain, mods, mods, w_in, qg, qgp, kg, kgp]
    if use_rope:
        tps = st.tiles_per_seg
        in_specs += [pl.BlockSpec((t, LANES), lambda i: (i % tps, 0))] * 2
        args += list(rope)
    out_specs, out_shape = [], []
    if need_q:
        out_specs.append(tile(wq))
        out_shape.append(SDS((n, wq), BF16))
    out_specs += [tile(wk), tile(wk)]
    out_shape += [SDS((n, wk), BF16)] * 2
    outs = pl.pallas_call(
        kern, grid=(st.tiles,), in_specs=in_specs, out_specs=out_specs, out_shape=out_shape,
        name="gqa_proj", compiler_params=_params(("parallel",), 48))(*args)
    return tuple(outs) if need_q else (None,) + tuple(outs)


def _gqa_window_call(batch, length, lc, q, k, v, kc, vc, sink):
    nb = length // BLOCK
    hd = GQA_HEAD_DIM
    wq, wk = GQA_Q_HEADS * hd, GQA_KV_HEADS * hd
    rows = GQA_GROUP * BLOCK

    def kern(sink_ref, q_ref, kp_ref, k0_ref, kn_ref, vp_ref, v0_ref, vn_ref, kc_ref, vc_ref, o_ref):
        nblk = pl.program_id(1)
        r = lax.broadcasted_iota(jnp.int32, (rows, 3 * BLOCK), 0) & (BLOCK - 1)
        c = lax.broadcasted_iota(jnp.int32, (rows, 3 * BLOCK), 1)
        valid = (c >= r + BLOCK - WINDOW) & (c <= r + BLOCK + WINDOW)
        valid = valid & ((c >= BLOCK) | (nblk > 0)) & ((c < 2 * BLOCK) | (nblk < nb - 1))
        hrow = lax.broadcasted_iota(jnp.int32, (rows, 1), 0) // BLOCK
        for g in range(GQA_KV_HEADS):
            sl = slice(g * hd, (g + 1) * hd)
            qg = jnp.concatenate([q_ref[:, (g * GQA_GROUP + j) * hd:(g * GQA_GROUP + j + 1) * hd]
                                  for j in range(GQA_GROUP)], axis=0)
            kband = jnp.concatenate([kp_ref[:, sl], k0_ref[:, sl], kn_ref[:, sl]], axis=0)
            vband = jnp.concatenate([vp_ref[:, sl], v0_ref[:, sl], vn_ref[:, sl]], axis=0)
            s_c = _nt(qg, kc_ref[:, sl])
            s_b = jnp.where(valid, _nt(qg, kband), -1e30)
            snk = jnp.zeros((rows, 1), F32)
            for j in range(GQA_GROUP):
                snk = jnp.where(hrow == j, sink_ref[g * GQA_GROUP + j], snk)
            m = jnp.maximum(jnp.maximum(jnp.max(s_c, axis=-1, keepdims=True),
                                        jnp.max(s_b, axis=-1, keepdims=True)), snk)
            p_c = jnp.exp2(s_c - m)
            p_b = jnp.exp2(s_b - m)
            den = (jnp.sum(p_c, axis=-1, keepdims=True) + jnp.sum(p_b, axis=-1, keepdims=True)
                   + jnp.exp2(snk - m))
            o = (_dot(p_c.astype(BF16), vc_ref[:, sl]) + _dot(p_b.astype(BF16), vband)) / den
            o_ref[:, g * GQA_GROUP * hd:(g + 1) * GQA_GROUP * hd] = jnp.concatenate(
                [o[j * BLOCK:(j + 1) * BLOCK, :] for j in range(GQA_GROUP)], axis=-1).astype(BF16)

    blk = lambda f: pl.BlockSpec((BLOCK, wk), f)
    prev_ = lambda b, i: (b * nb + jnp.maximum(i - 1, 0), 0)
    cur_ = lambda b, i: (b * nb + i, 0)
    next_ = lambda b, i: (b * nb + jnp.minimum(i + 1, nb - 1), 0)
    ctx_spec = pl.BlockSpec((lc, wk), lambda b, i: (b, 0))
    return pl.pallas_call(
        kern, grid=(batch, nb),
        in_specs=[pl.BlockSpec(memory_space=pltpu.SMEM),
                  pl.BlockSpec((BLOCK, wq), cur_), blk(prev_), blk(cur_), blk(next_),
                  blk(prev_), blk(cur_), blk(next_), ctx_spec, ctx_spec],
        out_specs=pl.BlockSpec((BLOCK, wq), cur_),
        out_shape=SDS((batch * length, wq), BF16), name="gqa_window",
        compiler_params=_params(("parallel", "arbitrary"), 40))(sink, q, k, k, k, v, v, v, kc, vc)


def _gqa_window_fast_call(batch, length, lc, q, k, v, kc, vc, sink2, shift):
    nb = length // BLOCK
    hd = GQA_HEAD_DIM
    wq, wk = GQA_Q_HEADS * hd, GQA_KV_HEADS * hd
    rows = GQA_GROUP * BLOCK

    def kern(sink_ref, shift_ref, q_ref, kp_ref, k0_ref, kn_ref, vp_ref, v0_ref, vn_ref, kc_ref, vc_ref, o_ref):
        nblk = pl.program_id(1)
        r = lax.broadcasted_iota(jnp.int32, (rows, 3 * BLOCK), 0) & (BLOCK - 1)
        c = lax.broadcasted_iota(jnp.int32, (rows, 3 * BLOCK), 1)
        valid = (c >= r + BLOCK - WINDOW) & (c <= r + BLOCK + WINDOW)
        valid = valid & ((c >= BLOCK) | (nblk > 0)) & ((c < 2 * BLOCK) | (nblk < nb - 1))
        hrow = lax.broadcasted_iota(jnp.int32, (rows, 1), 0) // BLOCK
        lane0_q = lax.broadcasted_iota(jnp.int32, (BLOCK, hd), 1) == 0

        def with_one(x):
            lane0 = lax.broadcasted_iota(jnp.int32, (x.shape[0], hd), 1) == 0
            return jnp.concatenate([x, jnp.where(lane0, 1.0, 0.0).astype(BF16)], axis=1)

        probs, values = [], []
        for g in range(GQA_KV_HEADS):
            sl = slice(g * hd, (g + 1) * hd)
            qg = jnp.concatenate(
                [jnp.concatenate([q_ref[:, h * hd:(h + 1) * hd],
                                  jnp.where(lane0_q, -shift_ref[h], 0.0).astype(BF16)], axis=1)
                 for h in range(g * GQA_GROUP, (g + 1) * GQA_GROUP)], axis=0)
            keys = with_one(jnp.concatenate([kc_ref[:, sl], kp_ref[:, sl], k0_ref[:, sl], kn_ref[:, sl]], axis=0))
            values.append(with_one(jnp.concatenate(
                [vc_ref[:, sl], vp_ref[:, sl], v0_ref[:, sl], vn_ref[:, sl]], axis=0)))
            s = _nt(qg, keys)
            s = jnp.concatenate([s[:, 0:lc], jnp.where(valid, s[:, lc:], -1e30)], axis=1)
            probs.append(jnp.exp2(s.astype(BF16)))
        for g in range(GQA_KV_HEADS):
            heads = range(g * GQA_GROUP, (g + 1) * GQA_GROUP)
            acc = _dot(probs[g], values[g])
            snk = jnp.zeros((rows, 1), F32)
            for j, h in enumerate(heads):
                snk = jnp.where(hrow == j, sink_ref[h] - shift_ref[h], snk)
            o = acc[:, 0:hd] / (acc[:, hd:hd + 1] + jnp.exp2(snk))
            o_ref[:, g * GQA_GROUP * hd:(g + 1) * GQA_GROUP * hd] = jnp.concatenate(
                [o[j * BLOCK:(j + 1) * BLOCK, :] for j in range(GQA_GROUP)], axis=-1).astype(BF16)

    blk = lambda f: pl.BlockSpec((BLOCK, wk), f)
    prev_ = lambda b, i: (b * nb + jnp.maximum(i - 1, 0), 0)
    cur_ = lambda b, i: (b * nb + i, 0)
    next_ = lambda b, i: (b * nb + jnp.minimum(i + 1, nb - 1), 0)
    ctx_spec = pl.BlockSpec((lc, wk), lambda b, i: (b, 0))
    smem = pl.BlockSpec(memory_space=pltpu.SMEM)
    return pl.pallas_call(
        kern, grid=(batch, nb),
        in_specs=[smem, smem, pl.BlockSpec((BLOCK, wq), cur_), blk(prev_), blk(cur_), blk(next_),
                  blk(prev_), blk(cur_), blk(next_), ctx_spec, ctx_spec],
        out_specs=pl.BlockSpec((BLOCK, wq), cur_),
        out_shape=SDS((batch * length, wq), BF16), name="gqa_window_fast",
        compiler_params=_params(("parallel", "arbitrary"), 40))(sink2, shift, q, k, k, k, v, v, v, kc, vc)


def _gqa_layer(streams, xs, mods, gain, p, rope, ctx_next):
    assert not ctx_next, "the windowed-GQA mixer is only implemented as the last layer"
    st_c, st_x = streams
    xc, xx = xs
    w_in, q_gain, k_gain, sink, w_out = p
    wq, wk = GQA_Q_HEADS * GQA_HEAD_DIM, GQA_KV_HEADS * GQA_HEAD_DIM
    w_q, w_k, w_v = w_in[:, :wq], w_in[:, wq:wq + wk], w_in[:, wq + wk:]
    w_in_p = jnp.concatenate([w_k, w_v, w_k[:, _rope_partner_perm(GQA_KV_HEADS)],
                              w_q, w_q[:, _rope_partner_perm(GQA_Q_HEADS)]], axis=1).astype(BF16)
    par = _rope_partner_perm(1)
    wts = (w_in_p, jnp.tile(q_gain, GQA_Q_HEADS).reshape(1, -1), jnp.tile(q_gain[par], GQA_Q_HEADS).reshape(1, -1),
           jnp.tile(k_gain, GQA_KV_HEADS).reshape(1, -1), jnp.tile(k_gain[par], GQA_KV_HEADS).reshape(1, -1))
    _, kc, vc = _gqa_proj_call(st_c.with_tile(PROJ_TILE), xc, mods, gain, wts, None, False)
    q, k, v = _gqa_proj_call(st_x.with_tile(PROJ_TILE), xx, mods, gain, wts, rope, True)
    sink2 = sink * LOG2E
    bound = LOG2E * GQA_SCALE * GQA_HEAD_DIM * jnp.max(jnp.abs(q_gain)) * jnp.max(jnp.abs(k_gain))
    shift = jnp.maximum(bound * SHIFT_MARGIN + 1.0, sink2).astype(BF16).astype(F32)
    args = (st_x.batch, st_x.seg, st_c.seg, q, k, v, kc, vc, sink2)
    a = lax.cond(jnp.max(shift) <= MAX_FIXED_SHIFT,
                 lambda: _gqa_window_fast_call(*args, shift),
                 lambda: _gqa_window_call(*args))
    return [None, _outproj_call(st_x, a, mods, w_out.astype(BF16), "gqa_out")]


def _cast_experts_call(w, layer):
    _, ne, a, b = w.shape
    eb = 4

    def kern(w_ref, o_ref):
        o_ref[...] = w_ref[...].astype(BF16)

    return pl.pallas_call(
        kern, grid=(ne // eb,),
        in_specs=[pl.BlockSpec((None, eb, a, b), lambda i: (layer, i, 0, 0))],
        out_specs=pl.BlockSpec((eb, a, b), lambda i: (i, 0, 0)),
        out_shape=SDS((ne, a, b), BF16), name="cast_experts",
        compiler_params=_params(("parallel",), 32))(w)


def _route(logits_t, bias_col):
    scores = _sigmoid(logits_t)
    biased = scores + bias_col
    rows = [biased[e:e + 1, :] for e in range(N_EXPERTS)]
    srow = [scores[e:e + 1, :] for e in range(N_EXPERTS)]
    epg = EXPERTS_PER_GROUP
    gscore = []
    for g in range(N_EXPERT_GROUPS):
        v = rows[g * epg:(g + 1) * epg]
        pair = [v[a] + v[b] for a in range(epg) for b in range(a + 1, epg)]
        gscore.append(functools.reduce(jnp.maximum, pair))
    ind, wloc = [], [None] * epg
    for g in range(N_EXPERT_GROUPS):
        best = None
        for g2 in range(N_EXPERT_GROUPS):
            if g2 == g:
                continue
            cnd = gscore[g] > gscore[g2] if g2 < g else gscore[g] >= gscore[g2]
            best = cnd if best is None else best & cnd
        ind.append(jnp.where(best, 1.0, 0.0))
        for j in range(epg):
            e = g * epg + j
            rank = None
            for e2 in range(g * epg, (g + 1) * epg):
                if e2 == e:
                    continue
                ahead = rows[e2] >= rows[e] if e2 < e else rows[e2] > rows[e]
                one = jnp.where(ahead, 1.0, 0.0)
                rank = one if rank is None else rank + one
            w = jnp.where(best & (rank < 2.0), srow[e], 0.0)
            wloc[j] = w if wloc[j] is None else wloc[j] + w
    den = functools.reduce(lambda a, b: a + b, wloc)
    return ind, [w / den for w in wloc]


MOE_TILE = 1024
MOE_CHUNK = 128


def _moe_call(st, x_in, y_in, mods, gain, router_wt, router_bias, w_gate, w_up, w_down):
    n, d = x_in.shape
    t = min(MOE_TILE, n)
    r = MOE_CHUNK
    ne, _, ff = w_gate.shape
    epg, ng = EXPERTS_PER_GROUP, N_EXPERT_GROUPS
    nch = t // r + ng - 1
    tiles = n // t
    wd2 = w_down.reshape(ne * ff, d)

    def kern(x_ref, y_ref, g_ref, sc_ref, sh_ref, g2_ref, rw_ref, rb_ref, wg_ref, wu_ref, wd_ref, o_ref,
             hb_ref, oh_ref, xg_ref):
        hb_ref[...] = _modnorm(x_ref[...] + y_ref[...].astype(F32), g_ref[...], sc_ref[...], sh_ref[...]).astype(BF16)
        ind, wloc = _route(_nt(rw_ref[...], hb_ref[...]), rb_ref[...])
        ind8 = jnp.concatenate(ind + [jnp.zeros((8 - ng, t), F32)], axis=0)
        upper = jnp.where(lax.broadcasted_iota(jnp.int32, (LANES, LANES), 0)
                          < lax.broadcasted_iota(jnp.int32, (LANES, LANES), 1), 1.0, 0.0).astype(BF16)
        blocks, before = [], jnp.zeros((8, 1), F32)
        for k in range(t // LANES):
            blk = ind8[:, k * LANES:(k + 1) * LANES]
            blocks.append(_dot(blk.astype(BF16), upper) + before)
            before = before + jnp.sum(blk, axis=1, keepdims=True)
        pos = jnp.concatenate(blocks, axis=1)
        count = [jnp.sum(ind[g]).astype(jnp.int32) for g in range(ng)]
        first = [jnp.int32(0)]
        for g in range(ng):
            first.append(first[g] + (count[g] + (r - 1)) // r)
        slot = functools.reduce(lambda a, b: a + b, [
            ind[g] * (pos[g:g + 1, :] + (first[g] * r).astype(F32)) for g in range(ng)])
        w_hi = [w.astype(BF16) for w in wloc]
        w_lo = [(w - h.astype(F32)).astype(BF16) for w, h in zip(wloc, w_hi)]
        wst = jnp.concatenate(w_hi + w_lo, axis=0)
        row_id = lax.broadcasted_iota(jnp.int32, (r, t), 0).astype(F32)
        onehot = lambda c: jnp.where(slot == row_id + float(c * r), 1.0, 0.0).astype(BF16)
        last = nch - 1
        for c in range(last):
            oh_ref[c * r:(c + 1) * r, :] = onehot(c)
        xg_ref[...] = _dot(oh_ref[...], hb_ref[...]).astype(BF16)
        wr_all = _nt(wst, oh_ref[...]).T

        def chunk_ffn(c, xg, wr):
            grp = functools.reduce(lambda a, b: a + b,
                                   [(c >= first[g]).astype(jnp.int32) for g in range(1, ng)])
            acts = []
            for j in range(epg):
                e = grp * epg + j
                gt = _dot(xg, wg_ref[e])
                up = _dot(xg, wu_ref[e])
                acts.append((gt * _sigmoid(gt) * up * (wr[:, j:j + 1] + wr[:, epg + j:epg + j + 1])).astype(BF16))
            wd_g = wd_ref[pl.ds(pl.multiple_of(grp * (epg * ff), epg * ff), epg * ff), :]
            return _dot(jnp.concatenate(acts, axis=1), wd_g).astype(BF16)

        for c in range(last):
            rows = slice(c * r, (c + 1) * r)

            def run(c=c, rows=rows):
                xg_ref[rows, :] = chunk_ffn(c, xg_ref[rows, :], wr_all[c * r:(c + 1) * r, :])

            if c < t // r:
                run()
            else:
                pl.when(c < first[ng])(run)

        tdot = lambda a, b: lax.dot_general(a, b, (((0,), (0,)), ((), ())), preferred_element_type=F32)
        o_ref[...] = x_ref[...] + y_ref[...].astype(F32) + g2_ref[...] * tdot(oh_ref[...], xg_ref[...])

        @pl.when(last < first[ng])
        def _():
            oh = onehot(last)
            out = chunk_ffn(last, _dot(oh, hb_ref[...]).astype(BF16), _nt(wst, oh).T)
            o_ref[...] += g2_ref[...] * tdot(oh, out)

    row = (lambda i: st.batch) if st.is_ctx else (lambda i: i // (st.seg // t))
    mod = lambda chunk: pl.BlockSpec((None, 1, d), lambda i: (row(i), 0, chunk))
    return pl.pallas_call(
        kern, grid=(tiles,),
        in_specs=[pl.BlockSpec((t, d), lambda i: (i, 0)),
                  pl.BlockSpec((t, d), lambda i: (i, 0)),
                  _full((1, d)), mod(SC2), mod(SH2), mod(G2), _full((ne, d)), _full((ne, 1)),
                  _resident((ne, d, ff)), _resident((ne, d, ff)), _resident((ne * ff, d))],
        out_specs=pl.BlockSpec((t, d), lambda i: (i, 0)),
        out_shape=SDS((n, d), F32),
        scratch_shapes=[pltpu.VMEM((t, d), BF16), pltpu.VMEM(((nch - 1) * r, t), BF16),
                        pltpu.VMEM(((nch - 1) * r, d), BF16)],
        name="moe", compiler_params=_params(("parallel",), 58))(
            x_in, y_in, gain, mods, mods, mods, router_wt, router_bias, w_gate, w_up, wd2)


def kernel(x, c, ctx, c_ctx, w_ada, b_ada, norm_mix, norm_ffn, fourier_w_out, conv_w_in, conv_w, conv_w_out,
           mla_w_in, mla_q_norm, mla_kv_norm, mla_w_uq, mla_w_ukv, mla_q_gain, mla_k_gain, mla_w_out,
           gqa_w_in, gqa_q_gain, gqa_k_gain, gqa_sink, gqa_w_out, router_w, router_bias,
           moe_w_gate, moe_w_up, moe_w_down):
    b, l, d = x.shape
    lc = ctx.shape[1]
    depth = w_ada.shape[0]
    st_c = Stream(b * lc, lc, min(b * lc, TOKEN_TILE), b, True)
    st_x = Stream(b * l, l, min(l, TOKEN_TILE), b, False)
    streams = [st_c, st_x]
    r8 = -(-(b + 1) // 8) * 8
    cvec = jnp.concatenate([c, c_ctx[None, :], jnp.zeros((r8 - b - 1, d), F32)], axis=0)
    mods_all = _ada_call(cvec, w_ada, b_ada).reshape(depth, r8, 1, 6 * d)
    rope = _rope_tables(l)
    router_wt = router_w.T.astype(BF16)
    router_b = router_bias.reshape(-1, 1)
    xs = [ctx.reshape(b * lc, d), x.reshape(b * l, d)]
    for i in range(depth):
        kind, j = i % 4, i // 4
        ctx_next = i < depth - 1
        mods = mods_all[i]
        gain = norm_mix[i].reshape(1, d)
        if not (ctx_next or kind >= 2):
            xs[0] = None
        if kind == 0:
            ys = _fourier_layer(streams, xs, mods, gain, fourier_w_out[j])
        elif kind == 1:
            ys = _conv_layer(streams, xs, mods, gain, conv_w_in[j], conv_w[j], conv_w_out[j])
        elif kind == 2:
            ys = _mla_layer(streams, xs, mods, gain,
                            (mla_w_in[j], mla_q_norm[j], mla_kv_norm[j], mla_w_uq[j], mla_w_ukv[j],
                             mla_q_gain[j], mla_k_gain[j], mla_w_out[j]), rope, ctx_next)
        else:
            ys = _gqa_layer(streams, xs, mods, gain,
                            (gqa_w_in[j], gqa_q_gain[j], gqa_k_gain[j], gqa_sink[j], gqa_w_out[j]),
                            rope, ctx_next)
        if not ctx_next:
            ys[0] = None
        gain2 = norm_ffn[i].reshape(1, d)
        wg, wu, wd = (_cast_experts_call(w, i) for w in (moe_w_gate, moe_w_up, moe_w_down))
        xs = [None if y is None else _moe_call(st, x, y, mods, gain2, router_wt, router_b, wg, wu, wd)
              for st, x, y in zip(streams, xs, ys)]
    return xs[1].reshape(b, l, d)
```

```python
import functools

import numpy as np
import jax
import jax.numpy as jnp
from jax import lax
from jax.experimental import pallas as pl
from jax.experimental.pallas import tpu as pltpu

F32, BF16 = jnp.float32, jnp.bfloat16
SDS = jax.ShapeDtypeStruct

EPS = 1e-6
GRID_W = 64
ROPE_THETA = 10000.0
N_FOURIER_GROUPS = 4
MLA_HEADS, MLA_Q_RANK, MLA_KV_RANK = 16, 256, 128
MLA_NOPE, MLA_ROPE, MLA_V = 128, 64, 128
MLA_SCALE = (MLA_NOPE + MLA_ROPE) ** -0.5
GQA_Q_HEADS, GQA_KV_HEADS, GQA_HEAD_DIM = 16, 4, 64
GQA_GROUP = GQA_Q_HEADS // GQA_KV_HEADS
GQA_SCALE = GQA_HEAD_DIM ** -0.5
WINDOW = 128
BLOCK = 128
N_EXPERTS, N_EXPERT_GROUPS, EXPERT_FF = 16, 4, 256
EXPERTS_PER_GROUP = N_EXPERTS // N_EXPERT_GROUPS
ROPE_DIM = 64
LOG2E = 1.4426950408889634
MAX_FIXED_SHIFT = 50.0
SHIFT_MARGIN = 1.02
TOKEN_TILE = 1024
PROJ_TILE = 512

V7X_VMEM_BYTES = 64 * 1024 * 1024
LANES = 128
SH1, SC1, G1, SH2, SC2, G2 = range(6)


def _params(sem, vmem_mb):
    return pltpu.CompilerParams(dimension_semantics=sem, vmem_limit_bytes=vmem_mb * 1024 * 1024)


def _sigmoid(v):
    return 1.0 / (1.0 + jnp.exp(-v))


def _modnorm(x, gain, sc, sh):
    ms = jnp.mean(x * x, axis=-1, keepdims=True)
    return x * lax.rsqrt(ms + EPS) * (gain * (1.0 + sc)) + sh


def _nt(a, b):
    return lax.dot_general(a, b, (((1,), (1,)), ((), ())), preferred_element_type=F32)


def _dot(a, b):
    return jnp.dot(a, b, preferred_element_type=F32)


class Stream:
    def __init__(self, n, seg, tile, batch, is_ctx):
        self.n, self.seg, self.tile, self.batch, self.is_ctx = n, seg, tile, batch, is_ctx
        self.tiles = n // tile
        self.tiles_per_seg = seg // tile

    def mod_row(self, t):
        return self.batch if self.is_ctx else t // self.tiles_per_seg

    def seg_row(self, b):
        return self.batch if self.is_ctx else b

    def with_tile(self, tile):
        return Stream(self.n, self.seg, min(tile, self.tile), self.batch, self.is_ctx)


def _mod_spec(st, chunk, d):
    return pl.BlockSpec((None, 1, d), lambda t: (st.mod_row(t), 0, chunk))


def _full(shape):
    nd = len(shape)
    return pl.BlockSpec(shape, lambda *_: (0,) * nd)


def _resident(shape):
    nd = len(shape)
    return pl.BlockSpec(shape, lambda *_: (0,) * nd, pipeline_mode=pl.Buffered(1))


def _ada_call(cvec, w_ada, b_ada):
    depth, d, d6 = w_ada.shape
    r8 = cvec.shape[0]
    tn = d6 // 4

    def kern(c_ref, w_ref, b_ref, o_ref):
        c = c_ref[...]
        s = (c * _sigmoid(c)).astype(BF16)
        o_ref[...] = _dot(s, w_ref[...].astype(BF16)) + b_ref[...]

    return pl.pallas_call(
        kern, grid=(depth, d6 // tn),
        in_specs=[pl.BlockSpec((r8, d), lambda i, j: (0, 0)),
                  pl.BlockSpec((None, d, tn), lambda i, j: (i, 0, j)),
                  pl.BlockSpec((None, 1, tn), lambda i, j: (i, 0, j))],
        out_specs=pl.BlockSpec((None, r8, tn), lambda i, j: (i, 0, j)),
        out_shape=SDS((depth, r8, d6), F32), name="ada",
        compiler_params=_params(("arbitrary", "arbitrary"), 40))(cvec, w_ada, b_ada.reshape(depth, 1, d6))


def _outproj_call(st, a, mods, w, name):
    n, k = a.shape
    d = w.shape[1]
    t = st.tile

    def kern(a_ref, g1_ref, w_ref, o_ref):
        o_ref[...] = (g1_ref[...] * _dot(a_ref[...], w_ref[...])).astype(BF16)

    return pl.pallas_call(
        kern, grid=(st.tiles,),
        in_specs=[pl.BlockSpec((t, k), lambda i: (i, 0)), _mod_spec(st, G1, d), _resident((k, d))],
        out_specs=pl.BlockSpec((t, d), lambda i: (i, 0)),
        out_shape=SDS((n, d), BF16), name=name,
        compiler_params=_params(("parallel",), 40))(a, mods, w)


def _dft_tables(length, radix):
    lr = length // radix
    m = np.arange(lr)[None, :, None]
    j = np.arange(radix)[:, None, None]
    nn = np.arange(lr)[None, None, :]
    ang = 2.0 * np.pi * (((radix * m + j) * nn) % length) / length
    e = np.concatenate([np.cos(ang), np.sin(ang)], axis=-1) / np.sqrt(length)
    return jnp.asarray(e, dtype=F32).astype(BF16)


def _group_dft_tables(group):
    k = np.arange(group)
    ang = 2.0 * np.pi * ((k[:, None] * k[None, :]) % group) / group
    return (jnp.asarray(np.cos(ang) / np.sqrt(group), dtype=F32),
            jnp.asarray(np.sin(ang) / np.sqrt(group), dtype=F32))


def _fourier_weight_call(w_out):
    d = w_out.shape[0]
    grp = d // N_FOURIER_GROUPS
    cg, sg = _group_dft_tables(grp)

    def kern(cg_ref, sg_ref, w_ref, o_ref):
        w = w_ref[...]
        o_ref[:, :d] = jnp.dot(cg_ref[...], w, preferred_element_type=F32,
                               precision=lax.Precision.HIGHEST).astype(BF16)
        o_ref[:, d:] = jnp.dot(sg_ref[...], w, preferred_element_type=F32,
                               precision=lax.Precision.HIGHEST).astype(BF16)

    return pl.pallas_call(
        kern, grid=(N_FOURIER_GROUPS,),
        in_specs=[_full((grp, grp)), _full((grp, grp)), pl.BlockSpec((grp, d), lambda g: (g, 0))],
        out_specs=pl.BlockSpec((grp, 2 * d), lambda g: (g, 0)),
        out_shape=SDS((d, 2 * d), BF16), name="fourier_w",
        compiler_params=_params(("arbitrary",), 32))(cg, sg, w_out)


def _radix_terms(radix, j):
    real, imag = [], []
    for q in range(radix):
        k = (j * q * (4 // radix)) % 4 if radix > 1 else 0
        if k == 0:
            real.append((1, 0, q)); imag.append((-1, 1, q))
        elif k == 1:
            real.append((-1, 1, q)); imag.append((-1, 0, q))
        elif k == 2:
            real.append((-1, 0, q)); imag.append((1, 1, q))
        else:
            real.append((1, 1, q)); imag.append((1, 0, q))
    return real, imag


def _fourier_mix_call(st, x, mods, gain, wcs, radix):
    n, d = x.shape
    seg = st.seg
    lr = seg // radix
    e = _dft_tables(seg, radix)
    nb = n // seg
    rt = min(seg, 512)
    cblk = 2 * LANES

    def kern(x_ref, g_ref, sc_ref, sh_ref, g1_ref, w_ref, e_ref, o_ref, p_ref, v_ref, *z_refs):
        for i in range(seg // rt):
            rows = slice(i * rt, (i + 1) * rt)
            h = _modnorm(x_ref[rows, :], g_ref[...], sc_ref[...], sh_ref[...]).astype(BF16)
            p_ref[rows, :] = _dot(h, w_ref[...]).astype(BF16)
        for c in range(d // cblk):
            cols = (slice(c * cblk, (c + 1) * cblk), slice(d + c * cblk, d + (c + 1) * cblk))
            for j in range(radix):
                real, imag = _radix_terms(radix, j)

                def comb(terms):
                    acc = None
                    for sgn, part, q in terms:
                        v = p_ref[q * lr:(q + 1) * lr, cols[part]]
                        if acc is None:
                            acc = v if sgn > 0 else -v
                        else:
                            acc = acc + v if sgn > 0 else acc - v
                    return acc

                v_ref[0:lr, :] = comb(real).astype(BF16)
                v_ref[lr:2 * lr, :] = comb(imag).astype(BF16)
                z = _dot(e_ref[j], v_ref[...])
                for k, z_ref in enumerate(z_refs):
                    z_ref[pl.ds(j, lr, stride=radix), :] = z[:, k * LANES:(k + 1) * LANES]
            for k, z_ref in enumerate(z_refs):
                sl = slice(c * cblk + k * LANES, c * cblk + (k + 1) * LANES)
                o_ref[:, sl] = (g1_ref[:, sl] * z_ref[...]).astype(BF16)

    mod = lambda chunk: pl.BlockSpec((None, 1, d), lambda b: (st.seg_row(b), 0, chunk))
    return pl.pallas_call(
        kern, grid=(nb,),
        in_specs=[pl.BlockSpec((seg, d), lambda b: (b, 0)), _full((1, d)), mod(SC1), mod(SH1), mod(G1),
                  _resident((d, 2 * d)), _resident((radix, lr, 2 * lr))],
        out_specs=pl.BlockSpec((seg, d), lambda b: (b, 0)),
        out_shape=SDS((n, d), BF16),
        scratch_shapes=[pltpu.VMEM((seg, 2 * d), BF16), pltpu.VMEM((2 * lr, cblk), BF16)]
        + [pltpu.VMEM((seg, LANES), F32)] * (cblk // LANES),
        name=f"fourier_mix_r{radix}",
        compiler_params=_params(("parallel",), 56))(x, gain, mods, mods, mods, wcs, e)


def _fourier_layer(streams, xs, mods, gain, w_out):
    wcs = _fourier_weight_call(w_out)
    return [None if x is None else _fourier_mix_call(st, x, mods, gain, wcs, 4 if st.seg >= 1024 else 1)
            for st, x in zip(streams, xs)]


def _conv_mix_call(st, x, mods, gain, w_in, conv_w, w_out):
    n, d = x.shape
    seg = st.seg
    nb = n // seg
    rt = min(seg, 512)
    nrt = seg // rt
    halo = 16

    def kern(x_ref, g_ref, sc_ref, sh_ref, g1_ref, win_ref, cw_ref, wout_ref, o_ref, bg_ref, u_ref):
        for i in range(nrt):
            rows = slice(i * rt, (i + 1) * rt)
            h = _modnorm(x_ref[rows, :], g_ref[...], sc_ref[...], sh_ref[...]).astype(BF16)
            bg_ref[rows, :] = _dot(h, win_ref[:, 0:d]).astype(BF16)
            u_ref[rows, :] = (_dot(h, win_ref[:, d:2 * d]) * _dot(h, win_ref[:, 2 * d:3 * d])).astype(BF16)
        row = lax.broadcasted_iota(jnp.int32, (rt, 1), 0)
        zero_row = jnp.zeros((1, d), F32)
        for i in range(nrt):
            rows = slice(i * rt, (i + 1) * rt)
            u = u_ref[rows, :].astype(F32)
            prev_row = zero_row if i == 0 else u_ref[i * rt - halo:i * rt, :].astype(F32)[halo - 1:halo, :]
            next_row = (zero_row if i == nrt - 1
                        else u_ref[(i + 1) * rt:(i + 1) * rt + halo, :].astype(F32)[0:1, :])
            um = jnp.where(row == 0, prev_row, pltpu.roll(u, 1, axis=0))
            up = jnp.where(row == rt - 1, next_row, pltpu.roll(u, rt - 1, axis=0))
            z = cw_ref[0:1, :] * um + cw_ref[1:2, :] * u + cw_ref[2:3, :] * up
            a = (bg_ref[rows, :].astype(F32) * z).astype(BF16)
            o_ref[rows, :] = (g1_ref[...] * _dot(a, wout_ref[...])).astype(BF16)

    mod = lambda chunk: pl.BlockSpec((None, 1, d), lambda b: (st.seg_row(b), 0, chunk))
    seg_spec = pl.BlockSpec((seg, d), lambda b: (b, 0))
    return pl.pallas_call(
        kern, grid=(nb,),
        in_specs=[seg_spec, _full((1, d)), mod(SC1), mod(SH1), mod(G1),
                  _resident((d, 3 * d)), _full((3, d)), _resident((d, d))],
        out_specs=seg_spec, out_shape=SDS((n, d), BF16),
        scratch_shapes=[pltpu.VMEM((seg, d), BF16), pltpu.VMEM((seg, d), BF16)],
        name="conv_mix", compiler_params=_params(("parallel",), 56))(
            x, gain, mods, mods, mods, w_in, conv_w, w_out)


def _conv_layer(streams, xs, mods, gain, w_in, conv_w, w_out):
    w_in_b, w_out_b = w_in.astype(BF16), w_out.astype(BF16)
    return [None if x is None else _conv_mix_call(st, x, mods, gain, w_in_b, conv_w, w_out_b)
            for st, x in zip(streams, xs)]


def _rope_tables(length):
    q = ROPE_DIM // 4
    pos = np.arange(length)
    rc = np.stack([pos // GRID_W, pos % GRID_W], axis=1).astype(np.float32)
    inv = (ROPE_THETA ** (-np.arange(q, dtype=np.float32) / q)).astype(np.float32)
    lane = np.arange(ROPE_DIM)
    ang = rc[:, lane >> 5] * inv[lane & (q - 1)][None, :]
    sign = np.where((lane & q) == 0, -1.0, 1.0)[None, :]
    cos = np.cos(ang.astype(np.float32)).astype(np.float32)
    sin = (np.sin(ang.astype(np.float32)) * sign).astype(np.float32)
    reps = LANES // ROPE_DIM
    return jnp.asarray(np.tile(cos, (1, reps))), jnp.asarray(np.tile(sin, (1, reps)))


def _rope_partner_perm(n_heads):
    lane = np.arange(n_heads * ROPE_DIM)
    return lane ^ (ROPE_DIM // 4)


def _seg_rinv(x, seg, on_mxu=False):
    t, w = x.shape
    cols = []
    if on_mxu:
        cw = 2 * LANES if w % (2 * LANES) == 0 else LANES
        same = (lax.broadcasted_iota(jnp.int32, (cw, cw), 0) // seg
                == lax.broadcasted_iota(jnp.int32, (cw, cw), 1) // seg)
        ones = jnp.where(same, 1.0, 0.0).astype(BF16)
        for c in range(w // cw):
            blk = x[:, c * cw:(c + 1) * cw]
            cols.append(lax.rsqrt(_dot((blk * blk).astype(BF16), ones) * (1.0 / seg) + EPS))
        return cols[0] if len(cols) == 1 else jnp.concatenate(cols, axis=-1)
    for c in range(w // LANES):
        blk = x[:, c * LANES:(c + 1) * LANES]
        sq = blk * blk
        if seg == LANES:
            cols.append(jnp.broadcast_to(
                lax.rsqrt(jnp.mean(sq, axis=-1, keepdims=True) + EPS), (t, LANES)))
        else:
            lane = lax.broadcasted_iota(jnp.int32, (t, LANES), 1)
            low = lane < seg
            lo = jnp.sum(jnp.where(low, sq, 0.0), axis=-1, keepdims=True) * (1.0 / seg)
            hi = jnp.sum(jnp.where(low, 0.0, sq), axis=-1, keepdims=True) * (1.0 / seg)
            cols.append(jnp.where(low, lax.rsqrt(lo + EPS), lax.rsqrt(hi + EPS)))
    return cols[0] if len(cols) == 1 else jnp.concatenate(cols, axis=-1)


def _tile_lanes(v, w):
    reps = w // LANES
    return v if reps == 1 else jnp.concatenate([v] * reps, axis=-1)


def _mla_proj_call(st, x, mods, gain, wts, rope, need_q):
    n, d = x.shape
    t = st.tile
    h = MLA_HEADS
    wn, wr = h * MLA_NOPE, h * MLA_ROPE
    (w_in, q_norm, kv_norm, w_uq, w_ukv, qg_n, qg_r, qg_rp, kg_n, kg_r, kg_rp) = wts
    use_rope = rope is not None

    def kern(*refs):
        refs = list(refs)
        x_ref, g_ref, sc_ref, sh_ref, win_ref, qn_ref_, kvn_ref_, wuq_ref, wukv_ref = refs[:9]
        qgn_ref, qgr_ref, qgrp_ref, kgn_ref, kgr_ref, kgrp_ref = refs[9:15]
        rest = refs[15:]
        if use_rope:
            cos_ref, sin_ref = rest[:2]
            rest = rest[2:]
        if need_q:
            oqn_ref, oqr_ref = rest[:2]
            rest = rest[2:]
        okn_ref, okr_ref, ov_ref = rest
        hm = _modnorm(x_ref[...], g_ref[...], sc_ref[...], sh_ref[...]).astype(BF16)
        ck = _dot(hm, win_ref[...])
        kr2 = ck[:, MLA_Q_RANK + MLA_KV_RANK:]
        kr_rinv = _seg_rinv(kr2, MLA_ROPE, on_mxu=True)[:, 0:MLA_ROPE]
        kr_raw = kr2[:, 0:MLA_ROPE] * kgr_ref[...]
        if use_rope:
            kr_par = kr2[:, MLA_ROPE:] * kgrp_ref[...]
            kr = kr_rinv * (kr_raw * cos_ref[:, 0:MLA_ROPE] + kr_par * sin_ref[:, 0:MLA_ROPE])
        else:
            kr = kr_rinv * kr_raw
        okr_ref[...] = kr.astype(BF16)
        ckv = ck[:, MLA_Q_RANK:MLA_Q_RANK + MLA_KV_RANK]
        ckv = (ckv * _seg_rinv(ckv, LANES) * kvn_ref_[...]).astype(BF16)
        kv = _dot(ckv, wukv_ref[...])
        kn = kv[:, 0:wn]
        okn_ref[...] = (kn * _seg_rinv(kn, MLA_NOPE) * kgn_ref[...]).astype(BF16)
        ov_ref[...] = kv[:, wn:].astype(BF16)
        if need_q:
            cq = ck[:, 0:MLA_Q_RANK]
            rq = lax.rsqrt(jnp.mean(cq * cq, axis=-1, keepdims=True) + EPS)
            cq = (cq * rq * qn_ref_[...]).astype(BF16)
            q = _dot(cq, wuq_ref[...])
            qn = q[:, 0:wn]
            oqn_ref[...] = (qn * _seg_rinv(qn, MLA_NOPE) * (qgn_ref[...] * (MLA_SCALE * LOG2E))).astype(BF16)
            qr_raw = q[:, wn:wn + wr]
            rinv = _seg_rinv(qr_raw, MLA_ROPE, on_mxu=True) * (MLA_SCALE * LOG2E)
            if use_rope:
                qr_par = q[:, wn + wr:]
                cos = _tile_lanes(cos_ref[...], wr)
                sin = _tile_lanes(sin_ref[...], wr)
                qr = rinv * (qr_raw * qgr_ref[...] * cos + qr_par * qgrp_ref[...] * sin)
            else:
                qr = rinv * (qr_raw * qgr_ref[...])
            oqr_ref[...] = qr.astype(BF16)

    tile = lambda w: pl.BlockSpec((t, w), lambda i: (i, 0))
    in_specs = [tile(d), _full((1, d)), _mod_spec(st, SC1, d), _mod_spec(st, SH1, d),
                _resident(w_in.shape), _full(q_norm.shape), _full(kv_norm.shape),
                _resident(w_uq.shape), _resident(w_ukv.shape),
                _full(qg_n.shape), _full(qg_r.shape), _full(qg_rp.shape),
                _full(kg_n.shape), _full(kg_r.shape), _full(kg_rp.shape)]
    args = [x, gain, mods, mods, w_in, q_norm, kv_norm, w_uq, w_ukv, qg_n, qg_r, qg_rp, kg_n, kg_r, kg_rp]
    if use_rope:
        tps = st.tiles_per_seg
        in_specs += [pl.BlockSpec((t, LANES), lambda i: (i % tps, 0))] * 2
        args += list(rope)
    out_specs, out_shape = [], []
    if need_q:
        out_specs += [tile(wn), tile(wr)]
        out_shape += [SDS((n, wn), BF16), SDS((n, wr), BF16)]
    out_specs += [tile(wn), tile(MLA_ROPE), tile(h * MLA_V)]
    out_shape += [SDS((n, wn), BF16), SDS((n, MLA_ROPE), BF16), SDS((n, h * MLA_V), BF16)]
    outs = pl.pallas_call(
        kern, grid=(st.tiles,), in_specs=in_specs, out_specs=out_specs, out_shape=out_shape,
        name="mla_proj", compiler_params=_params(("parallel",), 48))(*args)
    if need_q:
        return tuple(outs)
    return (None, None) + tuple(outs)


def _mla_attn_fast_call(batch, lq, tq, q, kv_sets, shift):
    qn, qr = q
    lk_total = sum(s[3] for s in kv_sets)
    hp = 4 if lk_total > 1024 else 8
    n_hp = MLA_HEADS // hp
    nq = lq // tq
    nsets = len(kv_sets)
    lengths = [s[3] for s in kv_sets]
    lk = sum(lengths)
    kw = 2 * LANES
    pad = kw - MLA_NOPE - MLA_ROPE

    def kern(*refs):
        sh_ref, qn_ref, qr_ref = refs[:3]
        sets = [refs[3 + 3 * s:6 + 3 * s] for s in range(nsets)]
        o_ref, kq_ref, vq_ref = refs[-3:]

        def build_kv():
            lane_k = lax.broadcasted_iota(jnp.int32, (lk, pad), 1)
            lane_v = lax.broadcasted_iota(jnp.int32, (lk, kw - MLA_V), 1)
            for hh in range(hp):
                off = 0
                for (kn_ref, kr_ref, v_ref), length in zip(sets, lengths):
                    kq_ref[hh, off:off + length, 0:MLA_NOPE] = kn_ref[:, hh * MLA_NOPE:(hh + 1) * MLA_NOPE]
                    kq_ref[hh, off:off + length, MLA_NOPE:MLA_NOPE + MLA_ROPE] = kr_ref[...]
                    vq_ref[hh, off:off + length, 0:MLA_V] = v_ref[:, hh * MLA_V:(hh + 1) * MLA_V]
                    off += length
                kq_ref[hh, :, MLA_NOPE + MLA_ROPE:] = jnp.where(lane_k == 0, -sh_ref[:, 0:pad], 0.0).astype(BF16)
                vq_ref[hh, :, MLA_V:] = jnp.where(lane_v == 0, 1.0, 0.0).astype(BF16)

        if nq == 1:
            build_kv()
        else:
            pl.when(pl.program_id(2) == 0)(build_kv)

        lane_q = lax.broadcasted_iota(jnp.int32, (tq, pad), 1)
        one = jnp.where(lane_q == 0, 1.0, 0.0).astype(BF16)
        for hh in range(hp):
            qc = jnp.concatenate([qn_ref[:, hh * MLA_NOPE:(hh + 1) * MLA_NOPE],
                                  qr_ref[:, hh * MLA_ROPE:(hh + 1) * MLA_ROPE], one], axis=1)
            p = jnp.exp2(_nt(qc, kq_ref[hh]).astype(BF16))
            acc = _dot(p, vq_ref[hh])
            o_ref[:, hh * MLA_V:(hh + 1) * MLA_V] = (acc[:, 0:MLA_V] / acc[:, MLA_V:MLA_V + 1]).astype(BF16)

    in_specs = [_full((1, LANES)),
                pl.BlockSpec((tq, hp * MLA_NOPE), lambda b, h, i: (b * nq + i, h)),
                pl.BlockSpec((tq, hp * MLA_ROPE), lambda b, h, i: (b * nq + i, h))]
    args = [shift, qn, qr]
    for kn, kr, v, length in kv_sets:
        in_specs += [pl.BlockSpec((length, hp * MLA_NOPE), lambda b, h, i: (b, h)),
                     pl.BlockSpec((length, MLA_ROPE), lambda b, h, i: (b, 0)),
                     pl.BlockSpec((length, hp * MLA_V), lambda b, h, i: (b, h))]
        args += [kn, kr, v]
    return pl.pallas_call(
        kern, grid=(batch, n_hp, nq), in_specs=in_specs,
        out_specs=pl.BlockSpec((tq, hp * MLA_V), lambda b, h, i: (b * nq + i, h)),
        out_shape=SDS((batch * lq, MLA_HEADS * MLA_V), BF16),
        scratch_shapes=[pltpu.VMEM((hp, lk, kw), BF16), pltpu.VMEM((hp, lk, kw), BF16)],
        name="mla_attn_fast",
        compiler_params=_params(("parallel", "parallel", "arbitrary"), 56))(*args)


def _mla_attn_call(batch, lq, tq, q, kv_sets):
    qn, qr = q
    hp = 2
    n_hp = MLA_HEADS // hp
    nq = lq // tq
    nsets = len(kv_sets)

    def kern(*refs):
        qn_ref, qr_ref = refs[:2]
        o_ref = refs[-1]
        sets = [refs[2 + 3 * s:5 + 3 * s] for s in range(nsets)]
        for hh in range(hp):
            qn_h = qn_ref[:, hh * MLA_NOPE:(hh + 1) * MLA_NOPE]
            qr_h = qr_ref[:, hh * MLA_ROPE:(hh + 1) * MLA_ROPE]
            scores = [_nt(qn_h, kn_ref[:, hh * MLA_NOPE:(hh + 1) * MLA_NOPE]) + _nt(qr_h, kr_ref[...])
                      for kn_ref, kr_ref, _ in sets]
            m = functools.reduce(jnp.maximum, [jnp.max(s, axis=-1, keepdims=True) for s in scores])
            ps = [jnp.exp2(s - m) for s in scores]
            den = functools.reduce(lambda a, b: a + b, [jnp.sum(p, axis=-1, keepdims=True) for p in ps])
            acc = functools.reduce(lambda a, b: a + b, [
                _dot(p.astype(BF16), v_ref[:, hh * MLA_V:(hh + 1) * MLA_V])
                for p, (_, _, v_ref) in zip(ps, sets)])
            o_ref[:, hh * MLA_V:(hh + 1) * MLA_V] = (acc / den).astype(BF16)

    in_specs = [pl.BlockSpec((tq, hp * MLA_NOPE), lambda b, h, i: (b * nq + i, h)),
                pl.BlockSpec((tq, hp * MLA_ROPE), lambda b, h, i: (b * nq + i, h))]
    args = [qn, qr]
    for kn, kr, v, length in kv_sets:
        in_specs += [pl.BlockSpec((length, hp * MLA_NOPE), lambda b, h, i: (b, h)),
                     pl.BlockSpec((length, MLA_ROPE), lambda b, h, i: (b, 0)),
                     pl.BlockSpec((length, hp * MLA_V), lambda b, h, i: (b, h))]
        args += [kn, kr, v]
    return pl.pallas_call(
        kern, grid=(batch, n_hp, nq), in_specs=in_specs,
        out_specs=pl.BlockSpec((tq, hp * MLA_V), lambda b, h, i: (b * nq + i, h)),
        out_shape=SDS((batch * lq, MLA_HEADS * MLA_V), BF16), name="mla_attn",
        compiler_params=_params(("parallel", "parallel", "arbitrary"), 56))(*args)


def _mla_layer(streams, xs, mods, gain, p, rope, ctx_next):
    st_c, st_x = streams
    xc, xx = xs
    h = MLA_HEADS
    w_in, q_norm, kv_norm, w_uq, w_ukv, q_gain, k_gain, w_out = p
    par = _rope_partner_perm(1)
    kr_cols = w_in[:, MLA_Q_RANK + MLA_KV_RANK:]
    w_in_p = jnp.concatenate([w_in, kr_cols[:, par]], axis=1).astype(BF16)
    wq = w_uq.reshape(MLA_Q_RANK, h, MLA_NOPE + MLA_ROPE)
    wq_n = wq[:, :, :MLA_NOPE].reshape(MLA_Q_RANK, h * MLA_NOPE)
    wq_r = wq[:, :, MLA_NOPE:]
    w_uq_p = jnp.concatenate([wq_n, wq_r.reshape(MLA_Q_RANK, h * MLA_ROPE),
                              wq_r[:, :, par].reshape(MLA_Q_RANK, h * MLA_ROPE)], axis=1).astype(BF16)
    wkv = w_ukv.reshape(MLA_KV_RANK, h, MLA_NOPE + MLA_V)
    w_ukv_p = jnp.concatenate([wkv[:, :, :MLA_NOPE].reshape(MLA_KV_RANK, h * MLA_NOPE),
                               wkv[:, :, MLA_NOPE:].reshape(MLA_KV_RANK, h * MLA_V)], axis=1).astype(BF16)
    qg_r = q_gain[MLA_NOPE:]
    kg_r = k_gain[MLA_NOPE:]
    wts = (w_in_p, q_norm.reshape(1, -1), kv_norm.reshape(1, -1), w_uq_p, w_ukv_p,
           jnp.tile(q_gain[:MLA_NOPE], h).reshape(1, -1), jnp.tile(qg_r, h).reshape(1, -1),
           jnp.tile(qg_r[par], h).reshape(1, -1), jnp.tile(k_gain[:MLA_NOPE], h).reshape(1, -1),
           kg_r.reshape(1, -1), kg_r[par].reshape(1, -1))
    w_out_b = w_out.astype(BF16)
    qn_c, qr_c, kn_c, kr_c, v_c = _mla_proj_call(st_c.with_tile(PROJ_TILE), xc, mods, gain, wts, None, ctx_next)
    qn_x, qr_x, kn_x, kr_x, v_x = _mla_proj_call(st_x.with_tile(PROJ_TILE), xx, mods, gain, wts, rope, True)
    b = st_x.batch
    amax = lambda g: jnp.max(jnp.abs(g))
    bound = LOG2E * MLA_SCALE * (MLA_NOPE * amax(q_gain[:MLA_NOPE]) * amax(k_gain[:MLA_NOPE])
                                 + MLA_ROPE * amax(qg_r) * amax(kg_r))
    shift = bound * SHIFT_MARGIN + 1.0
    shift_row = jnp.full((1, LANES), shift, F32)

    def attend(lq, tq, q, kv_sets):
        return lax.cond(shift <= MAX_FIXED_SHIFT,
                        lambda: _mla_attn_fast_call(b, lq, tq, q, kv_sets, shift_row),
                        lambda: _mla_attn_call(b, lq, min(tq, 512), q, kv_sets))

    a_x = attend(st_x.seg, min(st_x.seg, 2048), (qn_x, qr_x), [(kn_c, kr_c, v_c, st_c.seg), (kn_x, kr_x, v_x, st_x.seg)])
    out_x = _outproj_call(st_x, a_x, mods, w_out_b, "mla_out")
    out_c = None
    if ctx_next:
        a_c = attend(st_c.seg, st_c.seg, (qn_c, qr_c), [(kn_c, kr_c, v_c, st_c.seg)])
        out_c = _outproj_call(st_c, a_c, mods, w_out_b, "mla_out")
    return [out_c, out_x]


def _gqa_proj_call(st, x, mods, gain, wts, rope, need_q):
    n, d = x.shape
    t = st.tile
    wq, wk = GQA_Q_HEADS * GQA_HEAD_DIM, GQA_KV_HEADS * GQA_HEAD_DIM
    w_in, qg, qgp, kg, kgp = wts
    use_rope = rope is not None

    def kern(*refs):
        refs = list(refs)
        x_ref, g_ref, sc_ref, sh_ref, w_ref, qg_ref, qgp_ref, kg_ref, kgp_ref = refs[:9]
        rest = refs[9:]
        if use_rope:
            cos_ref, sin_ref = rest[:2]
            rest = rest[2:]
        if need_q:
            oq_ref = rest[0]
            rest = rest[1:]
        ok_ref, ov_ref = rest
        hm = _modnorm(x_ref[...], g_ref[...], sc_ref[...], sh_ref[...]).astype(BF16)
        kvp = _dot(hm, w_ref[:, 0:3 * wk])
        k_raw = kvp[:, 0:wk]
        k_rinv = _seg_rinv(k_raw, GQA_HEAD_DIM)
        ov_ref[...] = kvp[:, wk:2 * wk].astype(BF16)
        if use_rope:
            cos_k, sin_k = _tile_lanes(cos_ref[...], wk), _tile_lanes(sin_ref[...], wk)
            k = k_rinv * (k_raw * kg_ref[...] * cos_k + kvp[:, 2 * wk:] * kgp_ref[...] * sin_k)
        else:
            k = k_rinv * (k_raw * kg_ref[...])
        ok_ref[...] = k.astype(BF16)
        if need_q:
            qp = _dot(hm, w_ref[:, 3 * wk:])
            q_raw = qp[:, 0:wq]
            rinv = _seg_rinv(q_raw, GQA_HEAD_DIM) * (GQA_SCALE * LOG2E)
            if use_rope:
                cos_q, sin_q = _tile_lanes(cos_ref[...], wq), _tile_lanes(sin_ref[...], wq)
                q = rinv * (q_raw * qg_ref[...] * cos_q + qp[:, wq:] * qgp_ref[...] * sin_q)
            else:
                q = rinv * (q_raw * qg_ref[...])
            oq_ref[...] = q.astype(BF16)

    tile = lambda w: pl.BlockSpec((t, w), lambda i: (i, 0))
    in_specs = [tile(d), _full((1, d)), _mod_spec(st, SC1, d), _mod_spec(st, SH1, d),
                _resident(w_in.shape), _full(qg.shape), _full(qgp.shape), _full(kg.shape), _full(kgp.shape)]
    args = [x, gain, mods, mods, w_in, qg, qgp, kg, kgp]
    if use_rope:
        tps = st.tiles_per_seg
        in_specs += [pl.BlockSpec((t, LANES), lambda i: (i % tps, 0))] * 2
        args += list(rope)
    out_specs, out_shape = [], []
    if need_q:
        out_specs.append(tile(wq))
        out_shape.append(SDS((n, wq), BF16))
    out_specs += [tile(wk), tile(wk)]
    out_shape += [SDS((n, wk), BF16)] * 2
    outs = pl.pallas_call(
        kern, grid=(st.tiles,), in_specs=in_specs, out_specs=out_specs, out_shape=out_shape,
        name="gqa_proj", compiler_params=_params(("parallel",), 48))(*args)
    return tuple(outs) if need_q else (None,) + tuple(outs)


def _gqa_window_call(batch, length, lc, q, k, v, kc, vc, sink):
    nb = length // BLOCK
    hd = GQA_HEAD_DIM
    wq, wk = GQA_Q_HEADS * hd, GQA_KV_HEADS * hd
    rows = GQA_GROUP * BLOCK

    def kern(sink_ref, q_ref, kp_ref, k0_ref, kn_ref, vp_ref, v0_ref, vn_ref, kc_ref, vc_ref, o_ref):
        nblk = pl.program_id(1)
        r = lax.broadcasted_iota(jnp.int32, (rows, 3 * BLOCK), 0) & (BLOCK - 1)
        c = lax.broadcasted_iota(jnp.int32, (rows, 3 * BLOCK), 1)
        valid = (c >= r + BLOCK - WINDOW) & (c <= r + BLOCK + WINDOW)
        valid = valid & ((c >= BLOCK) | (nblk > 0)) & ((c < 2 * BLOCK) | (nblk < nb - 1))
        hrow = lax.broadcasted_iota(jnp.int32, (rows, 1), 0) // BLOCK
        for g in range(GQA_KV_HEADS):
            sl = slice(g * hd, (g + 1) * hd)
            qg = jnp.concatenate([q_ref[:, (g * GQA_GROUP + j) * hd:(g * GQA_GROUP + j + 1) * hd]
                                  for j in range(GQA_GROUP)], axis=0)
            kband = jnp.concatenate([kp_ref[:, sl], k0_ref[:, sl], kn_ref[:, sl]], axis=0)
            vband = jnp.concatenate([vp_ref[:, sl], v0_ref[:, sl], vn_ref[:, sl]], axis=0)
            s_c = _nt(qg, kc_ref[:, sl])
            s_b = jnp.where(valid, _nt(qg, kband), -1e30)
            snk = jnp.zeros((rows, 1), F32)
            for j in range(GQA_GROUP):
                snk = jnp.where(hrow == j, sink_ref[g * GQA_GROUP + j], snk)
            m = jnp.maximum(jnp.maximum(jnp.max(s_c, axis=-1, keepdims=True),
                                        jnp.max(s_b, axis=-1, keepdims=True)), snk)
            p_c = jnp.exp2(s_c - m)
            p_b = jnp.exp2(s_b - m)
            den = (jnp.sum(p_c, axis=-1, keepdims=True) + jnp.sum(p_b, axis=-1, keepdims=True)
                   + jnp.exp2(snk - m))
            o = (_dot(p_c.astype(BF16), vc_ref[:, sl]) + _dot(p_b.astype(BF16), vband)) / den
            o_ref[:, g * GQA_GROUP * hd:(g + 1) * GQA_GROUP * hd] = jnp.concatenate(
                [o[j * BLOCK:(j + 1) * BLOCK, :] for j in range(GQA_GROUP)], axis=-1).astype(BF16)

    blk = lambda f: pl.BlockSpec((BLOCK, wk), f)
    prev_ = lambda b, i: (b * nb + jnp.maximum(i - 1, 0), 0)
    cur_ = lambda b, i: (b * nb + i, 0)
    next_ = lambda b, i: (b * nb + jnp.minimum(i + 1, nb - 1), 0)
    ctx_spec = pl.BlockSpec((lc, wk), lambda b, i: (b, 0))
    return pl.pallas_call(
        kern, grid=(batch, nb),
        in_specs=[pl.BlockSpec(memory_space=pltpu.SMEM),
                  pl.BlockSpec((BLOCK, wq), cur_), blk(prev_), blk(cur_), blk(next_),
                  blk(prev_), blk(cur_), blk(next_), ctx_spec, ctx_spec],
        out_specs=pl.BlockSpec((BLOCK, wq), cur_),
        out_shape=SDS((batch * length, wq), BF16), name="gqa_window",
        compiler_params=_params(("parallel", "arbitrary"), 40))(sink, q, k, k, k, v, v, v, kc, vc)


def _gqa_window_fast_call(batch, length, lc, q, k, v, kc, vc, sink2, shift):
    nb = length // BLOCK
    hd = GQA_HEAD_DIM
    wq, wk = GQA_Q_HEADS * hd, GQA_KV_HEADS * hd
    rows = GQA_GROUP * BLOCK

    def kern(sink_ref, shift_ref, q_ref, kp_ref, k0_ref, kn_ref, vp_ref, v0_ref, vn_ref, kc_ref, vc_ref, o_ref):
        nblk = pl.program_id(1)
        r = lax.broadcasted_iota(jnp.int32, (rows, 3 * BLOCK), 0) & (BLOCK - 1)
        c = lax.broadcasted_iota(jnp.int32, (rows, 3 * BLOCK), 1)
        valid = (c >= r + BLOCK - WINDOW) & (c <= r + BLOCK + WINDOW)
        valid = valid & ((c >= BLOCK) | (nblk > 0)) & ((c < 2 * BLOCK) | (nblk < nb - 1))
        hrow = lax.broadcasted_iota(jnp.int32, (rows, 1), 0) // BLOCK
        lane0_q = lax.broadcasted_iota(jnp.int32, (BLOCK, hd), 1) == 0

        def with_one(x):
            lane0 = lax.broadcasted_iota(jnp.int32, (x.shape[0], hd), 1) == 0
            return jnp.concatenate([x, jnp.where(lane0, 1.0, 0.0).astype(BF16)], axis=1)

        probs, values = [], []
        for g in range(GQA_KV_HEADS):
            sl = slice(g * hd, (g + 1) * hd)
            qg = jnp.concatenate(
                [jnp.concatenate([q_ref[:, h * hd:(h + 1) * hd],
                                  jnp.where(lane0_q, -shift_ref[h], 0.0).astype(BF16)], axis=1)
                 for h in range(g * GQA_GROUP, (g + 1) * GQA_GROUP)], axis=0)
            keys = with_one(jnp.concatenate([kc_ref[:, sl], kp_ref[:, sl], k0_ref[:, sl], kn_ref[:, sl]], axis=0))
            values.append(with_one(jnp.concatenate(
                [vc_ref[:, sl], vp_ref[:, sl], v0_ref[:, sl], vn_ref[:, sl]], axis=0)))
            s = _nt(qg, keys)
            s = jnp.concatenate([s[:, 0:lc], jnp.where(valid, s[:, lc:], -1e30)], axis=1)
            probs.append(jnp.exp2(s.astype(BF16)))
        for g in range(GQA_KV_HEADS):
            heads = range(g * GQA_GROUP, (g + 1) * GQA_GROUP)
            acc = _dot(probs[g], values[g])
            snk = jnp.zeros((rows, 1), F32)
            for j, h in enumerate(heads):
                snk = jnp.where(hrow == j, sink_ref[h] - shift_ref[h], snk)
            o = acc[:, 0:hd] / (acc[:, hd:hd + 1] + jnp.exp2(snk))
            o_ref[:, g * GQA_GROUP * hd:(g + 1) * GQA_GROUP * hd] = jnp.concatenate(
                [o[j * BLOCK:(j + 1) * BLOCK, :] for j in range(GQA_GROUP)], axis=-1).astype(BF16)

    blk = lambda f: pl.BlockSpec((BLOCK, wk), f)
    prev_ = lambda b, i: (b * nb + jnp.maximum(i - 1, 0), 0)
    cur_ = lambda b, i: (b * nb + i, 0)
    next_ = lambda b, i: (b * nb + jnp.minimum(i + 1, nb - 1), 0)
    ctx_spec = pl.BlockSpec((lc, wk), lambda b, i: (b, 0))
    smem = pl.BlockSpec(memory_space=pltpu.SMEM)
    return pl.pallas_call(
        kern, grid=(batch, nb),
        in_specs=[smem, smem, pl.BlockSpec((BLOCK, wq), cur_), blk(prev_), blk(cur_), blk(next_),
                  blk(prev_), blk(cur_), blk(next_), ctx_spec, ctx_spec],
        out_specs=pl.BlockSpec((BLOCK, wq), cur_),
        out_shape=SDS((batch * length, wq), BF16), name="gqa_window_fast",
        compiler_params=_params(("parallel", "arbitrary"), 40))(sink2, shift, q, k, k, k, v, v, v, kc, vc)


def _gqa_layer(streams, xs, mods, gain, p, rope, ctx_next):
    assert not ctx_next, "the windowed-GQA mixer is only implemented as the last layer"
    st_c, st_x = streams
    xc, xx = xs
    w_in, q_gain, k_gain, sink, w_out = p
    wq, wk = GQA_Q_HEADS * GQA_HEAD_DIM, GQA_KV_HEADS * GQA_HEAD_DIM
    w_q, w_k, w_v = w_in[:, :wq], w_in[:, wq:wq + wk], w_in[:, wq + wk:]
    w_in_p = jnp.concatenate([w_k, w_v, w_k[:, _rope_partner_perm(GQA_KV_HEADS)],
                              w_q, w_q[:, _rope_partner_perm(GQA_Q_HEADS)]], axis=1).astype(BF16)
    par = _rope_partner_perm(1)
    wts = (w_in_p, jnp.tile(q_gain, GQA_Q_HEADS).reshape(1, -1), jnp.tile(q_gain[par], GQA_Q_HEADS).reshape(1, -1),
           jnp.tile(k_gain, GQA_KV_HEADS).reshape(1, -1), jnp.tile(k_gain[par], GQA_KV_HEADS).reshape(1, -1))
    _, kc, vc = _gqa_proj_call(st_c.with_tile(PROJ_TILE), xc, mods, gain, wts, None, False)
    q, k, v = _gqa_proj_call(st_x.with_tile(PROJ_TILE), xx, mods, gain, wts, rope, True)
    sink2 = sink * LOG2E
    bound = LOG2E * GQA_SCALE * GQA_HEAD_DIM * jnp.max(jnp.abs(q_gain)) * jnp.max(jnp.abs(k_gain))
    shift = jnp.maximum(bound * SHIFT_MARGIN + 1.0, sink2).astype(BF16).astype(F32)
    args = (st_x.batch, st_x.seg, st_c.seg, q, k, v, kc, vc, sink2)
    a = lax.cond(jnp.max(shift) <= MAX_FIXED_SHIFT,
                 lambda: _gqa_window_fast_call(*args, shift),
                 lambda: _gqa_window_call(*args))
    return [None, _outproj_call(st_x, a, mods, w_out.astype(BF16), "gqa_out")]


def _cast_experts_call(w, layer):
    _, ne, a, b = w.shape
    eb = 4

    def kern(w_ref, o_ref):
        o_ref[...] = w_ref[...].astype(BF16)

    return pl.pallas_call(
        kern, grid=(ne // eb,),
        in_specs=[pl.BlockSpec((None, eb, a, b), lambda i: (layer, i, 0, 0))],
        out_specs=pl.BlockSpec((eb, a, b), lambda i: (i, 0, 0)),
        out_shape=SDS((ne, a, b), BF16), name="cast_experts",
        compiler_params=_params(("parallel",), 32))(w)


def _route(logits_t, bias_col):
    scores = _sigmoid(logits_t)
    biased = scores + bias_col
    rows = [biased[e:e + 1, :] for e in range(N_EXPERTS)]
    srow = [scores[e:e + 1, :] for e in range(N_EXPERTS)]
    epg = EXPERTS_PER_GROUP
    gscore = []
    for g in range(N_EXPERT_GROUPS):
        v = rows[g * epg:(g + 1) * epg]
        pair = [v[a] + v[b] for a in range(epg) for b in range(a + 1, epg)]
        gscore.append(functools.reduce(jnp.maximum, pair))
    ind, wloc = [], [None] * epg
    for g in range(N_EXPERT_GROUPS):
        best = None
        for g2 in range(N_EXPERT_GROUPS):
            if g2 == g:
                continue
            cnd = gscore[g] > gscore[g2] if g2 < g else gscore[g] >= gscore[g2]
            best = cnd if best is None else best & cnd
        ind.append(jnp.where(best, 1.0, 0.0))
        for j in range(epg):
            e = g * epg + j
            rank = None
            for e2 in range(g * epg, (g + 1) * epg):
                if e2 == e:
                    continue
                ahead = rows[e2] >= rows[e] if e2 < e else rows[e2] > rows[e]
                one = jnp.where(ahead, 1.0, 0.0)
                rank = one if rank is None else rank + one
            w = jnp.where(best & (rank < 2.0), srow[e], 0.0)
            wloc[j] = w if wloc[j] is None else wloc[j] + w
    den = functools.reduce(lambda a, b: a + b, wloc)
    return ind, [w / den for w in wloc]


MOE_TILE = 1024
MOE_CHUNK = 128


def _moe_call(st, x_in, y_in, mods, gain, router_wt, router_bias, w_gate, w_up, w_down):
    n, d = x_in.shape
    t = min(MOE_TILE, n)
    r = MOE_CHUNK
    ne, _, ff = w_gate.shape
    epg, ng = EXPERTS_PER_GROUP, N_EXPERT_GROUPS
    nch = t // r + ng - 1
    tiles = n // t
    wd2 = w_down.reshape(ne * ff, d)

    def kern(x_ref, y_ref, g_ref, sc_ref, sh_ref, g2_ref, rw_ref, rb_ref, wg_ref, wu_ref, wd_ref, o_ref,
             hb_ref, oh_ref, xg_ref):
        hb_ref[...] = _modnorm(x_ref[...] + y_ref[...].astype(F32), g_ref[...], sc_ref[...], sh_ref[...]).astype(BF16)
        ind, wloc = _route(_nt(rw_ref[...], hb_ref[...]), rb_ref[...])
        ind8 = jnp.concatenate(ind + [jnp.zeros((8 - ng, t), F32)], axis=0)
        upper = jnp.where(lax.broadcasted_iota(jnp.int32, (LANES, LANES), 0)
                          < lax.broadcasted_iota(jnp.int32, (LANES, LANES), 1), 1.0, 0.0).astype(BF16)
        blocks, before = [], jnp.zeros((8, 1), F32)
        for k in range(t // LANES):
            blk = ind8[:, k * LANES:(k + 1) * LANES]
            blocks.append(_dot(blk.astype(BF16), upper) + before)
            before = before + jnp.sum(blk, axis=1, keepdims=True)
        pos = jnp.concatenate(blocks, axis=1)
        count = [jnp.sum(ind[g]).astype(jnp.int32) for g in range(ng)]
        first = [jnp.int32(0)]
        for g in range(ng):
            first.append(first[g] + (count[g] + (r - 1)) // r)
        slot = functools.reduce(lambda a, b: a + b, [
            ind[g] * (pos[g:g + 1, :] + (first[g] * r).astype(F32)) for g in range(ng)])
        w_hi = [w.astype(BF16) for w in wloc]
        w_lo = [(w - h.astype(F32)).astype(BF16) for w, h in zip(wloc, w_hi)]
        wst = jnp.concatenate(w_hi + w_lo, axis=0)
        row_id = lax.broadcasted_iota(jnp.int32, (r, t), 0).astype(F32)
        onehot = lambda c: jnp.where(slot == row_id + float(c * r), 1.0, 0.0).astype(BF16)
        last = nch - 1
        for c in range(last):
            oh_ref[c * r:(c + 1) * r, :] = onehot(c)
        xg_ref[...] = _dot(oh_ref[...], hb_ref[...]).astype(BF16)
        wr_all = _nt(wst, oh_ref[...]).T

        def chunk_ffn(c, xg, wr):
            grp = functools.reduce(lambda a, b: a + b,
                                   [(c >= first[g]).astype(jnp.int32) for g in range(1, ng)])
            acts = []
            for j in range(epg):
                e = grp * epg + j
                gt = _dot(xg, wg_ref[e])
                up = _dot(xg, wu_ref[e])
                acts.append((gt * _sigmoid(gt) * up * (wr[:, j:j + 1] + wr[:, epg + j:epg + j + 1])).astype(BF16))
            wd_g = wd_ref[pl.ds(pl.multiple_of(grp * (epg * ff), epg * ff), epg * ff), :]
            return _dot(jnp.concatenate(acts, axis=1), wd_g).astype(BF16)

        for c in range(last):
            rows = slice(c * r, (c + 1) * r)

            def run(c=c, rows=rows):
                xg_ref[rows, :] = chunk_ffn(c, xg_ref[rows, :], wr_all[c * r:(c + 1) * r, :])

            run()

        tdot = lambda a, b: lax.dot_general(a, b, (((0,), (0,)), ((), ())), preferred_element_type=F32)
        o_ref[...] = x_ref[...] + y_ref[...].astype(F32) + g2_ref[...] * tdot(oh_ref[...], xg_ref[...])

        @pl.when(last < first[ng])
        def _():
            oh = onehot(last)
            out = chunk_ffn(last, _dot(oh, hb_ref[...]).astype(BF16), _nt(wst, oh).T)
            o_ref[...] += g2_ref[...] * tdot(oh, out)

    row = (lambda i: st.batch) if st.is_ctx else (lambda i: i // (st.seg // t))
    mod = lambda chunk: pl.BlockSpec((None, 1, d), lambda i: (row(i), 0, chunk))
    return pl.pallas_call(
        kern, grid=(tiles,),
        in_specs=[pl.BlockSpec((t, d), lambda i: (i, 0)),
                  pl.BlockSpec((t, d), lambda i: (i, 0)),
                  _full((1, d)), mod(SC2), mod(SH2), mod(G2), _full((ne, d)), _full((ne, 1)),
                  _resident((ne, d, ff)), _resident((ne, d, ff)), _resident((ne * ff, d))],
        out_specs=pl.BlockSpec((t, d), lambda i: (i, 0)),
        out_shape=SDS((n, d), F32),
        scratch_shapes=[pltpu.VMEM((t, d), BF16), pltpu.VMEM(((nch - 1) * r, t), BF16),
                        pltpu.VMEM(((nch - 1) * r, d), BF16)],
        name="moe", compiler_params=_params(("parallel",), 58))(
            x_in, y_in, gain, mods, mods, mods, router_wt, router_bias, w_gate, w_up, wd2)


def kernel(x, c, ctx, c_ctx, w_ada, b_ada, norm_mix, norm_ffn, fourier_w_out, conv_w_in, conv_w, conv_w_out,
           mla_w_in, mla_q_norm, mla_kv_norm, mla_w_uq, mla_w_ukv, mla_q_gain, mla_k_gain, mla_w_out,
           gqa_w_in, gqa_q_gain, gqa_k_gain, gqa_sink, gqa_w_out, router_w, router_bias,
           moe_w_gate, moe_w_up, moe_w_down):
    b, l, d = x.shape
    lc = ctx.shape[1]
    depth = w_ada.shape[0]
    st_c = Stream(b * lc, lc, min(b * lc, TOKEN_TILE), b, True)
    st_x = Stream(b * l, l, min(l, TOKEN_TILE), b, False)
    streams = [st_c, st_x]
    r8 = -(-(b + 1) // 8) * 8
    cvec = jnp.concatenate([c, c_ctx[None, :], jnp.zeros((r8 - b - 1, d), F32)], axis=0)
    mods_all = _ada_call(cvec, w_ada, b_ada).reshape(depth, r8, 1, 6 * d)
    rope = _rope_tables(l)
    router_wt = router_w.T.astype(BF16)
    router_b = router_bias.reshape(-1, 1)
    xs = [ctx.reshape(b * lc, d), x.reshape(b * l, d)]
    for i in range(depth):
        kind, j = i % 4, i // 4
        ctx_next = i < depth - 1
        mods = mods_all[i]
        gain = norm_mix[i].reshape(1, d)
        if not (ctx_next or kind >= 2):
            xs[0] = None
        if kind == 0:
            ys = _fourier_layer(streams, xs, mods, gain, fourier_w_out[j])
        elif kind == 1:
            ys = _conv_layer(streams, xs, mods, gain, conv_w_in[j], conv_w[j], conv_w_out[j])
        elif kind == 2:
            ys = _mla_layer(streams, xs, mods, gain,
                            (mla_w_in[j], mla_q_norm[j], mla_kv_norm[j], mla_w_uq[j], mla_w_ukv[j],
                             mla_q_gain[j], mla_k_gain[j], mla_w_out[j]), rope, ctx_next)
        else:
            ys = _gqa_layer(streams, xs, mods, gain,
                            (gqa_w_in[j], gqa_q_gain[j], gqa_k_gain[j], gqa_sink[j], gqa_w_out[j]),
                            rope, ctx_next)
        if not ctx_next:
            ys[0] = None
        gain2 = norm_ffn[i].reshape(1, d)
        wg, wu, wd = (_cast_experts_call(w, i) for w in (moe_w_gate, moe_w_up, moe_w_down))
        xs = [None if y is None else _moe_call(st, x, y, mods, gain2, router_wt, router_b, wg, wu, wd)
              for st, x, y in zip(streams, xs, ys)]
    return xs[1].reshape(b, l, d)
```

```python
import functools

import numpy as np
import jax
import jax.numpy as jnp
from jax import lax
from jax.experimental import pallas as pl
from jax.experimental.pallas import tpu as pltpu

F32, BF16 = jnp.float32, jnp.bfloat16
SDS = jax.ShapeDtypeStruct

EPS = 1e-6
GRID_W = 64
ROPE_THETA = 10000.0
N_FOURIER_GROUPS = 4
MLA_HEADS, MLA_Q_RANK, MLA_KV_RANK = 16, 256, 128
MLA_NOPE, MLA_ROPE, MLA_V = 128, 64, 128
MLA_SCALE = (MLA_NOPE + MLA_ROPE) ** -0.5
GQA_Q_HEADS, GQA_KV_HEADS, GQA_HEAD_DIM = 16, 4, 64
GQA_GROUP = GQA_Q_HEADS // GQA_KV_HEADS
GQA_SCALE = GQA_HEAD_DIM ** -0.5
WINDOW = 128
BLOCK = 128
N_EXPERTS, N_EXPERT_GROUPS, EXPERT_FF = 16, 4, 256
EXPERTS_PER_GROUP = N_EXPERTS // N_EXPERT_GROUPS
ROPE_DIM = 64
LOG2E = 1.4426950408889634
MAX_FIXED_SHIFT = 50.0
SHIFT_MARGIN = 1.02
TOKEN_TILE = 1024
PROJ_TILE = 512

V7X_VMEM_BYTES = 64 * 1024 * 1024
LANES = 128
SH1, SC1, G1, SH2, SC2, G2 = range(6)


def _params(sem, vmem_mb):
    return pltpu.CompilerParams(dimension_semantics=sem, vmem_limit_bytes=vmem_mb * 1024 * 1024)


def _sigmoid(v):
    return 1.0 / (1.0 + jnp.exp(-v))


def _modnorm(x, gain, sc, sh):
    ms = jnp.mean(x * x, axis=-1, keepdims=True)
    return x * lax.rsqrt(ms + EPS) * (gain * (1.0 + sc)) + sh


def _nt(a, b):
    return lax.dot_general(a, b, (((1,), (1,)), ((), ())), preferred_element_type=F32)


def _dot(a, b):
    return jnp.dot(a, b, preferred_element_type=F32)


class Stream:
    def __init__(self, n, seg, tile, batch, is_ctx):
        self.n, self.seg, self.tile, self.batch, self.is_ctx = n, seg, tile, batch, is_ctx
        self.tiles = n // tile
        self.tiles_per_seg = seg // tile

    def mod_row(self, t):
        return self.batch if self.is_ctx else t // self.tiles_per_seg

    def seg_row(self, b):
        return self.batch if self.is_ctx else b

    def with_tile(self, tile):
        return Stream(self.n, self.seg, min(tile, self.tile), self.batch, self.is_ctx)


def _mod_spec(st, chunk, d):
    return pl.BlockSpec((None, 1, d), lambda t: (st.mod_row(t), 0, chunk))


def _full(shape):
    nd = len(shape)
    return pl.BlockSpec(shape, lambda *_: (0,) * nd)


def _resident(shape):
    nd = len(shape)
    return pl.BlockSpec(shape, lambda *_: (0,) * nd, pipeline_mode=pl.Buffered(1))


def _ada_call(cvec, w_ada, b_ada):
    depth, d, d6 = w_ada.shape
    r8 = cvec.shape[0]
    tn = d6 // 4

    def kern(c_ref, w_ref, b_ref, o_ref):
        c = c_ref[...]
        s = (c * _sigmoid(c)).astype(BF16)
        o_ref[...] = _dot(s, w_ref[...].astype(BF16)) + b_ref[...]

    return pl.pallas_call(
        kern, grid=(depth, d6 // tn),
        in_specs=[pl.BlockSpec((r8, d), lambda i, j: (0, 0)),
                  pl.BlockSpec((None, d, tn), lambda i, j: (i, 0, j)),
                  pl.BlockSpec((None, 1, tn), lambda i, j: (i, 0, j))],
        out_specs=pl.BlockSpec((None, r8, tn), lambda i, j: (i, 0, j)),
        out_shape=SDS((depth, r8, d6), F32), name="ada",
        compiler_params=_params(("arbitrary", "arbitrary"), 40))(cvec, w_ada, b_ada.reshape(depth, 1, d6))


def _outproj_call(st, a, mods, w, name):
    n, k = a.shape
    d = w.shape[1]
    t = st.tile

    def kern(a_ref, g1_ref, w_ref, o_ref):
        o_ref[...] = (g1_ref[...] * _dot(a_ref[...], w_ref[...])).astype(BF16)

    return pl.pallas_call(
        kern, grid=(st.tiles,),
        in_specs=[pl.BlockSpec((t, k), lambda i: (i, 0)), _mod_spec(st, G1, d), _resident((k, d))],
        out_specs=pl.BlockSpec((t, d), lambda i: (i, 0)),
        out_shape=SDS((n, d), BF16), name=name,
        compiler_params=_params(("parallel",), 40))(a, mods, w)


def _dft_tables(length, radix):
    lr = length // radix
    m = np.arange(lr)[None, :, None]
    j = np.arange(radix)[:, None, None]
    nn = np.arange(lr)[None, None, :]
    ang = 2.0 * np.pi * (((radix * m + j) * nn) % length) / length
    e = np.concatenate([np.cos(ang), np.sin(ang)], axis=-1) / np.sqrt(length)
    return jnp.asarray(e, dtype=F32).astype(BF16)


def _group_dft_tables(group):
    k = np.arange(group)
    ang = 2.0 * np.pi * ((k[:, None] * k[None, :]) % group) / group
    return (jnp.asarray(np.cos(ang) / np.sqrt(group), dtype=F32),
            jnp.asarray(np.sin(ang) / np.sqrt(group), dtype=F32))


def _fourier_weight_call(w_out):
    d = w_out.shape[0]
    grp = d // N_FOURIER_GROUPS
    cg, sg = _group_dft_tables(grp)

    def kern(cg_ref, sg_ref, w_ref, o_ref):
        w = w_ref[...].astype(BF16)
        o_ref[:, :d] = _dot(cg_ref[...].astype(BF16), w).astype(BF16)
        o_ref[:, d:] = _dot(sg_ref[...].astype(BF16), w).astype(BF16)

    return pl.pallas_call(
        kern, grid=(N_FOURIER_GROUPS,),
        in_specs=[_full((grp, grp)), _full((grp, grp)), pl.BlockSpec((grp, d), lambda g: (g, 0))],
        out_specs=pl.BlockSpec((grp, 2 * d), lambda g: (g, 0)),
        out_shape=SDS((d, 2 * d), BF16), name="fourier_w",
        compiler_params=_params(("arbitrary",), 32))(cg, sg, w_out)


def _radix_terms(radix, j):
    real, imag = [], []
    for q in range(radix):
        k = (j * q * (4 // radix)) % 4 if radix > 1 else 0
        if k == 0:
            real.append((1, 0, q)); imag.append((-1, 1, q))
        elif k == 1:
            real.append((-1, 1, q)); imag.append((-1, 0, q))
        elif k == 2:
            real.append((-1, 0, q)); imag.append((1, 1, q))
        else:
            real.append((1, 1, q)); imag.append((1, 0, q))
    return real, imag


def _fourier_mix_call(st, x, mods, gain, wcs, radix):
    n, d = x.shape
    seg = st.seg
    lr = seg // radix
    e = _dft_tables(seg, radix)
    nb = n // seg
    rt = min(seg, 512)
    cblk = 2 * LANES

    def kern(x_ref, g_ref, sc_ref, sh_ref, g1_ref, w_ref, e_ref, o_ref, p_ref, v_ref, *z_refs):
        for i in range(seg // rt):
            rows = slice(i * rt, (i + 1) * rt)
            h = _modnorm(x_ref[rows, :], g_ref[...], sc_ref[...], sh_ref[...]).astype(BF16)
            p_ref[rows, :] = _dot(h, w_ref[...]).astype(BF16)
        for c in range(d // cblk):
            cols = (slice(c * cblk, (c + 1) * cblk), slice(d + c * cblk, d + (c + 1) * cblk))
            for j in range(radix):
                real, imag = _radix_terms(radix, j)

                def comb(terms):
                    acc = None
                    for sgn, part, q in terms:
                        v = p_ref[q * lr:(q + 1) * lr, cols[part]]
                        if acc is None:
                            acc = v if sgn > 0 else -v
                        else:
                            acc = acc + v if sgn > 0 else acc - v
                    return acc

                v_ref[0:lr, :] = comb(real).astype(BF16)
                v_ref[lr:2 * lr, :] = comb(imag).astype(BF16)
                z = _dot(e_ref[j], v_ref[...])
                for k, z_ref in enumerate(z_refs):
                    z_ref[pl.ds(j, lr, stride=radix), :] = z[:, k * LANES:(k + 1) * LANES]
            for k, z_ref in enumerate(z_refs):
                sl = slice(c * cblk + k * LANES, c * cblk + (k + 1) * LANES)
                o_ref[:, sl] = (g1_ref[:, sl] * z_ref[...]).astype(BF16)

    mod = lambda chunk: pl.BlockSpec((None, 1, d), lambda b: (st.seg_row(b), 0, chunk))
    return pl.pallas_call(
        kern, grid=(nb,),
        in_specs=[pl.BlockSpec((seg, d), lambda b: (b, 0)), _full((1, d)), mod(SC1), mod(SH1), mod(G1),
                  _resident((d, 2 * d)), _resident((radix, lr, 2 * lr))],
        out_specs=pl.BlockSpec((seg, d), lambda b: (b, 0)),
        out_shape=SDS((n, d), BF16),
        scratch_shapes=[pltpu.VMEM((seg, 2 * d), BF16), pltpu.VMEM((2 * lr, cblk), BF16)]
        + [pltpu.VMEM((seg, LANES), F32)] * (cblk // LANES),
        name=f"fourier_mix_r{radix}",
        compiler_params=_params(("parallel",), 56))(x, gain, mods, mods, mods, wcs, e)


def _fourier_layer(streams, xs, mods, gain, w_out):
    wcs = _fourier_weight_call(w_out)
    return [None if x is None else _fourier_mix_call(st, x, mods, gain, wcs, 4 if st.seg >= 1024 else 1)
            for st, x in zip(streams, xs)]


def _conv_mix_call(st, x, mods, gain, w_in, conv_w, w_out):
    n, d = x.shape
    seg = st.seg
    nb = n // seg
    rt = min(seg, 512)
    nrt = seg // rt
    halo = 16

    def kern(x_ref, g_ref, sc_ref, sh_ref, g1_ref, win_ref, cw_ref, wout_ref, o_ref, bg_ref, u_ref):
        for i in range(nrt):
            rows = slice(i * rt, (i + 1) * rt)
            h = _modnorm(x_ref[rows, :], g_ref[...], sc_ref[...], sh_ref[...]).astype(BF16)
            bg_ref[rows, :] = _dot(h, win_ref[:, 0:d]).astype(BF16)
            u_ref[rows, :] = (_dot(h, win_ref[:, d:2 * d]) * _dot(h, win_ref[:, 2 * d:3 * d])).astype(BF16)
        row = lax.broadcasted_iota(jnp.int32, (rt, 1), 0)
        zero_row = jnp.zeros((1, d), F32)
        for i in range(nrt):
            rows = slice(i * rt, (i + 1) * rt)
            u = u_ref[rows, :].astype(F32)
            prev_row = zero_row if i == 0 else u_ref[i * rt - halo:i * rt, :].astype(F32)[halo - 1:halo, :]
            next_row = (zero_row if i == nrt - 1
                        else u_ref[(i + 1) * rt:(i + 1) * rt + halo, :].astype(F32)[0:1, :])
            um = jnp.where(row == 0, prev_row, pltpu.roll(u, 1, axis=0))
            up = jnp.where(row == rt - 1, next_row, pltpu.roll(u, rt - 1, axis=0))
            z = cw_ref[0:1, :] * um + cw_ref[1:2, :] * u + cw_ref[2:3, :] * up
            a = (bg_ref[rows, :].astype(F32) * z).astype(BF16)
            o_ref[rows, :] = (g1_ref[...] * _dot(a, wout_ref[...])).astype(BF16)

    mod = lambda chunk: pl.BlockSpec((None, 1, d), lambda b: (st.seg_row(b), 0, chunk))
    seg_spec = pl.BlockSpec((seg, d), lambda b: (b, 0))
    return pl.pallas_call(
        kern, grid=(nb,),
        in_specs=[seg_spec, _full((1, d)), mod(SC1), mod(SH1), mod(G1),
                  _resident((d, 3 * d)), _full((3, d)), _resident((d, d))],
        out_specs=seg_spec, out_shape=SDS((n, d), BF16),
        scratch_shapes=[pltpu.VMEM((seg, d), BF16), pltpu.VMEM((seg, d), BF16)],
        name="conv_mix", compiler_params=_params(("parallel",), 56))(
            x, gain, mods, mods, mods, w_in, conv_w, w_out)


def _conv_layer(streams, xs, mods, gain, w_in, conv_w, w_out):
    w_in_b, w_out_b = w_in.astype(BF16), w_out.astype(BF16)
    return [None if x is None else _conv_mix_call(st, x, mods, gain, w_in_b, conv_w, w_out_b)
            for st, x in zip(streams, xs)]


def _rope_tables(length):
    q = ROPE_DIM // 4
    pos = np.arange(length)
    rc = np.stack([pos // GRID_W, pos % GRID_W], axis=1).astype(np.float32)
    inv = (ROPE_THETA ** (-np.arange(q, dtype=np.float32) / q)).astype(np.float32)
    lane = np.arange(ROPE_DIM)
    ang = rc[:, lane >> 5] * inv[lane & (q - 1)][None, :]
    sign = np.where((lane & q) == 0, -1.0, 1.0)[None, :]
    cos = np.cos(ang.astype(np.float32)).astype(np.float32)
    sin = (np.sin(ang.astype(np.float32)) * sign).astype(np.float32)
    reps = LANES // ROPE_DIM
    return jnp.asarray(np.tile(cos, (1, reps))), jnp.asarray(np.tile(sin, (1, reps)))


def _rope_partner_perm(n_heads):
    lane = np.arange(n_heads * ROPE_DIM)
    return lane ^ (ROPE_DIM // 4)


def _seg_rinv(x, seg, on_mxu=False):
    t, w = x.shape
    cols = []
    if on_mxu:
        cw = 2 * LANES if w % (2 * LANES) == 0 else LANES
        same = (lax.broadcasted_iota(jnp.int32, (cw, cw), 0) // seg
                == lax.broadcasted_iota(jnp.int32, (cw, cw), 1) // seg)
        ones = jnp.where(same, 1.0, 0.0).astype(BF16)
        for c in range(w // cw):
            blk = x[:, c * cw:(c + 1) * cw]
            cols.append(lax.rsqrt(_dot((blk * blk).astype(BF16), ones) * (1.0 / seg) + EPS))
        return cols[0] if len(cols) == 1 else jnp.concatenate(cols, axis=-1)
    for c in range(w // LANES):
        blk = x[:, c * LANES:(c + 1) * LANES]
        sq = blk * blk
        if seg == LANES:
            cols.append(jnp.broadcast_to(
                lax.rsqrt(jnp.mean(sq, axis=-1, keepdims=True) + EPS), (t, LANES)))
        else:
            lane = lax.broadcasted_iota(jnp.int32, (t, LANES), 1)
            low = lane < seg
            lo = jnp.sum(jnp.where(low, sq, 0.0), axis=-1, keepdims=True) * (1.0 / seg)
            hi = jnp.sum(jnp.where(low, 0.0, sq), axis=-1, keepdims=True) * (1.0 / seg)
            cols.append(jnp.where(low, lax.rsqrt(lo + EPS), lax.rsqrt(hi + EPS)))
    return cols[0] if len(cols) == 1 else jnp.concatenate(cols, axis=-1)


def _tile_lanes(v, w):
    reps = w // LANES
    return v if reps == 1 else jnp.concatenate([v] * reps, axis=-1)


def _mla_proj_call(st, x, mods, gain, wts, rope, need_q):
    n, d = x.shape
    t = st.tile
    h = MLA_HEADS
    wn, wr = h * MLA_NOPE, h * MLA_ROPE
    (w_in, q_norm, kv_norm, w_uq, w_ukv, qg_n, qg_r, qg_rp, kg_n, kg_r, kg_rp) = wts
    use_rope = rope is not None

    def kern(*refs):
        refs = list(refs)
        x_ref, g_ref, sc_ref, sh_ref, win_ref, qn_ref_, kvn_ref_, wuq_ref, wukv_ref = refs[:9]
        qgn_ref, qgr_ref, qgrp_ref, kgn_ref, kgr_ref, kgrp_ref = refs[9:15]
        rest = refs[15:]
        if use_rope:
            cos_ref, sin_ref = rest[:2]
            rest = rest[2:]
        if need_q:
            oqn_ref, oqr_ref = rest[:2]
            rest = rest[2:]
        okn_ref, okr_ref, ov_ref = rest
        hm = _modnorm(x_ref[...], g_ref[...], sc_ref[...], sh_ref[...]).astype(BF16)
        ck = _dot(hm, win_ref[...])
        kr2 = ck[:, MLA_Q_RANK + MLA_KV_RANK:]
        kr_rinv = _seg_rinv(kr2, MLA_ROPE, on_mxu=True)[:, 0:MLA_ROPE]
        kr_raw = kr2[:, 0:MLA_ROPE] * kgr_ref[...]
        if use_rope:
            kr_par = kr2[:, MLA_ROPE:] * kgrp_ref[...]
            kr = kr_rinv * (kr_raw * cos_ref[:, 0:MLA_ROPE] + kr_par * sin_ref[:, 0:MLA_ROPE])
        else:
            kr = kr_rinv * kr_raw
        okr_ref[...] = kr.astype(BF16)
        ckv = ck[:, MLA_Q_RANK:MLA_Q_RANK + MLA_KV_RANK]
        ckv = (ckv * _seg_rinv(ckv, LANES) * kvn_ref_[...]).astype(BF16)
        kv = _dot(ckv, wukv_ref[...])
        kn = kv[:, 0:wn]
        okn_ref[...] = (kn * _seg_rinv(kn, MLA_NOPE) * kgn_ref[...]).astype(BF16)
        ov_ref[...] = kv[:, wn:].astype(BF16)
        if need_q:
            cq = ck[:, 0:MLA_Q_RANK]
            rq = lax.rsqrt(jnp.mean(cq * cq, axis=-1, keepdims=True) + EPS)
            cq = (cq * rq * qn_ref_[...]).astype(BF16)
            q = _dot(cq, wuq_ref[...])
            qn = q[:, 0:wn]
            oqn_ref[...] = (qn * _seg_rinv(qn, MLA_NOPE) * (qgn_ref[...] * (MLA_SCALE * LOG2E))).astype(BF16)
            qr_raw = q[:, wn:wn + wr]
            rinv = _seg_rinv(qr_raw, MLA_ROPE, on_mxu=True) * (MLA_SCALE * LOG2E)
            if use_rope:
                qr_par = q[:, wn + wr:]
                cos = _tile_lanes(cos_ref[...], wr)
                sin = _tile_lanes(sin_ref[...], wr)
                qr = rinv * (qr_raw * qgr_ref[...] * cos + qr_par * qgrp_ref[...] * sin)
            else:
                qr = rinv * (qr_raw * qgr_ref[...])
            oqr_ref[...] = qr.astype(BF16)

    tile = lambda w: pl.BlockSpec((t, w), lambda i: (i, 0))
    in_specs = [tile(d), _full((1, d)), _mod_spec(st, SC1, d), _mod_spec(st, SH1, d),
                _resident(w_in.shape), _full(q_norm.shape), _full(kv_norm.shape),
                _resident(w_uq.shape), _resident(w_ukv.shape),
                _full(qg_n.shape), _full(qg_r.shape), _full(qg_rp.shape),
                _full(kg_n.shape), _full(kg_r.shape), _full(kg_rp.shape)]
    args = [x, gain, mods, mods, w_in, q_norm, kv_norm, w_uq, w_ukv, qg_n, qg_r, qg_rp, kg_n, kg_r, kg_rp]
    if use_rope:
        tps = st.tiles_per_seg
        in_specs += [pl.BlockSpec((t, LANES), lambda i: (i % tps, 0))] * 2
        args += list(rope)
    out_specs, out_shape = [], []
    if need_q:
        out_specs += [tile(wn), tile(wr)]
        out_shape += [SDS((n, wn), BF16), SDS((n, wr), BF16)]
    out_specs += [tile(wn), tile(MLA_ROPE), tile(h * MLA_V)]
    out_shape += [SDS((n, wn), BF16), SDS((n, MLA_ROPE), BF16), SDS((n, h * MLA_V), BF16)]
    outs = pl.pallas_call(
        kern, grid=(st.tiles,), in_specs=in_specs, out_specs=out_specs, out_shape=out_shape,
        name="mla_proj", compiler_params=_params(("parallel",), 48))(*args)
    if need_q:
        return tuple(outs)
    return (None, None) + tuple(outs)


def _mla_attn_fast_call(batch, lq, tq, q, kv_sets, shift):
    qn, qr = q
    lk_total = sum(s[3] for s in kv_sets)
    hp = 4 if lk_total > 1024 else 8
    n_hp = MLA_HEADS // hp
    nq = lq // tq
    nsets = len(kv_sets)
    lengths = [s[3] for s in kv_sets]
    lk = sum(lengths)
    kw = 2 * LANES
    pad = kw - MLA_NOPE - MLA_ROPE

    def kern(*refs):
        sh_ref, qn_ref, qr_ref = refs[:3]
        sets = [refs[3 + 3 * s:6 + 3 * s] for s in range(nsets)]
        o_ref, kq_ref, vq_ref = refs[-3:]

        def build_kv():
            lane_k = lax.broadcasted_iota(jnp.int32, (lk, pad), 1)
            lane_v = lax.broadcasted_iota(jnp.int32, (lk, kw - MLA_V), 1)
            for hh in range(hp):
                off = 0
                for (kn_ref, kr_ref, v_ref), length in zip(sets, lengths):
                    kq_ref[hh, off:off + length, 0:MLA_NOPE] = kn_ref[:, hh * MLA_NOPE:(hh + 1) * MLA_NOPE]
                    kq_ref[hh, off:off + length, MLA_NOPE:MLA_NOPE + MLA_ROPE] = kr_ref[...]
                    vq_ref[hh, off:off + length, 0:MLA_V] = v_ref[:, hh * MLA_V:(hh + 1) * MLA_V]
                    off += length
                kq_ref[hh, :, MLA_NOPE + MLA_ROPE:] = jnp.where(lane_k == 0, -sh_ref[:, 0:pad], 0.0).astype(BF16)
                vq_ref[hh, :, MLA_V:] = jnp.where(lane_v == 0, 1.0, 0.0).astype(BF16)

        if nq == 1:
            build_kv()
        else:
            pl.when(pl.program_id(2) == 0)(build_kv)

        lane_q = lax.broadcasted_iota(jnp.int32, (tq, pad), 1)
        one = jnp.where(lane_q == 0, 1.0, 0.0).astype(BF16)
        for hh in range(hp):
            qc = jnp.concatenate([qn_ref[:, hh * MLA_NOPE:(hh + 1) * MLA_NOPE],
                                  qr_ref[:, hh * MLA_ROPE:(hh + 1) * MLA_ROPE], one], axis=1)
            p = jnp.exp2(_nt(qc, kq_ref[hh]).astype(BF16))
            acc = _dot(p, vq_ref[hh])
            o_ref[:, hh * MLA_V:(hh + 1) * MLA_V] = (acc[:, 0:MLA_V] / acc[:, MLA_V:MLA_V + 1]).astype(BF16)

    in_specs = [_full((1, LANES)),
                pl.BlockSpec((tq, hp * MLA_NOPE), lambda b, h, i: (b * nq + i, h)),
                pl.BlockSpec((tq, hp * MLA_ROPE), lambda b, h, i: (b * nq + i, h))]
    args = [shift, qn, qr]
    for kn, kr, v, length in kv_sets:
        in_specs += [pl.BlockSpec((length, hp * MLA_NOPE), lambda b, h, i: (b, h)),
                     pl.BlockSpec((length, MLA_ROPE), lambda b, h, i: (b, 0)),
                     pl.BlockSpec((length, hp * MLA_V), lambda b, h, i: (b, h))]
        args += [kn, kr, v]
    return pl.pallas_call(
        kern, grid=(batch, n_hp, nq), in_specs=in_specs,
        out_specs=pl.BlockSpec((tq, hp * MLA_V), lambda b, h, i: (b * nq + i, h)),
        out_shape=SDS((batch * lq, MLA_HEADS * MLA_V), BF16),
        scratch_shapes=[pltpu.VMEM((hp, lk, kw), BF16), pltpu.VMEM((hp, lk, kw), BF16)],
        name="mla_attn_fast",
        compiler_params=_params(("parallel", "parallel", "arbitrary"), 56))(*args)


def _mla_attn_call(batch, lq, tq, q, kv_sets):
    qn, qr = q
    hp = 2
    n_hp = MLA_HEADS // hp
    nq = lq // tq
    nsets = len(kv_sets)

    def kern(*refs):
        qn_ref, qr_ref = refs[:2]
        o_ref = refs[-1]
        sets = [refs[2 + 3 * s:5 + 3 * s] for s in range(nsets)]
        for hh in range(hp):
            qn_h = qn_ref[:, hh * MLA_NOPE:(hh + 1) * MLA_NOPE]
            qr_h = qr_ref[:, hh * MLA_ROPE:(hh + 1) * MLA_ROPE]
            scores = [_nt(qn_h, kn_ref[:, hh * MLA_NOPE:(hh + 1) * MLA_NOPE]) + _nt(qr_h, kr_ref[...])
                      for kn_ref, kr_ref, _ in sets]
            m = functools.reduce(jnp.maximum, [jnp.max(s, axis=-1, keepdims=True) for s in scores])
            ps = [jnp.exp2(s - m) for s in scores]
            den = functools.reduce(lambda a, b: a + b, [jnp.sum(p, axis=-1, keepdims=True) for p in ps])
            acc = functools.reduce(lambda a, b: a + b, [
                _dot(p.astype(BF16), v_ref[:, hh * MLA_V:(hh + 1) * MLA_V])
                for p, (_, _, v_ref) in zip(ps, sets)])
            o_ref[:, hh * MLA_V:(hh + 1) * MLA_V] = (acc / den).astype(BF16)

    in_specs = [pl.BlockSpec((tq, hp * MLA_NOPE), lambda b, h, i: (b * nq + i, h)),
                pl.BlockSpec((tq, hp * MLA_ROPE), lambda b, h, i: (b * nq + i, h))]
    args = [qn, qr]
    for kn, kr, v, length in kv_sets:
        in_specs += [pl.BlockSpec((length, hp * MLA_NOPE), lambda b, h, i: (b, h)),
                     pl.BlockSpec((length, MLA_ROPE), lambda b, h, i: (b, 0)),
                     pl.BlockSpec((length, hp * MLA_V), lambda b, h, i: (b, h))]
        args += [kn, kr, v]
    return pl.pallas_call(
        kern, grid=(batch, n_hp, nq), in_specs=in_specs,
        out_specs=pl.BlockSpec((tq, hp * MLA_V), lambda b, h, i: (b * nq + i, h)),
        out_shape=SDS((batch * lq, MLA_HEADS * MLA_V), BF16), name="mla_attn",
        compiler_params=_params(("parallel", "parallel", "arbitrary"), 56))(*args)


def _mla_layer(streams, xs, mods, gain, p, rope, ctx_next):
    st_c, st_x = streams
    xc, xx = xs
    h = MLA_HEADS
    w_in, q_norm, kv_norm, w_uq, w_ukv, q_gain, k_gain, w_out = p
    par = _rope_partner_perm(1)
    kr_cols = w_in[:, MLA_Q_RANK + MLA_KV_RANK:]
    w_in_p = jnp.concatenate([w_in, kr_cols[:, par]], axis=1).astype(BF16)
    wq = w_uq.reshape(MLA_Q_RANK, h, MLA_NOPE + MLA_ROPE)
    wq_n = wq[:, :, :MLA_NOPE].reshape(MLA_Q_RANK, h * MLA_NOPE)
    wq_r = wq[:, :, MLA_NOPE:]
    w_uq_p = jnp.concatenate([wq_n, wq_r.reshape(MLA_Q_RANK, h * MLA_ROPE),
                              wq_r[:, :, par].reshape(MLA_Q_RANK, h * MLA_ROPE)], axis=1).astype(BF16)
    wkv = w_ukv.reshape(MLA_KV_RANK, h, MLA_NOPE + MLA_V)
    w_ukv_p = jnp.concatenate([wkv[:, :, :MLA_NOPE].reshape(MLA_KV_RANK, h * MLA_NOPE),
                               wkv[:, :, MLA_NOPE:].reshape(MLA_KV_RANK, h * MLA_V)], axis=1).astype(BF16)
    qg_r = q_gain[MLA_NOPE:]
    kg_r = k_gain[MLA_NOPE:]
    wts = (w_in_p, q_norm.reshape(1, -1), kv_norm.reshape(1, -1), w_uq_p, w_ukv_p,
           jnp.tile(q_gain[:MLA_NOPE], h).reshape(1, -1), jnp.tile(qg_r, h).reshape(1, -1),
           jnp.tile(qg_r[par], h).reshape(1, -1), jnp.tile(k_gain[:MLA_NOPE], h).reshape(1, -1),
           kg_r.reshape(1, -1), kg_r[par].reshape(1, -1))
    w_out_b = w_out.astype(BF16)
    qn_c, qr_c, kn_c, kr_c, v_c = _mla_proj_call(st_c.with_tile(PROJ_TILE), xc, mods, gain, wts, None, ctx_next)
    qn_x, qr_x, kn_x, kr_x, v_x = _mla_proj_call(st_x.with_tile(PROJ_TILE), xx, mods, gain, wts, rope, True)
    b = st_x.batch
    amax = lambda g: jnp.max(jnp.abs(g))
    bound = LOG2E * MLA_SCALE * (MLA_NOPE * amax(q_gain[:MLA_NOPE]) * amax(k_gain[:MLA_NOPE])
                                 + MLA_ROPE * amax(qg_r) * amax(kg_r))
    shift = bound * SHIFT_MARGIN + 1.0
    shift_row = jnp.full((1, LANES), shift, F32)

    def attend(lq, tq, q, kv_sets):
        return lax.cond(shift <= MAX_FIXED_SHIFT,
                        lambda: _mla_attn_fast_call(b, lq, tq, q, kv_sets, shift_row),
                        lambda: _mla_attn_call(b, lq, min(tq, 512), q, kv_sets))

    a_x = attend(st_x.seg, min(st_x.seg, 2048), (qn_x, qr_x), [(kn_c, kr_c, v_c, st_c.seg), (kn_x, kr_x, v_x, st_x.seg)])
    out_x = _outproj_call(st_x, a_x, mods, w_out_b, "mla_out")
    out_c = None
    if ctx_next:
        a_c = attend(st_c.seg, st_c.seg, (qn_c, qr_c), [(kn_c, kr_c, v_c, st_c.seg)])
        out_c = _outproj_call(st_c, a_c, mods, w_out_b, "mla_out")
    return [out_c, out_x]


def _gqa_proj_call(st, x, mods, gain, wts, rope, need_q):
    n, d = x.shape
    t = st.tile
    wq, wk = GQA_Q_HEADS * GQA_HEAD_DIM, GQA_KV_HEADS * GQA_HEAD_DIM
    w_in, qg, qgp, kg, kgp = wts
    use_rope = rope is not None

    def kern(*refs):
        refs = list(refs)
        x_ref, g_ref, sc_ref, sh_ref, w_ref, qg_ref, qgp_ref, kg_ref, kgp_ref = refs[:9]
        rest = refs[9:]
        if use_rope:
            cos_ref, sin_ref = rest[:2]
            rest = rest[2:]
        if need_q:
            oq_ref = rest[0]
            rest = rest[1:]
        ok_ref, ov_ref = rest
        hm = _modnorm(x_ref[...], g_ref[...], sc_ref[...], sh_ref[...]).astype(BF16)
        kvp = _dot(hm, w_ref[:, 0:3 * wk])
        k_raw = kvp[:, 0:wk]
        k_rinv = _seg_rinv(k_raw, GQA_HEAD_DIM)
        ov_ref[...] = kvp[:, wk:2 * wk].astype(BF16)
        if use_rope:
            cos_k, sin_k = _tile_lanes(cos_ref[...], wk), _tile_lanes(sin_ref[...], wk)
            k = k_rinv * (k_raw * kg_ref[...] * cos_k + kvp[:, 2 * wk:] * kgp_ref[...] * sin_k)
        else:
            k = k_rinv * (k_raw * kg_ref[...])
        ok_ref[...] = k.astype(BF16)
        if need_q:
            qp = _dot(hm, w_ref[:, 3 * wk:])
            q_raw = qp[:, 0:wq]
            rinv = _seg_rinv(q_raw, GQA_HEAD_DIM) * (GQA_SCALE * LOG2E)
            if use_rope:
                cos_q, sin_q = _tile_lanes(cos_ref[...], wq), _tile_lanes(sin_ref[...], wq)
                q = rinv * (q_raw * qg_ref[...] * cos_q + qp[:, wq:] * qgp_ref[...] * sin_q)
            else:
                q = rinv * (q_raw * qg_ref[...])
            oq_ref[...] = q.astype(BF16)

    tile = lambda w: pl.BlockSpec((t, w), lambda i: (i, 0))
    in_specs = [tile(d), _full((1, d)), _mod_spec(st, SC1, d), _mod_spec(st, SH1, d),
                _resident(w_in.shape), _full(qg.shape), _full(qgp.shape), _full(kg.shape), _full(kgp.shape)]
    args = [x, gain, mods, mods, w_in, qg, qgp, kg, kgp]
    if use_rope:
        tps = st.tiles_per_seg
        in_specs += [pl.BlockSpec((t, LANES), lambda i: (i % tps, 0))] * 2
        args += list(rope)
    out_specs, out_shape = [], []
    if need_q:
        out_specs.append(tile(wq))
        out_shape.append(SDS((n, wq), BF16))
    out_specs += [tile(wk), tile(wk)]
    out_shape += [SDS((n, wk), BF16)] * 2
    outs = pl.pallas_call(
        kern, grid=(st.tiles,), in_specs=in_specs, out_specs=out_specs, out_shape=out_shape,
        name="gqa_proj", compiler_params=_params(("parallel",), 48))(*args)
    return tuple(outs) if need_q else (None,) + tuple(outs)


def _gqa_window_call(batch, length, lc, q, k, v, kc, vc, sink):
    nb = length // BLOCK
    hd = GQA_HEAD_DIM
    wq, wk = GQA_Q_HEADS * hd, GQA_KV_HEADS * hd
    rows = GQA_GROUP * BLOCK

    def kern(sink_ref, q_ref, kp_ref, k0_ref, kn_ref, vp_ref, v0_ref, vn_ref, kc_ref, vc_ref, o_ref):
        nblk = pl.program_id(1)
        r = lax.broadcasted_iota(jnp.int32, (rows, 3 * BLOCK), 0) & (BLOCK - 1)
        c = lax.broadcasted_iota(jnp.int32, (rows, 3 * BLOCK), 1)
        valid = (c >= r + BLOCK - WINDOW) & (c <= r + BLOCK + WINDOW)
        valid = valid & ((c >= BLOCK) | (nblk > 0)) & ((c < 2 * BLOCK) | (nblk < nb - 1))
        hrow = lax.broadcasted_iota(jnp.int32, (rows, 1), 0) // BLOCK
        for g in range(GQA_KV_HEADS):
            sl = slice(g * hd, (g + 1) * hd)
            qg = jnp.concatenate([q_ref[:, (g * GQA_GROUP + j) * hd:(g * GQA_GROUP + j + 1) * hd]
                                  for j in range(GQA_GROUP)], axis=0)
            kband = jnp.concatenate([kp_ref[:, sl], k0_ref[:, sl], kn_ref[:, sl]], axis=0)
            vband = jnp.concatenate([vp_ref[:, sl], v0_ref[:, sl], vn_ref[:, sl]], axis=0)
            s_c = _nt(qg, kc_ref[:, sl])
            s_b = jnp.where(valid, _nt(qg, kband), -1e30)
            snk = jnp.zeros((rows, 1), F32)
            for j in range(GQA_GROUP):
                snk = jnp.where(hrow == j, sink_ref[g * GQA_GROUP + j], snk)
            m = jnp.maximum(jnp.maximum(jnp.max(s_c, axis=-1, keepdims=True),
                                        jnp.max(s_b, axis=-1, keepdims=True)), snk)
            p_c = jnp.exp2(s_c - m)
            p_b = jnp.exp2(s_b - m)
            den = (jnp.sum(p_c, axis=-1, keepdims=True) + jnp.sum(p_b, axis=-1, keepdims=True)
                   + jnp.exp2(snk - m))
            o = (_dot(p_c.astype(BF16), vc_ref[:, sl]) + _dot(p_b.astype(BF16), vband)) / den
            o_ref[:, g * GQA_GROUP * hd:(g + 1) * GQA_GROUP * hd] = jnp.concatenate(
                [o[j * BLOCK:(j + 1) * BLOCK, :] for j in range(GQA_GROUP)], axis=-1).astype(BF16)

    blk = lambda f: pl.BlockSpec((BLOCK, wk), f)
    prev_ = lambda b, i: (b * nb + jnp.maximum(i - 1, 0), 0)
    cur_ = lambda b, i: (b * nb + i, 0)
    next_ = lambda b, i: (b * nb + jnp.minimum(i + 1, nb - 1), 0)
    ctx_spec = pl.BlockSpec((lc, wk), lambda b, i: (b, 0))
    return pl.pallas_call(
        kern, grid=(batch, nb),
        in_specs=[pl.BlockSpec(memory_space=pltpu.SMEM),
                  pl.BlockSpec((BLOCK, wq), cur_), blk(prev_), blk(cur_), blk(next_),
                  blk(prev_), blk(cur_), blk(next_), ctx_spec, ctx_spec],
        out_specs=pl.BlockSpec((BLOCK, wq), cur_),
        out_shape=SDS((batch * length, wq), BF16), name="gqa_window",
        compiler_params=_params(("parallel", "arbitrary"), 40))(sink, q, k, k, k, v, v, v, kc, vc)


def _gqa_window_fast_call(batch, length, lc, q, k, v, kc, vc, sink2, shift):
    nb = length // BLOCK
    hd = GQA_HEAD_DIM
    wq, wk = GQA_Q_HEADS * hd, GQA_KV_HEADS * hd
    rows = GQA_GROUP * BLOCK

    def kern(sink_ref, shift_ref, q_ref, kp_ref, k0_ref, kn_ref, vp_ref, v0_ref, vn_ref, kc_ref, vc_ref, o_ref):
        nblk = pl.program_id(1)
        r = lax.broadcasted_iota(jnp.int32, (rows, 3 * BLOCK), 0) & (BLOCK - 1)
        c = lax.broadcasted_iota(jnp.int32, (rows, 3 * BLOCK), 1)
        valid = (c >= r + BLOCK - WINDOW) & (c <= r + BLOCK + WINDOW)
        valid = valid & ((c >= BLOCK) | (nblk > 0)) & ((c < 2 * BLOCK) | (nblk < nb - 1))
        hrow = lax.broadcasted_iota(jnp.int32, (rows, 1), 0) // BLOCK
        lane0_q = lax.broadcasted_iota(jnp.int32, (BLOCK, hd), 1) == 0

        def with_one(x):
            lane0 = lax.broadcasted_iota(jnp.int32, (x.shape[0], hd), 1) == 0
            return jnp.concatenate([x, jnp.where(lane0, 1.0, 0.0).astype(BF16)], axis=1)

        probs, values = [], []
        for g in range(GQA_KV_HEADS):
            sl = slice(g * hd, (g + 1) * hd)
            qg = jnp.concatenate(
                [jnp.concatenate([q_ref[:, h * hd:(h + 1) * hd],
                                  jnp.where(lane0_q, -shift_ref[h], 0.0).astype(BF16)], axis=1)
                 for h in range(g * GQA_GROUP, (g + 1) * GQA_GROUP)], axis=0)
            keys = with_one(jnp.concatenate([kc_ref[:, sl], kp_ref[:, sl], k0_ref[:, sl], kn_ref[:, sl]], axis=0))
            values.append(with_one(jnp.concatenate(
                [vc_ref[:, sl], vp_ref[:, sl], v0_ref[:, sl], vn_ref[:, sl]], axis=0)))
            s = _nt(qg, keys)
            s = jnp.concatenate([s[:, 0:lc], jnp.where(valid, s[:, lc:], -1e30)], axis=1)
            probs.append(jnp.exp2(s.astype(BF16)))
        for g in range(GQA_KV_HEADS):
            heads = range(g * GQA_GROUP, (g + 1) * GQA_GROUP)
            acc = _dot(probs[g], values[g])
            snk = jnp.zeros((rows, 1), F32)
            for j, h in enumerate(heads):
                snk = jnp.where(hrow == j, sink_ref[h] - shift_ref[h], snk)
            o = acc[:, 0:hd] / (acc[:, hd:hd + 1] + jnp.exp2(snk))
            o_ref[:, g * GQA_GROUP * hd:(g + 1) * GQA_GROUP * hd] = jnp.concatenate(
                [o[j * BLOCK:(j + 1) * BLOCK, :] for j in range(GQA_GROUP)], axis=-1).astype(BF16)

    blk = lambda f: pl.BlockSpec((BLOCK, wk), f)
    prev_ = lambda b, i: (b * nb + jnp.maximum(i - 1, 0), 0)
    cur_ = lambda b, i: (b * nb + i, 0)
    next_ = lambda b, i: (b * nb + jnp.minimum(i + 1, nb - 1), 0)
    ctx_spec = pl.BlockSpec((lc, wk), lambda b, i: (b, 0))
    smem = pl.BlockSpec(memory_space=pltpu.SMEM)
    return pl.pallas_call(
        kern, grid=(batch, nb),
        in_specs=[smem, smem, pl.BlockSpec((BLOCK, wq), cur_), blk(prev_), blk(cur_), blk(next_),
                  blk(prev_), blk(cur_), blk(next_), ctx_spec, ctx_spec],
        out_specs=pl.BlockSpec((BLOCK, wq), cur_),
        out_shape=SDS((batch * length, wq), BF16), name="gqa_window_fast",
        compiler_params=_params(("parallel", "arbitrary"), 40))(sink2, shift, q, k, k, k, v, v, v, kc, vc)


def _gqa_layer(streams, xs, mods, gain, p, rope, ctx_next):
    assert not ctx_next, "the windowed-GQA mixer is only implemented as the last layer"
    st_c, st_x = streams
    xc, xx = xs
    w_in, q_gain, k_gain, sink, w_out = p
    wq, wk = GQA_Q_HEADS * GQA_HEAD_DIM, GQA_KV_HEADS * GQA_HEAD_DIM
    w_q, w_k, w_v = w_in[:, :wq], w_in[:, wq:wq + wk], w_in[:, wq + wk:]
    w_in_p = jnp.concatenate([w_k, w_v, w_k[:, _rope_partner_perm(GQA_KV_HEADS)],
                              w_q, w_q[:, _rope_partner_perm(GQA_Q_HEADS)]], axis=1).astype(BF16)
    par = _rope_partner_perm(1)
    wts = (w_in_p, jnp.tile(q_gain, GQA_Q_HEADS).reshape(1, -1), jnp.tile(q_gain[par], GQA_Q_HEADS).reshape(1, -1),
           jnp.tile(k_gain, GQA_KV_HEADS).reshape(1, -1), jnp.tile(k_gain[par], GQA_KV_HEADS).reshape(1, -1))
    _, kc, vc = _gqa_proj_call(st_c.with_tile(PROJ_TILE), xc, mods, gain, wts, None, False)
    q, k, v = _gqa_proj_call(st_x.with_tile(PROJ_TILE), xx, mods, gain, wts, rope, True)
    sink2 = sink * LOG2E
    bound = LOG2E * GQA_SCALE * GQA_HEAD_DIM * jnp.max(jnp.abs(q_gain)) * jnp.max(jnp.abs(k_gain))
    shift = jnp.maximum(bound * SHIFT_MARGIN + 1.0, sink2).astype(BF16).astype(F32)
    args = (st_x.batch, st_x.seg, st_c.seg, q, k, v, kc, vc, sink2)
    a = lax.cond(jnp.max(shift) <= MAX_FIXED_SHIFT,
                 lambda: _gqa_window_fast_call(*args, shift),
                 lambda: _gqa_window_call(*args))
    return [None, _outproj_call(st_x, a, mods, w_out.astype(BF16), "gqa_out")]


def _cast_experts_call(w, layer):
    _, ne, a, b = w.shape
    eb = 4

    def kern(w_ref, o_ref):
        o_ref[...] = w_ref[...].astype(BF16)

    return pl.pallas_call(
        kern, grid=(ne // eb,),
        in_specs=[pl.BlockSpec((None, eb, a, b), lambda i: (layer, i, 0, 0))],
        out_specs=pl.BlockSpec((eb, a, b), lambda i: (i, 0, 0)),
        out_shape=SDS((ne, a, b), BF16), name="cast_experts",
        compiler_params=_params(("parallel",), 32))(w)


def _route(logits_t, bias_col):
    scores = _sigmoid(logits_t)
    biased = scores + bias_col
    rows = [biased[e:e + 1, :] for e in range(N_EXPERTS)]
    srow = [scores[e:e + 1, :] for e in range(N_EXPERTS)]
    epg = EXPERTS_PER_GROUP
    gscore = []
    for g in range(N_EXPERT_GROUPS):
        v = rows[g * epg:(g + 1) * epg]
        pair = [v[a] + v[b] for a in range(epg) for b in range(a + 1, epg)]
        gscore.append(functools.reduce(jnp.maximum, pair))
    ind, wloc = [], [None] * epg
    for g in range(N_EXPERT_GROUPS):
        best = None
        for g2 in range(N_EXPERT_GROUPS):
            if g2 == g:
                continue
            cnd = gscore[g] > gscore[g2] if g2 < g else gscore[g] >= gscore[g2]
            best = cnd if best is None else best & cnd
        ind.append(jnp.where(best, 1.0, 0.0))
        for j in range(epg):
            e = g * epg + j
            rank = None
            for e2 in range(g * epg, (g + 1) * epg):
                if e2 == e:
                    continue
                ahead = rows[e2] >= rows[e] if e2 < e else rows[e2] > rows[e]
                one = jnp.where(ahead, 1.0, 0.0)
                rank = one if rank is None else rank + one
            w = jnp.where(best & (rank < 2.0), srow[e], 0.0)
            wloc[j] = w if wloc[j] is None else wloc[j] + w
    den = functools.reduce(lambda a, b: a + b, wloc)
    return ind, [w / den for w in wloc]


MOE_TILE = 1024
MOE_CHUNK = 128


def _moe_call(st, x_in, y_in, mods, gain, router_wt, router_bias, w_gate, w_up, w_down):
    n, d = x_in.shape
    t = min(MOE_TILE, n)
    r = MOE_CHUNK
    ne, _, ff = w_gate.shape
    epg, ng = EXPERTS_PER_GROUP, N_EXPERT_GROUPS
    nch = t // r + ng - 1
    tiles = n // t
    wd2 = w_down.reshape(ne * ff, d)

    def kern(x_ref, y_ref, g_ref, sc_ref, sh_ref, g2_ref, rw_ref, rb_ref, wg_ref, wu_ref, wd_ref, o_ref,
             hb_ref, oh_ref, xg_ref):
        hb_ref[...] = _modnorm(x_ref[...] + y_ref[...].astype(F32), g_ref[...], sc_ref[...], sh_ref[...]).astype(BF16)
        ind, wloc = _route(_nt(rw_ref[...], hb_ref[...]), rb_ref[...])
        ind8 = jnp.concatenate(ind + [jnp.zeros((8 - ng, t), F32)], axis=0)
        upper = jnp.where(lax.broadcasted_iota(jnp.int32, (LANES, LANES), 0)
                          < lax.broadcasted_iota(jnp.int32, (LANES, LANES), 1), 1.0, 0.0).astype(BF16)
        blocks, before = [], jnp.zeros((8, 1), F32)
        for k in range(t // LANES):
            blk = ind8[:, k * LANES:(k + 1) * LANES]
            blocks.append(_dot(blk.astype(BF16), upper) + before)
            before = before + jnp.sum(blk, axis=1, keepdims=True)
        pos = jnp.concatenate(blocks, axis=1)
        count = [jnp.sum(ind[g]).astype(jnp.int32) for g in range(ng)]
        first = [jnp.int32(0)]
        for g in range(ng):
            first.append(first[g] + (count[g] + (r - 1)) // r)
        slot = functools.reduce(lambda a, b: a + b, [
            ind[g] * (pos[g:g + 1, :] + (first[g] * r).astype(F32)) for g in range(ng)])
        w_hi = [w.astype(BF16) for w in wloc]
        w_lo = [(w - h.astype(F32)).astype(BF16) for w, h in zip(wloc, w_hi)]
        wst = jnp.concatenate(w_hi + w_lo, axis=0)
        row_id = lax.broadcasted_iota(jnp.int32, (r, t), 0).astype(F32)
        onehot = lambda c: jnp.where(slot == row_id + float(c * r), 1.0, 0.0).astype(BF16)
        last = nch - 1
        for c in range(last):
            oh_ref[c * r:(c + 1) * r, :] = onehot(c)
        xg_ref[...] = _dot(oh_ref[...], hb_ref[...]).astype(BF16)
        wr_all = _nt(wst, oh_ref[...]).T

        def chunk_ffn(c, xg, wr):
            grp = functools.reduce(lambda a, b: a + b,
                                   [(c >= first[g]).astype(jnp.int32) for g in range(1, ng)])
            acts = []
            for j in range(epg):
                e = grp * epg + j
                gt = _dot(xg, wg_ref[e])
                up = _dot(xg, wu_ref[e])
                acts.append((gt * _sigmoid(gt) * up * (wr[:, j:j + 1] + wr[:, epg + j:epg + j + 1])).astype(BF16))
            wd_g = wd_ref[pl.ds(pl.multiple_of(grp * (epg * ff), epg * ff), epg * ff), :]
            return _dot(jnp.concatenate(acts, axis=1), wd_g).astype(BF16)

        for c in range(last):
            rows = slice(c * r, (c + 1) * r)

            def run(c=c, rows=rows):
                xg_ref[rows, :] = chunk_ffn(c, xg_ref[rows, :], wr_all[c * r:(c + 1) * r, :])

            run()

        tdot = lambda a, b: lax.dot_general(a, b, (((0,), (0,)), ((), ())), preferred_element_type=F32)
        o_ref[...] = x_ref[...] + y_ref[...].astype(F32) + g2_ref[...] * tdot(oh_ref[...], xg_ref[...])

        @pl.when(last < first[ng])
        def _():
            oh = onehot(last)
            out = chunk_ffn(last, _dot(oh, hb_ref[...]).astype(BF16), _nt(wst, oh).T)
            o_ref[...] += g2_ref[...] * tdot(oh, out)

    row = (lambda i: st.batch) if st.is_ctx else (lambda i: i // (st.seg // t))
    mod = lambda chunk: pl.BlockSpec((None, 1, d), lambda i: (row(i), 0, chunk))
    return pl.pallas_call(
        kern, grid=(tiles,),
        in_specs=[pl.BlockSpec((t, d), lambda i: (i, 0)),
                  pl.BlockSpec((t, d), lambda i: (i, 0)),
                  _full((1, d)), mod(SC2), mod(SH2), mod(G2), _full((ne, d)), _full((ne, 1)),
                  _resident((ne, d, ff)), _resident((ne, d, ff)), _resident((ne * ff, d))],
        out_specs=pl.BlockSpec((t, d), lambda i: (i, 0)),
        out_shape=SDS((n, d), F32),
        scratch_shapes=[pltpu.VMEM((t, d), BF16), pltpu.VMEM(((nch - 1) * r, t), BF16),
                        pltpu.VMEM(((nch - 1) * r, d), BF16)],
        name="moe", compiler_params=_params(("parallel",), 58))(
            x_in, y_in, gain, mods, mods, mods, router_wt, router_bias, w_gate, w_up, wd2)


def kernel(x, c, ctx, c_ctx, w_ada, b_ada, norm_mix, norm_ffn, fourier_w_out, conv_w_in, conv_w, conv_w_out,
           mla_w_in, mla_q_norm, mla_kv_norm, mla_w_uq, mla_w_ukv, mla_q_gain, mla_k_gain, mla_w_out,
           gqa_w_in, gqa_q_gain, gqa_k_gain, gqa_sink, gqa_w_out, router_w, router_bias,
           moe_w_gate, moe_w_up, moe_w_down):
    b, l, d = x.shape
    lc = ctx.shape[1]
    depth = w_ada.shape[0]
    st_c = Stream(b * lc, lc, min(b * lc, TOKEN_TILE), b, True)
    st_x = Stream(b * l, l, min(l, TOKEN_TILE), b, False)
    streams = [st_c, st_x]
    r8 = -(-(b + 1) // 8) * 8
    cvec = jnp.concatenate([c, c_ctx[None, :], jnp.zeros((r8 - b - 1, d), F32)], axis=0)
    mods_all = _ada_call(cvec, w_ada, b_ada).reshape(depth, r8, 1, 6 * d)
    rope = _rope_tables(l)
    router_wt = router_w.T.astype(BF16)
    router_b = router_bias.reshape(-1, 1)
    xs = [ctx.reshape(b * lc, d), x.reshape(b * l, d)]
    for i in range(depth):
        kind, j = i % 4, i // 4
        ctx_next = i < depth - 1
        mods = mods_all[i]
        gain = norm_mix[i].reshape(1, d)
        if not (ctx_next or kind >= 2):
            xs[0] = None
        if kind == 0:
            ys = _fourier_layer(streams, xs, mods, gain, fourier_w_out[j])
        elif kind == 1:
            ys = _conv_layer(streams, xs, mods, gain, conv_w_in[j], conv_w[j], conv_w_out[j])
        elif kind == 2:
            ys = _mla_layer(streams, xs, mods, gain,
                            (mla_w_in[j], mla_q_norm[j], mla_kv_norm[j], mla_w_uq[j], mla_w_ukv[j],
                             mla_q_gain[j], mla_k_gain[j], mla_w_out[j]), rope, ctx_next)
        else:
            ys = _gqa_layer(streams, xs, mods, gain,
                            (gqa_w_in[j], gqa_q_gain[j], gqa_k_gain[j], gqa_sink[j], gqa_w_out[j]),
                            rope, ctx_next)
        if not ctx_next:
            ys[0] = None
        gain2 = norm_ffn[i].reshape(1, d)
        wg, wu, wd = (_cast_experts_call(w, i) for w in (moe_w_gate, moe_w_up, moe_w_down))
        xs = [None if y is None else _moe_call(st, x, y, mods, gain2, router_wt, router_b, wg, wu, wd)
              for st, x, y in zip(streams, xs, ys)]
    return xs[1].reshape(b, l, d)
```
